```python
import math
import jax, jax.numpy as jnp
from jax import lax
import numpy as np

D_MODEL = 1024
BATCH = 8
SEQ = 4096
DEPTH = 2

MEM_LEN = 256
D_MIX = D_MODEL
HEAD_DIM = 64
ATTN_Q_HEADS = 8
ATTN_KV_HEADS = 2
ATTN_WIDTH = ATTN_Q_HEADS * HEAD_DIM
KV_WIDTH = ATTN_KV_HEADS * HEAD_DIM
WINDOW = 128
BLOCK = 128
SC_WIDTH = D_MIX // 4
SC_CONV = 3
RG_HEADS = 4
RG_WIDTH = D_MIX // 4
RG_HEAD_DIM = RG_WIDTH // RG_HEADS
RG_CONV = 4
RG_C = 8.0
N_BUCKETS = 32
MAX_EXACT = N_BUCKETS // 2
MAX_DISTANCE = 128
XA_HEADS = 4
XA_HEAD_DIM = 128
XA_WIDTH = XA_HEADS * XA_HEAD_DIM
D_FF_DENSE = 2816
N_EXPERTS = 8
TOP_K = 2
D_FF_EXPERT = 3584
N_DENSE = (DEPTH + 1) // 2
N_MOE = DEPTH // 2
EPS = 1e-6
NEG_INF = -1e30
IN_COLS = ATTN_WIDTH + 2 * KV_WIDTH + 3 * SC_WIDTH + 2 * RG_WIDTH

kernel_name = "hymba_style_hybrid_swa_shortconv_rglru_moe"


def rms_norm(x, g):
    xf = x.astype(jnp.float32)
    y = xf * lax.rsqrt(jnp.mean(xf * xf, axis=-1, keepdims=True) + EPS) * g.astype(jnp.float32)
    return y.astype(x.dtype)


def causal_dwconv(u, w, b):
    k_width = w.shape[0]
    s = u.shape[1]
    up = jnp.pad(u, ((0, 0), (k_width - 1, 0), (0, 0)))
    out = b
    for k in range(k_width):
        out = out + w[k] * up[:, k:k + s]
    return out


def t5_bucket(dist):
    n = jnp.maximum(dist, 0)
    large = MAX_EXACT + (jnp.log(jnp.maximum(n, 1).astype(jnp.float32) / MAX_EXACT)
                         / math.log(MAX_DISTANCE / MAX_EXACT) * (N_BUCKETS - MAX_EXACT)).astype(jnp.int32)
    large = jnp.minimum(large, N_BUCKETS - 1)
    return jnp.where(n < MAX_EXACT, n, large)


def sliding_window_sink_attention(q, k, v, sinks, rel_bias):
    b, s, _ = q.shape
    nb = s // BLOCK
    g = ATTN_Q_HEADS // ATTN_KV_HEADS
    qb = q.reshape(b, nb, BLOCK, ATTN_KV_HEADS, g, HEAD_DIM)
    kb = k.reshape(b, nb, BLOCK, ATTN_KV_HEADS, HEAD_DIM)
    vb = v.reshape(b, nb, BLOCK, ATTN_KV_HEADS, HEAD_DIM)

    def band(t):
        prev = jnp.pad(t, ((0, 0), (1, 0), (0, 0), (0, 0), (0, 0)))[:, :-1]
        return jnp.concatenate([prev, t], axis=2)

    kband, vband = band(kb), band(vb)
    scores = jnp.einsum('bnqhgd,bnkhd->bnhgqk', qb, kband,
                        preferred_element_type=jnp.float32) / math.sqrt(HEAD_DIM)
    q_idx = jnp.arange(BLOCK)
    k_idx = jnp.arange(2 * BLOCK)
    dist = q_idx[:, None] + BLOCK - k_idx[None, :]
    in_window = (dist >= 0) & (dist < WINDOW)
    has_prev = (jnp.arange(nb)[:, None] > 0) | (k_idx[None, :] >= BLOCK)
    mask = in_window[None, :, :] & has_prev[:, None, :]
    bias = rel_bias[t5_bucket(dist)].astype(jnp.float32)
    bias = bias.transpose(2, 0, 1).reshape(ATTN_KV_HEADS, g, BLOCK, 2 * BLOCK)
    logits = jnp.where(mask[None, :, None, None, :, :], scores + bias, NEG_INF)
    sink = sinks.astype(jnp.float32).reshape(ATTN_KV_HEADS, g)[None, None, :, :, None, None]
    m = jnp.maximum(jnp.max(logits, axis=-1, keepdims=True), sink)
    p = jnp.exp(logits - m)
    probs = p / (jnp.sum(p, axis=-1, keepdims=True) + jnp.exp(sink - m))
    out = jnp.einsum('bnhgqk,bnkhd->bnqhgd', probs.astype(v.dtype), vband)
    return out.reshape(b, s, ATTN_WIDTH)


def rg_lru(x, w_a, b_a, w_x, b_x, lam):
    b, s, _ = x.shape
    xf = x.astype(jnp.float32)
    xb = xf.reshape(b, s, RG_HEADS, RG_HEAD_DIM)
    r = jax.nn.sigmoid(jnp.einsum('bshi,hij->bshj', xb, w_a.astype(jnp.float32)).reshape(b, s, RG_WIDTH)
                       + b_a.astype(jnp.float32))
    i = jax.nn.sigmoid(jnp.einsum('bshi,hij->bshj', xb, w_x.astype(jnp.float32)).reshape(b, s, RG_WIDTH)
                       + b_x.astype(jnp.float32))
    log_a = -RG_C * r * jax.nn.softplus(-lam.astype(jnp.float32))
    a = jnp.exp(log_a)
    u = jnp.sqrt(-jnp.expm1(2.0 * log_a)) * (i * xf)

    def combine(c1, c2):
        a1, b1 = c1
        a2, b2 = c2
        return a1 * a2, a2 * b1 + b2

    _, h = lax.associative_scan(combine, (a, u), axis=1)
    return h.astype(x.dtype)


def memory_cross_attention(h, mem_n, wq, wk, wv, wo):
    b, s, _ = h.shape
    q = (h @ wq).reshape(b, s, XA_HEADS, XA_HEAD_DIM)
    k = (mem_n @ wk).reshape(b, MEM_LEN, XA_HEADS, XA_HEAD_DIM)
    v = (mem_n @ wv).reshape(b, MEM_LEN, XA_HEADS, XA_HEAD_DIM)
    scores = jnp.einsum('bshd,bmhd->bhsm', q, k, preferred_element_type=jnp.float32) / math.sqrt(XA_HEAD_DIM)
    probs = jax.nn.softmax(scores, axis=-1).astype(v.dtype)
    out = jnp.einsum('bhsm,bmhd->bshd', probs, v).reshape(b, s, XA_WIDTH)
    return out @ wo


def swiglu(h, wg, wu, wd):
    return (jax.nn.silu(h @ wg) * (h @ wu)) @ wd


def moe_swiglu(h, router, wg, wu, wd):
    logits = (h @ router).astype(jnp.float32)
    top_val, top_idx = lax.top_k(logits, TOP_K)
    w = jax.nn.softmax(top_val, axis=-1)
    gates = jnp.sum(jax.nn.one_hot(top_idx, N_EXPERTS, dtype=jnp.float32) * w[..., None], axis=-2)
    out = jnp.zeros_like(h)
    for e in range(N_EXPERTS):
        out = out + gates[..., e:e + 1].astype(h.dtype) * swiglu(h, wg[e], wu[e], wd[e])
    return out


def setup_inputs(seed: int = 0) -> dict:
    key = jax.random.key(seed)
    ks = jax.random.split(key, 32)
    L = DEPTH

    def nrm(k, shape, scale):
        return jax.random.normal(k, shape, jnp.float32) * scale

    def gain(k, shape):
        return 1.0 + 0.05 * jax.random.normal(k, shape, jnp.float32)

    a0 = jax.random.uniform(ks[14], (L, RG_WIDTH), jnp.float32, minval=0.9, maxval=0.999)
    return {
        "x": nrm(ks[0], (BATCH, SEQ, D_MODEL), 1.0),
        "mem": nrm(ks[1], (BATCH, MEM_LEN, D_MODEL), 1.0),
        "rel_bias": nrm(ks[2], (N_BUCKETS, ATTN_Q_HEADS), 0.5),
        "mix_norm": gain(ks[3], (L, D_MODEL)),
        "w_in": nrm(ks[4], (L, D_MODEL, IN_COLS), D_MODEL ** -0.5),
        "attn_sinks": nrm(ks[5], (L, ATTN_Q_HEADS), 0.5),
        "sc_conv_w": nrm(ks[6], (L, SC_CONV, SC_WIDTH), SC_CONV ** -0.5),
        "sc_conv_b": nrm(ks[7], (L, SC_WIDTH), 0.01),
        "rg_conv_w": nrm(ks[8], (L, RG_CONV, RG_WIDTH), RG_CONV ** -0.5),
        "rg_conv_b": nrm(ks[9], (L, RG_WIDTH), 0.01),
        "rg_w_a": nrm(ks[10], (L, RG_HEADS, RG_HEAD_DIM, RG_HEAD_DIM), RG_HEAD_DIM ** -0.5),
        "rg_b_a": nrm(ks[11], (L, RG_WIDTH), 0.01),
        "rg_w_x": nrm(ks[12], (L, RG_HEADS, RG_HEAD_DIM, RG_HEAD_DIM), RG_HEAD_DIM ** -0.5),
        "rg_b_x": nrm(ks[13], (L, RG_WIDTH), 0.01),
        "rg_lambda": jnp.log(a0) - jnp.log1p(-a0),
        "w_out": nrm(ks[15], (L, D_MIX, D_MODEL), D_MIX ** -0.5),
        "xa_norm": gain(ks[16], (L, D_MODEL)),
        "mem_norm": gain(ks[17], (L, D_MODEL)),
        "xa_wq": nrm(ks[18], (L, D_MODEL, XA_WIDTH), D_MODEL ** -0.5),
        "xa_wk": nrm(ks[19], (L, D_MODEL, XA_WIDTH), D_MODEL ** -0.5),
        "xa_wv": nrm(ks[20], (L, D_MODEL, XA_WIDTH), D_MODEL ** -0.5),
        "xa_wo": nrm(ks[21], (L, XA_WIDTH, D_MODEL), XA_WIDTH ** -0.5),
        "ffn_norm": gain(ks[22], (L, D_MODEL)),
        "dense_wg": nrm(ks[23], (N_DENSE, D_MODEL, D_FF_DENSE), D_MODEL ** -0.5),
        "dense_wu": nrm(ks[24], (N_DENSE, D_MODEL, D_FF_DENSE), D_MODEL ** -0.5),
        "dense_wd": nrm(ks[25], (N_DENSE, D_FF_DENSE, D_MODEL), D_FF_DENSE ** -0.5),
        "moe_router": nrm(ks[26], (N_MOE, D_MODEL, N_EXPERTS), D_MODEL ** -0.5),
        "moe_wg": nrm(ks[27], (N_MOE, N_EXPERTS, D_MODEL, D_FF_EXPERT), D_MODEL ** -0.5),
        "moe_wu": nrm(ks[28], (N_MOE, N_EXPERTS, D_MODEL, D_FF_EXPERT), D_MODEL ** -0.5),
        "moe_wd": nrm(ks[29], (N_MOE, N_EXPERTS, D_FF_EXPERT, D_MODEL), D_FF_EXPERT ** -0.5),
        "final_norm": gain(ks[30], (D_MODEL,)),
    }


def reference(x, mem, rel_bias, mix_norm, w_in, attn_sinks, sc_conv_w, sc_conv_b,
              rg_conv_w, rg_conv_b, rg_w_a, rg_b_a, rg_w_x, rg_b_x, rg_lambda, w_out,
              xa_norm, mem_norm, xa_wq, xa_wk, xa_wv, xa_wo, ffn_norm,
              dense_wg, dense_wu, dense_wd, moe_router, moe_wg, moe_wu, moe_wd, final_norm):
    widths = [ATTN_WIDTH, KV_WIDTH, KV_WIDTH, SC_WIDTH, SC_WIDTH, SC_WIDTH, RG_WIDTH, RG_WIDTH]
    offsets = [int(o) for o in np.cumsum(widths)[:-1]]
    for layer in range(DEPTH):
        h = rms_norm(x, mix_norm[layer])
        proj = h @ w_in[layer]
        q, k, v, sc_b, sc_c, sc_x, rg_x, rg_g = jnp.split(proj, offsets, axis=-1)
        attn_out = sliding_window_sink_attention(q, k, v, attn_sinks[layer], rel_bias)
        conv_out = sc_b * causal_dwconv(sc_c * sc_x, sc_conv_w[layer], sc_conv_b[layer])
        rg_in = causal_dwconv(rg_x, rg_conv_w[layer], rg_conv_b[layer])
        rg_out = rg_lru(rg_in, rg_w_a[layer], rg_b_a[layer], rg_w_x[layer], rg_b_x[layer],
                        rg_lambda[layer]) * jax.nn.gelu(rg_g)
        mixed = jnp.concatenate([attn_out, conv_out, rg_out], axis=-1)
        x = x + mixed @ w_out[layer]
        hx = rms_norm(x, xa_norm[layer])
        mem_n = rms_norm(mem, mem_norm[layer])
        x = x + memory_cross_attention(hx, mem_n, xa_wq[layer], xa_wk[layer], xa_wv[layer], xa_wo[layer])
        hf = rms_norm(x, ffn_norm[layer])
        if layer % 2 == 0:
            j = layer // 2
            x = x + swiglu(hf, dense_wg[j], dense_wu[j], dense_wd[j])
        else:
            j = layer // 2
            x = x + moe_swiglu(hf, moe_router[j], moe_wg[j], moe_wu[j], moe_wd[j])
    return rms_norm(x, final_norm)
```

```python
import functools
import math

import jax
import jax.numpy as jnp
import numpy as np
from jax import lax
from jax.experimental import pallas as pl
from jax.experimental.pallas import tpu as pltpu

F32 = jnp.float32
BF16 = jnp.bfloat16

D_MODEL = 1024
MEM_LEN = 256
HEAD_DIM = 64
ATTN_Q_HEADS = 8
ATTN_KV_HEADS = 2
ATTN_WIDTH = ATTN_Q_HEADS * HEAD_DIM
KV_WIDTH = ATTN_KV_HEADS * HEAD_DIM
BLOCK = 128
SC_WIDTH = 256
SC_CONV = 3
RG_WIDTH = 256
RG_HEADS = 4
RG_HEAD_DIM = RG_WIDTH // RG_HEADS
RG_CONV = 4
RG_C = 8.0
N_BUCKETS = 32
MAX_EXACT = N_BUCKETS // 2
MAX_DISTANCE = 128
XA_HEADS = 4
XA_HEAD_DIM = 128
XA_WIDTH = XA_HEADS * XA_HEAD_DIM
N_EXPERTS = 8
EPS = 1e-6
NEG_INF = -1e30
REST_WIDTH = 3 * SC_WIDTH + 2 * RG_WIDTH

LANES_V7X = 128
SUBLANES_V7X = 8
VMEM_BYTES_V7X = 64 * 1024 * 1024

TM = 512
TS = 512
TM_ROUTE = 512
TM_GROUP = 512
TM_MOVE = 256
FF_SPLIT = 2
CARRY_ROWS = SUBLANES_V7X


def _mib(n):
    return int(n * 1024 * 1024)


def _params(semantics, vmem_mib):
    assert _mib(vmem_mib) < VMEM_BYTES_V7X
    return pltpu.CompilerParams(dimension_semantics=semantics, vmem_limit_bytes=_mib(vmem_mib))


def _rms(x, g):
    ms = jnp.mean(x * x, axis=-1, keepdims=True)
    return x * lax.rsqrt(ms + EPS) * g


def _const_spec(shape):
    nd = len(shape)
    return pl.BlockSpec(shape, lambda *_: (0,) * nd)


def _in_proj_kernel(x_ref, g_ref, w_ref, q_ref, k_ref, v_ref, r_ref):
    h = _rms(x_ref[...], g_ref[...]).astype(BF16)
    p = jnp.dot(h, w_ref[...], preferred_element_type=F32)
    q_ref[...] = p[:, :ATTN_WIDTH].astype(BF16)
    k_ref[...] = p[:, ATTN_WIDTH:ATTN_WIDTH + KV_WIDTH].astype(BF16)
    v_ref[...] = p[:, ATTN_WIDTH + KV_WIDTH:ATTN_WIDTH + 2 * KV_WIDTH].astype(BF16)
    r_ref[...] = p[:, ATTN_WIDTH + 2 * KV_WIDTH:]


def _in_proj(x, g, w):
    t = x.shape[0]
    n = w.shape[1]
    row = lambda width: pl.BlockSpec((TM, width), lambda i: (i, 0))
    return pl.pallas_call(
        _in_proj_kernel,
        grid=(t // TM,),
        in_specs=[row(D_MODEL), _const_spec((1, D_MODEL)), _const_spec((D_MODEL, n))],
        out_specs=[row(ATTN_WIDTH), row(KV_WIDTH), row(KV_WIDTH), row(REST_WIDTH)],
        out_shape=[jax.ShapeDtypeStruct((t, ATTN_WIDTH), BF16),
                   jax.ShapeDtypeStruct((t, KV_WIDTH), BF16),
                   jax.ShapeDtypeStruct((t, KV_WIDTH), BF16),
                   jax.ShapeDtypeStruct((t, REST_WIDTH), F32)],
        compiler_params=_params(("parallel",), 40),
        name="in_proj",
    )(x, g, w)


def _attn_kernel(sink_ref, q_ref, kp_ref, kc_ref, vp_ref, vc_ref, bias_ref, o_ref):
    q = q_ref[...]
    kband = jnp.concatenate([kp_ref[...], kc_ref[...]], axis=0)
    vband = jnp.concatenate([vp_ref[...], vc_ref[...]], axis=0)
    group = ATTN_Q_HEADS // ATTN_KV_HEADS
    outs = []
    for h in range(ATTN_Q_HEADS):
        kv = slice((h // group) * HEAD_DIM, (h // group + 1) * HEAD_DIM)
        qh = q[:, h * HEAD_DIM:(h + 1) * HEAD_DIM]
        s = lax.dot_general(qh, kband[:, kv], (((1,), (1,)), ((), ())), preferred_element_type=F32)
        logits = s * (1.0 / math.sqrt(HEAD_DIM)) + bias_ref[h]
        sink = sink_ref[h]
        m = jnp.maximum(jnp.max(logits, axis=-1, keepdims=True), sink)
        p = jnp.exp(logits - m)
        denom = jnp.sum(p, axis=-1, keepdims=True) + jnp.exp(sink - m)
        o = jnp.dot(p.astype(BF16), vband[:, kv], preferred_element_type=F32)
        outs.append(o / denom)
    o_ref[...] = jnp.concatenate(outs, axis=-1).astype(BF16)


def _attention(q, k, v, sinks, bias_tbl, blocks_per_seq):
    t = q.shape[0]
    cur = lambda i: (i, 0)
    prev = lambda i: (jnp.maximum(i - 1, 0), 0)
    return pl.pallas_call(
        _attn_kernel,
        grid=(t // BLOCK,),
        in_specs=[pl.BlockSpec(memory_space=pltpu.SMEM),
                  pl.BlockSpec((BLOCK, ATTN_WIDTH), cur),
                  pl.BlockSpec((BLOCK, KV_WIDTH), prev),
                  pl.BlockSpec((BLOCK, KV_WIDTH), cur),
                  pl.BlockSpec((BLOCK, KV_WIDTH), prev),
                  pl.BlockSpec((BLOCK, KV_WIDTH), cur),
                  pl.BlockSpec((None, ATTN_Q_HEADS, BLOCK, 2 * BLOCK),
                               lambda i: (jnp.minimum(i % blocks_per_seq, 1), 0, 0, 0))],
        out_specs=pl.BlockSpec((BLOCK, ATTN_WIDTH), cur),
        out_shape=jax.ShapeDtypeStruct((t, ATTN_WIDTH), BF16),
        compiler_params=_params(("parallel",), 24),
        name="swa_attention",
    )(sinks, q, k, k, v, v, bias_tbl)


def _attention_bias_tables(rel_bias):
    q_idx = np.arange(BLOCK)
    k_idx = np.arange(2 * BLOCK)
    dist = q_idx[:, None] + BLOCK - k_idx[None, :]
    in_window = (dist >= 0) & (dist < BLOCK)
    n = np.maximum(dist, 0)
    large = MAX_EXACT + (np.log(np.maximum(n, 1).astype(np.float32) / np.float32(MAX_EXACT))
                         / np.float32(math.log(MAX_DISTANCE / MAX_EXACT))
                         * np.float32(N_BUCKETS - MAX_EXACT)).astype(np.int32)
    bucket = np.where(n < MAX_EXACT, n, np.minimum(large, N_BUCKETS - 1))
    bias = rel_bias[jnp.asarray(bucket)].astype(F32).transpose(2, 0, 1)
    band = jnp.where(jnp.asarray(in_window)[None], bias, NEG_INF)
    first = jnp.where(jnp.asarray(in_window & (k_idx[None, :] >= BLOCK))[None], bias, NEG_INF)
    return jnp.stack([first, band])


def _shift_rows(x, s, fill):
    return jnp.concatenate([jnp.full((s, x.shape[1]), fill, x.dtype), x[:x.shape[0] - s]], axis=0)


def _conv_rg_kernel(r_ref, scw_ref, scb_ref, rgw_ref, rgb_ref, wgate_ref, bgate_ref, lam_ref,
                    o_ref, sc_ext, rg_ext, h_carry):
    ts = r_ref.shape[0]
    c0 = CARRY_ROWS

    @pl.when(pl.program_id(1) == 0)
    def _():
        sc_ext[0:c0, :] = jnp.zeros((c0, SC_WIDTH), F32)
        rg_ext[0:c0, :] = jnp.zeros((c0, RG_WIDTH), F32)
        h_carry[...] = jnp.zeros_like(h_carry)

    sc_b = r_ref[:, 0:SC_WIDTH]
    sc_ext[c0:c0 + ts, :] = r_ref[:, SC_WIDTH:2 * SC_WIDTH] * r_ref[:, 2 * SC_WIDTH:3 * SC_WIDTH]
    rg_ext[c0:c0 + ts, :] = r_ref[:, 3 * SC_WIDTH:3 * SC_WIDTH + RG_WIDTH]
    rg_g = r_ref[:, 3 * SC_WIDTH + RG_WIDTH:]

    conv = scb_ref[...]
    for k in range(SC_CONV):
        off = c0 - (SC_CONV - 1) + k
        conv = conv + scw_ref[k:k + 1, :] * sc_ext[off:off + ts, :]
    conv_out = sc_b * conv

    rg_in = rgb_ref[...]
    for k in range(RG_CONV):
        off = c0 - (RG_CONV - 1) + k
        rg_in = rg_in + rgw_ref[k:k + 1, :] * rg_ext[off:off + ts, :]

    sc_ext[0:c0, :] = sc_ext[ts:ts + c0, :]
    rg_ext[0:c0, :] = rg_ext[ts:ts + c0, :]

    gates = jnp.dot(rg_in.astype(BF16), wgate_ref[...], preferred_element_type=F32) + bgate_ref[...]
    r_gate = jax.nn.sigmoid(gates[:, :RG_WIDTH])
    i_gate = jax.nn.sigmoid(gates[:, RG_WIDTH:])
    neg_lam = -lam_ref[...]
    softplus = jnp.maximum(neg_lam, 0.0) + jnp.log1p(jnp.exp(-jnp.abs(neg_lam)))
    log_a = -RG_C * r_gate * softplus
    a = jnp.exp(log_a)
    u = jnp.sqrt(jnp.tanh(-log_a) * (1.0 + a * a)) * (i_gate * rg_in)

    s = 1
    while s < ts:
        u = a * _shift_rows(u, s, 0.0) + u
        a = a * _shift_rows(a, s, 1.0)
        s *= 2
    h = a * h_carry[...] + u
    h_carry[...] = h[ts - 1:ts, :]

    c = math.sqrt(2.0 / math.pi)
    gelu = 0.5 * rg_g * (1.0 + jnp.tanh(c * (rg_g + 0.044715 * (rg_g * rg_g * rg_g))))
    o_ref[:, 0:SC_WIDTH] = conv_out.astype(BF16)
    o_ref[:, SC_WIDTH:] = (h * gelu).astype(BF16)


def _conv_rg(rest, sc_w, sc_b, rg_w, rg_b, w_gate, b_gate, lam, batch, seq):
    t = rest.shape[0]
    steps = seq // TS
    row = lambda b, s: (b * steps + s, 0)
    return pl.pallas_call(
        _conv_rg_kernel,
        grid=(batch, steps),
        in_specs=[pl.BlockSpec((TS, REST_WIDTH), row),
                  _const_spec((SC_CONV, SC_WIDTH)), _const_spec((1, SC_WIDTH)),
                  _const_spec((RG_CONV, RG_WIDTH)), _const_spec((1, RG_WIDTH)),
                  _const_spec((RG_WIDTH, 2 * RG_WIDTH)), _const_spec((1, 2 * RG_WIDTH)),
                  _const_spec((1, RG_WIDTH))],
        out_specs=pl.BlockSpec((TS, SC_WIDTH + RG_WIDTH), row),
        out_shape=jax.ShapeDtypeStruct((t, SC_WIDTH + RG_WIDTH), BF16),
        scratch_shapes=[pltpu.VMEM((TS + 2 * CARRY_ROWS, SC_WIDTH), F32),
                        pltpu.VMEM((TS + 2 * CARRY_ROWS, RG_WIDTH), F32),
                        pltpu.VMEM((1, RG_WIDTH), F32)],
        compiler_params=_params(("arbitrary", "arbitrary"), 32),
        name="conv_rglru",
    )(rest, sc_w, sc_b, rg_w, rg_b, w_gate, b_gate, lam)


def _out_proj_kernel(x_ref, a_ref, c_ref, w_ref, o_ref):
    mixed = jnp.concatenate([a_ref[...], c_ref[...]], axis=-1)
    o_ref[...] = x_ref[...] + jnp.dot(mixed, w_ref[...], preferred_element_type=F32)


def _out_proj(x, attn, cr, w):
    t = x.shape[0]
    row = lambda width: pl.BlockSpec((TM, width), lambda i: (i, 0))
    return pl.pallas_call(
        _out_proj_kernel,
        grid=(t // TM,),
        in_specs=[row(D_MODEL), row(ATTN_WIDTH), row(SC_WIDTH + RG_WIDTH), _const_spec((D_MODEL, D_MODEL))],
        out_specs=row(D_MODEL),
        out_shape=jax.ShapeDtypeStruct((t, D_MODEL), F32),
        compiler_params=_params(("parallel",), 24),
        name="out_proj",
    )(x, attn, cr, w)


def _mem_kv_kernel(m_ref, g_ref, w_ref, k_ref, v_ref):
    h = _rms(m_ref[...], g_ref[...]).astype(BF16)
    p = jnp.dot(h, w_ref[...], preferred_element_type=F32)
    k_ref[...] = p[:, :XA_WIDTH].astype(BF16)
    v_ref[...] = p[:, XA_WIDTH:].astype(BF16)


def _mem_kv(mem, g, wkv):
    t = mem.shape[0]
    row = lambda width: pl.BlockSpec((MEM_LEN, width), lambda i: (i, 0))
    return pl.pallas_call(
        _mem_kv_kernel,
        grid=(t // MEM_LEN,),
        in_specs=[row(D_MODEL), _const_spec((1, D_MODEL)), _const_spec((D_MODEL, 2 * XA_WIDTH))],
        out_specs=[row(XA_WIDTH), row(XA_WIDTH)],
        out_shape=[jax.ShapeDtypeStruct((t, XA_WIDTH), BF16)] * 2,
        compiler_params=_params(("parallel",), 24),
        name="mem_kv",
    )(mem, g, wkv)


def _xattn_kernel(x_ref, g_ref, wq_ref, k_ref, v_ref, wo_ref, o_ref):
    x = x_ref[...]
    h = _rms(x, g_ref[...]).astype(BF16)
    q = jnp.dot(h, wq_ref[...], preferred_element_type=F32).astype(BF16)
    k = k_ref[...]
    v = v_ref[...]
    outs = []
    for hd in range(XA_HEADS):
        sl = slice(hd * XA_HEAD_DIM, (hd + 1) * XA_HEAD_DIM)
        s = lax.dot_general(q[:, sl], k[:, sl], (((1,), (1,)), ((), ())), preferred_element_type=F32)
        s = s * (1.0 / math.sqrt(XA_HEAD_DIM))
        p = jnp.exp(s - jnp.max(s, axis=-1, keepdims=True))
        o = jnp.dot(p.astype(BF16), v[:, sl], preferred_element_type=F32)
        outs.append(o / jnp.sum(p, axis=-1, keepdims=True))
    att = jnp.concatenate(outs, axis=-1).astype(BF16)
    o_ref[...] = x + jnp.dot(att, wo_ref[...], preferred_element_type=F32)


def _xattn(x, g, wq, k, v, wo, seq):
    t = x.shape[0]
    per_seq = seq // TM
    row = pl.BlockSpec((TM, D_MODEL), lambda i: (i, 0))
    mem_blk = pl.BlockSpec((MEM_LEN, XA_WIDTH), lambda i: (i // per_seq, 0))
    return pl.pallas_call(
        _xattn_kernel,
        grid=(t // TM,),
        in_specs=[row, _const_spec((1, D_MODEL)), _const_spec((D_MODEL, XA_WIDTH)), mem_blk, mem_blk,
                  _const_spec((XA_WIDTH, D_MODEL))],
        out_specs=row,
        out_shape=jax.ShapeDtypeStruct((t, D_MODEL), F32),
        compiler_params=_params(("parallel",), 32),
        name="mem_xattn",
    )(x, g, wq, k, v, wo)


def _ffn_kernel(x_ref, g_ref, wg_ref, wu_ref, wd_ref, o_ref):
    x = x_ref[...]
    h = _rms(x, g_ref[...]).astype(BF16)
    gate = jnp.dot(h, wg_ref[...], preferred_element_type=F32)
    up = jnp.dot(h, wu_ref[...], preferred_element_type=F32)
    act = (gate * jax.nn.sigmoid(gate) * up).astype(BF16)
    o_ref[...] = x + jnp.dot(act, wd_ref[...], preferred_element_type=F32)


def _dense_ffn(x, g, wg, wu, wd):
    t = x.shape[0]
    d_ff = wg.shape[1]
    row = pl.BlockSpec((TM, D_MODEL), lambda i: (i, 0))
    return pl.pallas_call(
        _ffn_kernel,
        grid=(t // TM,),
        in_specs=[row, _const_spec((1, D_MODEL)), _const_spec((D_MODEL, d_ff)), _const_spec((D_MODEL, d_ff)),
                  _const_spec((d_ff, D_MODEL))],
        out_specs=row,
        out_shape=jax.ShapeDtypeStruct((t, D_MODEL), F32),
        compiler_params=_params(("parallel",), 60),
        name="dense_swiglu",
    )(x, g, wg, wu, wd)


META_E1, META_E2, META_R1, META_R2, META_W1, META_W2 = range(6)


def _router_kernel(x_ref, g_ref, wr_ref, tri_ref, hf_ref, meta_ref, cnt_ref, carry):
    @pl.when(pl.program_id(0) == 0)
    def _():
        carry[...] = jnp.zeros_like(carry)

    h = _rms(x_ref[...], g_ref[...])
    hf_ref[...] = h
    logits = jnp.dot(h, wr_ref[...], preferred_element_type=F32, precision=lax.Precision.HIGHEST)
    tm = logits.shape[0]
    lane = lax.broadcasted_iota(jnp.int32, (tm, LANES_V7X), 1)
    lg = jnp.where(lane < N_EXPERTS, logits, -jnp.inf)
    m1 = jnp.max(lg, axis=-1, keepdims=True)
    e1 = jnp.min(jnp.where(lg == m1, lane, LANES_V7X), axis=-1, keepdims=True)
    lg2 = jnp.where(lane == e1, -jnp.inf, lg)
    m2 = jnp.max(lg2, axis=-1, keepdims=True)
    e2 = jnp.min(jnp.where(lg2 == m2, lane, LANES_V7X), axis=-1, keepdims=True)
    ex = jnp.exp(m2 - m1)
    w1 = 1.0 / (1.0 + ex)
    w2 = ex / (1.0 + ex)

    hit1 = lane == e1
    hit2 = lane == e2
    onehot = (hit1 | hit2).astype(BF16)
    ahead = jnp.dot(tri_ref[...], onehot, preferred_element_type=F32) + carry[...]
    r1 = jnp.sum(jnp.where(hit1, ahead, 0.0), axis=-1, keepdims=True)
    r2 = jnp.sum(jnp.where(hit2, ahead, 0.0), axis=-1, keepdims=True)
    carry[...] = carry[...] + jnp.sum(onehot.astype(F32), axis=0, keepdims=True)

    meta = jnp.zeros((tm, LANES_V7X), F32)
    for col, val in ((META_E1, e1.astype(F32)), (META_E2, e2.astype(F32)), (META_R1, r1), (META_R2, r2),
                     (META_W1, w1), (META_W2, w2)):
        meta = jnp.where(lane == col, val, meta)
    meta_ref[...] = meta
    cnt_ref[...] = carry[...]


def _router(x, g, wr_pad, tri):
    t = x.shape[0]
    row = lambda width: pl.BlockSpec((TM_ROUTE, width), lambda i: (i, 0))
    return pl.pallas_call(
        _router_kernel,
        grid=(t // TM_ROUTE,),
        in_specs=[row(D_MODEL), _const_spec((1, D_MODEL)), _const_spec((D_MODEL, LANES_V7X)),
                  _const_spec((TM_ROUTE, TM_ROUTE))],
        out_specs=[row(D_MODEL), row(LANES_V7X), _const_spec((1, LANES_V7X))],
        out_shape=[jax.ShapeDtypeStruct((t, D_MODEL), F32),
                   jax.ShapeDtypeStruct((t, LANES_V7X), F32),
                   jax.ShapeDtypeStruct((1, LANES_V7X), F32)],
        scratch_shapes=[pltpu.VMEM((1, LANES_V7X), F32)],
        compiler_params=_params(("arbitrary",), 32),
        name="moe_router",
    )(x, g, wr_pad, tri)


def _row_copy(src_ref, src_row, dst_ref, dst_row, sem):
    return pltpu.make_async_copy(src_ref.at[pl.ds(src_row, 1)], dst_ref.at[pl.ds(dst_row, 1)], sem)


def _dispatch_kernel(fill_ref, pos1_ref, pos2_ref, hf_ref, xs_ref, zeros, sem_z, sem):
    @pl.when(pl.program_id(0) == 0)
    def _():
        zeros[...] = jnp.zeros_like(zeros)

        def tile_fill(e):
            start = pl.multiple_of(fill_ref[e], TM_GROUP)
            return pltpu.make_async_copy(zeros, xs_ref.at[pl.ds(start, TM_GROUP)], sem_z)

        for e in range(2 * N_EXPERTS):
            @pl.when(fill_ref[e] >= 0)
            def _():
                tile_fill(e).start()
        for e in range(2 * N_EXPERTS):
            @pl.when(fill_ref[e] >= 0)
            def _():
                tile_fill(e).wait()

    def issue(r, carry):
        _row_copy(hf_ref, r, xs_ref, pos1_ref[r], sem).start()
        _row_copy(hf_ref, r, xs_ref, pos2_ref[r], sem).start()
        return carry

    def drain(r, carry):
        _row_copy(hf_ref, 0, xs_ref, 0, sem).wait()
        _row_copy(hf_ref, 0, xs_ref, 0, sem).wait()
        return carry

    lax.fori_loop(0, TM_MOVE, issue, 0)
    lax.fori_loop(0, TM_MOVE, drain, 0)


def _dispatch(fill_start, pos1, pos2, hf, n_rows):
    t = hf.shape[0]
    idx = pl.BlockSpec((TM_MOVE,), lambda i, tail: (i,), memory_space=pltpu.SMEM)
    return pl.pallas_call(
        _dispatch_kernel,
        grid_spec=pltpu.PrefetchScalarGridSpec(
            num_scalar_prefetch=1,
            grid=(t // TM_MOVE,),
            in_specs=[idx, idx, pl.BlockSpec((TM_MOVE, D_MODEL), lambda i, tail: (i, 0))],
            out_specs=pl.BlockSpec(memory_space=pl.ANY),
            scratch_shapes=[pltpu.VMEM((TM_GROUP, D_MODEL), F32), pltpu.SemaphoreType.DMA(()),
                            pltpu.SemaphoreType.DMA(())]),
        out_shape=jax.ShapeDtypeStruct((n_rows, D_MODEL), F32),
        compiler_params=_params(("arbitrary",), 16),
        name="moe_dispatch",
    )(fill_start, pos1, pos2, hf)


def _grouped_kernel(te_ref, used_ref, x_ref, wg_ref, wu_ref, wd_ref, y_ref):
    i = pl.program_id(0)
    j = pl.program_id(1)

    @pl.when(i < used_ref[0])
    def _():
        h = x_ref[...].astype(BF16)
        gate = jnp.dot(h, wg_ref[...], preferred_element_type=F32)
        up = jnp.dot(h, wu_ref[...], preferred_element_type=F32)
        act = (gate * jax.nn.sigmoid(gate) * up).astype(BF16)
        part = jnp.dot(act, wd_ref[...], preferred_element_type=F32)

        @pl.when(j == 0)
        def _():
            y_ref[...] = part

        @pl.when(j > 0)
        def _():
            y_ref[...] += part

    @pl.when((i >= used_ref[0]) & (j == 0))
    def _():
        y_ref[...] = jnp.zeros_like(y_ref)


def _grouped_swiglu(tile_expert, n_used, xs, wg, wu, wd):
    n_rows = xs.shape[0]
    d_ff = wg.shape[2]
    ff = d_ff // FF_SPLIT
    src = lambda i, j, te, used: (jnp.minimum(i, used[0] - 1), 0)
    return pl.pallas_call(
        _grouped_kernel,
        grid_spec=pltpu.PrefetchScalarGridSpec(
            num_scalar_prefetch=2,
            grid=(n_rows // TM_GROUP, FF_SPLIT),
            in_specs=[pl.BlockSpec((TM_GROUP, D_MODEL), src),
                      pl.BlockSpec((None, D_MODEL, ff), lambda i, j, te, used: (te[i], 0, j)),
                      pl.BlockSpec((None, D_MODEL, ff), lambda i, j, te, used: (te[i], 0, j)),
                      pl.BlockSpec((None, ff, D_MODEL), lambda i, j, te, used: (te[i], j, 0))],
            out_specs=pl.BlockSpec((TM_GROUP, D_MODEL), lambda i, j, te, used: (i, 0))),
        out_shape=jax.ShapeDtypeStruct((n_rows, D_MODEL), F32),
        compiler_params=_params(("arbitrary", "arbitrary"), 56),
        name="moe_grouped_swiglu",
    )(tile_expert, n_used, xs, wg, wu, wd)


def _combine_kernel(pos1_ref, pos2_ref, x_ref, meta_ref, g_ref, y_ref, o_ref, y1, y2, sem):
    def issue(r, carry):
        _row_copy(y_ref, pos1_ref[r], y1, r, sem).start()
        _row_copy(y_ref, pos2_ref[r], y2, r, sem).start()
        return carry

    def drain(r, carry):
        _row_copy(y_ref, 0, y1, 0, sem).wait()
        _row_copy(y_ref, 0, y2, 0, sem).wait()
        return carry

    lax.fori_loop(0, TM_MOVE, issue, 0)
    lax.fori_loop(0, TM_MOVE, drain, 0)
    w1 = meta_ref[:, META_W1:META_W1 + 1]
    w2 = meta_ref[:, META_W2:META_W2 + 1]
    out = x_ref[...] + (w1 * y1[...] + w2 * y2[...])
    o_ref[...] = _rms(out, g_ref[...])


def _combine(pos1, pos2, x, meta, g, y):
    t = x.shape[0]
    idx = pl.BlockSpec((TM_MOVE,), lambda i: (i,), memory_space=pltpu.SMEM)
    row = lambda width: pl.BlockSpec((TM_MOVE, width), lambda i: (i, 0))
    return pl.pallas_call(
        _combine_kernel,
        grid=(t // TM_MOVE,),
        in_specs=[idx, idx, row(D_MODEL), row(LANES_V7X), _const_spec((1, D_MODEL)),
                  pl.BlockSpec(memory_space=pl.ANY)],
        out_specs=row(D_MODEL),
        out_shape=jax.ShapeDtypeStruct((t, D_MODEL), F32),
        scratch_shapes=[pltpu.VMEM((TM_MOVE, D_MODEL), F32), pltpu.VMEM((TM_MOVE, D_MODEL), F32),
                        pltpu.SemaphoreType.DMA(())],
        compiler_params=_params(("arbitrary",), 16),
        name="moe_combine_norm",
    )(pos1, pos2, x, meta, g, y)


def _moe_layer(x, ffn_g, router_w, wg, wu, wd, final_g):
    t = x.shape[0]
    n_rows = 2 * t + N_EXPERTS * TM_GROUP
    wr_pad = jnp.zeros((D_MODEL, LANES_V7X), F32).at[:, :N_EXPERTS].set(router_w)
    tri = jnp.tril(jnp.ones((TM_ROUTE, TM_ROUTE), BF16), -1)
    hf, meta, counts = _router(x, ffn_g, wr_pad, tri)

    cnt = counts[0, :N_EXPERTS].astype(jnp.int32)
    padded = (cnt + TM_GROUP - 1) // TM_GROUP * TM_GROUP
    ends = jnp.cumsum(padded)
    starts = ends - padded
    e1 = meta[:, META_E1].astype(jnp.int32)
    e2 = meta[:, META_E2].astype(jnp.int32)
    pos1 = starts[e1] + meta[:, META_R1].astype(jnp.int32)
    pos2 = starts[e2] + meta[:, META_R2].astype(jnp.int32)
    n_used = (ends[-1] // TM_GROUP).astype(jnp.int32).reshape(1)
    tile_row = jnp.minimum(jnp.arange(n_rows // TM_GROUP, dtype=jnp.int32), n_used[0] - 1) * TM_GROUP
    tile_expert = jnp.sum(ends[None, :] <= tile_row[:, None], axis=1).astype(jnp.int32)
    tail_start = jnp.where(padded > 0, ends - TM_GROUP, -1)
    spare = ends[-1] + TM_GROUP * jnp.arange(N_EXPERTS, dtype=jnp.int32)
    fill_start = jnp.concatenate([tail_start, jnp.where(spare < n_rows, spare, -1)]).astype(jnp.int32)

    xs = _dispatch(fill_start, pos1, pos2, hf, n_rows)
    y = _grouped_swiglu(tile_expert, n_used, xs, wg, wu, wd)
    return _combine(pos1, pos2, x, meta, final_g, y)


def _block_diag(w):
    heads, d, _ = w.shape
    eye = jnp.eye(heads, dtype=w.dtype)
    return jnp.einsum('hij,hg->higj', w, eye).reshape(heads * d, heads * d)


def kernel(x, mem, rel_bias, mix_norm, w_in, attn_sinks, sc_conv_w, sc_conv_b, rg_conv_w, rg_conv_b, rg_w_a,
           rg_b_a, rg_w_x, rg_b_x, rg_lambda, w_out, xa_norm, mem_norm, xa_wq, xa_wk, xa_wv, xa_wo, ffn_norm,
           dense_wg, dense_wu, dense_wd, moe_router, moe_wg, moe_wu, moe_wd, final_norm):
    batch, seq, _ = x.shape
    depth = w_in.shape[0]
    assert depth == 2 and seq % TS == 0 and seq % TM == 0 and (batch * seq) % TM_ROUTE == 0
    xt = x.reshape(batch * seq, D_MODEL)
    memt = mem.reshape(batch * MEM_LEN, D_MODEL)
    bias_tbl = _attention_bias_tables(rel_bias)
    vec = lambda a: a.reshape(1, -1)

    for layer in range(depth):
        q, k, v, rest = _in_proj(xt, vec(mix_norm[layer]), w_in[layer].astype(BF16))
        attn = _attention(q, k, v, attn_sinks[layer], bias_tbl, seq // BLOCK)
        w_gate = jnp.concatenate([_block_diag(rg_w_a[layer]), _block_diag(rg_w_x[layer])], axis=1).astype(BF16)
        b_gate = jnp.concatenate([rg_b_a[layer], rg_b_x[layer]]).reshape(1, -1)
        cr = _conv_rg(rest, sc_conv_w[layer], vec(sc_conv_b[layer]), rg_conv_w[layer], vec(rg_conv_b[layer]),
                      w_gate, b_gate, vec(rg_lambda[layer]), batch, seq)
        xt = _out_proj(xt, attn, cr, w_out[layer].astype(BF16))

        wkv = jnp.concatenate([xa_wk[layer], xa_wv[layer]], axis=1).astype(BF16)
        mk, mv = _mem_kv(memt, vec(mem_norm[layer]), wkv)
        xt = _xattn(xt, vec(xa_norm[layer]), xa_wq[layer].astype(BF16), mk, mv, xa_wo[layer].astype(BF16), seq)

        j = layer // 2
        if layer % 2 == 0:
            xt = _dense_ffn(xt, vec(ffn_norm[layer]), dense_wg[j].astype(BF16), dense_wu[j].astype(BF16),
                            dense_wd[j].astype(BF16))
        else:
            xt = _moe_layer(xt, vec(ffn_norm[layer]), moe_router[j], moe_wg[j].astype(BF16),
                            moe_wu[j].astype(BF16), moe_wd[j].astype(BF16), vec(final_norm))
    return xt.reshape(batch, seq, D_MODEL)
```

```python
import functools
import math

import jax
import jax.numpy as jnp
import numpy as np
from jax import lax
from jax.experimental import pallas as pl
from jax.experimental.pallas import tpu as pltpu

F32 = jnp.float32
BF16 = jnp.bfloat16

D_MODEL = 1024
MEM_LEN = 256
HEAD_DIM = 64
ATTN_Q_HEADS = 8
ATTN_KV_HEADS = 2
ATTN_WIDTH = ATTN_Q_HEADS * HEAD_DIM
KV_WIDTH = ATTN_KV_HEADS * HEAD_DIM
BLOCK = 128
SC_WIDTH = 256
SC_CONV = 3
RG_WIDTH = 256
RG_HEADS = 4
RG_HEAD_DIM = RG_WIDTH // RG_HEADS
RG_CONV = 4
RG_C = 8.0
N_BUCKETS = 32
MAX_EXACT = N_BUCKETS // 2
MAX_DISTANCE = 128
XA_HEADS = 4
XA_HEAD_DIM = 128
XA_WIDTH = XA_HEADS * XA_HEAD_DIM
N_EXPERTS = 8
EPS = 1e-6
NEG_INF = -1e30
REST_WIDTH = 3 * SC_WIDTH + 2 * RG_WIDTH

LANES_V7X = 128
SUBLANES_V7X = 8
VMEM_BYTES_V7X = 64 * 1024 * 1024
ROW_CHUNKS = D_MODEL // LANES_V7X
assert ROW_CHUNKS == SUBLANES_V7X

TM = 512
TS = 512
TM_ROUTE = 512
TM_GROUP = 512
TM_MOVE = 256
FF_SPLIT = 2
CARRY_ROWS = SUBLANES_V7X


def _mib(n):
    return int(n * 1024 * 1024)


def _params(semantics, vmem_mib):
    assert _mib(vmem_mib) < VMEM_BYTES_V7X
    return pltpu.CompilerParams(dimension_semantics=semantics, vmem_limit_bytes=_mib(vmem_mib))


def _rms(x, g):
    ms = jnp.mean(x * x, axis=-1, keepdims=True)
    return x * lax.rsqrt(ms + EPS) * g


def _const_spec(shape):
    nd = len(shape)
    return pl.BlockSpec(shape, lambda *_: (0,) * nd)


def _in_proj_kernel(x_ref, g_ref, w_ref, q_ref, k_ref, v_ref, r_ref):
    h = _rms(x_ref[...], g_ref[...]).astype(BF16)
    p = jnp.dot(h, w_ref[...], preferred_element_type=F32)
    q_ref[...] = p[:, :ATTN_WIDTH].astype(BF16)
    k_ref[...] = p[:, ATTN_WIDTH:ATTN_WIDTH + KV_WIDTH].astype(BF16)
    v_ref[...] = p[:, ATTN_WIDTH + KV_WIDTH:ATTN_WIDTH + 2 * KV_WIDTH].astype(BF16)
    r_ref[...] = p[:, ATTN_WIDTH + 2 * KV_WIDTH:]


def _in_proj(x, g, w):
    t = x.shape[0]
    n = w.shape[1]
    row = lambda width: pl.BlockSpec((TM, width), lambda i: (i, 0))
    return pl.pallas_call(
        _in_proj_kernel,
        grid=(t // TM,),
        in_specs=[row(D_MODEL), _const_spec((1, D_MODEL)), _const_spec((D_MODEL, n))],
        out_specs=[row(ATTN_WIDTH), row(KV_WIDTH), row(KV_WIDTH), row(REST_WIDTH)],
        out_shape=[jax.ShapeDtypeStruct((t, ATTN_WIDTH), BF16),
                   jax.ShapeDtypeStruct((t, KV_WIDTH), BF16),
                   jax.ShapeDtypeStruct((t, KV_WIDTH), BF16),
                   jax.ShapeDtypeStruct((t, REST_WIDTH), F32)],
        compiler_params=_params(("parallel",), 40),
        name="in_proj",
    )(x, g, w)


def _attn_kernel(sink_ref, q_ref, kp_ref, kc_ref, vp_ref, vc_ref, bias_ref, o_ref):
    q = q_ref[...]
    kband = jnp.concatenate([kp_ref[...], kc_ref[...]], axis=0)
    vband = jnp.concatenate([vp_ref[...], vc_ref[...]], axis=0)
    group = ATTN_Q_HEADS // ATTN_KV_HEADS
    outs = []
    for h in range(ATTN_Q_HEADS):
        kv = slice((h // group) * HEAD_DIM, (h // group + 1) * HEAD_DIM)
        qh = q[:, h * HEAD_DIM:(h + 1) * HEAD_DIM]
        s = lax.dot_general(qh, kband[:, kv], (((1,), (1,)), ((), ())), preferred_element_type=F32)
        logits = s * (1.0 / math.sqrt(HEAD_DIM)) + bias_ref[h]
        sink = sink_ref[h]
        m = jnp.maximum(jnp.max(logits, axis=-1, keepdims=True), sink)
        p = jnp.exp(logits - m)
        denom = jnp.sum(p, axis=-1, keepdims=True) + jnp.exp(sink - m)
        o = jnp.dot(p.astype(BF16), vband[:, kv], preferred_element_type=F32)
        outs.append(o / denom)
    o_ref[...] = jnp.concatenate(outs, axis=-1).astype(BF16)


def _attention(q, k, v, sinks, bias_tbl, blocks_per_seq):
    t = q.shape[0]
    cur = lambda i: (i, 0)
    prev = lambda i: (jnp.maximum(i - 1, 0), 0)
    return pl.pallas_call(
        _attn_kernel,
        grid=(t // BLOCK,),
        in_specs=[pl.BlockSpec(memory_space=pltpu.SMEM),
                  pl.BlockSpec((BLOCK, ATTN_WIDTH), cur),
                  pl.BlockSpec((BLOCK, KV_WIDTH), prev),
                  pl.BlockSpec((BLOCK, KV_WIDTH), cur),
                  pl.BlockSpec((BLOCK, KV_WIDTH), prev),
                  pl.BlockSpec((BLOCK, KV_WIDTH), cur),
                  pl.BlockSpec((None, ATTN_Q_HEADS, BLOCK, 2 * BLOCK),
                               lambda i: (jnp.minimum(i % blocks_per_seq, 1), 0, 0, 0))],
        out_specs=pl.BlockSpec((BLOCK, ATTN_WIDTH), cur),
        out_shape=jax.ShapeDtypeStruct((t, ATTN_WIDTH), BF16),
        compiler_params=_params(("parallel",), 24),
        name="swa_attention",
    )(sinks, q, k, k, v, v, bias_tbl)


def _attention_bias_tables(rel_bias):
    q_idx = np.arange(BLOCK)
    k_idx = np.arange(2 * BLOCK)
    dist = q_idx[:, None] + BLOCK - k_idx[None, :]
    in_window = (dist >= 0) & (dist < BLOCK)
    n = np.maximum(dist, 0)
    large = MAX_EXACT + (np.log(np.maximum(n, 1).astype(np.float32) / np.float32(MAX_EXACT))
                         / np.float32(math.log(MAX_DISTANCE / MAX_EXACT))
                         * np.float32(N_BUCKETS - MAX_EXACT)).astype(np.int32)
    bucket = np.where(n < MAX_EXACT, n, np.minimum(large, N_BUCKETS - 1))
    bias = rel_bias[jnp.asarray(bucket)].astype(F32).transpose(2, 0, 1)
    band = jnp.where(jnp.asarray(in_window)[None], bias, NEG_INF)
    first = jnp.where(jnp.asarray(in_window & (k_idx[None, :] >= BLOCK))[None], bias, NEG_INF)
    return jnp.stack([first, band])


def _shift_rows(x, s, fill):
    return jnp.concatenate([jnp.full((s, x.shape[1]), fill, x.dtype), x[:x.shape[0] - s]], axis=0)


def _conv_rg_kernel(r_ref, scw_ref, scb_ref, rgw_ref, rgb_ref, wgate_ref, bgate_ref, lam_ref,
                    o_ref, sc_ext, rg_ext, h_carry):
    ts = r_ref.shape[0]
    c0 = CARRY_ROWS

    @pl.when(pl.program_id(1) == 0)
    def _():
        sc_ext[0:c0, :] = jnp.zeros((c0, SC_WIDTH), F32)
        rg_ext[0:c0, :] = jnp.zeros((c0, RG_WIDTH), F32)
        h_carry[...] = jnp.zeros_like(h_carry)

    sc_b = r_ref[:, 0:SC_WIDTH]
    sc_ext[c0:c0 + ts, :] = r_ref[:, SC_WIDTH:2 * SC_WIDTH] * r_ref[:, 2 * SC_WIDTH:3 * SC_WIDTH]
    rg_ext[c0:c0 + ts, :] = r_ref[:, 3 * SC_WIDTH:3 * SC_WIDTH + RG_WIDTH]
    rg_g = r_ref[:, 3 * SC_WIDTH + RG_WIDTH:]

    conv = scb_ref[...]
    for k in range(SC_CONV):
        off = c0 - (SC_CONV - 1) + k
        conv = conv + scw_ref[k:k + 1, :] * sc_ext[off:off + ts, :]
    conv_out = sc_b * conv

    rg_in = rgb_ref[...]
    for k in range(RG_CONV):
        off = c0 - (RG_CONV - 1) + k
        rg_in = rg_in + rgw_ref[k:k + 1, :] * rg_ext[off:off + ts, :]

    sc_ext[0:c0, :] = sc_ext[ts:ts + c0, :]
    rg_ext[0:c0, :] = rg_ext[ts:ts + c0, :]

    gates = jnp.dot(rg_in.astype(BF16), wgate_ref[...], preferred_element_type=F32) + bgate_ref[...]
    r_gate = jax.nn.sigmoid(gates[:, :RG_WIDTH])
    i_gate = jax.nn.sigmoid(gates[:, RG_WIDTH:])
    neg_lam = -lam_ref[...]
    softplus = jnp.maximum(neg_lam, 0.0) + jnp.log1p(jnp.exp(-jnp.abs(neg_lam)))
    log_a = -RG_C * r_gate * softplus
    a = jnp.exp(log_a)
    u = jnp.sqrt(jnp.tanh(-log_a) * (1.0 + a * a)) * (i_gate * rg_in)

    s = 1
    while s < ts:
        u = a * _shift_rows(u, s, 0.0) + u
        a = a * _shift_rows(a, s, 1.0)
        s *= 2
    h = a * h_carry[...] + u
    h_carry[...] = h[ts - 1:ts, :]

    c = math.sqrt(2.0 / math.pi)
    gelu = 0.5 * rg_g * (1.0 + jnp.tanh(c * (rg_g + 0.044715 * (rg_g * rg_g * rg_g))))
    o_ref[:, 0:SC_WIDTH] = conv_out.astype(BF16)
    o_ref[:, SC_WIDTH:] = (h * gelu).astype(BF16)


def _conv_rg(rest, sc_w, sc_b, rg_w, rg_b, w_gate, b_gate, lam, batch, seq):
    t = rest.shape[0]
    steps = seq // TS
    row = lambda b, s: (b * steps + s, 0)
    return pl.pallas_call(
        _conv_rg_kernel,
        grid=(batch, steps),
        in_specs=[pl.BlockSpec((TS, REST_WIDTH), row),
                  _const_spec((SC_CONV, SC_WIDTH)), _const_spec((1, SC_WIDTH)),
                  _const_spec((RG_CONV, RG_WIDTH)), _const_spec((1, RG_WIDTH)),
                  _const_spec((RG_WIDTH, 2 * RG_WIDTH)), _const_spec((1, 2 * RG_WIDTH)),
                  _const_spec((1, RG_WIDTH))],
        out_specs=pl.BlockSpec((TS, SC_WIDTH + RG_WIDTH), row),
        out_shape=jax.ShapeDtypeStruct((t, SC_WIDTH + RG_WIDTH), BF16),
        scratch_shapes=[pltpu.VMEM((TS + 2 * CARRY_ROWS, SC_WIDTH), F32),
                        pltpu.VMEM((TS + 2 * CARRY_ROWS, RG_WIDTH), F32),
                        pltpu.VMEM((1, RG_WIDTH), F32)],
        compiler_params=_params(("arbitrary", "arbitrary"), 32),
        name="conv_rglru",
    )(rest, sc_w, sc_b, rg_w, rg_b, w_gate, b_gate, lam)


def _out_proj_kernel(x_ref, a_ref, c_ref, w_ref, o_ref):
    mixed = jnp.concatenate([a_ref[...], c_ref[...]], axis=-1)
    o_ref[...] = x_ref[...] + jnp.dot(mixed, w_ref[...], preferred_element_type=F32)


def _out_proj(x, attn, cr, w):
    t = x.shape[0]
    row = lambda width: pl.BlockSpec((TM, width), lambda i: (i, 0))
    return pl.pallas_call(
        _out_proj_kernel,
        grid=(t // TM,),
        in_specs=[row(D_MODEL), row(ATTN_WIDTH), row(SC_WIDTH + RG_WIDTH), _const_spec((D_MODEL, D_MODEL))],
        out_specs=row(D_MODEL),
        out_shape=jax.ShapeDtypeStruct((t, D_MODEL), F32),
        compiler_params=_params(("parallel",), 24),
        name="out_proj",
    )(x, attn, cr, w)


def _mem_kv_kernel(m_ref, g_ref, w_ref, k_ref, v_ref):
    h = _rms(m_ref[...], g_ref[...]).astype(BF16)
    p = jnp.dot(h, w_ref[...], preferred_element_type=F32)
    k_ref[...] = p[:, :XA_WIDTH].astype(BF16)
    v_ref[...] = p[:, XA_WIDTH:].astype(BF16)


def _mem_kv(mem, g, wkv):
    t = mem.shape[0]
    row = lambda width: pl.BlockSpec((MEM_LEN, width), lambda i: (i, 0))
    return pl.pallas_call(
        _mem_kv_kernel,
        grid=(t // MEM_LEN,),
        in_specs=[row(D_MODEL), _const_spec((1, D_MODEL)), _const_spec((D_MODEL, 2 * XA_WIDTH))],
        out_specs=[row(XA_WIDTH), row(XA_WIDTH)],
        out_shape=[jax.ShapeDtypeStruct((t, XA_WIDTH), BF16)] * 2,
        compiler_params=_params(("parallel",), 24),
        name="mem_kv",
    )(mem, g, wkv)


def _xattn_kernel(x_ref, g_ref, wq_ref, k_ref, v_ref, wo_ref, o_ref):
    x = x_ref[...]
    h = _rms(x, g_ref[...]).astype(BF16)
    q = jnp.dot(h, wq_ref[...], preferred_element_type=F32).astype(BF16)
    k = k_ref[...]
    v = v_ref[...]
    outs = []
    for hd in range(XA_HEADS):
        sl = slice(hd * XA_HEAD_DIM, (hd + 1) * XA_HEAD_DIM)
        s = lax.dot_general(q[:, sl], k[:, sl], (((1,), (1,)), ((), ())), preferred_element_type=F32)
        s = s * (1.0 / math.sqrt(XA_HEAD_DIM))
        p = jnp.exp(s - jnp.max(s, axis=-1, keepdims=True))
        o = jnp.dot(p.astype(BF16), v[:, sl], preferred_element_type=F32)
        outs.append(o / jnp.sum(p, axis=-1, keepdims=True))
    att = jnp.concatenate(outs, axis=-1).astype(BF16)
    o_ref[...] = x + jnp.dot(att, wo_ref[...], preferred_element_type=F32)


def _xattn(x, g, wq, k, v, wo, seq):
    t = x.shape[0]
    per_seq = seq // TM
    row = pl.BlockSpec((TM, D_MODEL), lambda i: (i, 0))
    mem_blk = pl.BlockSpec((MEM_LEN, XA_WIDTH), lambda i: (i // per_seq, 0))
    return pl.pallas_call(
        _xattn_kernel,
        grid=(t // TM,),
        in_specs=[row, _const_spec((1, D_MODEL)), _const_spec((D_MODEL, XA_WIDTH)), mem_blk, mem_blk,
                  _const_spec((XA_WIDTH, D_MODEL))],
        out_specs=row,
        out_shape=jax.ShapeDtypeStruct((t, D_MODEL), F32),
        compiler_params=_params(("parallel",), 32),
        name="mem_xattn",
    )(x, g, wq, k, v, wo)


def _ffn_kernel(x_ref, g_ref, wg_ref, wu_ref, wd_ref, o_ref):
    x = x_ref[...]
    h = _rms(x, g_ref[...]).astype(BF16)
    gate = jnp.dot(h, wg_ref[...], preferred_element_type=F32)
    up = jnp.dot(h, wu_ref[...], preferred_element_type=F32)
    act = (gate * jax.nn.sigmoid(gate) * up).astype(BF16)
    o_ref[...] = x + jnp.dot(act, wd_ref[...], preferred_element_type=F32)


def _dense_ffn(x, g, wg, wu, wd):
    t = x.shape[0]
    d_ff = wg.shape[1]
    row = pl.BlockSpec((TM, D_MODEL), lambda i: (i, 0))
    return pl.pallas_call(
        _ffn_kernel,
        grid=(t // TM,),
        in_specs=[row, _const_spec((1, D_MODEL)), _const_spec((D_MODEL, d_ff)), _const_spec((D_MODEL, d_ff)),
                  _const_spec((d_ff, D_MODEL))],
        out_specs=row,
        out_shape=jax.ShapeDtypeStruct((t, D_MODEL), F32),
        compiler_params=_params(("parallel",), 60),
        name="dense_swiglu",
    )(x, g, wg, wu, wd)


META_E1, META_E2, META_R1, META_R2, META_W1, META_W2 = range(6)


def _to_token_tiles(ref, rows):
    for c in range(ROW_CHUNKS):
        ref[:, c, :] = rows[:, c * LANES_V7X:(c + 1) * LANES_V7X]


def _from_token_tiles(ref):
    return jnp.concatenate([ref[:, c, :] for c in range(ROW_CHUNKS)], axis=-1)


def _router_kernel(x_ref, g_ref, wr_ref, tri_ref, sel_ref, hf_ref, meta_ref, metat_ref, cnt_ref, carry):
    @pl.when(pl.program_id(0) == 0)
    def _():
        carry[...] = jnp.zeros_like(carry)

    h = _rms(x_ref[...], g_ref[...])
    _to_token_tiles(hf_ref, h)
    logits = jnp.dot(h, wr_ref[...], preferred_element_type=F32, precision=lax.Precision.HIGHEST)
    tm = logits.shape[0]
    lane = lax.broadcasted_iota(jnp.int32, (tm, LANES_V7X), 1)
    lg = jnp.where(lane < N_EXPERTS, logits, -jnp.inf)
    m1 = jnp.max(lg, axis=-1, keepdims=True)
    e1 = jnp.min(jnp.where(lg == m1, lane, LANES_V7X), axis=-1, keepdims=True)
    lg2 = jnp.where(lane == e1, -jnp.inf, lg)
    m2 = jnp.max(lg2, axis=-1, keepdims=True)
    e2 = jnp.min(jnp.where(lg2 == m2, lane, LANES_V7X), axis=-1, keepdims=True)
    ex = jnp.exp(m2 - m1)
    w1 = 1.0 / (1.0 + ex)
    w2 = ex / (1.0 + ex)

    hit1 = lane == e1
    hit2 = lane == e2
    onehot = (hit1 | hit2).astype(BF16)
    ahead = jnp.dot(tri_ref[...], onehot, preferred_element_type=F32) + carry[...]
    r1 = jnp.sum(jnp.where(hit1, ahead, 0.0), axis=-1, keepdims=True)
    r2 = jnp.sum(jnp.where(hit2, ahead, 0.0), axis=-1, keepdims=True)
    carry[...] = carry[...] + jnp.sum(onehot.astype(F32), axis=0, keepdims=True)

    meta = jnp.zeros((tm, LANES_V7X), F32)
    for col, val in ((META_E1, e1.astype(F32)), (META_E2, e2.astype(F32)), (META_R1, r1), (META_R2, r2),
                     (META_W1, w1), (META_W2, w2)):
        meta = jnp.where(lane == col, val, meta)
    meta_ref[...] = meta
    metat_ref[...] = lax.dot_general(sel_ref[...], meta, (((1,), (1,)), ((), ())), preferred_element_type=F32,
                                     precision=lax.Precision.HIGHEST)
    cnt_ref[...] = carry[...]


def _router(x, g, wr_pad, tri, sel):
    t = x.shape[0]
    row = lambda width: pl.BlockSpec((TM_ROUTE, width), lambda i: (i, 0))
    return pl.pallas_call(
        _router_kernel,
        grid=(t // TM_ROUTE,),
        in_specs=[row(D_MODEL), _const_spec((1, D_MODEL)), _const_spec((D_MODEL, LANES_V7X)),
                  _const_spec((TM_ROUTE, TM_ROUTE)), _const_spec((SUBLANES_V7X, LANES_V7X))],
        out_specs=[pl.BlockSpec((TM_ROUTE, ROW_CHUNKS, LANES_V7X), lambda i: (i, 0, 0)), row(LANES_V7X),
                   pl.BlockSpec((SUBLANES_V7X, TM_ROUTE), lambda i: (0, i)), _const_spec((1, LANES_V7X))],
        out_shape=[jax.ShapeDtypeStruct((t, ROW_CHUNKS, LANES_V7X), F32),
                   jax.ShapeDtypeStruct((t, LANES_V7X), F32),
                   jax.ShapeDtypeStruct((SUBLANES_V7X, t), F32),
                   jax.ShapeDtypeStruct((1, LANES_V7X), F32)],
        scratch_shapes=[pltpu.VMEM((1, LANES_V7X), F32)],
        compiler_params=_params(("arbitrary",), 32),
        name="moe_router",
    )(x, g, wr_pad, tri, sel)


def _row_copy(src_ref, src_row, dst_ref, dst_row, sem):
    return pltpu.make_async_copy(src_ref.at[src_row], dst_ref.at[dst_row], sem)


def _dispatch_kernel(fill_ref, pos1_ref, pos2_ref, hf_ref, xs_ref, zeros, sem_z, sem):
    @pl.when(pl.program_id(0) == 0)
    def _():
        zeros[...] = jnp.zeros_like(zeros)

        def tile_fill(e):
            start = pl.multiple_of(fill_ref[e], TM_GROUP)
            return pltpu.make_async_copy(zeros, xs_ref.at[pl.ds(start, TM_GROUP)], sem_z)

        for e in range(2 * N_EXPERTS):
            @pl.when(fill_ref[e] >= 0)
            def _():
                tile_fill(e).start()
        for e in range(2 * N_EXPERTS):
            @pl.when(fill_ref[e] >= 0)
            def _():
                tile_fill(e).wait()

    def issue(r, carry):
        _row_copy(hf_ref, r, xs_ref, pos1_ref[r], sem).start(priority=0)
        _row_copy(hf_ref, r, xs_ref, pos2_ref[r], sem).start(priority=1)
        return carry

    def drain(r, carry):
        _row_copy(hf_ref, 0, xs_ref, 0, sem).wait()
        _row_copy(hf_ref, 0, xs_ref, 0, sem).wait()
        return carry

    lax.fori_loop(0, TM_MOVE, issue, 0)
    lax.fori_loop(0, TM_MOVE, drain, 0)


def _dispatch(fill_start, pos1, pos2, hf, n_rows):
    t = hf.shape[0]
    idx = pl.BlockSpec((TM_MOVE,), lambda i, fill: (i,), memory_space=pltpu.SMEM)
    return pl.pallas_call(
        _dispatch_kernel,
        grid_spec=pltpu.PrefetchScalarGridSpec(
            num_scalar_prefetch=1,
            grid=(t // TM_MOVE,),
            in_specs=[idx, idx, pl.BlockSpec((TM_MOVE, ROW_CHUNKS, LANES_V7X), lambda i, fill: (i, 0, 0))],
            out_specs=pl.BlockSpec(memory_space=pl.ANY),
            scratch_shapes=[pltpu.VMEM((TM_GROUP, ROW_CHUNKS, LANES_V7X), F32), pltpu.SemaphoreType.DMA(()),
                            pltpu.SemaphoreType.DMA(())]),
        out_shape=jax.ShapeDtypeStruct((n_rows, ROW_CHUNKS, LANES_V7X), F32),
        compiler_params=_params(("arbitrary",), 16),
        name="moe_dispatch",
    )(fill_start, pos1, pos2, hf)


def _grouped_kernel(te_ref, used_ref, x_ref, wg_ref, wu_ref, wd_ref, y_ref, acc):
    i = pl.program_id(0)
    j = pl.program_id(1)
    last = FF_SPLIT - 1

    @pl.when(i < used_ref[0])
    def _():
        h = _from_token_tiles(x_ref).astype(BF16)
        gate = jnp.dot(h, wg_ref[...], preferred_element_type=F32)
        up = jnp.dot(h, wu_ref[...], preferred_element_type=F32)
        act = (gate * jax.nn.sigmoid(gate) * up).astype(BF16)
        part = jnp.dot(act, wd_ref[...], preferred_element_type=F32)

        @pl.when(j == 0)
        def _():
            acc[...] = part

        @pl.when((j > 0) & (j < last))
        def _():
            acc[...] += part

        @pl.when(j == last)
        def _():
            _to_token_tiles(y_ref, part if FF_SPLIT == 1 else acc[...] + part)

    @pl.when((i >= used_ref[0]) & (j == last))
    def _():
        y_ref[...] = jnp.zeros_like(y_ref)


def _grouped_swiglu(tile_expert, n_used, xs, wg, wu, wd):
    n_rows = xs.shape[0]
    d_ff = wg.shape[2]
    ff = d_ff // FF_SPLIT
    tile = (TM_GROUP, ROW_CHUNKS, LANES_V7X)
    src = lambda i, j, te, used: (jnp.minimum(i, used[0] - 1), 0, 0)
    return pl.pallas_call(
        _grouped_kernel,
        grid_spec=pltpu.PrefetchScalarGridSpec(
            num_scalar_prefetch=2,
            grid=(n_rows // TM_GROUP, FF_SPLIT),
            in_specs=[pl.BlockSpec(tile, src),
                      pl.BlockSpec((None, D_MODEL, ff), lambda i, j, te, used: (te[i], 0, j)),
                      pl.BlockSpec((None, D_MODEL, ff), lambda i, j, te, used: (te[i], 0, j)),
                      pl.BlockSpec((None, ff, D_MODEL), lambda i, j, te, used: (te[i], j, 0))],
            out_specs=pl.BlockSpec(tile, lambda i, j, te, used: (i, 0, 0)),
            scratch_shapes=[pltpu.VMEM((TM_GROUP, D_MODEL), F32)]),
        out_shape=jax.ShapeDtypeStruct((n_rows, ROW_CHUNKS, LANES_V7X), F32),
        compiler_params=_params(("arbitrary", "arbitrary"), 56),
        name="moe_grouped_swiglu",
    )(tile_expert, n_used, xs, wg, wu, wd)


def _combine_kernel(pos1_ref, pos2_ref, x_ref, meta_ref, g_ref, y_ref, o_ref, y1, y2, sem):
    def issue(r, carry):
        _row_copy(y_ref, pos1_ref[r], y1, r, sem).start(priority=0)
        _row_copy(y_ref, pos2_ref[r], y2, r, sem).start(priority=1)
        return carry

    def drain(r, carry):
        _row_copy(y_ref, 0, y1, 0, sem).wait()
        _row_copy(y_ref, 0, y2, 0, sem).wait()
        return carry

    lax.fori_loop(0, TM_MOVE, issue, 0)
    lax.fori_loop(0, TM_MOVE, drain, 0)
    w1 = meta_ref[:, META_W1:META_W1 + 1]
    w2 = meta_ref[:, META_W2:META_W2 + 1]
    out = x_ref[...] + (w1 * _from_token_tiles(y1) + w2 * _from_token_tiles(y2))
    o_ref[...] = _rms(out, g_ref[...])


def _combine(pos1, pos2, x, meta, g, y):
    t = x.shape[0]
    idx = pl.BlockSpec((TM_MOVE,), lambda i: (i,), memory_space=pltpu.SMEM)
    row = lambda width: pl.BlockSpec((TM_MOVE, width), lambda i: (i, 0))
    return pl.pallas_call(
        _combine_kernel,
        grid=(t // TM_MOVE,),
        in_specs=[idx, idx, row(D_MODEL), row(LANES_V7X), _const_spec((1, D_MODEL)),
                  pl.BlockSpec(memory_space=pl.ANY)],
        out_specs=row(D_MODEL),
        out_shape=jax.ShapeDtypeStruct((t, D_MODEL), F32),
        scratch_shapes=[pltpu.VMEM((TM_MOVE, ROW_CHUNKS, LANES_V7X), F32),
                        pltpu.VMEM((TM_MOVE, ROW_CHUNKS, LANES_V7X), F32), pltpu.SemaphoreType.DMA(())],
        compiler_params=_params(("arbitrary",), 16),
        name="moe_combine_norm",
    )(pos1, pos2, x, meta, g, y)


def _moe_layer(x, ffn_g, router_w, wg, wu, wd, final_g):
    t = x.shape[0]
    n_rows = 2 * t + N_EXPERTS * TM_GROUP
    wr_pad = jnp.zeros((D_MODEL, LANES_V7X), F32).at[:, :N_EXPERTS].set(router_w)
    tri = jnp.tril(jnp.ones((TM_ROUTE, TM_ROUTE), BF16), -1)
    sel = jnp.eye(SUBLANES_V7X, LANES_V7X, dtype=F32)
    hf, meta, metat, counts = _router(x, ffn_g, wr_pad, tri, sel)

    cnt = counts[0, :N_EXPERTS].astype(jnp.int32)
    padded = (cnt + TM_GROUP - 1) // TM_GROUP * TM_GROUP
    ends = jnp.cumsum(padded)
    starts = ends - padded
    experts = jnp.arange(N_EXPERTS, dtype=jnp.int32)[:, None]
    group_start = lambda e: jnp.sum(jnp.where(e[None, :] == experts, starts[:, None], 0), axis=0)
    pos1 = group_start(metat[META_E1].astype(jnp.int32)) + metat[META_R1].astype(jnp.int32)
    pos2 = group_start(metat[META_E2].astype(jnp.int32)) + metat[META_R2].astype(jnp.int32)
    n_used = (ends[-1] // TM_GROUP).astype(jnp.int32).reshape(1)
    tile_row = jnp.minimum(jnp.arange(n_rows // TM_GROUP, dtype=jnp.int32), n_used[0] - 1) * TM_GROUP
    tile_expert = jnp.sum(ends[None, :] <= tile_row[:, None], axis=1).astype(jnp.int32)
    tail_start = jnp.where(padded > 0, ends - TM_GROUP, -1)
    spare = ends[-1] + TM_GROUP * jnp.arange(N_EXPERTS, dtype=jnp.int32)
    fill_start = jnp.concatenate([tail_start, jnp.where(spare < n_rows, spare, -1)]).astype(jnp.int32)

    xs = _dispatch(fill_start, pos1, pos2, hf, n_rows)
    y = _grouped_swiglu(tile_expert, n_used, xs, wg, wu, wd)
    return _combine(pos1, pos2, x, meta, final_g, y)


def _block_diag(w):
    heads, d, _ = w.shape
    eye = jnp.eye(heads, dtype=w.dtype)
    return jnp.einsum('hij,hg->higj', w, eye).reshape(heads * d, heads * d)


def kernel(x, mem, rel_bias, mix_norm, w_in, attn_sinks, sc_conv_w, sc_conv_b, rg_conv_w, rg_conv_b, rg_w_a,
           rg_b_a, rg_w_x, rg_b_x, rg_lambda, w_out, xa_norm, mem_norm, xa_wq, xa_wk, xa_wv, xa_wo, ffn_norm,
           dense_wg, dense_wu, dense_wd, moe_router, moe_wg, moe_wu, moe_wd, final_norm):
    batch, seq, _ = x.shape
    depth = w_in.shape[0]
    assert depth == 2 and seq % TS == 0 and seq % TM == 0 and (batch * seq) % TM_ROUTE == 0
    xt = x.reshape(batch * seq, D_MODEL)
    memt = mem.reshape(batch * MEM_LEN, D_MODEL)
    bias_tbl = _attention_bias_tables(rel_bias)
    vec = lambda a: a.reshape(1, -1)

    for layer in range(depth):
        q, k, v, rest = _in_proj(xt, vec(mix_norm[layer]), w_in[layer].astype(BF16))
        attn = _attention(q, k, v, attn_sinks[layer], bias_tbl, seq // BLOCK)
        w_gate = jnp.concatenate([_block_diag(rg_w_a[layer]), _block_diag(rg_w_x[layer])], axis=1).astype(BF16)
        b_gate = jnp.concatenate([rg_b_a[layer], rg_b_x[layer]]).reshape(1, -1)
        cr = _conv_rg(rest, sc_conv_w[layer], vec(sc_conv_b[layer]), rg_conv_w[layer], vec(rg_conv_b[layer]),
                      w_gate, b_gate, vec(rg_lambda[layer]), batch, seq)
        xt = _out_proj(xt, attn, cr, w_out[layer].astype(BF16))

        wkv = jnp.concatenate([xa_wk[layer], xa_wv[layer]], axis=1).astype(BF16)
        mk, mv = _mem_kv(memt, vec(mem_norm[layer]), wkv)
        xt = _xattn(xt, vec(xa_norm[layer]), xa_wq[layer].astype(BF16), mk, mv, xa_wo[layer].astype(BF16), seq)

        j = layer // 2
        if layer % 2 == 0:
            xt = _dense_ffn(xt, vec(ffn_norm[layer]), dense_wg[j].astype(BF16), dense_wu[j].astype(BF16),
                            dense_wd[j].astype(BF16))
        else:
            xt = _moe_layer(xt, vec(ffn_norm[layer]), moe_router[j], moe_wg[j].astype(BF16),
                            moe_wu[j].astype(BF16), moe_wd[j].astype(BF16), vec(final_norm))
    return xt.reshape(batch, seq, D_MODEL)
```

```python
import functools
import math

import jax
import jax.numpy as jnp
import numpy as np
from jax import lax
from jax.experimental import pallas as pl
from jax.experimental.pallas import tpu as pltpu

F32 = jnp.float32
BF16 = jnp.bfloat16

D_MODEL = 1024
MEM_LEN = 256
HEAD_DIM = 64
ATTN_Q_HEADS = 8
ATTN_KV_HEADS = 2
ATTN_WIDTH = ATTN_Q_HEADS * HEAD_DIM
KV_WIDTH = ATTN_KV_HEADS * HEAD_DIM
BLOCK = 128
SC_WIDTH = 256
SC_CONV = 3
RG_WIDTH = 256
RG_HEADS = 4
RG_HEAD_DIM = RG_WIDTH // RG_HEADS
RG_CONV = 4
RG_C = 8.0
N_BUCKETS = 32
MAX_EXACT = N_BUCKETS // 2
MAX_DISTANCE = 128
XA_HEADS = 4
XA_HEAD_DIM = 128
XA_WIDTH = XA_HEADS * XA_HEAD_DIM
N_EXPERTS = 8
EPS = 1e-6
NEG_INF = -1e30
REST_WIDTH = 3 * SC_WIDTH + 2 * RG_WIDTH

LANES_V7X = 128
SUBLANES_V7X = 8
VMEM_BYTES_V7X = 64 * 1024 * 1024
ROW_CHUNKS = D_MODEL // LANES_V7X
assert ROW_CHUNKS == SUBLANES_V7X

TM = 512
TS = 512
TM_ROUTE = 512
TM_GROUP = 512
TM_MOVE = 2048
TM_COMBINE = 256
ISSUE_UNROLL = 8
FF_SPLIT = 2
CARRY_ROWS = SUBLANES_V7X


def _mib(n):
    return int(n * 1024 * 1024)


def _params(semantics, vmem_mib):
    assert _mib(vmem_mib) < VMEM_BYTES_V7X
    return pltpu.CompilerParams(dimension_semantics=semantics, vmem_limit_bytes=_mib(vmem_mib))


def _rms(x, g):
    ms = jnp.mean(x * x, axis=-1, keepdims=True)
    return x * lax.rsqrt(ms + EPS) * g


def _const_spec(shape):
    nd = len(shape)
    return pl.BlockSpec(shape, lambda *_: (0,) * nd)


def _in_proj_kernel(x_ref, g_ref, w_ref, q_ref, k_ref, v_ref, r_ref):
    h = _rms(x_ref[...], g_ref[...]).astype(BF16)
    p = jnp.dot(h, w_ref[...], preferred_element_type=F32)
    q_ref[...] = p[:, :ATTN_WIDTH].astype(BF16)
    k_ref[...] = p[:, ATTN_WIDTH:ATTN_WIDTH + KV_WIDTH].astype(BF16)
    v_ref[...] = p[:, ATTN_WIDTH + KV_WIDTH:ATTN_WIDTH + 2 * KV_WIDTH].astype(BF16)
    r_ref[...] = p[:, ATTN_WIDTH + 2 * KV_WIDTH:]


def _in_proj(x, g, w):
    t = x.shape[0]
    n = w.shape[1]
    row = lambda width: pl.BlockSpec((TM, width), lambda i: (i, 0))
    return pl.pallas_call(
        _in_proj_kernel,
        grid=(t // TM,),
        in_specs=[row(D_MODEL), _const_spec((1, D_MODEL)), _const_spec((D_MODEL, n))],
        out_specs=[row(ATTN_WIDTH), row(KV_WIDTH), row(KV_WIDTH), row(REST_WIDTH)],
        out_shape=[jax.ShapeDtypeStruct((t, ATTN_WIDTH), BF16),
                   jax.ShapeDtypeStruct((t, KV_WIDTH), BF16),
                   jax.ShapeDtypeStruct((t, KV_WIDTH), BF16),
                   jax.ShapeDtypeStruct((t, REST_WIDTH), F32)],
        compiler_params=_params(("parallel",), 40),
        name="in_proj",
    )(x, g, w)


def _attn_kernel(sink_ref, q_ref, kp_ref, kc_ref, vp_ref, vc_ref, bias_ref, o_ref):
    q = q_ref[...]
    kband = jnp.concatenate([kp_ref[...], kc_ref[...]], axis=0)
    vband = jnp.concatenate([vp_ref[...], vc_ref[...]], axis=0)
    group = ATTN_Q_HEADS // ATTN_KV_HEADS
    outs = []
    for h in range(ATTN_Q_HEADS):
        kv = slice((h // group) * HEAD_DIM, (h // group + 1) * HEAD_DIM)
        qh = q[:, h * HEAD_DIM:(h + 1) * HEAD_DIM]
        s = lax.dot_general(qh, kband[:, kv], (((1,), (1,)), ((), ())), preferred_element_type=F32)
        logits = s * (1.0 / math.sqrt(HEAD_DIM)) + bias_ref[h]
        sink = sink_ref[h]
        m = jnp.maximum(jnp.max(logits, axis=-1, keepdims=True), sink)
        p = jnp.exp(logits - m)
        denom = jnp.sum(p, axis=-1, keepdims=True) + jnp.exp(sink - m)
        o = jnp.dot(p.astype(BF16), vband[:, kv], preferred_element_type=F32)
        outs.append(o / denom)
    o_ref[...] = jnp.concatenate(outs, axis=-1).astype(BF16)


def _attention(q, k, v, sinks, bias_tbl, blocks_per_seq):
    t = q.shape[0]
    cur = lambda i: (i, 0)
    prev = lambda i: (jnp.maximum(i - 1, 0), 0)
    return pl.pallas_call(
        _attn_kernel,
        grid=(t // BLOCK,),
        in_specs=[pl.BlockSpec(memory_space=pltpu.SMEM),
                  pl.BlockSpec((BLOCK, ATTN_WIDTH), cur),
                  pl.BlockSpec((BLOCK, KV_WIDTH), prev),
                  pl.BlockSpec((BLOCK, KV_WIDTH), cur),
                  pl.BlockSpec((BLOCK, KV_WIDTH), prev),
                  pl.BlockSpec((BLOCK, KV_WIDTH), cur),
                  pl.BlockSpec((None, ATTN_Q_HEADS, BLOCK, 2 * BLOCK),
                               lambda i: (jnp.minimum(i % blocks_per_seq, 1), 0, 0, 0))],
        out_specs=pl.BlockSpec((BLOCK, ATTN_WIDTH), cur),
        out_shape=jax.ShapeDtypeStruct((t, ATTN_WIDTH), BF16),
        compiler_params=_params(("parallel",), 24),
        name="swa_attention",
    )(sinks, q, k, k, v, v, bias_tbl)


def _attention_bias_tables(rel_bias):
    q_idx = np.arange(BLOCK)
    k_idx = np.arange(2 * BLOCK)
    dist = q_idx[:, None] + BLOCK - k_idx[None, :]
    in_window = (dist >= 0) & (dist < BLOCK)
    n = np.maximum(dist, 0)
    large = MAX_EXACT + (np.log(np.maximum(n, 1).astype(np.float32) / np.float32(MAX_EXACT))
                         / np.float32(math.log(MAX_DISTANCE / MAX_EXACT))
                         * np.float32(N_BUCKETS - MAX_EXACT)).astype(np.int32)
    bucket = np.where(n < MAX_EXACT, n, np.minimum(large, N_BUCKETS - 1))
    bias = rel_bias[jnp.asarray(bucket)].astype(F32).transpose(2, 0, 1)
    band = jnp.where(jnp.asarray(in_window)[None], bias, NEG_INF)
    first = jnp.where(jnp.asarray(in_window & (k_idx[None, :] >= BLOCK))[None], bias, NEG_INF)
    return jnp.stack([first, band])


def _shift_rows(x, s, fill):
    return jnp.concatenate([jnp.full((s, x.shape[1]), fill, x.dtype), x[:x.shape[0] - s]], axis=0)


def _conv_rg_kernel(r_ref, scw_ref, scb_ref, rgw_ref, rgb_ref, wgate_ref, bgate_ref, lam_ref,
                    o_ref, sc_ext, rg_ext, h_carry):
    ts = r_ref.shape[0]
    c0 = CARRY_ROWS

    @pl.when(pl.program_id(1) == 0)
    def _():
        sc_ext[0:c0, :] = jnp.zeros((c0, SC_WIDTH), F32)
        rg_ext[0:c0, :] = jnp.zeros((c0, RG_WIDTH), F32)
        h_carry[...] = jnp.zeros_like(h_carry)

    sc_b = r_ref[:, 0:SC_WIDTH]
    sc_ext[c0:c0 + ts, :] = r_ref[:, SC_WIDTH:2 * SC_WIDTH] * r_ref[:, 2 * SC_WIDTH:3 * SC_WIDTH]
    rg_ext[c0:c0 + ts, :] = r_ref[:, 3 * SC_WIDTH:3 * SC_WIDTH + RG_WIDTH]
    rg_g = r_ref[:, 3 * SC_WIDTH + RG_WIDTH:]

    conv = scb_ref[...]
    for k in range(SC_CONV):
        off = c0 - (SC_CONV - 1) + k
        conv = conv + scw_ref[k:k + 1, :] * sc_ext[off:off + ts, :]
    conv_out = sc_b * conv

    rg_in = rgb_ref[...]
    for k in range(RG_CONV):
        off = c0 - (RG_CONV - 1) + k
        rg_in = rg_in + rgw_ref[k:k + 1, :] * rg_ext[off:off + ts, :]

    sc_ext[0:c0, :] = sc_ext[ts:ts + c0, :]
    rg_ext[0:c0, :] = rg_ext[ts:ts + c0, :]

    gates = jnp.dot(rg_in.astype(BF16), wgate_ref[...], preferred_element_type=F32) + bgate_ref[...]
    r_gate = jax.nn.sigmoid(gates[:, :RG_WIDTH])
    i_gate = jax.nn.sigmoid(gates[:, RG_WIDTH:])
    neg_lam = -lam_ref[...]
    softplus = jnp.maximum(neg_lam, 0.0) + jnp.log1p(jnp.exp(-jnp.abs(neg_lam)))
    log_a = -RG_C * r_gate * softplus
    a = jnp.exp(log_a)
    u = jnp.sqrt(jnp.tanh(-log_a) * (1.0 + a * a)) * (i_gate * rg_in)

    s = 1
    while s < ts:
        u = a * _shift_rows(u, s, 0.0) + u
        a = a * _shift_rows(a, s, 1.0)
        s *= 2
    h = a * h_carry[...] + u
    h_carry[...] = h[ts - 1:ts, :]

    c = math.sqrt(2.0 / math.pi)
    gelu = 0.5 * rg_g * (1.0 + jnp.tanh(c * (rg_g + 0.044715 * (rg_g * rg_g * rg_g))))
    o_ref[:, 0:SC_WIDTH] = conv_out.astype(BF16)
    o_ref[:, SC_WIDTH:] = (h * gelu).astype(BF16)


def _conv_rg(rest, sc_w, sc_b, rg_w, rg_b, w_gate, b_gate, lam, batch, seq):
    t = rest.shape[0]
    steps = seq // TS
    row = lambda b, s: (b * steps + s, 0)
    return pl.pallas_call(
        _conv_rg_kernel,
        grid=(batch, steps),
        in_specs=[pl.BlockSpec((TS, REST_WIDTH), row),
                  _const_spec((SC_CONV, SC_WIDTH)), _const_spec((1, SC_WIDTH)),
                  _const_spec((RG_CONV, RG_WIDTH)), _const_spec((1, RG_WIDTH)),
                  _const_spec((RG_WIDTH, 2 * RG_WIDTH)), _const_spec((1, 2 * RG_WIDTH)),
                  _const_spec((1, RG_WIDTH))],
        out_specs=pl.BlockSpec((TS, SC_WIDTH + RG_WIDTH), row),
        out_shape=jax.ShapeDtypeStruct((t, SC_WIDTH + RG_WIDTH), BF16),
        scratch_shapes=[pltpu.VMEM((TS + 2 * CARRY_ROWS, SC_WIDTH), F32),
                        pltpu.VMEM((TS + 2 * CARRY_ROWS, RG_WIDTH), F32),
                        pltpu.VMEM((1, RG_WIDTH), F32)],
        compiler_params=_params(("arbitrary", "arbitrary"), 32),
        name="conv_rglru",
    )(rest, sc_w, sc_b, rg_w, rg_b, w_gate, b_gate, lam)


def _out_proj_kernel(x_ref, a_ref, c_ref, w_ref, o_ref):
    mixed = jnp.concatenate([a_ref[...], c_ref[...]], axis=-1)
    o_ref[...] = x_ref[...] + jnp.dot(mixed, w_ref[...], preferred_element_type=F32)


def _out_proj(x, attn, cr, w):
    t = x.shape[0]
    row = lambda width: pl.BlockSpec((TM, width), lambda i: (i, 0))
    return pl.pallas_call(
        _out_proj_kernel,
        grid=(t // TM,),
        in_specs=[row(D_MODEL), row(ATTN_WIDTH), row(SC_WIDTH + RG_WIDTH), _const_spec((D_MODEL, D_MODEL))],
        out_specs=row(D_MODEL),
        out_shape=jax.ShapeDtypeStruct((t, D_MODEL), F32),
        compiler_params=_params(("parallel",), 24),
        name="out_proj",
    )(x, attn, cr, w)


def _mem_kv_kernel(m_ref, g_ref, w_ref, k_ref, v_ref):
    h = _rms(m_ref[...], g_ref[...]).astype(BF16)
    p = jnp.dot(h, w_ref[...], preferred_element_type=F32)
    k_ref[...] = p[:, :XA_WIDTH].astype(BF16)
    v_ref[...] = p[:, XA_WIDTH:].astype(BF16)


def _mem_kv(mem, g, wkv):
    t = mem.shape[0]
    row = lambda width: pl.BlockSpec((MEM_LEN, width), lambda i: (i, 0))
    return pl.pallas_call(
        _mem_kv_kernel,
        grid=(t // MEM_LEN,),
        in_specs=[row(D_MODEL), _const_spec((1, D_MODEL)), _const_spec((D_MODEL, 2 * XA_WIDTH))],
        out_specs=[row(XA_WIDTH), row(XA_WIDTH)],
        out_shape=[jax.ShapeDtypeStruct((t, XA_WIDTH), BF16)] * 2,
        compiler_params=_params(("parallel",), 24),
        name="mem_kv",
    )(mem, g, wkv)


def _xattn_kernel(x_ref, g_ref, wq_ref, k_ref, v_ref, wo_ref, o_ref):
    x = x_ref[...]
    h = _rms(x, g_ref[...]).astype(BF16)
    q = jnp.dot(h, wq_ref[...], preferred_element_type=F32).astype(BF16)
    k = k_ref[...]
    v = v_ref[...]
    outs = []
    for hd in range(XA_HEADS):
        sl = slice(hd * XA_HEAD_DIM, (hd + 1) * XA_HEAD_DIM)
        s = lax.dot_general(q[:, sl], k[:, sl], (((1,), (1,)), ((), ())), preferred_element_type=F32)
        s = s * (1.0 / math.sqrt(XA_HEAD_DIM))
        p = jnp.exp(s - jnp.max(s, axis=-1, keepdims=True))
        o = jnp.dot(p.astype(BF16), v[:, sl], preferred_element_type=F32)
        outs.append(o / jnp.sum(p, axis=-1, keepdims=True))
    att = jnp.concatenate(outs, axis=-1).astype(BF16)
    o_ref[...] = x + jnp.dot(att, wo_ref[...], preferred_element_type=F32)


def _xattn(x, g, wq, k, v, wo, seq):
    t = x.shape[0]
    per_seq = seq // TM
    row = pl.BlockSpec((TM, D_MODEL), lambda i: (i, 0))
    mem_blk = pl.BlockSpec((MEM_LEN, XA_WIDTH), lambda i: (i // per_seq, 0))
    return pl.pallas_call(
        _xattn_kernel,
        grid=(t // TM,),
        in_specs=[row, _const_spec((1, D_MODEL)), _const_spec((D_MODEL, XA_WIDTH)), mem_blk, mem_blk,
                  _const_spec((XA_WIDTH, D_MODEL))],
        out_specs=row,
        out_shape=jax.ShapeDtypeStruct((t, D_MODEL), F32),
        compiler_params=_params(("parallel",), 32),
        name="mem_xattn",
    )(x, g, wq, k, v, wo)


def _ffn_kernel(x_ref, g_ref, wg_ref, wu_ref, wd_ref, o_ref):
    x = x_ref[...]
    h = _rms(x, g_ref[...]).astype(BF16)
    gate = jnp.dot(h, wg_ref[...], preferred_element_type=F32)
    up = jnp.dot(h, wu_ref[...], preferred_element_type=F32)
    act = (gate * jax.nn.sigmoid(gate) * up).astype(BF16)
    o_ref[...] = x + jnp.dot(act, wd_ref[...], preferred_element_type=F32)


def _dense_ffn(x, g, wg, wu, wd):
    t = x.shape[0]
    d_ff = wg.shape[1]
    row = pl.BlockSpec((TM, D_MODEL), lambda i: (i, 0))
    return pl.pallas_call(
        _ffn_kernel,
        grid=(t // TM,),
        in_specs=[row, _const_spec((1, D_MODEL)), _const_spec((D_MODEL, d_ff)), _const_spec((D_MODEL, d_ff)),
                  _const_spec((d_ff, D_MODEL))],
        out_specs=row,
        out_shape=jax.ShapeDtypeStruct((t, D_MODEL), F32),
        compiler_params=_params(("parallel",), 60),
        name="dense_swiglu",
    )(x, g, wg, wu, wd)


META_E1, META_E2, META_R1, META_R2, META_W1, META_W2 = range(6)


def _to_token_tiles(ref, rows):
    m = rows.shape[0]
    for c in range(ROW_CHUNKS):
        ref[pl.ds(c, m, stride=ROW_CHUNKS), :] = rows[:, c * LANES_V7X:(c + 1) * LANES_V7X]


def _from_token_tiles(ref):
    m = ref.shape[0] // ROW_CHUNKS
    return jnp.concatenate([ref[pl.ds(c, m, stride=ROW_CHUNKS), :] for c in range(ROW_CHUNKS)], axis=-1)


def _router_kernel(x_ref, g_ref, wr_ref, tri_ref, sel_ref, hf_ref, meta_ref, metat_ref, cnt_ref, carry):
    @pl.when(pl.program_id(0) == 0)
    def _():
        carry[...] = jnp.zeros_like(carry)

    h = _rms(x_ref[...], g_ref[...])
    _to_token_tiles(hf_ref, h)
    logits = jnp.dot(h, wr_ref[...], preferred_element_type=F32, precision=lax.Precision.HIGHEST)
    tm = logits.shape[0]
    lane = lax.broadcasted_iota(jnp.int32, (tm, LANES_V7X), 1)
    lg = jnp.where(lane < N_EXPERTS, logits, -jnp.inf)
    m1 = jnp.max(lg, axis=-1, keepdims=True)
    e1 = jnp.min(jnp.where(lg == m1, lane, LANES_V7X), axis=-1, keepdims=True)
    lg2 = jnp.where(lane == e1, -jnp.inf, lg)
    m2 = jnp.max(lg2, axis=-1, keepdims=True)
    e2 = jnp.min(jnp.where(lg2 == m2, lane, LANES_V7X), axis=-1, keepdims=True)
    ex = jnp.exp(m2 - m1)
    w1 = 1.0 / (1.0 + ex)
    w2 = ex / (1.0 + ex)

    hit1 = lane == e1
    hit2 = lane == e2
    onehot = (hit1 | hit2).astype(BF16)
    ahead = jnp.dot(tri_ref[...], onehot, preferred_element_type=F32) + carry[...]
    r1 = jnp.sum(jnp.where(hit1, ahead, 0.0), axis=-1, keepdims=True)
    r2 = jnp.sum(jnp.where(hit2, ahead, 0.0), axis=-1, keepdims=True)
    carry[...] = carry[...] + jnp.sum(onehot.astype(F32), axis=0, keepdims=True)

    meta = jnp.zeros((tm, LANES_V7X), F32)
    for col, val in ((META_E1, e1.astype(F32)), (META_E2, e2.astype(F32)), (META_R1, r1), (META_R2, r2),
                     (META_W1, w1), (META_W2, w2)):
        meta = jnp.where(lane == col, val, meta)
    meta_ref[...] = meta
    metat_ref[...] = lax.dot_general(sel_ref[...], meta, (((1,), (1,)), ((), ())), preferred_element_type=F32,
                                     precision=lax.Precision.HIGHEST)
    cnt_ref[...] = carry[...]


def _router(x, g, wr_pad, tri, sel):
    t = x.shape[0]
    row = lambda width: pl.BlockSpec((TM_ROUTE, width), lambda i: (i, 0))
    return pl.pallas_call(
        _router_kernel,
        grid=(t // TM_ROUTE,),
        in_specs=[row(D_MODEL), _const_spec((1, D_MODEL)), _const_spec((D_MODEL, LANES_V7X)),
                  _const_spec((TM_ROUTE, TM_ROUTE)), _const_spec((SUBLANES_V7X, LANES_V7X))],
        out_specs=[pl.BlockSpec((TM_ROUTE * ROW_CHUNKS, LANES_V7X), lambda i: (i, 0)), row(LANES_V7X),
                   pl.BlockSpec((SUBLANES_V7X, TM_ROUTE), lambda i: (0, i)), _const_spec((1, LANES_V7X))],
        out_shape=[jax.ShapeDtypeStruct((t * ROW_CHUNKS, LANES_V7X), F32),
                   jax.ShapeDtypeStruct((t, LANES_V7X), F32),
                   jax.ShapeDtypeStruct((SUBLANES_V7X, t), F32),
                   jax.ShapeDtypeStruct((1, LANES_V7X), F32)],
        scratch_shapes=[pltpu.VMEM((1, LANES_V7X), F32)],
        compiler_params=_params(("arbitrary",), 32),
        name="moe_router",
    )(x, g, wr_pad, tri, sel)


def _token_rows(ref, first_token, n_tokens):
    start = pl.multiple_of(first_token * ROW_CHUNKS, ROW_CHUNKS)
    return ref.at[pl.ds(start, n_tokens * ROW_CHUNKS)]


def _row_copy(src_ref, src_token, dst_ref, dst_token, sem):
    return pltpu.make_async_copy(_token_rows(src_ref, src_token, 1), _token_rows(dst_ref, dst_token, 1), sem)


def _dispatch_kernel(fill_ref, pos1_ref, pos2_ref, hf_ref, xs_ref, zeros, sem_z, sem):
    @pl.when(pl.program_id(0) == 0)
    def _():
        zeros[...] = jnp.zeros_like(zeros)

        def tile_fill(e):
            return pltpu.make_async_copy(zeros, _token_rows(xs_ref, pl.multiple_of(fill_ref[e], TM_GROUP), TM_GROUP),
                                         sem_z)

        for e in range(2 * N_EXPERTS):
            @pl.when(fill_ref[e] >= 0)
            def _():
                tile_fill(e).start()
        for e in range(2 * N_EXPERTS):
            @pl.when(fill_ref[e] >= 0)
            def _():
                tile_fill(e).wait()

    base = pl.program_id(0) * TM_MOVE

    def issue(r, carry):
        _row_copy(hf_ref, base + r, xs_ref, pos1_ref[r], sem).start(priority=0)
        _row_copy(hf_ref, base + r, xs_ref, pos2_ref[r], sem).start(priority=1)
        return carry

    lax.fori_loop(0, TM_MOVE, issue, 0, unroll=ISSUE_UNROLL)
    whole_step = pltpu.make_async_copy(_token_rows(hf_ref, 0, TM_MOVE), _token_rows(xs_ref, 0, TM_MOVE), sem)
    whole_step.wait()
    whole_step.wait()


def _dispatch(fill_start, pos1, pos2, hf, n_rows):
    t = hf.shape[0] // ROW_CHUNKS
    idx = pl.BlockSpec((TM_MOVE,), lambda i, fill: (i,), memory_space=pltpu.SMEM)
    return pl.pallas_call(
        _dispatch_kernel,
        grid_spec=pltpu.PrefetchScalarGridSpec(
            num_scalar_prefetch=1,
            grid=(t // TM_MOVE,),
            in_specs=[idx, idx, pl.BlockSpec(memory_space=pl.ANY)],
            out_specs=pl.BlockSpec(memory_space=pl.ANY),
            scratch_shapes=[pltpu.VMEM((TM_GROUP * ROW_CHUNKS, LANES_V7X), F32), pltpu.SemaphoreType.DMA(()),
                            pltpu.SemaphoreType.DMA(())]),
        out_shape=jax.ShapeDtypeStruct((n_rows * ROW_CHUNKS, LANES_V7X), F32),
        compiler_params=_params(("arbitrary",), 16),
        name="moe_dispatch",
    )(fill_start, pos1, pos2, hf)


def _grouped_kernel(te_ref, used_ref, x_ref, wg_ref, wu_ref, wd_ref, y_ref, acc):
    i = pl.program_id(0)
    j = pl.program_id(1)
    last = FF_SPLIT - 1

    @pl.when(i < used_ref[0])
    def _():
        h = _from_token_tiles(x_ref).astype(BF16)
        gate = jnp.dot(h, wg_ref[...], preferred_element_type=F32)
        up = jnp.dot(h, wu_ref[...], preferred_element_type=F32)
        act = (gate * jax.nn.sigmoid(gate) * up).astype(BF16)
        part = jnp.dot(act, wd_ref[...], preferred_element_type=F32)

        @pl.when(j == 0)
        def _():
            acc[...] = part

        @pl.when((j > 0) & (j < last))
        def _():
            acc[...] += part

        @pl.when(j == last)
        def _():
            _to_token_tiles(y_ref, part if FF_SPLIT == 1 else acc[...] + part)

    @pl.when((i >= used_ref[0]) & (j == last))
    def _():
        y_ref[...] = jnp.zeros_like(y_ref)


def _grouped_swiglu(tile_expert, n_used, xs, wg, wu, wd):
    n_rows = xs.shape[0] // ROW_CHUNKS
    d_ff = wg.shape[2]
    ff = d_ff // FF_SPLIT
    tile = (TM_GROUP * ROW_CHUNKS, LANES_V7X)
    src = lambda i, j, te, used: (jnp.maximum(jnp.minimum(i, used[0] - 1), 0), 0)
    return pl.pallas_call(
        _grouped_kernel,
        grid_spec=pltpu.PrefetchScalarGridSpec(
            num_scalar_prefetch=2,
            grid=(n_rows // TM_GROUP, FF_SPLIT),
            in_specs=[pl.BlockSpec(tile, src),
                      pl.BlockSpec((None, D_MODEL, ff), lambda i, j, te, used: (te[i], 0, j)),
                      pl.BlockSpec((None, D_MODEL, ff), lambda i, j, te, used: (te[i], 0, j)),
                      pl.BlockSpec((None, ff, D_MODEL), lambda i, j, te, used: (te[i], j, 0))],
            out_specs=pl.BlockSpec(tile, lambda i, j, te, used: (i, 0)),
            scratch_shapes=[pltpu.VMEM((TM_GROUP, D_MODEL), F32)]),
        out_shape=jax.ShapeDtypeStruct((n_rows * ROW_CHUNKS, LANES_V7X), F32),
        compiler_params=_params(("arbitrary", "arbitrary"), 56),
        name="moe_grouped_swiglu",
    )(tile_expert, n_used, xs, wg, wu, wd)


def _combine_kernel(pos1_ref, pos2_ref, pos1_next_ref, pos2_next_ref, x_ref, meta_ref, g_ref, y_ref, o_ref,
                    y1, y2, sems):
    i = pl.program_id(0)

    def gather(p1_ref, p2_ref, slot):
        def issue(r, carry):
            _row_copy(y_ref, p1_ref[r], y1.at[slot], r, sems.at[slot]).start(priority=0)
            _row_copy(y_ref, p2_ref[r], y2.at[slot], r, sems.at[slot]).start(priority=1)
            return carry
        lax.fori_loop(0, TM_COMBINE, issue, 0, unroll=ISSUE_UNROLL)

    def finish(slot):
        for buf in (y1, y2):
            pltpu.make_async_copy(_token_rows(y_ref, 0, TM_COMBINE), buf.at[slot], sems.at[slot]).wait()
        w1 = meta_ref[:, META_W1:META_W1 + 1]
        w2 = meta_ref[:, META_W2:META_W2 + 1]
        out = x_ref[...] + (w1 * _from_token_tiles(y1.at[slot]) + w2 * _from_token_tiles(y2.at[slot]))
        o_ref[...] = _rms(out, g_ref[...])

    @pl.when(i == 0)
    def _():
        gather(pos1_ref, pos2_ref, 0)

    for slot in range(2):
        @pl.when(i % 2 == slot)
        def _():
            @pl.when(i + 1 < pl.num_programs(0))
            def _():
                gather(pos1_next_ref, pos2_next_ref, 1 - slot)
            finish(slot)


def _combine(pos1, pos2, x, meta, g, y):
    t = x.shape[0]
    steps = t // TM_COMBINE
    idx = pl.BlockSpec((TM_COMBINE,), lambda i: (i,), memory_space=pltpu.SMEM)
    idx_next = pl.BlockSpec((TM_COMBINE,), lambda i: (jnp.minimum(i + 1, steps - 1),), memory_space=pltpu.SMEM)
    row = lambda width: pl.BlockSpec((TM_COMBINE, width), lambda i: (i, 0))
    slots = pltpu.VMEM((2, TM_COMBINE * ROW_CHUNKS, LANES_V7X), F32)
    return pl.pallas_call(
        _combine_kernel,
        grid=(steps,),
        in_specs=[idx, idx, idx_next, idx_next, row(D_MODEL), row(LANES_V7X), _const_spec((1, D_MODEL)),
                  pl.BlockSpec(memory_space=pl.ANY)],
        out_specs=row(D_MODEL),
        out_shape=jax.ShapeDtypeStruct((t, D_MODEL), F32),
        scratch_shapes=[slots, slots, pltpu.SemaphoreType.DMA((2,))],
        compiler_params=_params(("arbitrary",), 24),
        name="moe_combine_norm",
    )(pos1, pos2, pos1, pos2, x, meta, g, y)


def _moe_layer(x, ffn_g, router_w, wg, wu, wd, final_g):
    t = x.shape[0]
    n_rows = 2 * t + N_EXPERTS * TM_GROUP
    wr_pad = jnp.zeros((D_MODEL, LANES_V7X), F32).at[:, :N_EXPERTS].set(router_w)
    tri = jnp.tril(jnp.ones((TM_ROUTE, TM_ROUTE), BF16), -1)
    sel = jnp.eye(SUBLANES_V7X, LANES_V7X, dtype=F32)
    hf, meta, metat, counts = _router(x, ffn_g, wr_pad, tri, sel)

    cnt = counts[0, :N_EXPERTS].astype(jnp.int32)
    padded = (cnt + TM_GROUP - 1) // TM_GROUP * TM_GROUP
    ends = jnp.cumsum(padded)
    starts = ends - padded
    experts = jnp.arange(N_EXPERTS, dtype=jnp.int32)[:, None]
    group_start = lambda e: jnp.sum(jnp.where(e[None, :] == experts, starts[:, None], 0), axis=0)
    pos1 = group_start(metat[META_E1].astype(jnp.int32)) + metat[META_R1].astype(jnp.int32)
    pos2 = group_start(metat[META_E2].astype(jnp.int32)) + metat[META_R2].astype(jnp.int32)
    n_used = (ends[-1] // TM_GROUP).astype(jnp.int32).reshape(1)
    tile_row = jnp.minimum(jnp.arange(n_rows // TM_GROUP, dtype=jnp.int32), n_used[0] - 1) * TM_GROUP
    tile_expert = jnp.sum(ends[None, :] <= tile_row[:, None], axis=1).astype(jnp.int32)
    tail_start = jnp.where(padded > 0, ends - TM_GROUP, -1)
    spare = ends[-1] + TM_GROUP * jnp.arange(N_EXPERTS, dtype=jnp.int32)
    fill_start = jnp.concatenate([tail_start, jnp.where(spare < n_rows, spare, -1)]).astype(jnp.int32)

    xs = _dispatch(fill_start, pos1, pos2, hf, n_rows)
    y = _grouped_swiglu(tile_expert, n_used, xs, wg, wu, wd)
    return _combine(pos1, pos2, x, meta, final_g, y)


def _block_diag(w):
    heads, d, _ = w.shape
    eye = jnp.eye(heads, dtype=w.dtype)
    return jnp.einsum('hij,hg->higj', w, eye).reshape(heads * d, heads * d)


def kernel(x, mem, rel_bias, mix_norm, w_in, attn_sinks, sc_conv_w, sc_conv_b, rg_conv_w, rg_conv_b, rg_w_a,
           rg_b_a, rg_w_x, rg_b_x, rg_lambda, w_out, xa_norm, mem_norm, xa_wq, xa_wk, xa_wv, xa_wo, ffn_norm,
           dense_wg, dense_wu, dense_wd, moe_router, moe_wg, moe_wu, moe_wd, final_norm):
    batch, seq, _ = x.shape
    depth = w_in.shape[0]
    assert depth == 2 and seq % TS == 0 and seq % TM == 0 and (batch * seq) % TM_ROUTE == 0
    xt = x.reshape(batch * seq, D_MODEL)
    memt = mem.reshape(batch * MEM_LEN, D_MODEL)
    bias_tbl = _attention_bias_tables(rel_bias)
    vec = lambda a: a.reshape(1, -1)

    for layer in range(depth):
        q, k, v, rest = _in_proj(xt, vec(mix_norm[layer]), w_in[layer].astype(BF16))
        attn = _attention(q, k, v, attn_sinks[layer], bias_tbl, seq // BLOCK)
        w_gate = jnp.concatenate([_block_diag(rg_w_a[layer]), _block_diag(rg_w_x[layer])], axis=1).astype(BF16)
        b_gate = jnp.concatenate([rg_b_a[layer], rg_b_x[layer]]).reshape(1, -1)
        cr = _conv_rg(rest, sc_conv_w[layer], vec(sc_conv_b[layer]), rg_conv_w[layer], vec(rg_conv_b[layer]),
                      w_gate, b_gate, vec(rg_lambda[layer]), batch, seq)
        xt = _out_proj(xt, attn, cr, w_out[layer].astype(BF16))

        wkv = jnp.concatenate([xa_wk[layer], xa_wv[layer]], axis=1).astype(BF16)
        mk, mv = _mem_kv(memt, vec(mem_norm[layer]), wkv)
        xt = _xattn(xt, vec(xa_norm[layer]), xa_wq[layer].astype(BF16), mk, mv, xa_wo[layer].astype(BF16), seq)

        j = layer // 2
        if layer % 2 == 0:
            xt = _dense_ffn(xt, vec(ffn_norm[layer]), dense_wg[j].astype(BF16), dense_wu[j].astype(BF16),
                            dense_wd[j].astype(BF16))
        else:
            xt = _moe_layer(xt, vec(ffn_norm[layer]), moe_router[j], moe_wg[j].astype(BF16),
                            moe_wu[j].astype(BF16), moe_wd[j].astype(BF16), vec(final_norm))
    return xt.reshape(batch, seq, D_MODEL)
```

```python
import functools
import math

import jax
import jax.numpy as jnp
import numpy as np
from jax import lax
from jax.experimental import pallas as pl
from jax.experimental.pallas import tpu as pltpu

F32 = jnp.float32
BF16 = jnp.bfloat16

D_MODEL = 1024
MEM_LEN = 256
HEAD_DIM = 64
ATTN_Q_HEADS = 8
ATTN_KV_HEADS = 2
ATTN_WIDTH = ATTN_Q_HEADS * HEAD_DIM
KV_WIDTH = ATTN_KV_HEADS * HEAD_DIM
BLOCK = 128
SC_WIDTH = 256
SC_CONV = 3
RG_WIDTH = 256
RG_HEADS = 4
RG_HEAD_DIM = RG_WIDTH // RG_HEADS
RG_CONV = 4
RG_C = 8.0
N_BUCKETS = 32
MAX_EXACT = N_BUCKETS // 2
MAX_DISTANCE = 128
XA_HEADS = 4
XA_HEAD_DIM = 128
XA_WIDTH = XA_HEADS * XA_HEAD_DIM
N_EXPERTS = 8
EPS = 1e-6
NEG_INF = -1e30
REST_WIDTH = 3 * SC_WIDTH + 2 * RG_WIDTH

LANES_V7X = 128
SUBLANES_V7X = 8
VMEM_BYTES_V7X = 64 * 1024 * 1024
ROW_CHUNKS = D_MODEL // LANES_V7X
assert ROW_CHUNKS == SUBLANES_V7X

TM = 512
TS = 512
TM_ROUTE = 512
TM_GROUP = 512
TM_MOVE = 1024
TM_COMBINE = 256
ISSUE_UNROLL = 8
FF_SPLIT = 2
CARRY_ROWS = SUBLANES_V7X


def _mib(n):
    return int(n * 1024 * 1024)


def _params(semantics, vmem_mib):
    assert _mib(vmem_mib) < VMEM_BYTES_V7X
    return pltpu.CompilerParams(dimension_semantics=semantics, vmem_limit_bytes=_mib(vmem_mib))


def _rms(x, g):
    ms = jnp.mean(x * x, axis=-1, keepdims=True)
    return x * lax.rsqrt(ms + EPS) * g


def _const_spec(shape):
    nd = len(shape)
    return pl.BlockSpec(shape, lambda *_: (0,) * nd)


def _in_proj_kernel(x_ref, g_ref, w_ref, q_ref, k_ref, v_ref, r_ref):
    h = _rms(x_ref[...], g_ref[...]).astype(BF16)
    p = jnp.dot(h, w_ref[...], preferred_element_type=F32)
    q_ref[...] = p[:, :ATTN_WIDTH].astype(BF16)
    k_ref[...] = p[:, ATTN_WIDTH:ATTN_WIDTH + KV_WIDTH].astype(BF16)
    v_ref[...] = p[:, ATTN_WIDTH + KV_WIDTH:ATTN_WIDTH + 2 * KV_WIDTH].astype(BF16)
    r_ref[...] = p[:, ATTN_WIDTH + 2 * KV_WIDTH:]


def _in_proj(x, g, w):
    t = x.shape[0]
    n = w.shape[1]
    row = lambda width: pl.BlockSpec((TM, width), lambda i: (i, 0))
    return pl.pallas_call(
        _in_proj_kernel,
        grid=(t // TM,),
        in_specs=[row(D_MODEL), _const_spec((1, D_MODEL)), _const_spec((D_MODEL, n))],
        out_specs=[row(ATTN_WIDTH), row(KV_WIDTH), row(KV_WIDTH), row(REST_WIDTH)],
        out_shape=[jax.ShapeDtypeStruct((t, ATTN_WIDTH), BF16),
                   jax.ShapeDtypeStruct((t, KV_WIDTH), BF16),
                   jax.ShapeDtypeStruct((t, KV_WIDTH), BF16),
                   jax.ShapeDtypeStruct((t, REST_WIDTH), F32)],
        compiler_params=_params(("parallel",), 40),
        name="in_proj",
    )(x, g, w)


def _attn_kernel(sink_ref, q_ref, kp_ref, kc_ref, vp_ref, vc_ref, bias_ref, o_ref):
    q = q_ref[...]
    kband = jnp.concatenate([kp_ref[...], kc_ref[...]], axis=0)
    vband = jnp.concatenate([vp_ref[...], vc_ref[...]], axis=0)
    group = ATTN_Q_HEADS // ATTN_KV_HEADS
    outs = []
    for h in range(ATTN_Q_HEADS):
        kv = slice((h // group) * HEAD_DIM, (h // group + 1) * HEAD_DIM)
        qh = q[:, h * HEAD_DIM:(h + 1) * HEAD_DIM]
        s = lax.dot_general(qh, kband[:, kv], (((1,), (1,)), ((), ())), preferred_element_type=F32)
        logits = s * (1.0 / math.sqrt(HEAD_DIM)) + bias_ref[h]
        sink = sink_ref[h]
        m = jnp.maximum(jnp.max(logits, axis=-1, keepdims=True), sink)
        p = jnp.exp(logits - m)
        denom = jnp.sum(p, axis=-1, keepdims=True) + jnp.exp(sink - m)
        o = jnp.dot(p.astype(BF16), vband[:, kv], preferred_element_type=F32)
        outs.append(o / denom)
    o_ref[...] = jnp.concatenate(outs, axis=-1).astype(BF16)


def _attention(q, k, v, sinks, bias_tbl, blocks_per_seq):
    t = q.shape[0]
    cur = lambda i: (i, 0)
    prev = lambda i: (jnp.maximum(i - 1, 0), 0)
    return pl.pallas_call(
        _attn_kernel,
        grid=(t // BLOCK,),
        in_specs=[pl.BlockSpec(memory_space=pltpu.SMEM),
                  pl.BlockSpec((BLOCK, ATTN_WIDTH), cur),
                  pl.BlockSpec((BLOCK, KV_WIDTH), prev),
                  pl.BlockSpec((BLOCK, KV_WIDTH), cur),
                  pl.BlockSpec((BLOCK, KV_WIDTH), prev),
                  pl.BlockSpec((BLOCK, KV_WIDTH), cur),
                  pl.BlockSpec((None, ATTN_Q_HEADS, BLOCK, 2 * BLOCK),
                               lambda i: (jnp.minimum(i % blocks_per_seq, 1), 0, 0, 0))],
        out_specs=pl.BlockSpec((BLOCK, ATTN_WIDTH), cur),
        out_shape=jax.ShapeDtypeStruct((t, ATTN_WIDTH), BF16),
        compiler_params=_params(("parallel",), 24),
        name="swa_attention",
    )(sinks, q, k, k, v, v, bias_tbl)


def _bias_table_kernel(rel_ref, bucket_ref, o_ref):
    for v in range(2):
        bucket = bucket_ref[v]
        hits = [bucket == b for b in range(N_BUCKETS)]
        for h in range(ATTN_Q_HEADS):
            tbl = jnp.full(bucket.shape, NEG_INF, F32)
            for b in range(N_BUCKETS):
                tbl = jnp.where(hits[b], rel_ref[b * ATTN_Q_HEADS + h], tbl)
            o_ref[v, h] = tbl


def _attention_bias_tables(rel_bias):
    q_idx = np.arange(BLOCK)
    k_idx = np.arange(2 * BLOCK)
    dist = q_idx[:, None] + BLOCK - k_idx[None, :]
    in_window = (dist >= 0) & (dist < BLOCK)
    n = np.maximum(dist, 0)
    large = MAX_EXACT + (np.log(np.maximum(n, 1).astype(np.float32) / np.float32(MAX_EXACT))
                         / np.float32(math.log(MAX_DISTANCE / MAX_EXACT))
                         * np.float32(N_BUCKETS - MAX_EXACT)).astype(np.int32)
    bucket = np.where(n < MAX_EXACT, n, np.minimum(large, N_BUCKETS - 1))
    band = np.where(in_window, bucket, -1)
    first = np.where(in_window & (k_idx[None, :] >= BLOCK), bucket, -1)
    buckets = jnp.asarray(np.stack([first, band]).astype(np.int32))
    return pl.pallas_call(
        _bias_table_kernel,
        in_specs=[pl.BlockSpec(memory_space=pltpu.SMEM), pl.BlockSpec(memory_space=pltpu.VMEM)],
        out_specs=pl.BlockSpec(memory_space=pltpu.VMEM),
        out_shape=jax.ShapeDtypeStruct((2, ATTN_Q_HEADS, BLOCK, 2 * BLOCK), F32),
        name="t5_bias_table",
    )(rel_bias.astype(F32).reshape(-1), buckets)


def _shift_rows(x, s, fill):
    return jnp.concatenate([jnp.full((s, x.shape[1]), fill, x.dtype), x[:x.shape[0] - s]], axis=0)


def _conv_rg_kernel(r_ref, scw_ref, scb_ref, rgw_ref, rgb_ref, wgate_ref, bgate_ref, lam_ref,
                    o_ref, sc_ext, rg_ext, h_carry):
    ts = r_ref.shape[0]
    c0 = CARRY_ROWS

    @pl.when(pl.program_id(1) == 0)
    def _():
        sc_ext[0:c0, :] = jnp.zeros((c0, SC_WIDTH), F32)
        rg_ext[0:c0, :] = jnp.zeros((c0, RG_WIDTH), F32)
        h_carry[...] = jnp.zeros_like(h_carry)

    sc_b = r_ref[:, 0:SC_WIDTH]
    sc_ext[c0:c0 + ts, :] = r_ref[:, SC_WIDTH:2 * SC_WIDTH] * r_ref[:, 2 * SC_WIDTH:3 * SC_WIDTH]
    rg_ext[c0:c0 + ts, :] = r_ref[:, 3 * SC_WIDTH:3 * SC_WIDTH + RG_WIDTH]
    rg_g = r_ref[:, 3 * SC_WIDTH + RG_WIDTH:]

    conv = scb_ref[...]
    for k in range(SC_CONV):
        off = c0 - (SC_CONV - 1) + k
        conv = conv + scw_ref[k:k + 1, :] * sc_ext[off:off + ts, :]
    conv_out = sc_b * conv

    rg_in = rgb_ref[...]
    for k in range(RG_CONV):
        off = c0 - (RG_CONV - 1) + k
        rg_in = rg_in + rgw_ref[k:k + 1, :] * rg_ext[off:off + ts, :]

    sc_ext[0:c0, :] = sc_ext[ts:ts + c0, :]
    rg_ext[0:c0, :] = rg_ext[ts:ts + c0, :]

    gates = jnp.dot(rg_in.astype(BF16), wgate_ref[...], preferred_element_type=F32) + bgate_ref[...]
    r_gate = jax.nn.sigmoid(gates[:, :RG_WIDTH])
    i_gate = jax.nn.sigmoid(gates[:, RG_WIDTH:])
    neg_lam = -lam_ref[...]
    softplus = jnp.maximum(neg_lam, 0.0) + jnp.log1p(jnp.exp(-jnp.abs(neg_lam)))
    log_a = -RG_C * r_gate * softplus
    a = jnp.exp(log_a)
    u = jnp.sqrt(jnp.tanh(-log_a) * (1.0 + a * a)) * (i_gate * rg_in)

    s = 1
    while s < ts:
        u = a * _shift_rows(u, s, 0.0) + u
        a = a * _shift_rows(a, s, 1.0)
        s *= 2
    h = a * h_carry[...] + u
    h_carry[...] = h[ts - 1:ts, :]

    c = math.sqrt(2.0 / math.pi)
    gelu = 0.5 * rg_g * (1.0 + jnp.tanh(c * (rg_g + 0.044715 * (rg_g * rg_g * rg_g))))
    o_ref[:, 0:SC_WIDTH] = conv_out.astype(BF16)
    o_ref[:, SC_WIDTH:] = (h * gelu).astype(BF16)


def _conv_rg(rest, sc_w, sc_b, rg_w, rg_b, w_gate, b_gate, lam, batch, seq):
    t = rest.shape[0]
    steps = seq // TS
    row = lambda b, s: (b * steps + s, 0)
    return pl.pallas_call(
        _conv_rg_kernel,
        grid=(batch, steps),
        in_specs=[pl.BlockSpec((TS, REST_WIDTH), row),
                  _const_spec((SC_CONV, SC_WIDTH)), _const_spec((1, SC_WIDTH)),
                  _const_spec((RG_CONV, RG_WIDTH)), _const_spec((1, RG_WIDTH)),
                  _const_spec((RG_WIDTH, 2 * RG_WIDTH)), _const_spec((1, 2 * RG_WIDTH)),
                  _const_spec((1, RG_WIDTH))],
        out_specs=pl.BlockSpec((TS, SC_WIDTH + RG_WIDTH), row),
        out_shape=jax.ShapeDtypeStruct((t, SC_WIDTH + RG_WIDTH), BF16),
        scratch_shapes=[pltpu.VMEM((TS + 2 * CARRY_ROWS, SC_WIDTH), F32),
                        pltpu.VMEM((TS + 2 * CARRY_ROWS, RG_WIDTH), F32),
                        pltpu.VMEM((1, RG_WIDTH), F32)],
        compiler_params=_params(("arbitrary", "arbitrary"), 32),
        name="conv_rglru",
    )(rest, sc_w, sc_b, rg_w, rg_b, w_gate, b_gate, lam)


def _out_proj_kernel(x_ref, a_ref, c_ref, w_ref, o_ref):
    mixed = jnp.concatenate([a_ref[...], c_ref[...]], axis=-1)
    o_ref[...] = x_ref[...] + jnp.dot(mixed, w_ref[...], preferred_element_type=F32)


def _out_proj(x, attn, cr, w):
    t = x.shape[0]
    row = lambda width: pl.BlockSpec((TM, width), lambda i: (i, 0))
    return pl.pallas_call(
        _out_proj_kernel,
        grid=(t // TM,),
        in_specs=[row(D_MODEL), row(ATTN_WIDTH), row(SC_WIDTH + RG_WIDTH), _const_spec((D_MODEL, D_MODEL))],
        out_specs=row(D_MODEL),
        out_shape=jax.ShapeDtypeStruct((t, D_MODEL), F32),
        compiler_params=_params(("parallel",), 24),
        name="out_proj",
    )(x, attn, cr, w)


def _mem_kv_kernel(m_ref, g_ref, w_ref, k_ref, v_ref):
    h = _rms(m_ref[...], g_ref[...]).astype(BF16)
    p = jnp.dot(h, w_ref[...], preferred_element_type=F32)
    k_ref[...] = p[:, :XA_WIDTH].astype(BF16)
    v_ref[...] = p[:, XA_WIDTH:].astype(BF16)


def _mem_kv(mem, g, wkv):
    t = mem.shape[0]
    row = lambda width: pl.BlockSpec((MEM_LEN, width), lambda i: (i, 0))
    return pl.pallas_call(
        _mem_kv_kernel,
        grid=(t // MEM_LEN,),
        in_specs=[row(D_MODEL), _const_spec((1, D_MODEL)), _const_spec((D_MODEL, 2 * XA_WIDTH))],
        out_specs=[row(XA_WIDTH), row(XA_WIDTH)],
        out_shape=[jax.ShapeDtypeStruct((t, XA_WIDTH), BF16)] * 2,
        compiler_params=_params(("parallel",), 24),
        name="mem_kv",
    )(mem, g, wkv)


def _xattn_kernel(x_ref, g_ref, wq_ref, k_ref, v_ref, wo_ref, o_ref):
    x = x_ref[...]
    h = _rms(x, g_ref[...]).astype(BF16)
    q = jnp.dot(h, wq_ref[...], preferred_element_type=F32).astype(BF16)
    k = k_ref[...]
    v = v_ref[...]
    outs = []
    for hd in range(XA_HEADS):
        sl = slice(hd * XA_HEAD_DIM, (hd + 1) * XA_HEAD_DIM)
        s = lax.dot_general(q[:, sl], k[:, sl], (((1,), (1,)), ((), ())), preferred_element_type=F32)
        s = s * (1.0 / math.sqrt(XA_HEAD_DIM))
        p = jnp.exp(s - jnp.max(s, axis=-1, keepdims=True))
        o = jnp.dot(p.astype(BF16), v[:, sl], preferred_element_type=F32)
        outs.append(o / jnp.sum(p, axis=-1, keepdims=True))
    att = jnp.concatenate(outs, axis=-1).astype(BF16)
    o_ref[...] = x + jnp.dot(att, wo_ref[...], preferred_element_type=F32)


def _xattn(x, g, wq, k, v, wo, seq):
    t = x.shape[0]
    per_seq = seq // TM
    row = pl.BlockSpec((TM, D_MODEL), lambda i: (i, 0))
    mem_blk = pl.BlockSpec((MEM_LEN, XA_WIDTH), lambda i: (i // per_seq, 0))
    return pl.pallas_call(
        _xattn_kernel,
        grid=(t // TM,),
        in_specs=[row, _const_spec((1, D_MODEL)), _const_spec((D_MODEL, XA_WIDTH)), mem_blk, mem_blk,
                  _const_spec((XA_WIDTH, D_MODEL))],
        out_specs=row,
        out_shape=jax.ShapeDtypeStruct((t, D_MODEL), F32),
        compiler_params=_params(("parallel",), 32),
        name="mem_xattn",
    )(x, g, wq, k, v, wo)


def _ffn_kernel(x_ref, g_ref, wg_ref, wu_ref, wd_ref, o_ref):
    x = x_ref[...]
    h = _rms(x, g_ref[...]).astype(BF16)
    gate = jnp.dot(h, wg_ref[...], preferred_element_type=F32)
    up = jnp.dot(h, wu_ref[...], preferred_element_type=F32)
    act = (gate * jax.nn.sigmoid(gate) * up).astype(BF16)
    o_ref[...] = x + jnp.dot(act, wd_ref[...], preferred_element_type=F32)


def _dense_ffn(x, g, wg, wu, wd):
    t = x.shape[0]
    d_ff = wg.shape[1]
    row = pl.BlockSpec((TM, D_MODEL), lambda i: (i, 0))
    return pl.pallas_call(
        _ffn_kernel,
        grid=(t // TM,),
        in_specs=[row, _const_spec((1, D_MODEL)), _const_spec((D_MODEL, d_ff)), _const_spec((D_MODEL, d_ff)),
                  _const_spec((d_ff, D_MODEL))],
        out_specs=row,
        out_shape=jax.ShapeDtypeStruct((t, D_MODEL), F32),
        compiler_params=_params(("parallel",), 60),
        name="dense_swiglu",
    )(x, g, wg, wu, wd)


META_E1, META_E2, META_R1, META_R2, META_W1, META_W2 = range(6)


def _to_token_tiles(ref, rows):
    m = rows.shape[0]
    for c in range(ROW_CHUNKS):
        ref[pl.ds(c, m, stride=ROW_CHUNKS), :] = rows[:, c * LANES_V7X:(c + 1) * LANES_V7X]


def _from_token_tiles(ref):
    m = ref.shape[0] // ROW_CHUNKS
    return jnp.concatenate([ref[pl.ds(c, m, stride=ROW_CHUNKS), :] for c in range(ROW_CHUNKS)], axis=-1)


def _router_kernel(x_ref, g_ref, wr_ref, tri_ref, sel_ref, hf_ref, meta_ref, metat_ref, cnt_ref, carry):
    @pl.when(pl.program_id(0) == 0)
    def _():
        carry[...] = jnp.zeros_like(carry)

    h = _rms(x_ref[...], g_ref[...])
    _to_token_tiles(hf_ref, h)
    logits = jnp.dot(h, wr_ref[...], preferred_element_type=F32, precision=lax.Precision.HIGHEST)
    tm = logits.shape[0]
    lane = lax.broadcasted_iota(jnp.int32, (tm, LANES_V7X), 1)
    lg = jnp.where(lane < N_EXPERTS, logits, -jnp.inf)
    m1 = jnp.max(lg, axis=-1, keepdims=True)
    e1 = jnp.min(jnp.where(lg == m1, lane, LANES_V7X), axis=-1, keepdims=True)
    lg2 = jnp.where(lane == e1, -jnp.inf, lg)
    m2 = jnp.max(lg2, axis=-1, keepdims=True)
    e2 = jnp.min(jnp.where(lg2 == m2, lane, LANES_V7X), axis=-1, keepdims=True)
    ex = jnp.exp(m2 - m1)
    w1 = 1.0 / (1.0 + ex)
    w2 = ex / (1.0 + ex)

    hit1 = lane == e1
    hit2 = lane == e2
    onehot = (hit1 | hit2).astype(BF16)
    ahead = jnp.dot(tri_ref[...], onehot, preferred_element_type=F32) + carry[...]
    r1 = jnp.sum(jnp.where(hit1, ahead, 0.0), axis=-1, keepdims=True)
    r2 = jnp.sum(jnp.where(hit2, ahead, 0.0), axis=-1, keepdims=True)
    carry[...] = carry[...] + jnp.sum(onehot.astype(F32), axis=0, keepdims=True)

    meta = jnp.zeros((tm, LANES_V7X), F32)
    for col, val in ((META_E1, e1.astype(F32)), (META_E2, e2.astype(F32)), (META_R1, r1), (META_R2, r2),
                     (META_W1, w1), (META_W2, w2)):
        meta = jnp.where(lane == col, val, meta)
    meta_ref[...] = meta
    metat_ref[...] = lax.dot_general(sel_ref[...], meta, (((1,), (1,)), ((), ())), preferred_element_type=F32,
                                     precision=lax.Precision.HIGHEST)
    cnt_ref[...] = carry[...]


def _router(x, g, wr_pad, tri, sel):
    t = x.shape[0]
    row = lambda width: pl.BlockSpec((TM_ROUTE, width), lambda i: (i, 0))
    return pl.pallas_call(
        _router_kernel,
        grid=(t // TM_ROUTE,),
        in_specs=[row(D_MODEL), _const_spec((1, D_MODEL)), _const_spec((D_MODEL, LANES_V7X)),
                  _const_spec((TM_ROUTE, TM_ROUTE)), _const_spec((SUBLANES_V7X, LANES_V7X))],
        out_specs=[pl.BlockSpec((TM_ROUTE * ROW_CHUNKS, LANES_V7X), lambda i: (i, 0)), row(LANES_V7X),
                   pl.BlockSpec((SUBLANES_V7X, TM_ROUTE), lambda i: (0, i)), _const_spec((1, LANES_V7X))],
        out_shape=[jax.ShapeDtypeStruct((t * ROW_CHUNKS, LANES_V7X), F32),
                   jax.ShapeDtypeStruct((t, LANES_V7X), F32),
                   jax.ShapeDtypeStruct((SUBLANES_V7X, t), F32),
                   jax.ShapeDtypeStruct((1, LANES_V7X), F32)],
        scratch_shapes=[pltpu.VMEM((1, LANES_V7X), F32)],
        compiler_params=_params(("arbitrary",), 32),
        name="moe_router",
    )(x, g, wr_pad, tri, sel)


def _token_rows(ref, first_token, n_tokens):
    start = pl.multiple_of(first_token * ROW_CHUNKS, ROW_CHUNKS)
    return ref.at[pl.ds(start, n_tokens * ROW_CHUNKS)]


def _row_copy(src_ref, src_token, dst_ref, dst_token, sem):
    return pltpu.make_async_copy(_token_rows(src_ref, src_token, 1), _token_rows(dst_ref, dst_token, 1), sem)


def _dispatch_kernel(fill_ref, pos1_ref, pos2_ref, hf_ref, xs_ref, zeros, sem_z, sem):
    @pl.when(pl.program_id(0) == 0)
    def _():
        zeros[...] = jnp.zeros_like(zeros)

        def tile_fill(e):
            return pltpu.make_async_copy(zeros, _token_rows(xs_ref, pl.multiple_of(fill_ref[e], TM_GROUP), TM_GROUP),
                                         sem_z)

        for e in range(2 * N_EXPERTS):
            @pl.when(fill_ref[e] >= 0)
            def _():
                tile_fill(e).start()
        for e in range(2 * N_EXPERTS):
            @pl.when(fill_ref[e] >= 0)
            def _():
                tile_fill(e).wait()

    def issue(r, carry):
        _row_copy(hf_ref, r, xs_ref, pos1_ref[r], sem).start(priority=0)
        _row_copy(hf_ref, r, xs_ref, pos2_ref[r], sem).start(priority=1)
        return carry

    lax.fori_loop(0, TM_MOVE, issue, 0, unroll=ISSUE_UNROLL)
    whole_step = pltpu.make_async_copy(hf_ref, _token_rows(xs_ref, 0, TM_MOVE), sem)
    whole_step.wait()
    whole_step.wait()


def _dispatch(fill_start, pos1, pos2, hf, n_rows):
    t = hf.shape[0] // ROW_CHUNKS
    idx = pl.BlockSpec((TM_MOVE,), lambda i, fill: (i,), memory_space=pltpu.SMEM)
    return pl.pallas_call(
        _dispatch_kernel,
        grid_spec=pltpu.PrefetchScalarGridSpec(
            num_scalar_prefetch=1,
            grid=(t // TM_MOVE,),
            in_specs=[idx, idx, pl.BlockSpec((TM_MOVE * ROW_CHUNKS, LANES_V7X), lambda i, fill: (i, 0))],
            out_specs=pl.BlockSpec(memory_space=pl.ANY),
            scratch_shapes=[pltpu.VMEM((TM_GROUP * ROW_CHUNKS, LANES_V7X), F32), pltpu.SemaphoreType.DMA(()),
                            pltpu.SemaphoreType.DMA(())]),
        out_shape=jax.ShapeDtypeStruct((n_rows * ROW_CHUNKS, LANES_V7X), F32),
        compiler_params=_params(("arbitrary",), 24),
        name="moe_dispatch",
    )(fill_start, pos1, pos2, hf)


def _grouped_kernel(te_ref, used_ref, x_ref, wg_ref, wu_ref, wd_ref, y_ref, acc):
    i = pl.program_id(0)
    j = pl.program_id(1)
    last = FF_SPLIT - 1

    @pl.when(i < used_ref[0])
    def _():
        h = _from_token_tiles(x_ref).astype(BF16)
        gate = jnp.dot(h, wg_ref[...], preferred_element_type=F32)
        up = jnp.dot(h, wu_ref[...], preferred_element_type=F32)
        act = (gate * jax.nn.sigmoid(gate) * up).astype(BF16)
        part = jnp.dot(act, wd_ref[...], preferred_element_type=F32)

        @pl.when(j == 0)
        def _():
            acc[...] = part

        @pl.when((j > 0) & (j < last))
        def _():
            acc[...] += part

        @pl.when(j == last)
        def _():
            _to_token_tiles(y_ref, part if FF_SPLIT == 1 else acc[...] + part)

    @pl.when((i >= used_ref[0]) & (j == last))
    def _():
        y_ref[...] = jnp.zeros_like(y_ref)


def _grouped_swiglu(tile_expert, n_used, xs, wg, wu, wd):
    n_rows = xs.shape[0] // ROW_CHUNKS
    d_ff = wg.shape[2]
    ff = d_ff // FF_SPLIT
    tile = (TM_GROUP * ROW_CHUNKS, LANES_V7X)
    src = lambda i, j, te, used: (jnp.maximum(jnp.minimum(i, used[0] - 1), 0), 0)
    return pl.pallas_call(
        _grouped_kernel,
        grid_spec=pltpu.PrefetchScalarGridSpec(
            num_scalar_prefetch=2,
            grid=(n_rows // TM_GROUP, FF_SPLIT),
            in_specs=[pl.BlockSpec(tile, src),
                      pl.BlockSpec((None, D_MODEL, ff), lambda i, j, te, used: (te[i], 0, j)),
                      pl.BlockSpec((None, D_MODEL, ff), lambda i, j, te, used: (te[i], 0, j)),
                      pl.BlockSpec((None, ff, D_MODEL), lambda i, j, te, used: (te[i], j, 0))],
            out_specs=pl.BlockSpec(tile, lambda i, j, te, used: (i, 0)),
            scratch_shapes=[pltpu.VMEM((TM_GROUP, D_MODEL), F32)]),
        out_shape=jax.ShapeDtypeStruct((n_rows * ROW_CHUNKS, LANES_V7X), F32),
        compiler_params=_params(("arbitrary", "arbitrary"), 56),
        name="moe_grouped_swiglu",
    )(tile_expert, n_used, xs, wg, wu, wd)


def _combine_kernel(pos1_ref, pos2_ref, pos1_next_ref, pos2_next_ref, x_ref, meta_ref, g_ref, y_ref, o_ref,
                    y1, y2, sems):
    i = pl.program_id(0)

    def gather(p1_ref, p2_ref, slot):
        def issue(r, carry):
            _row_copy(y_ref, p1_ref[r], y1.at[slot], r, sems.at[slot]).start(priority=0)
            _row_copy(y_ref, p2_ref[r], y2.at[slot], r, sems.at[slot]).start(priority=1)
            return carry
        lax.fori_loop(0, TM_COMBINE, issue, 0, unroll=ISSUE_UNROLL)

    def finish(slot):
        for buf in (y1, y2):
            pltpu.make_async_copy(_token_rows(y_ref, 0, TM_COMBINE), buf.at[slot], sems.at[slot]).wait()
        w1 = meta_ref[:, META_W1:META_W1 + 1]
        w2 = meta_ref[:, META_W2:META_W2 + 1]
        out = x_ref[...] + (w1 * _from_token_tiles(y1.at[slot]) + w2 * _from_token_tiles(y2.at[slot]))
        o_ref[...] = _rms(out, g_ref[...])

    @pl.when(i == 0)
    def _():
        gather(pos1_ref, pos2_ref, 0)

    for slot in range(2):
        @pl.when(i % 2 == slot)
        def _():
            @pl.when(i + 1 < pl.num_programs(0))
            def _():
                gather(pos1_next_ref, pos2_next_ref, 1 - slot)
            finish(slot)


def _combine(pos1, pos2, x, meta, g, y):
    t = x.shape[0]
    steps = t // TM_COMBINE
    idx = pl.BlockSpec((TM_COMBINE,), lambda i: (i,), memory_space=pltpu.SMEM)
    idx_next = pl.BlockSpec((TM_COMBINE,), lambda i: (jnp.minimum(i + 1, steps - 1),), memory_space=pltpu.SMEM)
    row = lambda width: pl.BlockSpec((TM_COMBINE, width), lambda i: (i, 0))
    slots = pltpu.VMEM((2, TM_COMBINE * ROW_CHUNKS, LANES_V7X), F32)
    return pl.pallas_call(
        _combine_kernel,
        grid=(steps,),
        in_specs=[idx, idx, idx_next, idx_next, row(D_MODEL), row(LANES_V7X), _const_spec((1, D_MODEL)),
                  pl.BlockSpec(memory_space=pl.ANY)],
        out_specs=row(D_MODEL),
        out_shape=jax.ShapeDtypeStruct((t, D_MODEL), F32),
        scratch_shapes=[slots, slots, pltpu.SemaphoreType.DMA((2,))],
        compiler_params=_params(("arbitrary",), 24),
        name="moe_combine_norm",
    )(pos1, pos2, pos1, pos2, x, meta, g, y)


def _moe_layer(x, ffn_g, router_w, wg, wu, wd, final_g):
    t = x.shape[0]
    n_rows = 2 * t + N_EXPERTS * TM_GROUP
    wr_pad = jnp.zeros((D_MODEL, LANES_V7X), F32).at[:, :N_EXPERTS].set(router_w)
    tri = jnp.tril(jnp.ones((TM_ROUTE, TM_ROUTE), BF16), -1)
    sel = jnp.eye(SUBLANES_V7X, LANES_V7X, dtype=F32)
    hf, meta, metat, counts = _router(x, ffn_g, wr_pad, tri, sel)

    cnt = counts[0, :N_EXPERTS].astype(jnp.int32)
    padded = (cnt + TM_GROUP - 1) // TM_GROUP * TM_GROUP
    ends = jnp.cumsum(padded)
    starts = ends - padded
    experts = jnp.arange(N_EXPERTS, dtype=jnp.int32)[:, None]
    group_start = lambda e: jnp.sum(jnp.where(e[None, :] == experts, starts[:, None], 0), axis=0)
    pos1 = group_start(metat[META_E1].astype(jnp.int32)) + metat[META_R1].astype(jnp.int32)
    pos2 = group_start(metat[META_E2].astype(jnp.int32)) + metat[META_R2].astype(jnp.int32)
    n_used = (ends[-1] // TM_GROUP).astype(jnp.int32).reshape(1)
    tile_row = jnp.minimum(jnp.arange(n_rows // TM_GROUP, dtype=jnp.int32), n_used[0] - 1) * TM_GROUP
    tile_expert = jnp.sum(ends[None, :] <= tile_row[:, None], axis=1).astype(jnp.int32)
    tail_start = jnp.where(padded > 0, ends - TM_GROUP, -1)
    spare = ends[-1] + TM_GROUP * jnp.arange(N_EXPERTS, dtype=jnp.int32)
    fill_start = jnp.concatenate([tail_start, jnp.where(spare < n_rows, spare, -1)]).astype(jnp.int32)

    xs = _dispatch(fill_start, pos1, pos2, hf, n_rows)
    y = _grouped_swiglu(tile_expert, n_used, xs, wg, wu, wd)
    return _combine(pos1, pos2, x, meta, final_g, y)


def _block_diag(w):
    heads, d, _ = w.shape
    eye = jnp.eye(heads, dtype=w.dtype)
    return jnp.einsum('hij,hg->higj', w, eye).reshape(heads * d, heads * d)


def kernel(x, mem, rel_bias, mix_norm, w_in, attn_sinks, sc_conv_w, sc_conv_b, rg_conv_w, rg_conv_b, rg_w_a,
           rg_b_a, rg_w_x, rg_b_x, rg_lambda, w_out, xa_norm, mem_norm, xa_wq, xa_wk, xa_wv, xa_wo, ffn_norm,
           dense_wg, dense_wu, dense_wd, moe_router, moe_wg, moe_wu, moe_wd, final_norm):
    batch, seq, _ = x.shape
    depth = w_in.shape[0]
    assert depth == 2 and seq % TS == 0 and seq % TM == 0 and (batch * seq) % TM_ROUTE == 0
    xt = x.reshape(batch * seq, D_MODEL)
    memt = mem.reshape(batch * MEM_LEN, D_MODEL)
    bias_tbl = _attention_bias_tables(rel_bias)
    vec = lambda a: a.reshape(1, -1)

    for layer in range(depth):
        q, k, v, rest = _in_proj(xt, vec(mix_norm[layer]), w_in[layer].astype(BF16))
        attn = _attention(q, k, v, attn_sinks[layer], bias_tbl, seq // BLOCK)
        w_gate = jnp.concatenate([_block_diag(rg_w_a[layer]), _block_diag(rg_w_x[layer])], axis=1).astype(BF16)
        b_gate = jnp.concatenate([rg_b_a[layer], rg_b_x[layer]]).reshape(1, -1)
        cr = _conv_rg(rest, sc_conv_w[layer], vec(sc_conv_b[layer]), rg_conv_w[layer], vec(rg_conv_b[layer]),
                      w_gate, b_gate, vec(rg_lambda[layer]), batch, seq)
        xt = _out_proj(xt, attn, cr, w_out[layer].astype(BF16))

        wkv = jnp.concatenate([xa_wk[layer], xa_wv[layer]], axis=1).astype(BF16)
        mk, mv = _mem_kv(memt, vec(mem_norm[layer]), wkv)
        xt = _xattn(xt, vec(xa_norm[layer]), xa_wq[layer].astype(BF16), mk, mv, xa_wo[layer].astype(BF16), seq)

        j = layer // 2
        if layer % 2 == 0:
            xt = _dense_ffn(xt, vec(ffn_norm[layer]), dense_wg[j].astype(BF16), dense_wu[j].astype(BF16),
                            dense_wd[j].astype(BF16))
        else:
            xt = _moe_layer(xt, vec(ffn_norm[layer]), moe_router[j], moe_wg[j].astype(BF16),
                            moe_wu[j].astype(BF16), moe_wd[j].astype(BF16), vec(final_norm))
    return xt.reshape(batch, seq, D_MODEL)
```

```python
import functools
import math

import jax
import jax.numpy as jnp
import numpy as np
from jax import lax
from jax.experimental import pallas as pl
from jax.experimental.pallas import tpu as pltpu

F32 = jnp.float32
BF16 = jnp.bfloat16

D_MODEL = 1024
MEM_LEN = 256
HEAD_DIM = 64
ATTN_Q_HEADS = 8
ATTN_KV_HEADS = 2
ATTN_WIDTH = ATTN_Q_HEADS * HEAD_DIM
KV_WIDTH = ATTN_KV_HEADS * HEAD_DIM
KV_DUP_WIDTH = 2 * KV_WIDTH
BLOCK = 128
SC_WIDTH = 256
SC_CONV = 3
RG_WIDTH = 256
RG_HEADS = 4
RG_HEAD_DIM = RG_WIDTH // RG_HEADS
RG_CONV = 4
RG_C = 8.0
N_BUCKETS = 32
MAX_EXACT = N_BUCKETS // 2
MAX_DISTANCE = 128
XA_HEADS = 4
XA_HEAD_DIM = 128
XA_WIDTH = XA_HEADS * XA_HEAD_DIM
N_EXPERTS = 8
EPS = 1e-6
NEG_INF = -1e30
REST_WIDTH = 3 * SC_WIDTH + 2 * RG_WIDTH

LANES_V7X = 128
SUBLANES_V7X = 8
VMEM_BYTES_V7X = 64 * 1024 * 1024
ROW_CHUNKS = D_MODEL // LANES_V7X
assert ROW_CHUNKS == SUBLANES_V7X

TM = 512
ATTN_BLOCKS = 4
TS = 512
TM_ROUTE = 512
TM_GROUP = 512
TM_MOVE = 1024
TM_COMBINE = 256
ISSUE_UNROLL = 8
FF_SPLIT = 2
CARRY_ROWS = SUBLANES_V7X


def _mib(n):
    return int(n * 1024 * 1024)


def _params(semantics, vmem_mib):
    assert _mib(vmem_mib) < VMEM_BYTES_V7X
    return pltpu.CompilerParams(dimension_semantics=semantics, vmem_limit_bytes=_mib(vmem_mib))


def _rms(x, g):
    ms = jnp.mean(x * x, axis=-1, keepdims=True)
    return x * lax.rsqrt(ms + EPS) * g


def _const_spec(shape):
    nd = len(shape)
    return pl.BlockSpec(shape, lambda *_: (0,) * nd)


def _in_proj_kernel(x_ref, g_ref, w_ref, q_ref, k_ref, v_ref, r_ref):
    h = _rms(x_ref[...], g_ref[...]).astype(BF16)
    p = jnp.dot(h, w_ref[...], preferred_element_type=F32)
    q_ref[...] = p[:, :ATTN_WIDTH].astype(BF16)
    k_ref[...] = p[:, ATTN_WIDTH:ATTN_WIDTH + KV_DUP_WIDTH].astype(BF16)
    v_ref[...] = p[:, ATTN_WIDTH + KV_DUP_WIDTH:ATTN_WIDTH + 2 * KV_DUP_WIDTH].astype(BF16)
    r_ref[...] = p[:, ATTN_WIDTH + 2 * KV_DUP_WIDTH:]


def _in_proj(x, g, w):
    t = x.shape[0]
    n = w.shape[1]
    row = lambda width: pl.BlockSpec((TM, width), lambda i: (i, 0))
    return pl.pallas_call(
        _in_proj_kernel,
        grid=(t // TM,),
        in_specs=[row(D_MODEL), _const_spec((1, D_MODEL)), _const_spec((D_MODEL, n))],
        out_specs=[row(ATTN_WIDTH), row(KV_DUP_WIDTH), row(KV_DUP_WIDTH), row(REST_WIDTH)],
        out_shape=[jax.ShapeDtypeStruct((t, ATTN_WIDTH), BF16),
                   jax.ShapeDtypeStruct((t, KV_DUP_WIDTH), BF16),
                   jax.ShapeDtypeStruct((t, KV_DUP_WIDTH), BF16),
                   jax.ShapeDtypeStruct((t, REST_WIDTH), F32)],
        compiler_params=_params(("parallel",), 40),
        name="in_proj",
    )(x, g, w)


def _attn_kernel(sink_ref, q_ref, kp_ref, kc_ref, vp_ref, vc_ref, bias0_ref, bias_ref, o_ref):
    pairs_per_group = ATTN_Q_HEADS // ATTN_KV_HEADS // 2
    row = lax.broadcasted_iota(jnp.int32, (BLOCK, BLOCK), 0)
    col = lax.broadcasted_iota(jnp.int32, (BLOCK, BLOCK), 1)
    from_prev = col > row
    low_lanes = lax.broadcasted_iota(jnp.int32, (2 * BLOCK, 2 * HEAD_DIM), 1) < HEAD_DIM
    low_out = col < HEAD_DIM
    zero = jnp.zeros((), BF16)

    def block_diag(band):
        return jnp.concatenate([jnp.where(low_lanes, band, zero), jnp.where(low_lanes, zero, band)], axis=0)

    def scores(blk):
        rows = slice(blk * BLOCK, (blk + 1) * BLOCK)
        prev_rows = slice((blk - 1) * BLOCK, blk * BLOCK)
        out = []
        for g in range(ATTN_KV_HEADS):
            lanes = slice(g * 2 * HEAD_DIM, (g + 1) * 2 * HEAD_DIM)
            k_prev = kp_ref[:, lanes] if blk == 0 else kc_ref[prev_rows, lanes]
            v_prev = vp_ref[:, lanes] if blk == 0 else vc_ref[prev_rows, lanes]
            k_bd = block_diag(jnp.concatenate([k_prev, kc_ref[rows, lanes]], axis=0))
            v_bd = block_diag(jnp.concatenate([v_prev, vc_ref[rows, lanes]], axis=0))
            for pair in range(pairs_per_group):
                slab = g * pairs_per_group + pair
                q2 = q_ref[rows, slab * 2 * HEAD_DIM:(slab + 1) * 2 * HEAD_DIM]
                s = lax.dot_general(q2, k_bd, (((1,), (1,)), ((), ())), preferred_element_type=F32)
                out.append((slab, s, v_bd))
        return out

    def finish(blk, scored):
        rows = slice(blk * BLOCK, (blk + 1) * BLOCK)
        tbl_ref = bias0_ref if blk == 0 else bias_ref
        staged = []
        for slab, s, v_bd in scored:
            probs, denoms = [], []
            for side in range(2):
                h = 2 * slab + side
                sh = s[:, side * 2 * BLOCK:(side + 1) * 2 * BLOCK]
                logits = jnp.where(from_prev, sh[:, :BLOCK], sh[:, BLOCK:]) + tbl_ref[h]
                sink = sink_ref[h]
                m = jnp.maximum(jnp.max(logits, axis=-1, keepdims=True), sink)
                p = jnp.exp(logits - m)
                denoms.append(jnp.sum(p, axis=-1, keepdims=True) + jnp.exp(sink - m))
                probs += [jnp.where(from_prev, p, 0.0), jnp.where(from_prev, 0.0, p)]
            staged.append((slab, jnp.concatenate(probs, axis=-1).astype(BF16), v_bd, denoms))
        for slab, p_band, v_bd, denoms in staged:
            o = jnp.dot(p_band, v_bd, preferred_element_type=F32)
            o = o / jnp.where(low_out, denoms[0], denoms[1])
            o_ref[rows, slab * 2 * HEAD_DIM:(slab + 1) * 2 * HEAD_DIM] = o.astype(BF16)

    pending = scores(0)
    for blk in range(ATTN_BLOCKS):
        upcoming = scores(blk + 1) if blk + 1 < ATTN_BLOCKS else None
        finish(blk, pending)
        pending = upcoming


def _attention(q, k, v, sinks, bias_tbl, blocks_per_seq):
    t = q.shape[0]
    tile = ATTN_BLOCKS * BLOCK
    cur = lambda i: (i, 0)
    prev = lambda i: (jnp.maximum(i * ATTN_BLOCKS - 1, 0), 0)
    tbl = (None, ATTN_Q_HEADS, BLOCK, BLOCK)
    return pl.pallas_call(
        _attn_kernel,
        grid=(t // tile,),
        in_specs=[pl.BlockSpec(memory_space=pltpu.SMEM),
                  pl.BlockSpec((tile, ATTN_WIDTH), cur),
                  pl.BlockSpec((BLOCK, KV_DUP_WIDTH), prev),
                  pl.BlockSpec((tile, KV_DUP_WIDTH), cur),
                  pl.BlockSpec((BLOCK, KV_DUP_WIDTH), prev),
                  pl.BlockSpec((tile, KV_DUP_WIDTH), cur),
                  pl.BlockSpec(tbl, lambda i: (jnp.minimum((i * ATTN_BLOCKS) % blocks_per_seq, 1), 0, 0, 0)),
                  pl.BlockSpec(tbl, lambda i: (1, 0, 0, 0))],
        out_specs=pl.BlockSpec((tile, ATTN_WIDTH), cur),
        out_shape=jax.ShapeDtypeStruct((t, ATTN_WIDTH), BF16),
        compiler_params=_params(("parallel",), 24),
        name="swa_attention",
    )(sinks, q, k, k, v, v, bias_tbl, bias_tbl)


def _bias_table_kernel(rel_ref, bucket_ref, o_ref):
    for v in range(2):
        bucket = bucket_ref[v]
        hits = [bucket == b for b in range(N_BUCKETS)]
        for h in range(ATTN_Q_HEADS):
            tbl = jnp.full(bucket.shape, NEG_INF, F32)
            for b in range(N_BUCKETS):
                tbl = jnp.where(hits[b], rel_ref[b * ATTN_Q_HEADS + h], tbl)
            o_ref[v, h] = tbl


def _attention_bias_tables(rel_bias):
    q_idx = np.arange(BLOCK)[:, None]
    j_idx = np.arange(BLOCK)[None, :]
    from_prev = j_idx > q_idx
    n = np.where(from_prev, q_idx + BLOCK - j_idx, q_idx - j_idx)
    large = MAX_EXACT + (np.log(np.maximum(n, 1).astype(np.float32) / np.float32(MAX_EXACT))
                         / np.float32(math.log(MAX_DISTANCE / MAX_EXACT))
                         * np.float32(N_BUCKETS - MAX_EXACT)).astype(np.int32)
    bucket = np.where(n < MAX_EXACT, n, np.minimum(large, N_BUCKETS - 1))
    first = np.where(from_prev, -1, bucket)
    buckets = jnp.asarray(np.stack([first, bucket]).astype(np.int32))
    return pl.pallas_call(
        _bias_table_kernel,
        in_specs=[pl.BlockSpec(memory_space=pltpu.SMEM), pl.BlockSpec(memory_space=pltpu.VMEM)],
        out_specs=pl.BlockSpec(memory_space=pltpu.VMEM),
        out_shape=jax.ShapeDtypeStruct((2, ATTN_Q_HEADS, BLOCK, BLOCK), F32),
        name="t5_bias_table",
    )(rel_bias.astype(F32).reshape(-1), buckets)


def _shift_rows(x, s, fill):
    return jnp.concatenate([jnp.full((s, x.shape[1]), fill, x.dtype), x[:x.shape[0] - s]], axis=0)


def _conv_rg_kernel(r_ref, scw_ref, scb_ref, rgw_ref, rgb_ref, wgate_ref, bgate_ref, lam_ref,
                    o_ref, sc_ext, rg_ext, h_carry):
    ts = r_ref.shape[0]
    c0 = CARRY_ROWS

    @pl.when(pl.program_id(1) == 0)
    def _():
        sc_ext[0:c0, :] = jnp.zeros((c0, SC_WIDTH), F32)
        rg_ext[0:c0, :] = jnp.zeros((c0, RG_WIDTH), F32)
        h_carry[...] = jnp.zeros_like(h_carry)

    sc_b = r_ref[:, 0:SC_WIDTH]
    sc_ext[c0:c0 + ts, :] = r_ref[:, SC_WIDTH:2 * SC_WIDTH] * r_ref[:, 2 * SC_WIDTH:3 * SC_WIDTH]
    rg_ext[c0:c0 + ts, :] = r_ref[:, 3 * SC_WIDTH:3 * SC_WIDTH + RG_WIDTH]
    rg_g = r_ref[:, 3 * SC_WIDTH + RG_WIDTH:]

    conv = scb_ref[...]
    for k in range(SC_CONV):
        off = c0 - (SC_CONV - 1) + k
        conv = conv + scw_ref[k:k + 1, :] * sc_ext[off:off + ts, :]
    conv_out = sc_b * conv

    rg_in = rgb_ref[...]
    for k in range(RG_CONV):
        off = c0 - (RG_CONV - 1) + k
        rg_in = rg_in + rgw_ref[k:k + 1, :] * rg_ext[off:off + ts, :]

    sc_ext[0:c0, :] = sc_ext[ts:ts + c0, :]
    rg_ext[0:c0, :] = rg_ext[ts:ts + c0, :]

    gates = jnp.dot(rg_in.astype(BF16), wgate_ref[...], preferred_element_type=F32) + bgate_ref[...]
    r_gate = jax.nn.sigmoid(gates[:, :RG_WIDTH])
    i_gate = jax.nn.sigmoid(gates[:, RG_WIDTH:])
    neg_lam = -lam_ref[...]
    softplus = jnp.maximum(neg_lam, 0.0) + jnp.log1p(jnp.exp(-jnp.abs(neg_lam)))
    log_a = -RG_C * r_gate * softplus
    a = jnp.exp(log_a)
    u = jnp.sqrt(jnp.tanh(-log_a) * (1.0 + a * a)) * (i_gate * rg_in)

    s = 1
    while s < ts:
        u = a * _shift_rows(u, s, 0.0) + u
        a = a * _shift_rows(a, s, 1.0)
        s *= 2
    h = a * h_carry[...] + u
    h_carry[...] = h[ts - 1:ts, :]

    c = math.sqrt(2.0 / math.pi)
    gelu = 0.5 * rg_g * (1.0 + jnp.tanh(c * (rg_g + 0.044715 * (rg_g * rg_g * rg_g))))
    o_ref[:, 0:SC_WIDTH] = conv_out.astype(BF16)
    o_ref[:, SC_WIDTH:] = (h * gelu).astype(BF16)


def _conv_rg(rest, sc_w, sc_b, rg_w, rg_b, w_gate, b_gate, lam, batch, seq):
    t = rest.shape[0]
    steps = seq // TS
    row = lambda b, s: (b * steps + s, 0)
    return pl.pallas_call(
        _conv_rg_kernel,
        grid=(batch, steps),
        in_specs=[pl.BlockSpec((TS, REST_WIDTH), row),
                  _const_spec((SC_CONV, SC_WIDTH)), _const_spec((1, SC_WIDTH)),
                  _const_spec((RG_CONV, RG_WIDTH)), _const_spec((1, RG_WIDTH)),
                  _const_spec((RG_WIDTH, 2 * RG_WIDTH)), _const_spec((1, 2 * RG_WIDTH)),
                  _const_spec((1, RG_WIDTH))],
        out_specs=pl.BlockSpec((TS, SC_WIDTH + RG_WIDTH), row),
        out_shape=jax.ShapeDtypeStruct((t, SC_WIDTH + RG_WIDTH), BF16),
        scratch_shapes=[pltpu.VMEM((TS + 2 * CARRY_ROWS, SC_WIDTH), F32),
                        pltpu.VMEM((TS + 2 * CARRY_ROWS, RG_WIDTH), F32),
                        pltpu.VMEM((1, RG_WIDTH), F32)],
        compiler_params=_params(("arbitrary", "arbitrary"), 32),
        name="conv_rglru",
    )(rest, sc_w, sc_b, rg_w, rg_b, w_gate, b_gate, lam)


def _out_proj_kernel(x_ref, a_ref, c_ref, w_ref, o_ref):
    mixed = jnp.concatenate([a_ref[...], c_ref[...]], axis=-1)
    o_ref[...] = x_ref[...] + jnp.dot(mixed, w_ref[...], preferred_element_type=F32)


def _out_proj(x, attn, cr, w):
    t = x.shape[0]
    row = lambda width: pl.BlockSpec((TM, width), lambda i: (i, 0))
    return pl.pallas_call(
        _out_proj_kernel,
        grid=(t // TM,),
        in_specs=[row(D_MODEL), row(ATTN_WIDTH), row(SC_WIDTH + RG_WIDTH), _const_spec((D_MODEL, D_MODEL))],
        out_specs=row(D_MODEL),
        out_shape=jax.ShapeDtypeStruct((t, D_MODEL), F32),
        compiler_params=_params(("parallel",), 24),
        name="out_proj",
    )(x, attn, cr, w)


def _mem_kv_kernel(m_ref, g_ref, w_ref, k_ref, v_ref):
    h = _rms(m_ref[...], g_ref[...]).astype(BF16)
    p = jnp.dot(h, w_ref[...], preferred_element_type=F32)
    k_ref[...] = p[:, :XA_WIDTH].astype(BF16)
    v_ref[...] = p[:, XA_WIDTH:].astype(BF16)


def _mem_kv(mem, g, wkv):
    t = mem.shape[0]
    row = lambda width: pl.BlockSpec((MEM_LEN, width), lambda i: (i, 0))
    return pl.pallas_call(
        _mem_kv_kernel,
        grid=(t // MEM_LEN,),
        in_specs=[row(D_MODEL), _const_spec((1, D_MODEL)), _const_spec((D_MODEL, 2 * XA_WIDTH))],
        out_specs=[row(XA_WIDTH), row(XA_WIDTH)],
        out_shape=[jax.ShapeDtypeStruct((t, XA_WIDTH), BF16)] * 2,
        compiler_params=_params(("parallel",), 24),
        name="mem_kv",
    )(mem, g, wkv)


def _xattn_kernel(x_ref, g_ref, wq_ref, k_ref, v_ref, wo_ref, o_ref):
    x = x_ref[...]
    h = _rms(x, g_ref[...]).astype(BF16)
    q = jnp.dot(h, wq_ref[...], preferred_element_type=F32).astype(BF16)
    k = k_ref[...]
    v = v_ref[...]
    outs = []
    for hd in range(XA_HEADS):
        sl = slice(hd * XA_HEAD_DIM, (hd + 1) * XA_HEAD_DIM)
        s = lax.dot_general(q[:, sl], k[:, sl], (((1,), (1,)), ((), ())), preferred_element_type=F32)
        s = s * (1.0 / math.sqrt(XA_HEAD_DIM))
        p = jnp.exp(s - jnp.max(s, axis=-1, keepdims=True))
        o = jnp.dot(p.astype(BF16), v[:, sl], preferred_element_type=F32)
        outs.append(o / jnp.sum(p, axis=-1, keepdims=True))
    att = jnp.concatenate(outs, axis=-1).astype(BF16)
    o_ref[...] = x + jnp.dot(att, wo_ref[...], preferred_element_type=F32)


def _xattn(x, g, wq, k, v, wo, seq):
    t = x.shape[0]
    per_seq = seq // TM
    row = pl.BlockSpec((TM, D_MODEL), lambda i: (i, 0))
    mem_blk = pl.BlockSpec((MEM_LEN, XA_WIDTH), lambda i: (i // per_seq, 0))
    return pl.pallas_call(
        _xattn_kernel,
        grid=(t // TM,),
        in_specs=[row, _const_spec((1, D_MODEL)), _const_spec((D_MODEL, XA_WIDTH)), mem_blk, mem_blk,
                  _const_spec((XA_WIDTH, D_MODEL))],
        out_specs=row,
        out_shape=jax.ShapeDtypeStruct((t, D_MODEL), F32),
        compiler_params=_params(("parallel",), 32),
        name="mem_xattn",
    )(x, g, wq, k, v, wo)


def _ffn_kernel(x_ref, g_ref, wg_ref, wu_ref, wd_ref, o_ref):
    x = x_ref[...]
    h = _rms(x, g_ref[...]).astype(BF16)
    gate = jnp.dot(h, wg_ref[...], preferred_element_type=F32)
    up = jnp.dot(h, wu_ref[...], preferred_element_type=F32)
    act = (gate * jax.nn.sigmoid(gate) * up).astype(BF16)
    o_ref[...] = x + jnp.dot(act, wd_ref[...], preferred_element_type=F32)


def _dense_ffn(x, g, wg, wu, wd):
    t = x.shape[0]
    d_ff = wg.shape[1]
    row = pl.BlockSpec((TM, D_MODEL), lambda i: (i, 0))
    return pl.pallas_call(
        _ffn_kernel,
        grid=(t // TM,),
        in_specs=[row, _const_spec((1, D_MODEL)), _const_spec((D_MODEL, d_ff)), _const_spec((D_MODEL, d_ff)),
                  _const_spec((d_ff, D_MODEL))],
        out_specs=row,
        out_shape=jax.ShapeDtypeStruct((t, D_MODEL), F32),
        compiler_params=_params(("parallel",), 60),
        name="dense_swiglu",
    )(x, g, wg, wu, wd)


META_E1, META_E2, META_R1, META_R2, META_W1, META_W2 = range(6)


def _to_token_tiles(ref, rows):
    m = rows.shape[0]
    for c in range(ROW_CHUNKS):
        ref[pl.ds(c, m, stride=ROW_CHUNKS), :] = rows[:, c * LANES_V7X:(c + 1) * LANES_V7X]


def _from_token_tiles(ref):
    m = ref.shape[0] // ROW_CHUNKS
    return jnp.concatenate([ref[pl.ds(c, m, stride=ROW_CHUNKS), :] for c in range(ROW_CHUNKS)], axis=-1)


def _router_kernel(x_ref, g_ref, wr_ref, tri_ref, sel_ref, hf_ref, meta_ref, metat_ref, cnt_ref, carry):
    @pl.when(pl.program_id(0) == 0)
    def _():
        carry[...] = jnp.zeros_like(carry)

    h = _rms(x_ref[...], g_ref[...])
    _to_token_tiles(hf_ref, h)
    logits = jnp.dot(h, wr_ref[...], preferred_element_type=F32, precision=lax.Precision.HIGHEST)
    tm = logits.shape[0]
    lane = lax.broadcasted_iota(jnp.int32, (tm, LANES_V7X), 1)
    lg = jnp.where(lane < N_EXPERTS, logits, -jnp.inf)
    m1 = jnp.max(lg, axis=-1, keepdims=True)
    e1 = jnp.min(jnp.where(lg == m1, lane, LANES_V7X), axis=-1, keepdims=True)
    lg2 = jnp.where(lane == e1, -jnp.inf, lg)
    m2 = jnp.max(lg2, axis=-1, keepdims=True)
    e2 = jnp.min(jnp.where(lg2 == m2, lane, LANES_V7X), axis=-1, keepdims=True)
    ex = jnp.exp(m2 - m1)
    w1 = 1.0 / (1.0 + ex)
    w2 = ex / (1.0 + ex)

    hit1 = lane == e1
    hit2 = lane == e2
    onehot = (hit1 | hit2).astype(BF16)
    ahead = jnp.dot(tri_ref[...], onehot, preferred_element_type=F32) + carry[...]
    r1 = jnp.sum(jnp.where(hit1, ahead, 0.0), axis=-1, keepdims=True)
    r2 = jnp.sum(jnp.where(hit2, ahead, 0.0), axis=-1, keepdims=True)
    carry[...] = carry[...] + jnp.sum(onehot.astype(F32), axis=0, keepdims=True)

    meta = jnp.zeros((tm, LANES_V7X), F32)
    for col, val in ((META_E1, e1.astype(F32)), (META_E2, e2.astype(F32)), (META_R1, r1), (META_R2, r2),
                     (META_W1, w1), (META_W2, w2)):
        meta = jnp.where(lane == col, val, meta)
    meta_ref[...] = meta
    metat_ref[...] = lax.dot_general(sel_ref[...], meta, (((1,), (1,)), ((), ())), preferred_element_type=F32,
                                     precision=lax.Precision.HIGHEST)
    cnt_ref[...] = carry[...]


def _router(x, g, wr_pad, tri, sel):
    t = x.shape[0]
    row = lambda width: pl.BlockSpec((TM_ROUTE, width), lambda i: (i, 0))
    return pl.pallas_call(
        _router_kernel,
        grid=(t // TM_ROUTE,),
        in_specs=[row(D_MODEL), _const_spec((1, D_MODEL)), _const_spec((D_MODEL, LANES_V7X)),
                  _const_spec((TM_ROUTE, TM_ROUTE)), _const_spec((SUBLANES_V7X, LANES_V7X))],
        out_specs=[pl.BlockSpec((TM_ROUTE * ROW_CHUNKS, LANES_V7X), lambda i: (i, 0)), row(LANES_V7X),
                   pl.BlockSpec((SUBLANES_V7X, TM_ROUTE), lambda i: (0, i)), _const_spec((1, LANES_V7X))],
        out_shape=[jax.ShapeDtypeStruct((t * ROW_CHUNKS, LANES_V7X), F32),
                   jax.ShapeDtypeStruct((t, LANES_V7X), F32),
                   jax.ShapeDtypeStruct((SUBLANES_V7X, t), F32),
                   jax.ShapeDtypeStruct((1, LANES_V7X), F32)],
        scratch_shapes=[pltpu.VMEM((1, LANES_V7X), F32)],
        compiler_params=_params(("arbitrary",), 32),
        name="moe_router",
    )(x, g, wr_pad, tri, sel)


def _token_rows(ref, first_token, n_tokens):
    start = pl.multiple_of(first_token * ROW_CHUNKS, ROW_CHUNKS)
    return ref.at[pl.ds(start, n_tokens * ROW_CHUNKS)]


def _row_copy(src_ref, src_token, dst_ref, dst_token, sem):
    return pltpu.make_async_copy(_token_rows(src_ref, src_token, 1), _token_rows(dst_ref, dst_token, 1), sem)


def _dispatch_kernel(fill_ref, pos1_ref, pos2_ref, hf_ref, xs_ref, zeros, sem_z, sem):
    @pl.when(pl.program_id(0) == 0)
    def _():
        zeros[...] = jnp.zeros_like(zeros)

        def tile_fill(e):
            return pltpu.make_async_copy(zeros, _token_rows(xs_ref, pl.multiple_of(fill_ref[e], TM_GROUP), TM_GROUP),
                                         sem_z)

        for e in range(2 * N_EXPERTS):
            @pl.when(fill_ref[e] >= 0)
            def _():
                tile_fill(e).start()
        for e in range(2 * N_EXPERTS):
            @pl.when(fill_ref[e] >= 0)
            def _():
                tile_fill(e).wait()

    def issue(r, carry):
        _row_copy(hf_ref, r, xs_ref, pos1_ref[r], sem).start(priority=0)
        _row_copy(hf_ref, r, xs_ref, pos2_ref[r], sem).start(priority=1)
        return carry

    lax.fori_loop(0, TM_MOVE, issue, 0, unroll=ISSUE_UNROLL)
    whole_step = pltpu.make_async_copy(hf_ref, _token_rows(xs_ref, 0, TM_MOVE), sem)
    whole_step.wait()
    whole_step.wait()


def _dispatch(fill_start, pos1, pos2, hf, n_rows):
    t = hf.shape[0] // ROW_CHUNKS
    idx = pl.BlockSpec((TM_MOVE,), lambda i, fill: (i,), memory_space=pltpu.SMEM)
    return pl.pallas_call(
        _dispatch_kernel,
        grid_spec=pltpu.PrefetchScalarGridSpec(
            num_scalar_prefetch=1,
            grid=(t // TM_MOVE,),
            in_specs=[idx, idx, pl.BlockSpec((TM_MOVE * ROW_CHUNKS, LANES_V7X), lambda i, fill: (i, 0))],
            out_specs=pl.BlockSpec(memory_space=pl.ANY),
            scratch_shapes=[pltpu.VMEM((TM_GROUP * ROW_CHUNKS, LANES_V7X), F32), pltpu.SemaphoreType.DMA(()),
                            pltpu.SemaphoreType.DMA(())]),
        out_shape=jax.ShapeDtypeStruct((n_rows * ROW_CHUNKS, LANES_V7X), F32),
        compiler_params=_params(("arbitrary",), 24),
        name="moe_dispatch",
    )(fill_start, pos1, pos2, hf)


def _grouped_kernel(te_ref, used_ref, x_ref, wg_ref, wu_ref, wd_ref, y_ref, acc):
    i = pl.program_id(0)
    j = pl.program_id(1)
    last = FF_SPLIT - 1

    @pl.when(i < used_ref[0])
    def _():
        h = _from_token_tiles(x_ref).astype(BF16)
        gate = jnp.dot(h, wg_ref[...], preferred_element_type=F32)
        up = jnp.dot(h, wu_ref[...], preferred_element_type=F32)
        act = (gate * jax.nn.sigmoid(gate) * up).astype(BF16)
        part = jnp.dot(act, wd_ref[...], preferred_element_type=F32)

        @pl.when(j == 0)
        def _():
            acc[...] = part

        @pl.when((j > 0) & (j < last))
        def _():
            acc[...] += part

        @pl.when(j == last)
        def _():
            _to_token_tiles(y_ref, part if FF_SPLIT == 1 else acc[...] + part)

    @pl.when((i >= used_ref[0]) & (j == last))
    def _():
        y_ref[...] = jnp.zeros_like(y_ref)


def _grouped_swiglu(tile_expert, n_used, xs, wg, wu, wd):
    n_rows = xs.shape[0] // ROW_CHUNKS
    d_ff = wg.shape[2]
    ff = d_ff // FF_SPLIT
    tile = (TM_GROUP * ROW_CHUNKS, LANES_V7X)
    src = lambda i, j, te, used: (jnp.maximum(jnp.minimum(i, used[0] - 1), 0), 0)
    return pl.pallas_call(
        _grouped_kernel,
        grid_spec=pltpu.PrefetchScalarGridSpec(
            num_scalar_prefetch=2,
            grid=(n_rows // TM_GROUP, FF_SPLIT),
            in_specs=[pl.BlockSpec(tile, src),
                      pl.BlockSpec((None, D_MODEL, ff), lambda i, j, te, used: (te[i], 0, j)),
                      pl.BlockSpec((None, D_MODEL, ff), lambda i, j, te, used: (te[i], 0, j)),
                      pl.BlockSpec((None, ff, D_MODEL), lambda i, j, te, used: (te[i], j, 0))],
            out_specs=pl.BlockSpec(tile, lambda i, j, te, used: (i, 0)),
            scratch_shapes=[pltpu.VMEM((TM_GROUP, D_MODEL), F32)]),
        out_shape=jax.ShapeDtypeStruct((n_rows * ROW_CHUNKS, LANES_V7X), F32),
        compiler_params=_params(("arbitrary", "arbitrary"), 56),
        name="moe_grouped_swiglu",
    )(tile_expert, n_used, xs, wg, wu, wd)


def _combine_kernel(pos1_ref, pos2_ref, pos1_next_ref, pos2_next_ref, x_ref, meta_ref, g_ref, y_ref, o_ref,
                    y1, y2, sems):
    i = pl.program_id(0)

    def gather(p1_ref, p2_ref, slot):
        def issue(r, carry):
            _row_copy(y_ref, p1_ref[r], y1.at[slot], r, sems.at[slot]).start(priority=0)
            _row_copy(y_ref, p2_ref[r], y2.at[slot], r, sems.at[slot]).start(priority=1)
            return carry
        lax.fori_loop(0, TM_COMBINE, issue, 0, unroll=ISSUE_UNROLL)

    def finish(slot):
        for buf in (y1, y2):
            pltpu.make_async_copy(_token_rows(y_ref, 0, TM_COMBINE), buf.at[slot], sems.at[slot]).wait()
        w1 = meta_ref[:, META_W1:META_W1 + 1]
        w2 = meta_ref[:, META_W2:META_W2 + 1]
        out = x_ref[...] + (w1 * _from_token_tiles(y1.at[slot]) + w2 * _from_token_tiles(y2.at[slot]))
        o_ref[...] = _rms(out, g_ref[...])

    @pl.when(i == 0)
    def _():
        gather(pos1_ref, pos2_ref, 0)

    for slot in range(2):
        @pl.when(i % 2 == slot)
        def _():
            @pl.when(i + 1 < pl.num_programs(0))
            def _():
                gather(pos1_next_ref, pos2_next_ref, 1 - slot)
            finish(slot)


def _combine(pos1, pos2, x, meta, g, y):
    t = x.shape[0]
    steps = t // TM_COMBINE
    idx = pl.BlockSpec((TM_COMBINE,), lambda i: (i,), memory_space=pltpu.SMEM)
    idx_next = pl.BlockSpec((TM_COMBINE,), lambda i: (jnp.minimum(i + 1, steps - 1),), memory_space=pltpu.SMEM)
    row = lambda width: pl.BlockSpec((TM_COMBINE, width), lambda i: (i, 0))
    slots = pltpu.VMEM((2, TM_COMBINE * ROW_CHUNKS, LANES_V7X), F32)
    return pl.pallas_call(
        _combine_kernel,
        grid=(steps,),
        in_specs=[idx, idx, idx_next, idx_next, row(D_MODEL), row(LANES_V7X), _const_spec((1, D_MODEL)),
                  pl.BlockSpec(memory_space=pl.ANY)],
        out_specs=row(D_MODEL),
        out_shape=jax.ShapeDtypeStruct((t, D_MODEL), F32),
        scratch_shapes=[slots, slots, pltpu.SemaphoreType.DMA((2,))],
        compiler_params=_params(("arbitrary",), 24),
        name="moe_combine_norm",
    )(pos1, pos2, pos1, pos2, x, meta, g, y)


def _moe_layer(x, ffn_g, router_w, wg, wu, wd, final_g):
    t = x.shape[0]
    n_rows = 2 * t + N_EXPERTS * TM_GROUP
    wr_pad = jnp.zeros((D_MODEL, LANES_V7X), F32).at[:, :N_EXPERTS].set(router_w)
    tri = jnp.tril(jnp.ones((TM_ROUTE, TM_ROUTE), BF16), -1)
    sel = jnp.eye(SUBLANES_V7X, LANES_V7X, dtype=F32)
    hf, meta, metat, counts = _router(x, ffn_g, wr_pad, tri, sel)

    cnt = counts[0, :N_EXPERTS].astype(jnp.int32)
    padded = (cnt + TM_GROUP - 1) // TM_GROUP * TM_GROUP
    ends = jnp.cumsum(padded)
    starts = ends - padded
    experts = jnp.arange(N_EXPERTS, dtype=jnp.int32)[:, None]
    group_start = lambda e: jnp.sum(jnp.where(e[None, :] == experts, starts[:, None], 0), axis=0)
    pos1 = group_start(metat[META_E1].astype(jnp.int32)) + metat[META_R1].astype(jnp.int32)
    pos2 = group_start(metat[META_E2].astype(jnp.int32)) + metat[META_R2].astype(jnp.int32)
    n_used = (ends[-1] // TM_GROUP).astype(jnp.int32).reshape(1)
    tile_row = jnp.minimum(jnp.arange(n_rows // TM_GROUP, dtype=jnp.int32), n_used[0] - 1) * TM_GROUP
    tile_expert = jnp.sum(ends[None, :] <= tile_row[:, None], axis=1).astype(jnp.int32)
    tail_start = jnp.where(padded > 0, ends - TM_GROUP, -1)
    spare = ends[-1] + TM_GROUP * jnp.arange(N_EXPERTS, dtype=jnp.int32)
    fill_start = jnp.concatenate([tail_start, jnp.where(spare < n_rows, spare, -1)]).astype(jnp.int32)

    xs = _dispatch(fill_start, pos1, pos2, hf, n_rows)
    y = _grouped_swiglu(tile_expert, n_used, xs, wg, wu, wd)
    return _combine(pos1, pos2, x, meta, final_g, y)


def _arrange_in_proj(w):
    q = w[:, :ATTN_WIDTH] * (1.0 / math.sqrt(HEAD_DIM))
    dup = lambda start: [w[:, start + h * HEAD_DIM:start + (h + 1) * HEAD_DIM]
                         for h in range(ATTN_KV_HEADS) for _ in range(2)]
    return jnp.concatenate([q] + dup(ATTN_WIDTH) + dup(ATTN_WIDTH + KV_WIDTH) + [w[:, ATTN_WIDTH + 2 * KV_WIDTH:]],
                           axis=1)


def _block_diag(w):
    heads, d, _ = w.shape
    eye = jnp.eye(heads, dtype=w.dtype)
    return jnp.einsum('hij,hg->higj', w, eye).reshape(heads * d, heads * d)


def kernel(x, mem, rel_bias, mix_norm, w_in, attn_sinks, sc_conv_w, sc_conv_b, rg_conv_w, rg_conv_b, rg_w_a,
           rg_b_a, rg_w_x, rg_b_x, rg_lambda, w_out, xa_norm, mem_norm, xa_wq, xa_wk, xa_wv, xa_wo, ffn_norm,
           dense_wg, dense_wu, dense_wd, moe_router, moe_wg, moe_wu, moe_wd, final_norm):
    batch, seq, _ = x.shape
    depth = w_in.shape[0]
    assert depth == 2 and seq % TS == 0 and seq % TM == 0 and (batch * seq) % TM_ROUTE == 0
    xt = x.reshape(batch * seq, D_MODEL)
    memt = mem.reshape(batch * MEM_LEN, D_MODEL)
    bias_tbl = _attention_bias_tables(rel_bias)
    vec = lambda a: a.reshape(1, -1)

    for layer in range(depth):
        q, k, v, rest = _in_proj(xt, vec(mix_norm[layer]), _arrange_in_proj(w_in[layer]).astype(BF16))
        attn = _attention(q, k, v, attn_sinks[layer], bias_tbl, seq // BLOCK)
        w_gate = jnp.concatenate([_block_diag(rg_w_a[layer]), _block_diag(rg_w_x[layer])], axis=1).astype(BF16)
        b_gate = jnp.concatenate([rg_b_a[layer], rg_b_x[layer]]).reshape(1, -1)
        cr = _conv_rg(rest, sc_conv_w[layer], vec(sc_conv_b[layer]), rg_conv_w[layer], vec(rg_conv_b[layer]),
                      w_gate, b_gate, vec(rg_lambda[layer]), batch, seq)
        xt = _out_proj(xt, attn, cr, w_out[layer].astype(BF16))

        wkv = jnp.concatenate([xa_wk[layer], xa_wv[layer]], axis=1).astype(BF16)
        mk, mv = _mem_kv(memt, vec(mem_norm[layer]), wkv)
        xt = _xattn(xt, vec(xa_norm[layer]), xa_wq[layer].astype(BF16), mk, mv, xa_wo[layer].astype(BF16), seq)

        j = layer // 2
        if layer % 2 == 0:
            xt = _dense_ffn(xt, vec(ffn_norm[layer]), dense_wg[j].astype(BF16), dense_wu[j].astype(BF16),
                            dense_wd[j].astype(BF16))
        else:
            xt = _moe_layer(xt, vec(ffn_norm[layer]), moe_router[j], moe_wg[j].astype(BF16),
                            moe_wu[j].astype(BF16), moe_wd[j].astype(BF16), vec(final_norm))
    return xt.reshape(batch, seq, D_MODEL)
```

```python
import functools
import math

import jax
import jax.numpy as jnp
import numpy as np
from jax import lax
from jax.experimental import pallas as pl
from jax.experimental.pallas import tpu as pltpu

F32 = jnp.float32
BF16 = jnp.bfloat16

D_MODEL = 1024
MEM_LEN = 256
HEAD_DIM = 64
ATTN_Q_HEADS = 8
ATTN_KV_HEADS = 2
ATTN_WIDTH = ATTN_Q_HEADS * HEAD_DIM
KV_WIDTH = ATTN_KV_HEADS * HEAD_DIM
KV_DUP_WIDTH = 2 * KV_WIDTH
BLOCK = 128
SC_WIDTH = 256
SC_CONV = 3
RG_WIDTH = 256
RG_HEADS = 4
RG_HEAD_DIM = RG_WIDTH // RG_HEADS
RG_CONV = 4
RG_C = 8.0
N_BUCKETS = 32
MAX_EXACT = N_BUCKETS // 2
MAX_DISTANCE = 128
XA_HEADS = 4
XA_HEAD_DIM = 128
XA_WIDTH = XA_HEADS * XA_HEAD_DIM
N_EXPERTS = 8
EPS = 1e-6
NEG_INF = -1e30
REST_WIDTH = 3 * SC_WIDTH + 2 * RG_WIDTH

LANES_V7X = 128
SUBLANES_V7X = 8
VMEM_BYTES_V7X = 64 * 1024 * 1024
ROW_CHUNKS = D_MODEL // LANES_V7X
assert ROW_CHUNKS == SUBLANES_V7X

TM = 512
ATTN_BLOCKS = 4
TS = 512
ROUTE_CHUNK = 128
TM_GROUP = 512
TM_MOVE = 1024
TM_COMBINE = 256
ISSUE_UNROLL = 8
FF_SPLIT = 2
CARRY_ROWS = SUBLANES_V7X


def _mib(n):
    return int(n * 1024 * 1024)


def _params(semantics, vmem_mib):
    assert _mib(vmem_mib) < VMEM_BYTES_V7X
    return pltpu.CompilerParams(dimension_semantics=semantics, vmem_limit_bytes=_mib(vmem_mib))


def _rms(x, g):
    ms = jnp.mean(x * x, axis=-1, keepdims=True)
    return x * lax.rsqrt(ms + EPS) * g


def _const_spec(shape):
    nd = len(shape)
    return pl.BlockSpec(shape, lambda *_: (0,) * nd, pipeline_mode=pl.Buffered(1))


def _in_proj_kernel(x_ref, g_ref, w_ref, q_ref, k_ref, v_ref, r_ref):
    h = _rms(x_ref[...], g_ref[...]).astype(BF16)
    p = jnp.dot(h, w_ref[...], preferred_element_type=F32)
    q_ref[...] = p[:, :ATTN_WIDTH].astype(BF16)
    k_ref[...] = p[:, ATTN_WIDTH:ATTN_WIDTH + KV_DUP_WIDTH].astype(BF16)
    v_ref[...] = p[:, ATTN_WIDTH + KV_DUP_WIDTH:ATTN_WIDTH + 2 * KV_DUP_WIDTH].astype(BF16)
    r_ref[...] = p[:, ATTN_WIDTH + 2 * KV_DUP_WIDTH:]


def _in_proj(x, g, w):
    t = x.shape[0]
    n = w.shape[1]
    row = lambda width: pl.BlockSpec((TM, width), lambda i: (i, 0))
    return pl.pallas_call(
        _in_proj_kernel,
        grid=(t // TM,),
        in_specs=[row(D_MODEL), _const_spec((1, D_MODEL)), _const_spec((D_MODEL, n))],
        out_specs=[row(ATTN_WIDTH), row(KV_DUP_WIDTH), row(KV_DUP_WIDTH), row(REST_WIDTH)],
        out_shape=[jax.ShapeDtypeStruct((t, ATTN_WIDTH), BF16),
                   jax.ShapeDtypeStruct((t, KV_DUP_WIDTH), BF16),
                   jax.ShapeDtypeStruct((t, KV_DUP_WIDTH), BF16),
                   jax.ShapeDtypeStruct((t, REST_WIDTH), F32)],
        compiler_params=_params(("parallel",), 40),
        name="in_proj",
    )(x, g, w)


def _attn_kernel(sink_ref, q_ref, kp_ref, kc_ref, vp_ref, vc_ref, bias0_ref, bias_ref, o_ref):
    pairs_per_group = ATTN_Q_HEADS // ATTN_KV_HEADS // 2
    row = lax.broadcasted_iota(jnp.int32, (BLOCK, BLOCK), 0)
    col = lax.broadcasted_iota(jnp.int32, (BLOCK, BLOCK), 1)
    from_prev = col > row
    low_lanes = lax.broadcasted_iota(jnp.int32, (2 * BLOCK, 2 * HEAD_DIM), 1) < HEAD_DIM
    low_out = col < HEAD_DIM
    zero = jnp.zeros((), BF16)

    def block_diag(band):
        return jnp.concatenate([jnp.where(low_lanes, band, zero), jnp.where(low_lanes, zero, band)], axis=0)

    def scores(blk):
        rows = slice(blk * BLOCK, (blk + 1) * BLOCK)
        prev_rows = slice((blk - 1) * BLOCK, blk * BLOCK)
        out = []
        for g in range(ATTN_KV_HEADS):
            lanes = slice(g * 2 * HEAD_DIM, (g + 1) * 2 * HEAD_DIM)
            k_prev = kp_ref[:, lanes] if blk == 0 else kc_ref[prev_rows, lanes]
            v_prev = vp_ref[:, lanes] if blk == 0 else vc_ref[prev_rows, lanes]
            k_bd = block_diag(jnp.concatenate([k_prev, kc_ref[rows, lanes]], axis=0))
            v_bd = block_diag(jnp.concatenate([v_prev, vc_ref[rows, lanes]], axis=0))
            for pair in range(pairs_per_group):
                slab = g * pairs_per_group + pair
                q2 = q_ref[rows, slab * 2 * HEAD_DIM:(slab + 1) * 2 * HEAD_DIM]
                s = lax.dot_general(q2, k_bd, (((1,), (1,)), ((), ())), preferred_element_type=F32)
                out.append((slab, s, v_bd))
        return out

    def finish(blk, scored):
        rows = slice(blk * BLOCK, (blk + 1) * BLOCK)
        tbl_ref = bias0_ref if blk == 0 else bias_ref
        staged = []
        for slab, s, v_bd in scored:
            probs, denoms = [], []
            for side in range(2):
                h = 2 * slab + side
                sh = s[:, side * 2 * BLOCK:(side + 1) * 2 * BLOCK]
                logits = jnp.where(from_prev, sh[:, :BLOCK], sh[:, BLOCK:]) + tbl_ref[h]
                sink = sink_ref[h]
                m = jnp.maximum(jnp.max(logits, axis=-1, keepdims=True), sink)
                p = jnp.exp(logits - m)
                denoms.append(jnp.sum(p, axis=-1, keepdims=True) + jnp.exp(sink - m))
                probs += [jnp.where(from_prev, p, 0.0), jnp.where(from_prev, 0.0, p)]
            staged.append((slab, jnp.concatenate(probs, axis=-1).astype(BF16), v_bd, denoms))
        for slab, p_band, v_bd, denoms in staged:
            o = jnp.dot(p_band, v_bd, preferred_element_type=F32)
            o = o / jnp.where(low_out, denoms[0], denoms[1])
            o_ref[rows, slab * 2 * HEAD_DIM:(slab + 1) * 2 * HEAD_DIM] = o.astype(BF16)

    pending = scores(0)
    for blk in range(ATTN_BLOCKS):
        upcoming = scores(blk + 1) if blk + 1 < ATTN_BLOCKS else None
        finish(blk, pending)
        pending = upcoming


def _attention(q, k, v, sinks, bias_tbl, blocks_per_seq):
    t = q.shape[0]
    tile = ATTN_BLOCKS * BLOCK
    cur = lambda i: (i, 0)
    prev = lambda i: (jnp.maximum(i * ATTN_BLOCKS - 1, 0), 0)
    tbl = (None, ATTN_Q_HEADS, BLOCK, BLOCK)
    return pl.pallas_call(
        _attn_kernel,
        grid=(t // tile,),
        in_specs=[pl.BlockSpec(memory_space=pltpu.SMEM),
                  pl.BlockSpec((tile, ATTN_WIDTH), cur),
                  pl.BlockSpec((BLOCK, KV_DUP_WIDTH), prev),
                  pl.BlockSpec((tile, KV_DUP_WIDTH), cur),
                  pl.BlockSpec((BLOCK, KV_DUP_WIDTH), prev),
                  pl.BlockSpec((tile, KV_DUP_WIDTH), cur),
                  pl.BlockSpec(tbl, lambda i: (jnp.minimum((i * ATTN_BLOCKS) % blocks_per_seq, 1), 0, 0, 0)),
                  pl.BlockSpec(tbl, lambda i: (1, 0, 0, 0))],
        out_specs=pl.BlockSpec((tile, ATTN_WIDTH), cur),
        out_shape=jax.ShapeDtypeStruct((t, ATTN_WIDTH), BF16),
        compiler_params=_params(("parallel",), 24),
        name="swa_attention",
    )(sinks, q, k, k, v, v, bias_tbl, bias_tbl)


def _bias_table_kernel(rel_ref, bucket_ref, o_ref):
    for v in range(2):
        bucket = bucket_ref[v]
        hits = [bucket == b for b in range(N_BUCKETS)]
        for h in range(ATTN_Q_HEADS):
            tbl = jnp.full(bucket.shape, NEG_INF, F32)
            for b in range(N_BUCKETS):
                tbl = jnp.where(hits[b], rel_ref[b * ATTN_Q_HEADS + h], tbl)
            o_ref[v, h] = tbl


def _attention_bias_tables(rel_bias):
    q_idx = np.arange(BLOCK)[:, None]
    j_idx = np.arange(BLOCK)[None, :]
    from_prev = j_idx > q_idx
    n = np.where(from_prev, q_idx + BLOCK - j_idx, q_idx - j_idx)
    large = MAX_EXACT + (np.log(np.maximum(n, 1).astype(np.float32) / np.float32(MAX_EXACT))
                         / np.float32(math.log(MAX_DISTANCE / MAX_EXACT))
                         * np.float32(N_BUCKETS - MAX_EXACT)).astype(np.int32)
    bucket = np.where(n < MAX_EXACT, n, np.minimum(large, N_BUCKETS - 1))
    first = np.where(from_prev, -1, bucket)
    buckets = jnp.asarray(np.stack([first, bucket]).astype(np.int32))
    return pl.pallas_call(
        _bias_table_kernel,
        in_specs=[pl.BlockSpec(memory_space=pltpu.SMEM), pl.BlockSpec(memory_space=pltpu.VMEM)],
        out_specs=pl.BlockSpec(memory_space=pltpu.VMEM),
        out_shape=jax.ShapeDtypeStruct((2, ATTN_Q_HEADS, BLOCK, BLOCK), F32),
        name="t5_bias_table",
    )(rel_bias.astype(F32).reshape(-1), buckets)


def _shift_rows(x, s, fill):
    return jnp.concatenate([jnp.full((s, x.shape[1]), fill, x.dtype), x[:x.shape[0] - s]], axis=0)


def _conv_rg_kernel(r_ref, scw_ref, scb_ref, rgw_ref, rgb_ref, wgate_ref, bgate_ref, lam_ref,
                    o_ref, sc_ext, rg_ext, h_carry):
    ts = r_ref.shape[0]
    c0 = CARRY_ROWS

    @pl.when(pl.program_id(1) == 0)
    def _():
        sc_ext[0:c0, :] = jnp.zeros((c0, SC_WIDTH), F32)
        rg_ext[0:c0, :] = jnp.zeros((c0, RG_WIDTH), F32)
        h_carry[...] = jnp.zeros_like(h_carry)

    sc_b = r_ref[:, 0:SC_WIDTH]
    sc_ext[c0:c0 + ts, :] = r_ref[:, SC_WIDTH:2 * SC_WIDTH] * r_ref[:, 2 * SC_WIDTH:3 * SC_WIDTH]
    rg_ext[c0:c0 + ts, :] = r_ref[:, 3 * SC_WIDTH:3 * SC_WIDTH + RG_WIDTH]
    rg_g = r_ref[:, 3 * SC_WIDTH + RG_WIDTH:]

    conv = scb_ref[...]
    for k in range(SC_CONV):
        off = c0 - (SC_CONV - 1) + k
        conv = conv + scw_ref[k:k + 1, :] * sc_ext[off:off + ts, :]
    conv_out = sc_b * conv

    rg_in = rgb_ref[...]
    for k in range(RG_CONV):
        off = c0 - (RG_CONV - 1) + k
        rg_in = rg_in + rgw_ref[k:k + 1, :] * rg_ext[off:off + ts, :]

    sc_ext[0:c0, :] = sc_ext[ts:ts + c0, :]
    rg_ext[0:c0, :] = rg_ext[ts:ts + c0, :]

    gates = jnp.dot(rg_in.astype(BF16), wgate_ref[...], preferred_element_type=F32) + bgate_ref[...]
    r_gate = jax.nn.sigmoid(gates[:, :RG_WIDTH])
    i_gate = jax.nn.sigmoid(gates[:, RG_WIDTH:])
    neg_lam = -lam_ref[...]
    softplus = jnp.maximum(neg_lam, 0.0) + jnp.log1p(jnp.exp(-jnp.abs(neg_lam)))
    log_a = -RG_C * r_gate * softplus
    a = jnp.exp(log_a)
    u = jnp.sqrt(jnp.tanh(-log_a) * (1.0 + a * a)) * (i_gate * rg_in)

    s = 1
    while s < ts:
        u = a * _shift_rows(u, s, 0.0) + u
        a = a * _shift_rows(a, s, 1.0)
        s *= 2
    h = a * h_carry[...] + u
    h_carry[...] = h[ts - 1:ts, :]

    c = math.sqrt(2.0 / math.pi)
    gelu = 0.5 * rg_g * (1.0 + jnp.tanh(c * (rg_g + 0.044715 * (rg_g * rg_g * rg_g))))
    o_ref[:, 0:SC_WIDTH] = conv_out.astype(BF16)
    o_ref[:, SC_WIDTH:] = (h * gelu).astype(BF16)


def _conv_rg(rest, sc_w, sc_b, rg_w, rg_b, w_gate, b_gate, lam, batch, seq):
    t = rest.shape[0]
    steps = seq // TS
    row = lambda b, s: (b * steps + s, 0)
    return pl.pallas_call(
        _conv_rg_kernel,
        grid=(batch, steps),
        in_specs=[pl.BlockSpec((TS, REST_WIDTH), row),
                  _const_spec((SC_CONV, SC_WIDTH)), _const_spec((1, SC_WIDTH)),
                  _const_spec((RG_CONV, RG_WIDTH)), _const_spec((1, RG_WIDTH)),
                  _const_spec((RG_WIDTH, 2 * RG_WIDTH)), _const_spec((1, 2 * RG_WIDTH)),
                  _const_spec((1, RG_WIDTH))],
        out_specs=pl.BlockSpec((TS, SC_WIDTH + RG_WIDTH), row),
        out_shape=jax.ShapeDtypeStruct((t, SC_WIDTH + RG_WIDTH), BF16),
        scratch_shapes=[pltpu.VMEM((TS + 2 * CARRY_ROWS, SC_WIDTH), F32),
                        pltpu.VMEM((TS + 2 * CARRY_ROWS, RG_WIDTH), F32),
                        pltpu.VMEM((1, RG_WIDTH), F32)],
        compiler_params=_params(("arbitrary", "arbitrary"), 32),
        name="conv_rglru",
    )(rest, sc_w, sc_b, rg_w, rg_b, w_gate, b_gate, lam)


def _mem_kv_kernel(m_ref, g_ref, w_ref, k_ref, v_ref):
    h = _rms(m_ref[...], g_ref[...]).astype(BF16)
    p = jnp.dot(h, w_ref[...], preferred_element_type=F32)
    k_ref[...] = p[:, :XA_WIDTH].astype(BF16)
    v_ref[...] = p[:, XA_WIDTH:].astype(BF16)


def _mem_kv(mem, g, wkv):
    t = mem.shape[0]
    row = lambda width: pl.BlockSpec((MEM_LEN, width), lambda i: (i, 0))
    return pl.pallas_call(
        _mem_kv_kernel,
        grid=(t // MEM_LEN,),
        in_specs=[row(D_MODEL), _const_spec((1, D_MODEL)), _const_spec((D_MODEL, 2 * XA_WIDTH))],
        out_specs=[row(XA_WIDTH), row(XA_WIDTH)],
        out_shape=[jax.ShapeDtypeStruct((t, XA_WIDTH), BF16)] * 2,
        compiler_params=_params(("parallel",), 24),
        name="mem_kv",
    )(mem, g, wkv)


def _post_mixer_kernel(x_ref, a_ref, c_ref, wout_ref, g_ref, wq_ref, k_ref, v_ref, wo_ref, *rest):
    o_ref = rest[-1] if len(rest) == 1 else rest[4]
    halves = [slice(i * (TM // 2), (i + 1) * (TM // 2)) for i in range(2)]
    heads = [slice(hd * XA_HEAD_DIM, (hd + 1) * XA_HEAD_DIM) for hd in range(XA_HEADS)]
    nt = (((1,), (1,)), ((), ()))
    k = k_ref[...]
    v = v_ref[...]

    x1 = [x_ref[hs, :] + jnp.dot(jnp.concatenate([a_ref[hs, :], c_ref[hs, :]], axis=-1), wout_ref[...],
                                 preferred_element_type=F32) for hs in halves]
    q = [jnp.dot(_rms(xh, g_ref[...]).astype(BF16), wq_ref[...], preferred_element_type=F32).astype(BF16)
         for xh in x1]
    scores = [[lax.dot_general(qh[:, sl], k[:, sl], nt, preferred_element_type=F32) for sl in heads] for qh in q]
    x2 = []
    for xh, per_head in zip(x1, scores):
        probs, sums = [], []
        for s in per_head:
            s = s * (1.0 / math.sqrt(XA_HEAD_DIM))
            p = jnp.exp(s - jnp.max(s, axis=-1, keepdims=True))
            probs.append(p.astype(BF16))
            sums.append(jnp.sum(p, axis=-1, keepdims=True))
        att = jnp.concatenate([jnp.dot(p, v[:, sl], preferred_element_type=F32) / l
                               for p, sl, l in zip(probs, heads, sums)], axis=-1).astype(BF16)
        x2.append(xh + jnp.dot(att, wo_ref[...], preferred_element_type=F32))

    for hs, xh in zip(halves, x2):
        o_ref[hs, :] = xh
    if len(rest) > 1:
        ffn_g_ref, wr_ref, tri_ref, sel_ref, _, hf_ref, meta_ref, metat_ref, cnt_ref, carry = rest
        h = _rms(jnp.concatenate(x2, axis=0), ffn_g_ref[...])
        _route(h, wr_ref, tri_ref, sel_ref, hf_ref, meta_ref, metat_ref, cnt_ref, carry)


def _post_mixer(x, attn, cr, w_out, g, wq, k, v, wo, seq, route=None):
    t = x.shape[0]
    per_seq = seq // TM
    row = lambda width: pl.BlockSpec((TM, width), lambda i: (i, 0))
    mem_blk = pl.BlockSpec((MEM_LEN, XA_WIDTH), lambda i: (i // per_seq, 0))
    in_specs = [row(D_MODEL), row(ATTN_WIDTH), row(SC_WIDTH + RG_WIDTH), _const_spec((D_MODEL, D_MODEL)),
                _const_spec((1, D_MODEL)), _const_spec((D_MODEL, XA_WIDTH)), mem_blk, mem_blk,
                _const_spec((XA_WIDTH, D_MODEL))]
    out_specs = [row(D_MODEL)]
    out_shape = [jax.ShapeDtypeStruct((t, D_MODEL), F32)]
    scratch = []
    args = [x, attn, cr, w_out, g, wq, k, v, wo]
    if route is not None:
        assert TM % ROUTE_CHUNK == 0
        in_specs += [_const_spec((1, D_MODEL)), _const_spec((3 * D_MODEL, LANES_V7X)), _const_spec((ROUTE_CHUNK, ROUTE_CHUNK)),
                     _const_spec((SUBLANES_V7X, LANES_V7X))]
        out_specs += [pl.BlockSpec((TM * ROW_CHUNKS, LANES_V7X), lambda i: (i, 0)), row(LANES_V7X),
                      pl.BlockSpec((SUBLANES_V7X, TM), lambda i: (0, i)),
                      pl.BlockSpec((1, LANES_V7X), lambda i: (0, 0))]
        out_shape += [jax.ShapeDtypeStruct((t * ROW_CHUNKS, LANES_V7X), F32),
                      jax.ShapeDtypeStruct((t, LANES_V7X), F32),
                      jax.ShapeDtypeStruct((SUBLANES_V7X, t), F32),
                      jax.ShapeDtypeStruct((1, LANES_V7X), F32)]
        scratch = [pltpu.VMEM((1, LANES_V7X), F32)]
        args += list(route)
    return pl.pallas_call(
        _post_mixer_kernel,
        grid=(t // TM,),
        in_specs=in_specs,
        out_specs=out_specs,
        out_shape=out_shape,
        scratch_shapes=scratch,
        compiler_params=_params(("arbitrary",), 40),
        name="post_mixer_route" if route is not None else "post_mixer",
    )(*args)


def _ffn_kernel(x_ref, g_ref, wg_ref, wu_ref, wd_ref, o_ref):
    x = x_ref[...]
    h = _rms(x, g_ref[...]).astype(BF16)
    gate = jnp.dot(h, wg_ref[...], preferred_element_type=F32)
    up = jnp.dot(h, wu_ref[...], preferred_element_type=F32)
    act = (gate * jax.nn.sigmoid(gate) * up).astype(BF16)
    o_ref[...] = x + jnp.dot(act, wd_ref[...], preferred_element_type=F32)


def _dense_ffn(x, g, wg, wu, wd):
    t = x.shape[0]
    d_ff = wg.shape[1]
    row = pl.BlockSpec((TM, D_MODEL), lambda i: (i, 0))
    return pl.pallas_call(
        _ffn_kernel,
        grid=(t // TM,),
        in_specs=[row, _const_spec((1, D_MODEL)), _const_spec((D_MODEL, d_ff)), _const_spec((D_MODEL, d_ff)),
                  _const_spec((d_ff, D_MODEL))],
        out_specs=row,
        out_shape=jax.ShapeDtypeStruct((t, D_MODEL), F32),
        compiler_params=_params(("parallel",), 60),
        name="dense_swiglu",
    )(x, g, wg, wu, wd)


META_E1, META_E2, META_R1, META_R2, META_W1, META_W2 = range(6)


def _to_token_tiles(ref, rows):
    m = rows.shape[0]
    for c in range(ROW_CHUNKS):
        ref[pl.ds(c, m, stride=ROW_CHUNKS), :] = rows[:, c * LANES_V7X:(c + 1) * LANES_V7X]


def _from_token_tiles(ref):
    m = ref.shape[0] // ROW_CHUNKS
    return jnp.concatenate([ref[pl.ds(c, m, stride=ROW_CHUNKS), :] for c in range(ROW_CHUNKS)], axis=-1)


def _route(h, wr_ref, tri_ref, sel_ref, hf_ref, meta_ref, metat_ref, cnt_ref, carry):
    @pl.when(pl.program_id(0) == 0)
    def _():
        carry[...] = jnp.zeros_like(carry)

    _to_token_tiles(hf_ref, h)
    chunks = [slice(c * ROUTE_CHUNK, (c + 1) * ROUTE_CHUNK) for c in range(h.shape[0] // ROUTE_CHUNK)]
    each = lambda fn, *lists: [fn(*vals) for vals in zip(*lists)]
    rowmax = lambda a: jnp.max(a, axis=-1, keepdims=True)
    rowsum = lambda a: jnp.sum(a, axis=-1, keepdims=True)
    lane = lax.broadcasted_iota(jnp.int32, (ROUTE_CHUNK, LANES_V7X), 1)
    first_hit = lambda lg, m: jnp.min(jnp.where(lg == m, lane, LANES_V7X), axis=-1, keepdims=True)

    h_hi = h.astype(BF16)
    h_lo = (h - h_hi.astype(F32)).astype(BF16)
    lhs = jnp.concatenate([h_hi, h_lo, h_hi], axis=-1)
    logits = [jnp.dot(lhs[c, :], wr_ref[...], preferred_element_type=F32) for c in chunks]
    lg = each(lambda l: jnp.where(lane < N_EXPERTS, l, -jnp.inf), logits)
    m1 = each(rowmax, lg)
    e1 = each(first_hit, lg, m1)
    lg2 = each(lambda l, e: jnp.where(lane == e, -jnp.inf, l), lg, e1)
    m2 = each(rowmax, lg2)
    e2 = each(first_hit, lg2, m2)
    ex = each(lambda a, b: jnp.exp(b - a), m1, m2)
    w1 = each(lambda e: 1.0 / (1.0 + e), ex)
    w2 = each(lambda e: e / (1.0 + e), ex)
    hit1 = each(lambda e: lane == e, e1)
    hit2 = each(lambda e: lane == e, e2)
    onehot = each(lambda a, b: (a | b).astype(BF16), hit1, hit2)
    within = each(lambda o: jnp.dot(tri_ref[...], o, preferred_element_type=F32), onehot)
    totals = each(lambda o: jnp.sum(o.astype(F32), axis=0, keepdims=True), onehot)
    ahead, base = [], carry[...]
    for w, tot in zip(within, totals):
        ahead.append(w + base)
        base = base + tot
    carry[...] = base
    cnt_ref[...] = base
    r1 = each(lambda hit, a: rowsum(jnp.where(hit, a, 0.0)), hit1, ahead)
    r2 = each(lambda hit, a: rowsum(jnp.where(hit, a, 0.0)), hit2, ahead)

    def record(*vals):
        meta = jnp.zeros((ROUTE_CHUNK, LANES_V7X), F32)
        for col, val in zip((META_E1, META_E2, META_R1, META_R2, META_W1, META_W2), vals):
            meta = jnp.where(lane == col, val.astype(F32), meta)
        return meta

    meta = each(record, e1, e2, r1, r2, w1, w2)
    metat = each(lambda mt: lax.dot_general(sel_ref[...], mt, (((1,), (1,)), ((), ())), preferred_element_type=F32,
                                            precision=lax.Precision.HIGHEST), meta)
    for c, mt, mtt in zip(chunks, meta, metat):
        meta_ref[c, :] = mt
        metat_ref[:, c] = mtt


def _token_rows(ref, first_token, n_tokens):
    start = pl.multiple_of(first_token * ROW_CHUNKS, ROW_CHUNKS)
    return ref.at[pl.ds(start, n_tokens * ROW_CHUNKS)]


def _row_copy(src_ref, src_token, dst_ref, dst_token, sem):
    return pltpu.make_async_copy(_token_rows(src_ref, src_token, 1), _token_rows(dst_ref, dst_token, 1), sem)


def _dispatch_kernel(fill_ref, pos1_ref, pos2_ref, hf_ref, xs_ref, zeros, sem_z, sem):
    @pl.when(pl.program_id(0) == 0)
    def _():
        zeros[...] = jnp.zeros_like(zeros)

        def tile_fill(e):
            return pltpu.make_async_copy(zeros, _token_rows(xs_ref, pl.multiple_of(fill_ref[e], TM_GROUP), TM_GROUP),
                                         sem_z)

        for e in range(2 * N_EXPERTS):
            @pl.when(fill_ref[e] >= 0)
            def _():
                tile_fill(e).start()
        for e in range(2 * N_EXPERTS):
            @pl.when(fill_ref[e] >= 0)
            def _():
                tile_fill(e).wait()

    def issue(r, carry):
        _row_copy(hf_ref, r, xs_ref, pos1_ref[r], sem).start(priority=0)
        _row_copy(hf_ref, r, xs_ref, pos2_ref[r], sem).start(priority=1)
        return carry

    lax.fori_loop(0, TM_MOVE, issue, 0, unroll=ISSUE_UNROLL)
    whole_step = pltpu.make_async_copy(hf_ref, _token_rows(xs_ref, 0, TM_MOVE), sem)
    whole_step.wait()
    whole_step.wait()


def _dispatch(fill_start, pos1, pos2, hf, n_rows):
    t = hf.shape[0] // ROW_CHUNKS
    idx = pl.BlockSpec((TM_MOVE,), lambda i, fill: (i,), memory_space=pltpu.SMEM)
    return pl.pallas_call(
        _dispatch_kernel,
        grid_spec=pltpu.PrefetchScalarGridSpec(
            num_scalar_prefetch=1,
            grid=(t // TM_MOVE,),
            in_specs=[idx, idx, pl.BlockSpec((TM_MOVE * ROW_CHUNKS, LANES_V7X), lambda i, fill: (i, 0))],
            out_specs=pl.BlockSpec(memory_space=pl.ANY),
            scratch_shapes=[pltpu.VMEM((TM_GROUP * ROW_CHUNKS, LANES_V7X), F32), pltpu.SemaphoreType.DMA(()),
                            pltpu.SemaphoreType.DMA(())]),
        out_shape=jax.ShapeDtypeStruct((n_rows * ROW_CHUNKS, LANES_V7X), F32),
        compiler_params=_params(("arbitrary",), 24),
        name="moe_dispatch",
    )(fill_start, pos1, pos2, hf)


def _grouped_kernel(te_ref, used_ref, x_ref, wg_ref, wu_ref, wd_ref, y_ref, acc):
    i = pl.program_id(0)
    j = pl.program_id(1)
    last = FF_SPLIT - 1

    @pl.when(i < used_ref[0])
    def _():
        h = _from_token_tiles(x_ref).astype(BF16)
        gate = jnp.dot(h, wg_ref[...], preferred_element_type=F32)
        up = jnp.dot(h, wu_ref[...], preferred_element_type=F32)
        act = (gate * jax.nn.sigmoid(gate) * up).astype(BF16)
        part = jnp.dot(act, wd_ref[...], preferred_element_type=F32)

        @pl.when(j == 0)
        def _():
            acc[...] = part

        @pl.when((j > 0) & (j < last))
        def _():
            acc[...] += part

        @pl.when(j == last)
        def _():
            _to_token_tiles(y_ref, part if FF_SPLIT == 1 else acc[...] + part)

    @pl.when((i >= used_ref[0]) & (j == last))
    def _():
        y_ref[...] = jnp.zeros_like(y_ref)


def _grouped_swiglu(tile_expert, n_used, xs, wg, wu, wd):
    n_rows = xs.shape[0] // ROW_CHUNKS
    d_ff = wg.shape[2]
    ff = d_ff // FF_SPLIT
    tile = (TM_GROUP * ROW_CHUNKS, LANES_V7X)
    src = lambda i, j, te, used: (jnp.maximum(jnp.minimum(i, used[0] - 1), 0), 0)
    return pl.pallas_call(
        _grouped_kernel,
        grid_spec=pltpu.PrefetchScalarGridSpec(
            num_scalar_prefetch=2,
            grid=(n_rows // TM_GROUP, FF_SPLIT),
            in_specs=[pl.BlockSpec(tile, src),
                      pl.BlockSpec((None, D_MODEL, ff), lambda i, j, te, used: (te[i], 0, j)),
                      pl.BlockSpec((None, D_MODEL, ff), lambda i, j, te, used: (te[i], 0, j)),
                      pl.BlockSpec((None, ff, D_MODEL), lambda i, j, te, used: (te[i], j, 0))],
            out_specs=pl.BlockSpec(tile, lambda i, j, te, used: (i, 0)),
            scratch_shapes=[pltpu.VMEM((TM_GROUP, D_MODEL), F32)]),
        out_shape=jax.ShapeDtypeStruct((n_rows * ROW_CHUNKS, LANES_V7X), F32),
        compiler_params=_params(("arbitrary", "arbitrary"), 56),
        name="moe_grouped_swiglu",
    )(tile_expert, n_used, xs, wg, wu, wd)


def _combine_kernel(pos1_ref, pos2_ref, pos1_next_ref, pos2_next_ref, x_ref, meta_ref, g_ref, y_ref, o_ref,
                    y1, y2, sems):
    i = pl.program_id(0)

    def gather(p1_ref, p2_ref, slot):
        def issue(r, carry):
            _row_copy(y_ref, p1_ref[r], y1.at[slot], r, sems.at[slot]).start(priority=0)
            _row_copy(y_ref, p2_ref[r], y2.at[slot], r, sems.at[slot]).start(priority=1)
            return carry
        lax.fori_loop(0, TM_COMBINE, issue, 0, unroll=ISSUE_UNROLL)

    def finish(slot):
        for buf in (y1, y2):
            pltpu.make_async_copy(_token_rows(y_ref, 0, TM_COMBINE), buf.at[slot], sems.at[slot]).wait()
        w1 = meta_ref[:, META_W1:META_W1 + 1]
        w2 = meta_ref[:, META_W2:META_W2 + 1]
        out = x_ref[...] + (w1 * _from_token_tiles(y1.at[slot]) + w2 * _from_token_tiles(y2.at[slot]))
        o_ref[...] = _rms(out, g_ref[...])

    @pl.when(i == 0)
    def _():
        gather(pos1_ref, pos2_ref, 0)

    for slot in range(2):
        @pl.when(i % 2 == slot)
        def _():
            @pl.when(i + 1 < pl.num_programs(0))
            def _():
                gather(pos1_next_ref, pos2_next_ref, 1 - slot)
            finish(slot)


def _combine(pos1, pos2, x, meta, g, y):
    t = x.shape[0]
    steps = t // TM_COMBINE
    idx = pl.BlockSpec((TM_COMBINE,), lambda i: (i,), memory_space=pltpu.SMEM)
    idx_next = pl.BlockSpec((TM_COMBINE,), lambda i: (jnp.minimum(i + 1, steps - 1),), memory_space=pltpu.SMEM)
    row = lambda width: pl.BlockSpec((TM_COMBINE, width), lambda i: (i, 0))
    slots = pltpu.VMEM((2, TM_COMBINE * ROW_CHUNKS, LANES_V7X), F32)
    return pl.pallas_call(
        _combine_kernel,
        grid=(steps,),
        in_specs=[idx, idx, idx_next, idx_next, row(D_MODEL), row(LANES_V7X), _const_spec((1, D_MODEL)),
                  pl.BlockSpec(memory_space=pl.ANY)],
        out_specs=row(D_MODEL),
        out_shape=jax.ShapeDtypeStruct((t, D_MODEL), F32),
        scratch_shapes=[slots, slots, pltpu.SemaphoreType.DMA((2,))],
        compiler_params=_params(("arbitrary",), 24),
        name="moe_combine_norm",
    )(pos1, pos2, pos1, pos2, x, meta, g, y)


def _route_operands(ffn_g, router_w):
    wr_pad = jnp.zeros((D_MODEL, LANES_V7X), F32).at[:, :N_EXPERTS].set(router_w)
    wr_hi = wr_pad.astype(BF16)
    wr_lo = (wr_pad - wr_hi.astype(F32)).astype(BF16)
    wr_split = jnp.concatenate([wr_hi, wr_hi, wr_lo], axis=0)
    tri = jnp.tril(jnp.ones((ROUTE_CHUNK, ROUTE_CHUNK), BF16), -1)
    sel = jnp.eye(SUBLANES_V7X, LANES_V7X, dtype=F32)
    return ffn_g, wr_split, tri, sel


def _moe_layer(x, hf, meta, metat, counts, wg, wu, wd, final_g):
    t = x.shape[0]
    n_rows = 2 * t + N_EXPERTS * TM_GROUP

    cnt = counts[0, :N_EXPERTS].astype(jnp.int32)
    padded = (cnt + TM_GROUP - 1) // TM_GROUP * TM_GROUP
    ends = jnp.cumsum(padded)
    starts = ends - padded
    experts = jnp.arange(N_EXPERTS, dtype=jnp.int32)[:, None]
    group_start = lambda e: jnp.sum(jnp.where(e[None, :] == experts, starts[:, None], 0), axis=0)
    pos1 = group_start(metat[META_E1].astype(jnp.int32)) + metat[META_R1].astype(jnp.int32)
    pos2 = group_start(metat[META_E2].astype(jnp.int32)) + metat[META_R2].astype(jnp.int32)
    n_used = (ends[-1] // TM_GROUP).astype(jnp.int32).reshape(1)
    tile_row = jnp.minimum(jnp.arange(n_rows // TM_GROUP, dtype=jnp.int32), n_used[0] - 1) * TM_GROUP
    tile_expert = jnp.sum(ends[None, :] <= tile_row[:, None], axis=1).astype(jnp.int32)
    tail_start = jnp.where(padded > 0, ends - TM_GROUP, -1)
    spare = ends[-1] + TM_GROUP * jnp.arange(N_EXPERTS, dtype=jnp.int32)
    fill_start = jnp.concatenate([tail_start, jnp.where(spare < n_rows, spare, -1)]).astype(jnp.int32)

    xs = _dispatch(fill_start, pos1, pos2, hf, n_rows)
    y = _grouped_swiglu(tile_expert, n_used, xs, wg, wu, wd)
    return _combine(pos1, pos2, x, meta, final_g, y)


def _arrange_in_proj(w):
    q = w[:, :ATTN_WIDTH] * (1.0 / math.sqrt(HEAD_DIM))
    dup = lambda start: [w[:, start + h * HEAD_DIM:start + (h + 1) * HEAD_DIM]
                         for h in range(ATTN_KV_HEADS) for _ in range(2)]
    return jnp.concatenate([q] + dup(ATTN_WIDTH) + dup(ATTN_WIDTH + KV_WIDTH) + [w[:, ATTN_WIDTH + 2 * KV_WIDTH:]],
                           axis=1)


def _block_diag(w):
    heads, d, _ = w.shape
    eye = jnp.eye(heads, dtype=w.dtype)
    return jnp.einsum('hij,hg->higj', w, eye).reshape(heads * d, heads * d)


def kernel(x, mem, rel_bias, mix_norm, w_in, attn_sinks, sc_conv_w, sc_conv_b, rg_conv_w, rg_conv_b, rg_w_a,
           rg_b_a, rg_w_x, rg_b_x, rg_lambda, w_out, xa_norm, mem_norm, xa_wq, xa_wk, xa_wv, xa_wo, ffn_norm,
           dense_wg, dense_wu, dense_wd, moe_router, moe_wg, moe_wu, moe_wd, final_norm):
    batch, seq, _ = x.shape
    depth = w_in.shape[0]
    assert depth == 2 and seq % TS == 0 and seq % TM == 0
    xt = x.reshape(batch * seq, D_MODEL)
    memt = mem.reshape(batch * MEM_LEN, D_MODEL)
    bias_tbl = _attention_bias_tables(rel_bias)
    vec = lambda a: a.reshape(1, -1)

    for layer in range(depth):
        q, k, v, rest = _in_proj(xt, vec(mix_norm[layer]), _arrange_in_proj(w_in[layer]).astype(BF16))
        attn = _attention(q, k, v, attn_sinks[layer], bias_tbl, seq // BLOCK)
        w_gate = jnp.concatenate([_block_diag(rg_w_a[layer]), _block_diag(rg_w_x[layer])], axis=1).astype(BF16)
        b_gate = jnp.concatenate([rg_b_a[layer], rg_b_x[layer]]).reshape(1, -1)
        cr = _conv_rg(rest, sc_conv_w[layer], vec(sc_conv_b[layer]), rg_conv_w[layer], vec(rg_conv_b[layer]),
                      w_gate, b_gate, vec(rg_lambda[layer]), batch, seq)
        wkv = jnp.concatenate([xa_wk[layer], xa_wv[layer]], axis=1).astype(BF16)
        mk, mv = _mem_kv(memt, vec(mem_norm[layer]), wkv)
        post = functools.partial(_post_mixer, xt, attn, cr, w_out[layer].astype(BF16), vec(xa_norm[layer]),
                                 xa_wq[layer].astype(BF16), mk, mv, xa_wo[layer].astype(BF16), seq)

        j = layer // 2
        if layer % 2 == 0:
            (xt,) = post()
            xt = _dense_ffn(xt, vec(ffn_norm[layer]), dense_wg[j].astype(BF16), dense_wu[j].astype(BF16),
                            dense_wd[j].astype(BF16))
        else:
            xt, hf, meta, metat, counts = post(route=_route_operands(vec(ffn_norm[layer]), moe_router[j]))
            xt = _moe_layer(xt, hf, meta, metat, counts, moe_wg[j].astype(BF16), moe_wu[j].astype(BF16),
                            moe_wd[j].astype(BF16), vec(final_norm))
    return xt.reshape(batch, seq, D_MODEL)
```

```python
import functools
import math

import jax
import jax.numpy as jnp
import numpy as np
from jax import lax
from jax.experimental import pallas as pl
from jax.experimental.pallas import tpu as pltpu

F32 = jnp.float32
BF16 = jnp.bfloat16

D_MODEL = 1024
MEM_LEN = 256
HEAD_DIM = 64
ATTN_Q_HEADS = 8
ATTN_KV_HEADS = 2
ATTN_WIDTH = ATTN_Q_HEADS * HEAD_DIM
KV_WIDTH = ATTN_KV_HEADS * HEAD_DIM
KV_DUP_WIDTH = 2 * KV_WIDTH
BLOCK = 128
SC_WIDTH = 256
SC_CONV = 3
RG_WIDTH = 256
RG_HEADS = 4
RG_HEAD_DIM = RG_WIDTH // RG_HEADS
RG_CONV = 4
RG_C = 8.0
N_BUCKETS = 32
MAX_EXACT = N_BUCKETS // 2
MAX_DISTANCE = 128
XA_HEADS = 4
XA_HEAD_DIM = 128
XA_WIDTH = XA_HEADS * XA_HEAD_DIM
N_EXPERTS = 8
EPS = 1e-6
NEG_INF = -1e30
REST_WIDTH = 3 * SC_WIDTH + 2 * RG_WIDTH

LANES_V7X = 128
SUBLANES_V7X = 8
VMEM_BYTES_V7X = 64 * 1024 * 1024
ROW_CHUNKS = D_MODEL // LANES_V7X
assert ROW_CHUNKS == SUBLANES_V7X

TM = 512
ATTN_BLOCKS = 8
TS = 512
ROUTE_CHUNK = 128
TM_GROUP = 512
TM_MOVE = 1024
TM_COMBINE = 256
ISSUE_UNROLL = 8
FF_SPLIT = 2
CARRY_ROWS = SUBLANES_V7X


def _mib(n):
    return int(n * 1024 * 1024)


def _params(semantics, vmem_mib):
    assert _mib(vmem_mib) < VMEM_BYTES_V7X
    return pltpu.CompilerParams(dimension_semantics=semantics, vmem_limit_bytes=_mib(vmem_mib))


def _rms(x, g):
    ms = jnp.mean(x * x, axis=-1, keepdims=True)
    return x * lax.rsqrt(ms + EPS) * g


def _const_spec(shape):
    nd = len(shape)
    return pl.BlockSpec(shape, lambda *_: (0,) * nd, pipeline_mode=pl.Buffered(1))


def _in_proj_kernel(x_ref, g_ref, w_ref, q_ref, k_ref, v_ref, r_ref):
    h = _rms(x_ref[...], g_ref[...]).astype(BF16)
    p = jnp.dot(h, w_ref[...], preferred_element_type=F32)
    q_ref[...] = p[:, :ATTN_WIDTH].astype(BF16)
    k_ref[...] = p[:, ATTN_WIDTH:ATTN_WIDTH + KV_DUP_WIDTH].astype(BF16)
    v_ref[...] = p[:, ATTN_WIDTH + KV_DUP_WIDTH:ATTN_WIDTH + 2 * KV_DUP_WIDTH].astype(BF16)
    r_ref[...] = p[:, ATTN_WIDTH + 2 * KV_DUP_WIDTH:]


def _in_proj(x, g, w):
    t = x.shape[0]
    n = w.shape[1]
    row = lambda width: pl.BlockSpec((TM, width), lambda i: (i, 0))
    return pl.pallas_call(
        _in_proj_kernel,
        grid=(t // TM,),
        in_specs=[row(D_MODEL), _const_spec((1, D_MODEL)), _const_spec((D_MODEL, n))],
        out_specs=[row(ATTN_WIDTH), row(KV_DUP_WIDTH), row(KV_DUP_WIDTH), row(REST_WIDTH)],
        out_shape=[jax.ShapeDtypeStruct((t, ATTN_WIDTH), BF16),
                   jax.ShapeDtypeStruct((t, KV_DUP_WIDTH), BF16),
                   jax.ShapeDtypeStruct((t, KV_DUP_WIDTH), BF16),
                   jax.ShapeDtypeStruct((t, REST_WIDTH), F32)],
        compiler_params=_params(("parallel",), 40),
        name="in_proj",
    )(x, g, w)


def _attn_kernel(sink_ref, q_ref, kp_ref, kc_ref, vp_ref, vc_ref, bias0_ref, bias_ref, o_ref):
    pairs_per_group = ATTN_Q_HEADS // ATTN_KV_HEADS // 2
    row = lax.broadcasted_iota(jnp.int32, (BLOCK, BLOCK), 0)
    col = lax.broadcasted_iota(jnp.int32, (BLOCK, BLOCK), 1)
    from_prev = col > row
    low_lanes = lax.broadcasted_iota(jnp.int32, (2 * BLOCK, 2 * HEAD_DIM), 1) < HEAD_DIM
    low_out = col < HEAD_DIM
    zero = jnp.zeros((), BF16)

    def block_diag(band):
        return jnp.concatenate([jnp.where(low_lanes, band, zero), jnp.where(low_lanes, zero, band)], axis=0)

    def scores(blk):
        rows = slice(blk * BLOCK, (blk + 1) * BLOCK)
        prev_rows = slice((blk - 1) * BLOCK, blk * BLOCK)
        out = []
        for g in range(ATTN_KV_HEADS):
            lanes = slice(g * 2 * HEAD_DIM, (g + 1) * 2 * HEAD_DIM)
            k_prev = kp_ref[:, lanes] if blk == 0 else kc_ref[prev_rows, lanes]
            v_prev = vp_ref[:, lanes] if blk == 0 else vc_ref[prev_rows, lanes]
            k_bd = block_diag(jnp.concatenate([k_prev, kc_ref[rows, lanes]], axis=0))
            v_bd = block_diag(jnp.concatenate([v_prev, vc_ref[rows, lanes]], axis=0))
            for pair in range(pairs_per_group):
                slab = g * pairs_per_group + pair
                q2 = q_ref[rows, slab * 2 * HEAD_DIM:(slab + 1) * 2 * HEAD_DIM]
                s = lax.dot_general(q2, k_bd, (((1,), (1,)), ((), ())), preferred_element_type=F32)
                out.append((slab, s, v_bd))
        return out

    def finish(blk, scored):
        rows = slice(blk * BLOCK, (blk + 1) * BLOCK)
        tbl_ref = bias0_ref if blk == 0 else bias_ref
        staged = []
        for slab, s, v_bd in scored:
            probs, denoms = [], []
            for side in range(2):
                h = 2 * slab + side
                sh = s[:, side * 2 * BLOCK:(side + 1) * 2 * BLOCK]
                logits = jnp.where(from_prev, sh[:, :BLOCK], sh[:, BLOCK:]) + tbl_ref[h]
                sink = sink_ref[h]
                m = jnp.maximum(jnp.max(logits, axis=-1, keepdims=True), sink)
                p = jnp.exp(logits - m)
                denoms.append(jnp.sum(p, axis=-1, keepdims=True) + jnp.exp(sink - m))
                probs += [jnp.where(from_prev, p, 0.0), jnp.where(from_prev, 0.0, p)]
            staged.append((slab, jnp.concatenate(probs, axis=-1).astype(BF16), v_bd, denoms))
        for slab, p_band, v_bd, denoms in staged:
            o = jnp.dot(p_band, v_bd, preferred_element_type=F32)
            o = o / jnp.where(low_out, denoms[0], denoms[1])
            o_ref[rows, slab * 2 * HEAD_DIM:(slab + 1) * 2 * HEAD_DIM] = o.astype(BF16)

    pending = scores(0)
    for blk in range(ATTN_BLOCKS):
        upcoming = scores(blk + 1) if blk + 1 < ATTN_BLOCKS else None
        finish(blk, pending)
        pending = upcoming


def _attention(q, k, v, sinks, bias_tbl, blocks_per_seq):
    t = q.shape[0]
    tile = ATTN_BLOCKS * BLOCK
    cur = lambda i: (i, 0)
    prev = lambda i: (jnp.maximum(i * ATTN_BLOCKS - 1, 0), 0)
    tbl = (None, ATTN_Q_HEADS, BLOCK, BLOCK)
    return pl.pallas_call(
        _attn_kernel,
        grid=(t // tile,),
        in_specs=[pl.BlockSpec(memory_space=pltpu.SMEM),
                  pl.BlockSpec((tile, ATTN_WIDTH), cur),
                  pl.BlockSpec((BLOCK, KV_DUP_WIDTH), prev),
                  pl.BlockSpec((tile, KV_DUP_WIDTH), cur),
                  pl.BlockSpec((BLOCK, KV_DUP_WIDTH), prev),
                  pl.BlockSpec((tile, KV_DUP_WIDTH), cur),
                  pl.BlockSpec(tbl, lambda i: (jnp.minimum((i * ATTN_BLOCKS) % blocks_per_seq, 1), 0, 0, 0)),
                  pl.BlockSpec(tbl, lambda i: (1, 0, 0, 0))],
        out_specs=pl.BlockSpec((tile, ATTN_WIDTH), cur),
        out_shape=jax.ShapeDtypeStruct((t, ATTN_WIDTH), BF16),
        compiler_params=_params(("parallel",), 24),
        name="swa_attention",
    )(sinks, q, k, k, v, v, bias_tbl, bias_tbl)


def _bias_table_kernel(rel_ref, bucket_ref, o_ref):
    for v in range(2):
        bucket = bucket_ref[v]
        hits = [bucket == b for b in range(N_BUCKETS)]
        for h in range(ATTN_Q_HEADS):
            tbl = jnp.full(bucket.shape, NEG_INF, F32)
            for b in range(N_BUCKETS):
                tbl = jnp.where(hits[b], rel_ref[b * ATTN_Q_HEADS + h], tbl)
            o_ref[v, h] = tbl


def _attention_bias_tables(rel_bias):
    q_idx = np.arange(BLOCK)[:, None]
    j_idx = np.arange(BLOCK)[None, :]
    from_prev = j_idx > q_idx
    n = np.where(from_prev, q_idx + BLOCK - j_idx, q_idx - j_idx)
    large = MAX_EXACT + (np.log(np.maximum(n, 1).astype(np.float32) / np.float32(MAX_EXACT))
                         / np.float32(math.log(MAX_DISTANCE / MAX_EXACT))
                         * np.float32(N_BUCKETS - MAX_EXACT)).astype(np.int32)
    bucket = np.where(n < MAX_EXACT, n, np.minimum(large, N_BUCKETS - 1))
    first = np.where(from_prev, -1, bucket)
    buckets = jnp.asarray(np.stack([first, bucket]).astype(np.int32))
    return pl.pallas_call(
        _bias_table_kernel,
        in_specs=[pl.BlockSpec(memory_space=pltpu.SMEM), pl.BlockSpec(memory_space=pltpu.VMEM)],
        out_specs=pl.BlockSpec(memory_space=pltpu.VMEM),
        out_shape=jax.ShapeDtypeStruct((2, ATTN_Q_HEADS, BLOCK, BLOCK), F32),
        name="t5_bias_table",
    )(rel_bias.astype(F32).reshape(-1), buckets)


def _shift_rows(x, s, fill):
    return jnp.concatenate([jnp.full((s, x.shape[1]), fill, x.dtype), x[:x.shape[0] - s]], axis=0)


def _conv_rg_kernel(r_ref, scw_ref, scb_ref, rgw_ref, rgb_ref, wgate_ref, bgate_ref, lam_ref,
                    o_ref, sc_ext, rg_ext, h_carry):
    ts = r_ref.shape[0]
    c0 = CARRY_ROWS

    @pl.when(pl.program_id(1) == 0)
    def _():
        sc_ext[0:c0, :] = jnp.zeros((c0, SC_WIDTH), F32)
        rg_ext[0:c0, :] = jnp.zeros((c0, RG_WIDTH), F32)
        h_carry[...] = jnp.zeros_like(h_carry)

    sc_b = r_ref[:, 0:SC_WIDTH]
    sc_ext[c0:c0 + ts, :] = r_ref[:, SC_WIDTH:2 * SC_WIDTH] * r_ref[:, 2 * SC_WIDTH:3 * SC_WIDTH]
    rg_ext[c0:c0 + ts, :] = r_ref[:, 3 * SC_WIDTH:3 * SC_WIDTH + RG_WIDTH]
    rg_g = r_ref[:, 3 * SC_WIDTH + RG_WIDTH:]

    conv = scb_ref[...]
    for k in range(SC_CONV):
        off = c0 - (SC_CONV - 1) + k
        conv = conv + scw_ref[k:k + 1, :] * sc_ext[off:off + ts, :]
    conv_out = sc_b * conv

    rg_in = rgb_ref[...]
    for k in range(RG_CONV):
        off = c0 - (RG_CONV - 1) + k
        rg_in = rg_in + rgw_ref[k:k + 1, :] * rg_ext[off:off + ts, :]

    sc_ext[0:c0, :] = sc_ext[ts:ts + c0, :]
    rg_ext[0:c0, :] = rg_ext[ts:ts + c0, :]

    gates = jnp.dot(rg_in.astype(BF16), wgate_ref[...], preferred_element_type=F32) + bgate_ref[...]
    r_gate = jax.nn.sigmoid(gates[:, :RG_WIDTH])
    i_gate = jax.nn.sigmoid(gates[:, RG_WIDTH:])
    neg_lam = -lam_ref[...]
    softplus = jnp.maximum(neg_lam, 0.0) + jnp.log1p(jnp.exp(-jnp.abs(neg_lam)))
    log_a = -RG_C * r_gate * softplus
    a = jnp.exp(log_a)
    u = jnp.sqrt(jnp.tanh(-log_a) * (1.0 + a * a)) * (i_gate * rg_in)

    s = 1
    while s < ts:
        u = a * _shift_rows(u, s, 0.0) + u
        a = a * _shift_rows(a, s, 1.0)
        s *= 2
    h = a * h_carry[...] + u
    h_carry[...] = h[ts - 1:ts, :]

    c = math.sqrt(2.0 / math.pi)
    gelu = 0.5 * rg_g * (1.0 + jnp.tanh(c * (rg_g + 0.044715 * (rg_g * rg_g * rg_g))))
    o_ref[:, 0:SC_WIDTH] = conv_out.astype(BF16)
    o_ref[:, SC_WIDTH:] = (h * gelu).astype(BF16)


def _conv_rg(rest, sc_w, sc_b, rg_w, rg_b, w_gate, b_gate, lam, batch, seq):
    t = rest.shape[0]
    steps = seq // TS
    row = lambda b, s: (b * steps + s, 0)
    return pl.pallas_call(
        _conv_rg_kernel,
        grid=(batch, steps),
        in_specs=[pl.BlockSpec((TS, REST_WIDTH), row),
                  _const_spec((SC_CONV, SC_WIDTH)), _const_spec((1, SC_WIDTH)),
                  _const_spec((RG_CONV, RG_WIDTH)), _const_spec((1, RG_WIDTH)),
                  _const_spec((RG_WIDTH, 2 * RG_WIDTH)), _const_spec((1, 2 * RG_WIDTH)),
                  _const_spec((1, RG_WIDTH))],
        out_specs=pl.BlockSpec((TS, SC_WIDTH + RG_WIDTH), row),
        out_shape=jax.ShapeDtypeStruct((t, SC_WIDTH + RG_WIDTH), BF16),
        scratch_shapes=[pltpu.VMEM((TS + 2 * CARRY_ROWS, SC_WIDTH), F32),
                        pltpu.VMEM((TS + 2 * CARRY_ROWS, RG_WIDTH), F32),
                        pltpu.VMEM((1, RG_WIDTH), F32)],
        compiler_params=_params(("arbitrary", "arbitrary"), 32),
        name="conv_rglru",
    )(rest, sc_w, sc_b, rg_w, rg_b, w_gate, b_gate, lam)


def _mem_kv_kernel(m_ref, g_ref, w_ref, k_ref, v_ref):
    h = _rms(m_ref[...], g_ref[...]).astype(BF16)
    p = jnp.dot(h, w_ref[...], preferred_element_type=F32)
    k_ref[...] = p[:, :XA_WIDTH].astype(BF16)
    v_ref[...] = p[:, XA_WIDTH:].astype(BF16)


def _mem_kv(mem, g, wkv):
    t = mem.shape[0]
    row = lambda width: pl.BlockSpec((MEM_LEN, width), lambda i: (i, 0))
    return pl.pallas_call(
        _mem_kv_kernel,
        grid=(t // MEM_LEN,),
        in_specs=[row(D_MODEL), _const_spec((1, D_MODEL)), _const_spec((D_MODEL, 2 * XA_WIDTH))],
        out_specs=[row(XA_WIDTH), row(XA_WIDTH)],
        out_shape=[jax.ShapeDtypeStruct((t, XA_WIDTH), BF16)] * 2,
        compiler_params=_params(("parallel",), 24),
        name="mem_kv",
    )(mem, g, wkv)


def _post_mixer_kernel(x_ref, a_ref, c_ref, wout_ref, g_ref, wq_ref, k_ref, v_ref, wo_ref, *rest):
    o_ref = rest[-1] if len(rest) == 1 else rest[4]
    halves = [slice(i * (TM // 2), (i + 1) * (TM // 2)) for i in range(2)]
    heads = [slice(hd * XA_HEAD_DIM, (hd + 1) * XA_HEAD_DIM) for hd in range(XA_HEADS)]
    nt = (((1,), (1,)), ((), ()))
    k = k_ref[...]
    v = v_ref[...]

    x1 = [x_ref[hs, :] + jnp.dot(jnp.concatenate([a_ref[hs, :], c_ref[hs, :]], axis=-1), wout_ref[...],
                                 preferred_element_type=F32) for hs in halves]
    q = [jnp.dot(_rms(xh, g_ref[...]).astype(BF16), wq_ref[...], preferred_element_type=F32).astype(BF16)
         for xh in x1]
    scores = [[lax.dot_general(qh[:, sl], k[:, sl], nt, preferred_element_type=F32) for sl in heads] for qh in q]
    x2 = []
    for xh, per_head in zip(x1, scores):
        probs, sums = [], []
        for s in per_head:
            s = s * (1.0 / math.sqrt(XA_HEAD_DIM))
            p = jnp.exp(s - jnp.max(s, axis=-1, keepdims=True))
            probs.append(p.astype(BF16))
            sums.append(jnp.sum(p, axis=-1, keepdims=True))
        att = jnp.concatenate([jnp.dot(p, v[:, sl], preferred_element_type=F32) / l
                               for p, sl, l in zip(probs, heads, sums)], axis=-1).astype(BF16)
        x2.append(xh + jnp.dot(att, wo_ref[...], preferred_element_type=F32))

    for hs, xh in zip(halves, x2):
        o_ref[hs, :] = xh
    if len(rest) > 1:
        ffn_g_ref, wr_ref, tri_ref, sel_ref, _, hf_ref, meta_ref, metat_ref, cnt_ref, carry = rest
        h = _rms(jnp.concatenate(x2, axis=0), ffn_g_ref[...])
        _route(h, wr_ref, tri_ref, sel_ref, hf_ref, meta_ref, metat_ref, cnt_ref, carry)


def _post_mixer(x, attn, cr, w_out, g, wq, k, v, wo, seq, route=None):
    t = x.shape[0]
    per_seq = seq // TM
    row = lambda width: pl.BlockSpec((TM, width), lambda i: (i, 0))
    mem_blk = pl.BlockSpec((MEM_LEN, XA_WIDTH), lambda i: (i // per_seq, 0))
    in_specs = [row(D_MODEL), row(ATTN_WIDTH), row(SC_WIDTH + RG_WIDTH), _const_spec((D_MODEL, D_MODEL)),
                _const_spec((1, D_MODEL)), _const_spec((D_MODEL, XA_WIDTH)), mem_blk, mem_blk,
                _const_spec((XA_WIDTH, D_MODEL))]
    out_specs = [row(D_MODEL)]
    out_shape = [jax.ShapeDtypeStruct((t, D_MODEL), F32)]
    scratch = []
    args = [x, attn, cr, w_out, g, wq, k, v, wo]
    if route is not None:
        assert TM % ROUTE_CHUNK == 0
        in_specs += [_const_spec((1, D_MODEL)), _const_spec((3 * D_MODEL, LANES_V7X)), _const_spec((ROUTE_CHUNK, ROUTE_CHUNK)),
                     _const_spec((SUBLANES_V7X, LANES_V7X))]
        out_specs += [pl.BlockSpec((TM * ROW_CHUNKS, LANES_V7X), lambda i: (i, 0)), row(LANES_V7X),
                      pl.BlockSpec((SUBLANES_V7X, TM), lambda i: (0, i)),
                      pl.BlockSpec((1, LANES_V7X), lambda i: (0, 0))]
        out_shape += [jax.ShapeDtypeStruct((t * ROW_CHUNKS, LANES_V7X), F32),
                      jax.ShapeDtypeStruct((t, LANES_V7X), F32),
                      jax.ShapeDtypeStruct((SUBLANES_V7X, t), F32),
                      jax.ShapeDtypeStruct((1, LANES_V7X), F32)]
        scratch = [pltpu.VMEM((1, LANES_V7X), F32)]
        args += list(route)
    return pl.pallas_call(
        _post_mixer_kernel,
        grid=(t // TM,),
        in_specs=in_specs,
        out_specs=out_specs,
        out_shape=out_shape,
        scratch_shapes=scratch,
        compiler_params=_params(("arbitrary",), 40),
        name="post_mixer_route" if route is not None else "post_mixer",
    )(*args)


def _ffn_kernel(x_ref, g_ref, wg_ref, wu_ref, wd_ref, o_ref):
    x = x_ref[...]
    h = _rms(x, g_ref[...]).astype(BF16)
    gate = jnp.dot(h, wg_ref[...], preferred_element_type=F32)
    up = jnp.dot(h, wu_ref[...], preferred_element_type=F32)
    act = (gate * jax.nn.sigmoid(gate) * up).astype(BF16)
    o_ref[...] = x + jnp.dot(act, wd_ref[...], preferred_element_type=F32)


def _dense_ffn(x, g, wg, wu, wd):
    t = x.shape[0]
    d_ff = wg.shape[1]
    row = pl.BlockSpec((TM, D_MODEL), lambda i: (i, 0))
    return pl.pallas_call(
        _ffn_kernel,
        grid=(t // TM,),
        in_specs=[row, _const_spec((1, D_MODEL)), _const_spec((D_MODEL, d_ff)), _const_spec((D_MODEL, d_ff)),
                  _const_spec((d_ff, D_MODEL))],
        out_specs=row,
        out_shape=jax.ShapeDtypeStruct((t, D_MODEL), F32),
        compiler_params=_params(("parallel",), 60),
        name="dense_swiglu",
    )(x, g, wg, wu, wd)


META_E1, META_E2, META_R1, META_R2, META_W1, META_W2 = range(6)


def _to_token_tiles(ref, rows):
    m = rows.shape[0]
    for c in range(ROW_CHUNKS):
        ref[pl.ds(c, m, stride=ROW_CHUNKS), :] = rows[:, c * LANES_V7X:(c + 1) * LANES_V7X]


def _from_token_tiles(ref):
    m = ref.shape[0] // ROW_CHUNKS
    return jnp.concatenate([ref[pl.ds(c, m, stride=ROW_CHUNKS), :] for c in range(ROW_CHUNKS)], axis=-1)


def _route(h, wr_ref, tri_ref, sel_ref, hf_ref, meta_ref, metat_ref, cnt_ref, carry):
    @pl.when(pl.program_id(0) == 0)
    def _():
        carry[...] = jnp.zeros_like(carry)

    _to_token_tiles(hf_ref, h)
    chunks = [slice(c * ROUTE_CHUNK, (c + 1) * ROUTE_CHUNK) for c in range(h.shape[0] // ROUTE_CHUNK)]
    each = lambda fn, *lists: [fn(*vals) for vals in zip(*lists)]
    rowmax = lambda a: jnp.max(a, axis=-1, keepdims=True)
    rowsum = lambda a: jnp.sum(a, axis=-1, keepdims=True)
    lane = lax.broadcasted_iota(jnp.int32, (ROUTE_CHUNK, LANES_V7X), 1)
    first_hit = lambda lg, m: jnp.min(jnp.where(lg == m, lane, LANES_V7X), axis=-1, keepdims=True)

    h_hi = h.astype(BF16)
    h_lo = (h - h_hi.astype(F32)).astype(BF16)
    lhs = jnp.concatenate([h_hi, h_lo, h_hi], axis=-1)
    logits = [jnp.dot(lhs[c, :], wr_ref[...], preferred_element_type=F32) for c in chunks]
    lg = each(lambda l: jnp.where(lane < N_EXPERTS, l, -jnp.inf), logits)
    m1 = each(rowmax, lg)
    e1 = each(first_hit, lg, m1)
    lg2 = each(lambda l, e: jnp.where(lane == e, -jnp.inf, l), lg, e1)
    m2 = each(rowmax, lg2)
    e2 = each(first_hit, lg2, m2)
    ex = each(lambda a, b: jnp.exp(b - a), m1, m2)
    w1 = each(lambda e: 1.0 / (1.0 + e), ex)
    w2 = each(lambda e: e / (1.0 + e), ex)
    hit1 = each(lambda e: lane == e, e1)
    hit2 = each(lambda e: lane == e, e2)
    onehot = each(lambda a, b: (a | b).astype(BF16), hit1, hit2)
    within = each(lambda o: jnp.dot(tri_ref[...], o, preferred_element_type=F32), onehot)
    totals = each(lambda o: jnp.sum(o.astype(F32), axis=0, keepdims=True), onehot)
    ahead, base = [], carry[...]
    for w, tot in zip(within, totals):
        ahead.append(w + base)
        base = base + tot
    carry[...] = base
    cnt_ref[...] = base
    r1 = each(lambda hit, a: rowsum(jnp.where(hit, a, 0.0)), hit1, ahead)
    r2 = each(lambda hit, a: rowsum(jnp.where(hit, a, 0.0)), hit2, ahead)

    def record(*vals):
        meta = jnp.zeros((ROUTE_CHUNK, LANES_V7X), F32)
        for col, val in zip((META_E1, META_E2, META_R1, META_R2, META_W1, META_W2), vals):
            meta = jnp.where(lane == col, val.astype(F32), meta)
        return meta

    meta = each(record, e1, e2, r1, r2, w1, w2)
    metat = each(lambda mt: lax.dot_general(sel_ref[...], mt, (((1,), (1,)), ((), ())), preferred_element_type=F32,
                                            precision=lax.Precision.HIGHEST), meta)
    for c, mt, mtt in zip(chunks, meta, metat):
        meta_ref[c, :] = mt
        metat_ref[:, c] = mtt


def _token_rows(ref, first_token, n_tokens):
    start = pl.multiple_of(first_token * ROW_CHUNKS, ROW_CHUNKS)
    return ref.at[pl.ds(start, n_tokens * ROW_CHUNKS)]


def _row_copy(src_ref, src_token, dst_ref, dst_token, sem):
    return pltpu.make_async_copy(_token_rows(src_ref, src_token, 1), _token_rows(dst_ref, dst_token, 1), sem)


def _dispatch_kernel(fill_ref, pos1_ref, pos2_ref, hf_ref, xs_ref, zeros, sem_z, sem):
    @pl.when(pl.program_id(0) == 0)
    def _():
        zeros[...] = jnp.zeros_like(zeros)

        def tile_fill(e):
            return pltpu.make_async_copy(zeros, _token_rows(xs_ref, pl.multiple_of(fill_ref[e], TM_GROUP), TM_GROUP),
                                         sem_z)

        for e in range(2 * N_EXPERTS):
            @pl.when(fill_ref[e] >= 0)
            def _():
                tile_fill(e).start()
        for e in range(2 * N_EXPERTS):
            @pl.when(fill_ref[e] >= 0)
            def _():
                tile_fill(e).wait()

    def issue(r, carry):
        _row_copy(hf_ref, r, xs_ref, pos1_ref[r], sem).start(priority=0)
        _row_copy(hf_ref, r, xs_ref, pos2_ref[r], sem).start(priority=1)
        return carry

    lax.fori_loop(0, TM_MOVE, issue, 0, unroll=ISSUE_UNROLL)
    whole_step = pltpu.make_async_copy(hf_ref, _token_rows(xs_ref, 0, TM_MOVE), sem)
    whole_step.wait()
    whole_step.wait()


def _dispatch(fill_start, pos1, pos2, hf, n_rows):
    t = hf.shape[0] // ROW_CHUNKS
    idx = pl.BlockSpec((TM_MOVE,), lambda i, fill: (i,), memory_space=pltpu.SMEM)
    return pl.pallas_call(
        _dispatch_kernel,
        grid_spec=pltpu.PrefetchScalarGridSpec(
            num_scalar_prefetch=1,
            grid=(t // TM_MOVE,),
            in_specs=[idx, idx, pl.BlockSpec((TM_MOVE * ROW_CHUNKS, LANES_V7X), lambda i, fill: (i, 0))],
            out_specs=pl.BlockSpec(memory_space=pl.ANY),
            scratch_shapes=[pltpu.VMEM((TM_GROUP * ROW_CHUNKS, LANES_V7X), F32), pltpu.SemaphoreType.DMA(()),
                            pltpu.SemaphoreType.DMA(())]),
        out_shape=jax.ShapeDtypeStruct((n_rows * ROW_CHUNKS, LANES_V7X), F32),
        compiler_params=_params(("arbitrary",), 24),
        name="moe_dispatch",
    )(fill_start, pos1, pos2, hf)


def _grouped_kernel(te_ref, used_ref, x_ref, wg_ref, wu_ref, wd_ref, y_ref):
    i = pl.program_id(0)

    @pl.when(i < used_ref[0])
    def _():
        h = _from_token_tiles(x_ref).astype(BF16)
        ff = wg_ref.shape[1] // FF_SPLIT
        cols = [slice(c * ff, (c + 1) * ff) for c in range(FF_SPLIT)]
        gates = [jnp.dot(h, wg_ref[:, c], preferred_element_type=F32) for c in cols]
        ups = [jnp.dot(h, wu_ref[:, c], preferred_element_type=F32) for c in cols]
        total = None
        for c, gate, up in zip(cols, gates, ups):
            act = (gate * jax.nn.sigmoid(gate) * up).astype(BF16)
            part = jnp.dot(act, wd_ref[c, :], preferred_element_type=F32)
            total = part if total is None else total + part
        _to_token_tiles(y_ref, total)

    @pl.when(i >= used_ref[0])
    def _():
        y_ref[...] = jnp.zeros_like(y_ref)


def _grouped_swiglu(tile_expert, n_used, xs, wg, wu, wd):
    n_rows = xs.shape[0] // ROW_CHUNKS
    d_ff = wg.shape[2]
    tile = (TM_GROUP * ROW_CHUNKS, LANES_V7X)
    src = lambda i, te, used: (jnp.maximum(jnp.minimum(i, used[0] - 1), 0), 0)
    expert = lambda shape: pl.BlockSpec((None,) + shape, lambda i, te, used: (te[i], 0, 0),
                                        pipeline_mode=pl.Buffered(1))
    return pl.pallas_call(
        _grouped_kernel,
        grid_spec=pltpu.PrefetchScalarGridSpec(
            num_scalar_prefetch=2,
            grid=(n_rows // TM_GROUP,),
            in_specs=[pl.BlockSpec(tile, src), expert((D_MODEL, d_ff)), expert((D_MODEL, d_ff)),
                      expert((d_ff, D_MODEL))],
            out_specs=pl.BlockSpec(tile, lambda i, te, used: (i, 0))),
        out_shape=jax.ShapeDtypeStruct((n_rows * ROW_CHUNKS, LANES_V7X), F32),
        compiler_params=_params(("arbitrary",), 58),
        name="moe_grouped_swiglu",
    )(tile_expert, n_used, xs, wg, wu, wd)


def _combine_kernel(pos1_ref, pos2_ref, pos1_next_ref, pos2_next_ref, x_ref, meta_ref, g_ref, y_ref, o_ref,
                    y1, y2, sems):
    i = pl.program_id(0)

    def gather(p1_ref, p2_ref, slot):
        def issue(r, carry):
            _row_copy(y_ref, p1_ref[r], y1.at[slot], r, sems.at[slot]).start(priority=0)
            _row_copy(y_ref, p2_ref[r], y2.at[slot], r, sems.at[slot]).start(priority=1)
            return carry
        lax.fori_loop(0, TM_COMBINE, issue, 0, unroll=ISSUE_UNROLL)

    def finish(slot):
        for buf in (y1, y2):
            pltpu.make_async_copy(_token_rows(y_ref, 0, TM_COMBINE), buf.at[slot], sems.at[slot]).wait()
        w1 = meta_ref[:, META_W1:META_W1 + 1]
        w2 = meta_ref[:, META_W2:META_W2 + 1]
        out = x_ref[...] + (w1 * _from_token_tiles(y1.at[slot]) + w2 * _from_token_tiles(y2.at[slot]))
        o_ref[...] = _rms(out, g_ref[...])

    @pl.when(i == 0)
    def _():
        gather(pos1_ref, pos2_ref, 0)

    for slot in range(2):
        @pl.when(i % 2 == slot)
        def _():
            @pl.when(i + 1 < pl.num_programs(0))
            def _():
                gather(pos1_next_ref, pos2_next_ref, 1 - slot)
            finish(slot)


def _combine(pos1, pos2, x, meta, g, y):
    t = x.shape[0]
    steps = t // TM_COMBINE
    idx = pl.BlockSpec((TM_COMBINE,), lambda i: (i,), memory_space=pltpu.SMEM)
    idx_next = pl.BlockSpec((TM_COMBINE,), lambda i: (jnp.minimum(i + 1, steps - 1),), memory_space=pltpu.SMEM)
    row = lambda width: pl.BlockSpec((TM_COMBINE, width), lambda i: (i, 0))
    slots = pltpu.VMEM((2, TM_COMBINE * ROW_CHUNKS, LANES_V7X), F32)
    return pl.pallas_call(
        _combine_kernel,
        grid=(steps,),
        in_specs=[idx, idx, idx_next, idx_next, row(D_MODEL), row(LANES_V7X), _const_spec((1, D_MODEL)),
                  pl.BlockSpec(memory_space=pl.ANY)],
        out_specs=row(D_MODEL),
        out_shape=jax.ShapeDtypeStruct((t, D_MODEL), F32),
        scratch_shapes=[slots, slots, pltpu.SemaphoreType.DMA((2,))],
        compiler_params=_params(("arbitrary",), 24),
        name="moe_combine_norm",
    )(pos1, pos2, pos1, pos2, x, meta, g, y)


def _route_operands(ffn_g, router_w):
    wr_pad = jnp.zeros((D_MODEL, LANES_V7X), F32).at[:, :N_EXPERTS].set(router_w)
    wr_hi = wr_pad.astype(BF16)
    wr_lo = (wr_pad - wr_hi.astype(F32)).astype(BF16)
    wr_split = jnp.concatenate([wr_hi, wr_hi, wr_lo], axis=0)
    tri = jnp.tril(jnp.ones((ROUTE_CHUNK, ROUTE_CHUNK), BF16), -1)
    sel = jnp.eye(SUBLANES_V7X, LANES_V7X, dtype=F32)
    return ffn_g, wr_split, tri, sel


def _moe_layer(x, hf, meta, metat, counts, wg, wu, wd, final_g):
    t = x.shape[0]
    n_rows = 2 * t + N_EXPERTS * TM_GROUP

    cnt = counts[0, :N_EXPERTS].astype(jnp.int32)
    padded = (cnt + TM_GROUP - 1) // TM_GROUP * TM_GROUP
    ends = jnp.cumsum(padded)
    starts = ends - padded
    experts = jnp.arange(N_EXPERTS, dtype=jnp.int32)[:, None]
    group_start = lambda e: jnp.sum(jnp.where(e[None, :] == experts, starts[:, None], 0), axis=0)
    pos1 = group_start(metat[META_E1].astype(jnp.int32)) + metat[META_R1].astype(jnp.int32)
    pos2 = group_start(metat[META_E2].astype(jnp.int32)) + metat[META_R2].astype(jnp.int32)
    n_used = (ends[-1] // TM_GROUP).astype(jnp.int32).reshape(1)
    tile_row = jnp.minimum(jnp.arange(n_rows // TM_GROUP, dtype=jnp.int32), n_used[0] - 1) * TM_GROUP
    tile_expert = jnp.sum(ends[None, :] <= tile_row[:, None], axis=1).astype(jnp.int32)
    tail_start = jnp.where(padded > 0, ends - TM_GROUP, -1)
    spare = ends[-1] + TM_GROUP * jnp.arange(N_EXPERTS, dtype=jnp.int32)
    fill_start = jnp.concatenate([tail_start, jnp.where(spare < n_rows, spare, -1)]).astype(jnp.int32)

    xs = _dispatch(fill_start, pos1, pos2, hf, n_rows)
    y = _grouped_swiglu(tile_expert, n_used, xs, wg, wu, wd)
    return _combine(pos1, pos2, x, meta, final_g, y)


def _arrange_in_proj(w):
    q = w[:, :ATTN_WIDTH] * (1.0 / math.sqrt(HEAD_DIM))
    dup = lambda start: [w[:, start + h * HEAD_DIM:start + (h + 1) * HEAD_DIM]
                         for h in range(ATTN_KV_HEADS) for _ in range(2)]
    return jnp.concatenate([q] + dup(ATTN_WIDTH) + dup(ATTN_WIDTH + KV_WIDTH) + [w[:, ATTN_WIDTH + 2 * KV_WIDTH:]],
                           axis=1)


def _block_diag(w):
    heads, d, _ = w.shape
    eye = jnp.eye(heads, dtype=w.dtype)
    return jnp.einsum('hij,hg->higj', w, eye).reshape(heads * d, heads * d)


def kernel(x, mem, rel_bias, mix_norm, w_in, attn_sinks, sc_conv_w, sc_conv_b, rg_conv_w, rg_conv_b, rg_w_a,
           rg_b_a, rg_w_x, rg_b_x, rg_lambda, w_out, xa_norm, mem_norm, xa_wq, xa_wk, xa_wv, xa_wo, ffn_norm,
           dense_wg, dense_wu, dense_wd, moe_router, moe_wg, moe_wu, moe_wd, final_norm):
    batch, seq, _ = x.shape
    depth = w_in.shape[0]
    assert depth == 2 and seq % TS == 0 and seq % TM == 0
    xt = x.reshape(batch * seq, D_MODEL)
    memt = mem.reshape(batch * MEM_LEN, D_MODEL)
    bias_tbl = _attention_bias_tables(rel_bias)
    vec = lambda a: a.reshape(1, -1)

    for layer in range(depth):
        q, k, v, rest = _in_proj(xt, vec(mix_norm[layer]), _arrange_in_proj(w_in[layer]).astype(BF16))
        attn = _attention(q, k, v, attn_sinks[layer], bias_tbl, seq // BLOCK)
        w_gate = jnp.concatenate([_block_diag(rg_w_a[layer]), _block_diag(rg_w_x[layer])], axis=1).astype(BF16)
        b_gate = jnp.concatenate([rg_b_a[layer], rg_b_x[layer]]).reshape(1, -1)
        cr = _conv_rg(rest, sc_conv_w[layer], vec(sc_conv_b[layer]), rg_conv_w[layer], vec(rg_conv_b[layer]),
                      w_gate, b_gate, vec(rg_lambda[layer]), batch, seq)
        wkv = jnp.concatenate([xa_wk[layer], xa_wv[layer]], axis=1).astype(BF16)
        mk, mv = _mem_kv(memt, vec(mem_norm[layer]), wkv)
        post = functools.partial(_post_mixer, xt, attn, cr, w_out[layer].astype(BF16), vec(xa_norm[layer]),
                                 xa_wq[layer].astype(BF16), mk, mv, xa_wo[layer].astype(BF16), seq)

        j = layer // 2
        if layer % 2 == 0:
            (xt,) = post()
            xt = _dense_ffn(xt, vec(ffn_norm[layer]), dense_wg[j].astype(BF16), dense_wu[j].astype(BF16),
                            dense_wd[j].astype(BF16))
        else:
            xt, hf, meta, metat, counts = post(route=_route_operands(vec(ffn_norm[layer]), moe_router[j]))
            xt = _moe_layer(xt, hf, meta, metat, counts, moe_wg[j].astype(BF16), moe_wu[j].astype(BF16),
                            moe_wd[j].astype(BF16), vec(final_norm))
    return xt.reshape(batch, seq, D_MODEL)
```

```python
import functools
import math

import jax
import jax.numpy as jnp
import numpy as np
from jax import lax
from jax.experimental import pallas as pl
from jax.experimental.pallas import tpu as pltpu

F32 = jnp.float32
BF16 = jnp.bfloat16

D_MODEL = 1024
MEM_LEN = 256
HEAD_DIM = 64
ATTN_Q_HEADS = 8
ATTN_KV_HEADS = 2
ATTN_WIDTH = ATTN_Q_HEADS * HEAD_DIM
KV_WIDTH = ATTN_KV_HEADS * HEAD_DIM
KV_DUP_WIDTH = 2 * KV_WIDTH
BLOCK = 128
SC_WIDTH = 256
SC_CONV = 3
RG_WIDTH = 256
RG_HEADS = 4
RG_HEAD_DIM = RG_WIDTH // RG_HEADS
RG_CONV = 4
RG_C = 8.0
N_BUCKETS = 32
MAX_EXACT = N_BUCKETS // 2
MAX_DISTANCE = 128
XA_HEADS = 4
XA_HEAD_DIM = 128
XA_WIDTH = XA_HEADS * XA_HEAD_DIM
N_EXPERTS = 8
EPS = 1e-6
NEG_INF = -1e30
REST_WIDTH = 3 * SC_WIDTH + 2 * RG_WIDTH

LANES_V7X = 128
SUBLANES_V7X = 8
VMEM_BYTES_V7X = 64 * 1024 * 1024
ROW_CHUNKS = D_MODEL // LANES_V7X
assert ROW_CHUNKS == SUBLANES_V7X

TM = 512
ATTN_BLOCKS = 8
TS = 512
ROUTE_CHUNK = 128
TM_GROUP = 512
TM_MOVE = 1024
TM_COMBINE = 256
ISSUE_UNROLL = 8
FF_SPLIT = 2
CARRY_ROWS = SUBLANES_V7X


def _mib(n):
    return int(n * 1024 * 1024)


def _params(semantics, vmem_mib):
    assert _mib(vmem_mib) < VMEM_BYTES_V7X
    return pltpu.CompilerParams(dimension_semantics=semantics, vmem_limit_bytes=_mib(vmem_mib))


def _rms(x, g):
    ms = jnp.mean(x * x, axis=-1, keepdims=True)
    return x * lax.rsqrt(ms + EPS) * g


def _const_spec(shape):
    nd = len(shape)
    return pl.BlockSpec(shape, lambda *_: (0,) * nd, pipeline_mode=pl.Buffered(1))


def _in_proj_kernel(x_ref, g_ref, w_ref, q_ref, k_ref, v_ref, r_ref):
    h = _rms(x_ref[...], g_ref[...]).astype(BF16)
    p = jnp.dot(h, w_ref[...], preferred_element_type=F32)
    q_ref[...] = p[:, :ATTN_WIDTH].astype(BF16)
    k_ref[...] = p[:, ATTN_WIDTH:ATTN_WIDTH + KV_DUP_WIDTH].astype(BF16)
    v_ref[...] = p[:, ATTN_WIDTH + KV_DUP_WIDTH:ATTN_WIDTH + 2 * KV_DUP_WIDTH].astype(BF16)
    r_ref[...] = p[:, ATTN_WIDTH + 2 * KV_DUP_WIDTH:]


def _in_proj(x, g, w):
    t = x.shape[0]
    n = w.shape[1]
    row = lambda width: pl.BlockSpec((TM, width), lambda i: (i, 0))
    return pl.pallas_call(
        _in_proj_kernel,
        grid=(t // TM,),
        in_specs=[row(D_MODEL), _const_spec((1, D_MODEL)), _const_spec((D_MODEL, n))],
        out_specs=[row(ATTN_WIDTH), row(KV_DUP_WIDTH), row(KV_DUP_WIDTH), row(REST_WIDTH)],
        out_shape=[jax.ShapeDtypeStruct((t, ATTN_WIDTH), BF16),
                   jax.ShapeDtypeStruct((t, KV_DUP_WIDTH), BF16),
                   jax.ShapeDtypeStruct((t, KV_DUP_WIDTH), BF16),
                   jax.ShapeDtypeStruct((t, REST_WIDTH), F32)],
        compiler_params=_params(("parallel",), 40),
        name="in_proj",
    )(x, g, w)


def _attn_kernel(sink_ref, q_ref, kp_ref, kc_ref, vp_ref, vc_ref, bias0_ref, bias_ref, o_ref):
    pairs_per_group = ATTN_Q_HEADS // ATTN_KV_HEADS // 2
    row = lax.broadcasted_iota(jnp.int32, (BLOCK, BLOCK), 0)
    col = lax.broadcasted_iota(jnp.int32, (BLOCK, BLOCK), 1)
    from_prev = col > row
    low_lanes = lax.broadcasted_iota(jnp.int32, (2 * BLOCK, 2 * HEAD_DIM), 1) < HEAD_DIM
    low_out = col < HEAD_DIM
    zero = jnp.zeros((), BF16)

    def block_diag(band):
        return jnp.concatenate([jnp.where(low_lanes, band, zero), jnp.where(low_lanes, zero, band)], axis=0)

    def scores(blk):
        rows = slice(blk * BLOCK, (blk + 1) * BLOCK)
        prev_rows = slice((blk - 1) * BLOCK, blk * BLOCK)
        out = []
        for g in range(ATTN_KV_HEADS):
            lanes = slice(g * 2 * HEAD_DIM, (g + 1) * 2 * HEAD_DIM)
            k_prev = kp_ref[:, lanes] if blk == 0 else kc_ref[prev_rows, lanes]
            v_prev = vp_ref[:, lanes] if blk == 0 else vc_ref[prev_rows, lanes]
            k_bd = block_diag(jnp.concatenate([k_prev, kc_ref[rows, lanes]], axis=0))
            v_bd = block_diag(jnp.concatenate([v_prev, vc_ref[rows, lanes]], axis=0))
            for pair in range(pairs_per_group):
                slab = g * pairs_per_group + pair
                q2 = q_ref[rows, slab * 2 * HEAD_DIM:(slab + 1) * 2 * HEAD_DIM]
                s = lax.dot_general(q2, k_bd, (((1,), (1,)), ((), ())), preferred_element_type=F32)
                out.append((slab, s, v_bd))
        return out

    def finish(blk, scored):
        rows = slice(blk * BLOCK, (blk + 1) * BLOCK)
        tbl_ref = bias0_ref if blk == 0 else bias_ref
        staged = []
        for slab, s, v_bd in scored:
            probs, denoms = [], []
            for side in range(2):
                h = 2 * slab + side
                sh = s[:, side * 2 * BLOCK:(side + 1) * 2 * BLOCK]
                logits = jnp.where(from_prev, sh[:, :BLOCK], sh[:, BLOCK:]) + tbl_ref[h]
                sink = sink_ref[h]
                m = jnp.maximum(jnp.max(logits, axis=-1, keepdims=True), sink)
                p = jnp.exp(logits - m)
                denoms.append(jnp.sum(p, axis=-1, keepdims=True) + jnp.exp(sink - m))
                probs += [jnp.where(from_prev, p, 0.0), jnp.where(from_prev, 0.0, p)]
            staged.append((slab, jnp.concatenate(probs, axis=-1).astype(BF16), v_bd, denoms))
        for slab, p_band, v_bd, denoms in staged:
            o = jnp.dot(p_band, v_bd, preferred_element_type=F32)
            o = o / jnp.where(low_out, denoms[0], denoms[1])
            o_ref[rows, slab * 2 * HEAD_DIM:(slab + 1) * 2 * HEAD_DIM] = o.astype(BF16)

    pending = scores(0)
    for blk in range(ATTN_BLOCKS):
        upcoming = scores(blk + 1) if blk + 1 < ATTN_BLOCKS else None
        finish(blk, pending)
        pending = upcoming


def _attention(q, k, v, sinks, bias_tbl, blocks_per_seq):
    t = q.shape[0]
    tile = ATTN_BLOCKS * BLOCK
    cur = lambda i: (i, 0)
    prev = lambda i: (jnp.maximum(i * ATTN_BLOCKS - 1, 0), 0)
    tbl = (None, ATTN_Q_HEADS, BLOCK, BLOCK)
    return pl.pallas_call(
        _attn_kernel,
        grid=(t // tile,),
        in_specs=[pl.BlockSpec(memory_space=pltpu.SMEM),
                  pl.BlockSpec((tile, ATTN_WIDTH), cur),
                  pl.BlockSpec((BLOCK, KV_DUP_WIDTH), prev),
                  pl.BlockSpec((tile, KV_DUP_WIDTH), cur),
                  pl.BlockSpec((BLOCK, KV_DUP_WIDTH), prev),
                  pl.BlockSpec((tile, KV_DUP_WIDTH), cur),
                  pl.BlockSpec(tbl, lambda i: (jnp.minimum((i * ATTN_BLOCKS) % blocks_per_seq, 1), 0, 0, 0)),
                  pl.BlockSpec(tbl, lambda i: (1, 0, 0, 0))],
        out_specs=pl.BlockSpec((tile, ATTN_WIDTH), cur),
        out_shape=jax.ShapeDtypeStruct((t, ATTN_WIDTH), BF16),
        compiler_params=_params(("parallel",), 24),
        name="swa_attention",
    )(sinks, q, k, k, v, v, bias_tbl, bias_tbl)


def _bias_table_kernel(rel_ref, bucket_ref, o_ref):
    for v in range(2):
        bucket = bucket_ref[v]
        hits = [bucket == b for b in range(N_BUCKETS)]
        for h in range(ATTN_Q_HEADS):
            tbl = jnp.full(bucket.shape, NEG_INF, F32)
            for b in range(N_BUCKETS):
                tbl = jnp.where(hits[b], rel_ref[b * ATTN_Q_HEADS + h], tbl)
            o_ref[v, h] = tbl


def _attention_bias_tables(rel_bias):
    q_idx = np.arange(BLOCK)[:, None]
    j_idx = np.arange(BLOCK)[None, :]
    from_prev = j_idx > q_idx
    n = np.where(from_prev, q_idx + BLOCK - j_idx, q_idx - j_idx)
    large = MAX_EXACT + (np.log(np.maximum(n, 1).astype(np.float32) / np.float32(MAX_EXACT))
                         / np.float32(math.log(MAX_DISTANCE / MAX_EXACT))
                         * np.float32(N_BUCKETS - MAX_EXACT)).astype(np.int32)
    bucket = np.where(n < MAX_EXACT, n, np.minimum(large, N_BUCKETS - 1))
    first = np.where(from_prev, -1, bucket)
    buckets = jnp.asarray(np.stack([first, bucket]).astype(np.int32))
    return pl.pallas_call(
        _bias_table_kernel,
        in_specs=[pl.BlockSpec(memory_space=pltpu.SMEM), pl.BlockSpec(memory_space=pltpu.VMEM)],
        out_specs=pl.BlockSpec(memory_space=pltpu.VMEM),
        out_shape=jax.ShapeDtypeStruct((2, ATTN_Q_HEADS, BLOCK, BLOCK), F32),
        name="t5_bias_table",
    )(rel_bias.astype(F32).reshape(-1), buckets)


def _shift_rows(x, s, fill):
    return jnp.concatenate([jnp.full((s, x.shape[1]), fill, x.dtype), x[:x.shape[0] - s]], axis=0)


def _conv_rg_kernel(r_ref, scw_ref, scb_ref, rgw_ref, rgb_ref, wgate_ref, bgate_ref, lam_ref,
                    o_ref, sc_ext, rg_ext, h_carry):
    ts = r_ref.shape[0]
    c0 = CARRY_ROWS

    @pl.when(pl.program_id(1) == 0)
    def _():
        sc_ext[0:c0, :] = jnp.zeros((c0, SC_WIDTH), F32)
        rg_ext[0:c0, :] = jnp.zeros((c0, RG_WIDTH), F32)
        h_carry[...] = jnp.zeros_like(h_carry)

    sc_b = r_ref[:, 0:SC_WIDTH]
    sc_ext[c0:c0 + ts, :] = r_ref[:, SC_WIDTH:2 * SC_WIDTH] * r_ref[:, 2 * SC_WIDTH:3 * SC_WIDTH]
    rg_ext[c0:c0 + ts, :] = r_ref[:, 3 * SC_WIDTH:3 * SC_WIDTH + RG_WIDTH]
    rg_g = r_ref[:, 3 * SC_WIDTH + RG_WIDTH:]

    conv = scb_ref[...]
    for k in range(SC_CONV):
        off = c0 - (SC_CONV - 1) + k
        conv = conv + scw_ref[k:k + 1, :] * sc_ext[off:off + ts, :]
    conv_out = sc_b * conv

    rg_in = rgb_ref[...]
    for k in range(RG_CONV):
        off = c0 - (RG_CONV - 1) + k
        rg_in = rg_in + rgw_ref[k:k + 1, :] * rg_ext[off:off + ts, :]

    sc_ext[0:c0, :] = sc_ext[ts:ts + c0, :]
    rg_ext[0:c0, :] = rg_ext[ts:ts + c0, :]

    gates = jnp.dot(rg_in.astype(BF16), wgate_ref[...], preferred_element_type=F32) + bgate_ref[...]
    r_gate = jax.nn.sigmoid(gates[:, :RG_WIDTH])
    i_gate = jax.nn.sigmoid(gates[:, RG_WIDTH:])
    neg_lam = -lam_ref[...]
    softplus = jnp.maximum(neg_lam, 0.0) + jnp.log1p(jnp.exp(-jnp.abs(neg_lam)))
    log_a = -RG_C * r_gate * softplus
    a = jnp.exp(log_a)
    u = jnp.sqrt(jnp.tanh(-log_a) * (1.0 + a * a)) * (i_gate * rg_in)

    s = 1
    while s < ts:
        u = a * _shift_rows(u, s, 0.0) + u
        a = a * _shift_rows(a, s, 1.0)
        s *= 2
    h = a * h_carry[...] + u
    h_carry[...] = h[ts - 1:ts, :]

    c = math.sqrt(2.0 / math.pi)
    gelu = 0.5 * rg_g * (1.0 + jnp.tanh(c * (rg_g + 0.044715 * (rg_g * rg_g * rg_g))))
    o_ref[:, 0:SC_WIDTH] = conv_out.astype(BF16)
    o_ref[:, SC_WIDTH:] = (h * gelu).astype(BF16)


def _conv_rg(rest, sc_w, sc_b, rg_w, rg_b, w_gate, b_gate, lam, batch, seq):
    t = rest.shape[0]
    steps = seq // TS
    row = lambda b, s: (b * steps + s, 0)
    return pl.pallas_call(
        _conv_rg_kernel,
        grid=(batch, steps),
        in_specs=[pl.BlockSpec((TS, REST_WIDTH), row),
                  _const_spec((SC_CONV, SC_WIDTH)), _const_spec((1, SC_WIDTH)),
                  _const_spec((RG_CONV, RG_WIDTH)), _const_spec((1, RG_WIDTH)),
                  _const_spec((RG_WIDTH, 2 * RG_WIDTH)), _const_spec((1, 2 * RG_WIDTH)),
                  _const_spec((1, RG_WIDTH))],
        out_specs=pl.BlockSpec((TS, SC_WIDTH + RG_WIDTH), row),
        out_shape=jax.ShapeDtypeStruct((t, SC_WIDTH + RG_WIDTH), BF16),
        scratch_shapes=[pltpu.VMEM((TS + 2 * CARRY_ROWS, SC_WIDTH), F32),
                        pltpu.VMEM((TS + 2 * CARRY_ROWS, RG_WIDTH), F32),
                        pltpu.VMEM((1, RG_WIDTH), F32)],
        compiler_params=_params(("arbitrary", "arbitrary"), 32),
        name="conv_rglru",
    )(rest, sc_w, sc_b, rg_w, rg_b, w_gate, b_gate, lam)


def _mem_kv_kernel(m_ref, g_ref, w_ref, k_ref, v_ref):
    h = _rms(m_ref[...], g_ref[...]).astype(BF16)
    p = jnp.dot(h, w_ref[...], preferred_element_type=F32)
    k_ref[...] = p[:, :XA_WIDTH].astype(BF16)
    v_ref[...] = p[:, XA_WIDTH:].astype(BF16)


def _mem_kv(mem, g, wkv):
    t = mem.shape[0]
    row = lambda width: pl.BlockSpec((MEM_LEN, width), lambda i: (i, 0))
    return pl.pallas_call(
        _mem_kv_kernel,
        grid=(t // MEM_LEN,),
        in_specs=[row(D_MODEL), _const_spec((1, D_MODEL)), _const_spec((D_MODEL, 2 * XA_WIDTH))],
        out_specs=[row(XA_WIDTH), row(XA_WIDTH)],
        out_shape=[jax.ShapeDtypeStruct((t, XA_WIDTH), BF16)] * 2,
        compiler_params=_params(("parallel",), 24),
        name="mem_kv",
    )(mem, g, wkv)


def _post_mixer_kernel(x_ref, a_ref, c_ref, wout_ref, g_ref, wq_ref, k_ref, v_ref, wo_ref, *rest):
    o_ref = rest[-1] if len(rest) == 1 else rest[4]
    halves = [slice(i * (TM // 2), (i + 1) * (TM // 2)) for i in range(2)]
    heads = [slice(hd * XA_HEAD_DIM, (hd + 1) * XA_HEAD_DIM) for hd in range(XA_HEADS)]
    nt = (((1,), (1,)), ((), ()))
    k = k_ref[...]
    v = v_ref[...]

    x1 = [x_ref[hs, :] + jnp.dot(jnp.concatenate([a_ref[hs, :], c_ref[hs, :]], axis=-1), wout_ref[...],
                                 preferred_element_type=F32) for hs in halves]
    q = [jnp.dot(_rms(xh, g_ref[...]).astype(BF16), wq_ref[...], preferred_element_type=F32).astype(BF16)
         for xh in x1]
    scores = [[lax.dot_general(qh[:, sl], k[:, sl], nt, preferred_element_type=F32) for sl in heads] for qh in q]
    x2 = []
    for xh, per_head in zip(x1, scores):
        probs, sums = [], []
        for s in per_head:
            s = s * (1.0 / math.sqrt(XA_HEAD_DIM))
            p = jnp.exp(s - jnp.max(s, axis=-1, keepdims=True))
            probs.append(p.astype(BF16))
            sums.append(jnp.sum(p, axis=-1, keepdims=True))
        att = jnp.concatenate([jnp.dot(p, v[:, sl], preferred_element_type=F32) / l
                               for p, sl, l in zip(probs, heads, sums)], axis=-1).astype(BF16)
        x2.append(xh + jnp.dot(att, wo_ref[...], preferred_element_type=F32))

    for hs, xh in zip(halves, x2):
        o_ref[hs, :] = xh
    if len(rest) > 1:
        ffn_g_ref, wr_ref, tri_ref, sel_ref, _, hf_ref, meta_ref, metat_ref, cnt_ref, carry = rest
        h = _rms(jnp.concatenate(x2, axis=0), ffn_g_ref[...])
        _route(h, wr_ref, tri_ref, sel_ref, hf_ref, meta_ref, metat_ref, cnt_ref, carry)


def _post_mixer(x, attn, cr, w_out, g, wq, k, v, wo, seq, route=None):
    t = x.shape[0]
    per_seq = seq // TM
    row = lambda width: pl.BlockSpec((TM, width), lambda i: (i, 0))
    mem_blk = pl.BlockSpec((MEM_LEN, XA_WIDTH), lambda i: (i // per_seq, 0))
    in_specs = [row(D_MODEL), row(ATTN_WIDTH), row(SC_WIDTH + RG_WIDTH), _const_spec((D_MODEL, D_MODEL)),
                _const_spec((1, D_MODEL)), _const_spec((D_MODEL, XA_WIDTH)), mem_blk, mem_blk,
                _const_spec((XA_WIDTH, D_MODEL))]
    out_specs = [row(D_MODEL)]
    out_shape = [jax.ShapeDtypeStruct((t, D_MODEL), F32)]
    scratch = []
    args = [x, attn, cr, w_out, g, wq, k, v, wo]
    if route is not None:
        assert TM % ROUTE_CHUNK == 0
        in_specs += [_const_spec((1, D_MODEL)), _const_spec((3 * D_MODEL, LANES_V7X)), _const_spec((ROUTE_CHUNK, ROUTE_CHUNK)),
                     _const_spec((SUBLANES_V7X, LANES_V7X))]
        out_specs += [pl.BlockSpec((TM * ROW_CHUNKS, LANES_V7X), lambda i: (i, 0)), row(LANES_V7X),
                      pl.BlockSpec((SUBLANES_V7X, TM), lambda i: (0, i)),
                      pl.BlockSpec((1, LANES_V7X), lambda i: (0, 0))]
        out_shape += [jax.ShapeDtypeStruct((t * ROW_CHUNKS, LANES_V7X), F32),
                      jax.ShapeDtypeStruct((t, LANES_V7X), F32),
                      jax.ShapeDtypeStruct((SUBLANES_V7X, t), F32),
                      jax.ShapeDtypeStruct((1, LANES_V7X), F32)]
        scratch = [pltpu.VMEM((1, LANES_V7X), F32)]
        args += list(route)
    return pl.pallas_call(
        _post_mixer_kernel,
        grid=(t // TM,),
        in_specs=in_specs,
        out_specs=out_specs,
        out_shape=out_shape,
        scratch_shapes=scratch,
        compiler_params=_params(("arbitrary",), 40),
        name="post_mixer_route" if route is not None else "post_mixer",
    )(*args)


def _ffn_kernel(x_ref, g_ref, wg_ref, wu_ref, wd_ref, *rest):
    n_cast = (len(rest) - 1) // 2
    o_ref = rest[n_cast]
    x = x_ref[...]
    h = _rms(x, g_ref[...]).astype(BF16)
    gate = jnp.dot(h, wg_ref[...], preferred_element_type=F32)
    up = jnp.dot(h, wu_ref[...], preferred_element_type=F32)
    act = (gate * jax.nn.sigmoid(gate) * up).astype(BF16)
    o_ref[...] = x + jnp.dot(act, wd_ref[...], preferred_element_type=F32)
    for src, dst in zip(rest[:n_cast], rest[n_cast + 1:]):
        dst[...] = src[...].astype(BF16)


def _dense_ffn(x, g, wg, wu, wd, cast_along=()):
    t = x.shape[0]
    steps = t // TM
    d_ff = wg.shape[1]
    row = pl.BlockSpec((TM, D_MODEL), lambda i: (i, 0))
    flat = [w.reshape(-1, w.shape[-1]) for w in cast_along]
    for w in flat:
        assert w.shape[0] % (steps * 2 * SUBLANES_V7X) == 0
    slabs = [pl.BlockSpec((w.shape[0] // steps, w.shape[1]), lambda i: (i, 0)) for w in flat]
    outs = pl.pallas_call(
        _ffn_kernel,
        grid=(steps,),
        in_specs=[row, _const_spec((1, D_MODEL)), _const_spec((D_MODEL, d_ff)), _const_spec((D_MODEL, d_ff)),
                  _const_spec((d_ff, D_MODEL))] + slabs,
        out_specs=[row] + slabs,
        out_shape=[jax.ShapeDtypeStruct((t, D_MODEL), F32)] + [jax.ShapeDtypeStruct(w.shape, BF16) for w in flat],
        compiler_params=_params(("parallel",), 60),
        name="dense_swiglu",
    )(x, g, wg, wu, wd, *flat)
    return outs[0], [o.reshape(w.shape) for o, w in zip(outs[1:], cast_along)]


META_E1, META_E2, META_R1, META_R2, META_W1, META_W2 = range(6)


def _to_token_tiles(ref, rows):
    m = rows.shape[0]
    for c in range(ROW_CHUNKS):
        ref[pl.ds(c, m, stride=ROW_CHUNKS), :] = rows[:, c * LANES_V7X:(c + 1) * LANES_V7X]


def _from_token_tiles(ref):
    m = ref.shape[0] // ROW_CHUNKS
    return jnp.concatenate([ref[pl.ds(c, m, stride=ROW_CHUNKS), :] for c in range(ROW_CHUNKS)], axis=-1)


def _route(h, wr_ref, tri_ref, sel_ref, hf_ref, meta_ref, metat_ref, cnt_ref, carry):
    @pl.when(pl.program_id(0) == 0)
    def _():
        carry[...] = jnp.zeros_like(carry)

    _to_token_tiles(hf_ref, h)
    chunks = [slice(c * ROUTE_CHUNK, (c + 1) * ROUTE_CHUNK) for c in range(h.shape[0] // ROUTE_CHUNK)]
    each = lambda fn, *lists: [fn(*vals) for vals in zip(*lists)]
    rowmax = lambda a: jnp.max(a, axis=-1, keepdims=True)
    rowsum = lambda a: jnp.sum(a, axis=-1, keepdims=True)
    lane = lax.broadcasted_iota(jnp.int32, (ROUTE_CHUNK, LANES_V7X), 1)
    first_hit = lambda lg, m: jnp.min(jnp.where(lg == m, lane, LANES_V7X), axis=-1, keepdims=True)

    h_hi = h.astype(BF16)
    h_lo = (h - h_hi.astype(F32)).astype(BF16)
    lhs = jnp.concatenate([h_hi, h_lo, h_hi], axis=-1)
    logits = [jnp.dot(lhs[c, :], wr_ref[...], preferred_element_type=F32) for c in chunks]
    lg = each(lambda l: jnp.where(lane < N_EXPERTS, l, -jnp.inf), logits)
    m1 = each(rowmax, lg)
    e1 = each(first_hit, lg, m1)
    lg2 = each(lambda l, e: jnp.where(lane == e, -jnp.inf, l), lg, e1)
    m2 = each(rowmax, lg2)
    e2 = each(first_hit, lg2, m2)
    ex = each(lambda a, b: jnp.exp(b - a), m1, m2)
    w1 = each(lambda e: 1.0 / (1.0 + e), ex)
    w2 = each(lambda e: e / (1.0 + e), ex)
    hit1 = each(lambda e: lane == e, e1)
    hit2 = each(lambda e: lane == e, e2)
    onehot = each(lambda a, b: (a | b).astype(BF16), hit1, hit2)
    within = each(lambda o: jnp.dot(tri_ref[...], o, preferred_element_type=F32), onehot)
    totals = each(lambda o: jnp.sum(o.astype(F32), axis=0, keepdims=True), onehot)
    ahead, base = [], carry[...]
    for w, tot in zip(within, totals):
        ahead.append(w + base)
        base = base + tot
    carry[...] = base
    cnt_ref[...] = base
    r1 = each(lambda hit, a: rowsum(jnp.where(hit, a, 0.0)), hit1, ahead)
    r2 = each(lambda hit, a: rowsum(jnp.where(hit, a, 0.0)), hit2, ahead)

    def record(*vals):
        meta = jnp.zeros((ROUTE_CHUNK, LANES_V7X), F32)
        for col, val in zip((META_E1, META_E2, META_R1, META_R2, META_W1, META_W2), vals):
            meta = jnp.where(lane == col, val.astype(F32), meta)
        return meta

    meta = each(record, e1, e2, r1, r2, w1, w2)
    metat = each(lambda mt: lax.dot_general(sel_ref[...], mt, (((1,), (1,)), ((), ())), preferred_element_type=F32,
                                            precision=lax.Precision.HIGHEST), meta)
    for c, mt, mtt in zip(chunks, meta, metat):
        meta_ref[c, :] = mt
        metat_ref[:, c] = mtt


def _token_rows(ref, first_token, n_tokens):
    start = pl.multiple_of(first_token * ROW_CHUNKS, ROW_CHUNKS)
    return ref.at[pl.ds(start, n_tokens * ROW_CHUNKS)]


def _row_copy(src_ref, src_token, dst_ref, dst_token, sem):
    return pltpu.make_async_copy(_token_rows(src_ref, src_token, 1), _token_rows(dst_ref, dst_token, 1), sem)


def _dispatch_kernel(fill_ref, pos1_ref, pos2_ref, hf_ref, xs_ref, zeros, sem_z, sem):
    @pl.when(pl.program_id(0) == 0)
    def _():
        zeros[...] = jnp.zeros_like(zeros)

        def tile_fill(e):
            return pltpu.make_async_copy(zeros, _token_rows(xs_ref, pl.multiple_of(fill_ref[e], TM_GROUP), TM_GROUP),
                                         sem_z)

        for e in range(2 * N_EXPERTS):
            @pl.when(fill_ref[e] >= 0)
            def _():
                tile_fill(e).start()
        for e in range(2 * N_EXPERTS):
            @pl.when(fill_ref[e] >= 0)
            def _():
                tile_fill(e).wait()

    def issue(r, carry):
        _row_copy(hf_ref, r, xs_ref, pos1_ref[r], sem).start(priority=0)
        _row_copy(hf_ref, r, xs_ref, pos2_ref[r], sem).start(priority=1)
        return carry

    lax.fori_loop(0, TM_MOVE, issue, 0, unroll=ISSUE_UNROLL)
    whole_step = pltpu.make_async_copy(hf_ref, _token_rows(xs_ref, 0, TM_MOVE), sem)
    whole_step.wait()
    whole_step.wait()


def _dispatch(fill_start, pos1, pos2, hf, n_rows):
    t = hf.shape[0] // ROW_CHUNKS
    idx = pl.BlockSpec((TM_MOVE,), lambda i, fill: (i,), memory_space=pltpu.SMEM)
    return pl.pallas_call(
        _dispatch_kernel,
        grid_spec=pltpu.PrefetchScalarGridSpec(
            num_scalar_prefetch=1,
            grid=(t // TM_MOVE,),
            in_specs=[idx, idx, pl.BlockSpec((TM_MOVE * ROW_CHUNKS, LANES_V7X), lambda i, fill: (i, 0))],
            out_specs=pl.BlockSpec(memory_space=pl.ANY),
            scratch_shapes=[pltpu.VMEM((TM_GROUP * ROW_CHUNKS, LANES_V7X), F32), pltpu.SemaphoreType.DMA(()),
                            pltpu.SemaphoreType.DMA(())]),
        out_shape=jax.ShapeDtypeStruct((n_rows * ROW_CHUNKS, LANES_V7X), F32),
        compiler_params=_params(("arbitrary",), 24),
        name="moe_dispatch",
    )(fill_start, pos1, pos2, hf)


def _grouped_kernel(te_ref, used_ref, x_ref, wg_ref, wu_ref, wd_ref, y_ref):
    i = pl.program_id(0)

    @pl.when(i < used_ref[0])
    def _():
        h = _from_token_tiles(x_ref).astype(BF16)
        ff = wg_ref.shape[1] // FF_SPLIT
        cols = [slice(c * ff, (c + 1) * ff) for c in range(FF_SPLIT)]
        gates = [jnp.dot(h, wg_ref[:, c], preferred_element_type=F32) for c in cols]
        ups = [jnp.dot(h, wu_ref[:, c], preferred_element_type=F32) for c in cols]
        total = None
        for c, gate, up in zip(cols, gates, ups):
            act = (gate * jax.nn.sigmoid(gate) * up).astype(BF16)
            part = jnp.dot(act, wd_ref[c, :], preferred_element_type=F32)
            total = part if total is None else total + part
        _to_token_tiles(y_ref, total)

    @pl.when(i >= used_ref[0])
    def _():
        y_ref[...] = jnp.zeros_like(y_ref)


def _grouped_swiglu(tile_expert, n_used, xs, wg, wu, wd):
    n_rows = xs.shape[0] // ROW_CHUNKS
    d_ff = wg.shape[2]
    tile = (TM_GROUP * ROW_CHUNKS, LANES_V7X)
    src = lambda i, te, used: (jnp.maximum(jnp.minimum(i, used[0] - 1), 0), 0)
    expert = lambda shape: pl.BlockSpec((None,) + shape, lambda i, te, used: (te[i], 0, 0),
                                        pipeline_mode=pl.Buffered(1))
    return pl.pallas_call(
        _grouped_kernel,
        grid_spec=pltpu.PrefetchScalarGridSpec(
            num_scalar_prefetch=2,
            grid=(n_rows // TM_GROUP,),
            in_specs=[pl.BlockSpec(tile, src), expert((D_MODEL, d_ff)), expert((D_MODEL, d_ff)),
                      expert((d_ff, D_MODEL))],
            out_specs=pl.BlockSpec(tile, lambda i, te, used: (i, 0))),
        out_shape=jax.ShapeDtypeStruct((n_rows * ROW_CHUNKS, LANES_V7X), F32),
        compiler_params=_params(("arbitrary",), 58),
        name="moe_grouped_swiglu",
    )(tile_expert, n_used, xs, wg, wu, wd)


def _combine_kernel(pos1_ref, pos2_ref, pos1_next_ref, pos2_next_ref, x_ref, meta_ref, g_ref, y_ref, o_ref,
                    y1, y2, sems):
    i = pl.program_id(0)

    def gather(p1_ref, p2_ref, slot):
        def issue(r, carry):
            _row_copy(y_ref, p1_ref[r], y1.at[slot], r, sems.at[slot]).start(priority=0)
            _row_copy(y_ref, p2_ref[r], y2.at[slot], r, sems.at[slot]).start(priority=1)
            return carry
        lax.fori_loop(0, TM_COMBINE, issue, 0, unroll=ISSUE_UNROLL)

    def finish(slot):
        for buf in (y1, y2):
            pltpu.make_async_copy(_token_rows(y_ref, 0, TM_COMBINE), buf.at[slot], sems.at[slot]).wait()
        w1 = meta_ref[:, META_W1:META_W1 + 1]
        w2 = meta_ref[:, META_W2:META_W2 + 1]
        out = x_ref[...] + (w1 * _from_token_tiles(y1.at[slot]) + w2 * _from_token_tiles(y2.at[slot]))
        o_ref[...] = _rms(out, g_ref[...])

    @pl.when(i == 0)
    def _():
        gather(pos1_ref, pos2_ref, 0)

    for slot in range(2):
        @pl.when(i % 2 == slot)
        def _():
            @pl.when(i + 1 < pl.num_programs(0))
            def _():
                gather(pos1_next_ref, pos2_next_ref, 1 - slot)
            finish(slot)


def _combine(pos1, pos2, x, meta, g, y):
    t = x.shape[0]
    steps = t // TM_COMBINE
    idx = pl.BlockSpec((TM_COMBINE,), lambda i: (i,), memory_space=pltpu.SMEM)
    idx_next = pl.BlockSpec((TM_COMBINE,), lambda i: (jnp.minimum(i + 1, steps - 1),), memory_space=pltpu.SMEM)
    row = lambda width: pl.BlockSpec((TM_COMBINE, width), lambda i: (i, 0))
    slots = pltpu.VMEM((2, TM_COMBINE * ROW_CHUNKS, LANES_V7X), F32)
    return pl.pallas_call(
        _combine_kernel,
        grid=(steps,),
        in_specs=[idx, idx, idx_next, idx_next, row(D_MODEL), row(LANES_V7X), _const_spec((1, D_MODEL)),
                  pl.BlockSpec(memory_space=pl.ANY)],
        out_specs=row(D_MODEL),
        out_shape=jax.ShapeDtypeStruct((t, D_MODEL), F32),
        scratch_shapes=[slots, slots, pltpu.SemaphoreType.DMA((2,))],
        compiler_params=_params(("arbitrary",), 24),
        name="moe_combine_norm",
    )(pos1, pos2, pos1, pos2, x, meta, g, y)


def _route_operands(ffn_g, router_w):
    wr_pad = jnp.zeros((D_MODEL, LANES_V7X), F32).at[:, :N_EXPERTS].set(router_w)
    wr_hi = wr_pad.astype(BF16)
    wr_lo = (wr_pad - wr_hi.astype(F32)).astype(BF16)
    wr_split = jnp.concatenate([wr_hi, wr_hi, wr_lo], axis=0)
    tri = jnp.tril(jnp.ones((ROUTE_CHUNK, ROUTE_CHUNK), BF16), -1)
    sel = jnp.eye(SUBLANES_V7X, LANES_V7X, dtype=F32)
    return ffn_g, wr_split, tri, sel


def _moe_layer(x, hf, meta, metat, counts, wg, wu, wd, final_g):
    t = x.shape[0]
    n_rows = 2 * t + N_EXPERTS * TM_GROUP

    cnt = counts[0, :N_EXPERTS].astype(jnp.int32)
    padded = (cnt + TM_GROUP - 1) // TM_GROUP * TM_GROUP
    ends = jnp.cumsum(padded)
    starts = ends - padded
    experts = jnp.arange(N_EXPERTS, dtype=jnp.int32)[:, None]
    group_start = lambda e: jnp.sum(jnp.where(e[None, :] == experts, starts[:, None], 0), axis=0)
    pos1 = group_start(metat[META_E1].astype(jnp.int32)) + metat[META_R1].astype(jnp.int32)
    pos2 = group_start(metat[META_E2].astype(jnp.int32)) + metat[META_R2].astype(jnp.int32)
    n_used = (ends[-1] // TM_GROUP).astype(jnp.int32).reshape(1)
    tile_row = jnp.minimum(jnp.arange(n_rows // TM_GROUP, dtype=jnp.int32), n_used[0] - 1) * TM_GROUP
    tile_expert = jnp.sum(ends[None, :] <= tile_row[:, None], axis=1).astype(jnp.int32)
    tail_start = jnp.where(padded > 0, ends - TM_GROUP, -1)
    spare = ends[-1] + TM_GROUP * jnp.arange(N_EXPERTS, dtype=jnp.int32)
    fill_start = jnp.concatenate([tail_start, jnp.where(spare < n_rows, spare, -1)]).astype(jnp.int32)

    xs = _dispatch(fill_start, pos1, pos2, hf, n_rows)
    y = _grouped_swiglu(tile_expert, n_used, xs, wg, wu, wd)
    return _combine(pos1, pos2, x, meta, final_g, y)


def _arrange_in_proj(w):
    q = w[:, :ATTN_WIDTH] * (1.0 / math.sqrt(HEAD_DIM))
    dup = lambda start: [w[:, start + h * HEAD_DIM:start + (h + 1) * HEAD_DIM]
                         for h in range(ATTN_KV_HEADS) for _ in range(2)]
    return jnp.concatenate([q] + dup(ATTN_WIDTH) + dup(ATTN_WIDTH + KV_WIDTH) + [w[:, ATTN_WIDTH + 2 * KV_WIDTH:]],
                           axis=1)


def _block_diag(w):
    heads, d, _ = w.shape
    eye = jnp.eye(heads, dtype=w.dtype)
    return jnp.einsum('hij,hg->higj', w, eye).reshape(heads * d, heads * d)


def kernel(x, mem, rel_bias, mix_norm, w_in, attn_sinks, sc_conv_w, sc_conv_b, rg_conv_w, rg_conv_b, rg_w_a,
           rg_b_a, rg_w_x, rg_b_x, rg_lambda, w_out, xa_norm, mem_norm, xa_wq, xa_wk, xa_wv, xa_wo, ffn_norm,
           dense_wg, dense_wu, dense_wd, moe_router, moe_wg, moe_wu, moe_wd, final_norm):
    batch, seq, _ = x.shape
    depth = w_in.shape[0]
    assert depth == 2 and seq % TS == 0 and seq % TM == 0
    xt = x.reshape(batch * seq, D_MODEL)
    memt = mem.reshape(batch * MEM_LEN, D_MODEL)
    bias_tbl = _attention_bias_tables(rel_bias)
    vec = lambda a: a.reshape(1, -1)

    for layer in range(depth):
        q, k, v, rest = _in_proj(xt, vec(mix_norm[layer]), _arrange_in_proj(w_in[layer]).astype(BF16))
        attn = _attention(q, k, v, attn_sinks[layer], bias_tbl, seq // BLOCK)
        w_gate = jnp.concatenate([_block_diag(rg_w_a[layer]), _block_diag(rg_w_x[layer])], axis=1).astype(BF16)
        b_gate = jnp.concatenate([rg_b_a[layer], rg_b_x[layer]]).reshape(1, -1)
        cr = _conv_rg(rest, sc_conv_w[layer], vec(sc_conv_b[layer]), rg_conv_w[layer], vec(rg_conv_b[layer]),
                      w_gate, b_gate, vec(rg_lambda[layer]), batch, seq)
        wkv = jnp.concatenate([xa_wk[layer], xa_wv[layer]], axis=1).astype(BF16)
        mk, mv = _mem_kv(memt, vec(mem_norm[layer]), wkv)
        post = functools.partial(_post_mixer, xt, attn, cr, w_out[layer].astype(BF16), vec(xa_norm[layer]),
                                 xa_wq[layer].astype(BF16), mk, mv, xa_wo[layer].astype(BF16), seq)

        j = layer // 2
        if layer % 2 == 0:
            (xt,) = post()
            xt, expert_w = _dense_ffn(xt, vec(ffn_norm[layer]), dense_wg[j].astype(BF16), dense_wu[j].astype(BF16),
                                      dense_wd[j].astype(BF16), cast_along=(moe_wg[j], moe_wu[j], moe_wd[j]))
        else:
            xt, hf, meta, metat, counts = post(route=_route_operands(vec(ffn_norm[layer]), moe_router[j]))
            xt = _moe_layer(xt, hf, meta, metat, counts, *expert_w, vec(final_norm))
    return xt.reshape(batch, seq, D_MODEL)
```

```python
import functools
import math

import jax
import jax.numpy as jnp
import numpy as np
from jax import lax
from jax.experimental import pallas as pl
from jax.experimental.pallas import tpu as pltpu

F32 = jnp.float32
BF16 = jnp.bfloat16

D_MODEL = 1024
MEM_LEN = 256
HEAD_DIM = 64
ATTN_Q_HEADS = 8
ATTN_KV_HEADS = 2
ATTN_WIDTH = ATTN_Q_HEADS * HEAD_DIM
KV_WIDTH = ATTN_KV_HEADS * HEAD_DIM
KV_DUP_WIDTH = 2 * KV_WIDTH
BLOCK = 128
SC_WIDTH = 256
SC_CONV = 3
RG_WIDTH = 256
RG_HEADS = 4
RG_HEAD_DIM = RG_WIDTH // RG_HEADS
RG_CONV = 4
RG_C = 8.0
N_BUCKETS = 32
MAX_EXACT = N_BUCKETS // 2
MAX_DISTANCE = 128
XA_HEADS = 4
XA_HEAD_DIM = 128
XA_WIDTH = XA_HEADS * XA_HEAD_DIM
N_EXPERTS = 8
EPS = 1e-6
NEG_INF = -1e30
REST_WIDTH = 3 * SC_WIDTH + 2 * RG_WIDTH

LANES_V7X = 128
SUBLANES_V7X = 8
VMEM_BYTES_V7X = 64 * 1024 * 1024
ROW_CHUNKS = D_MODEL // LANES_V7X
assert ROW_CHUNKS == SUBLANES_V7X

TM = 512
ATTN_BLOCKS = 8
TS = 512
ROUTE_CHUNK = 128
TM_GROUP = 512
TM_MOVE = 1024
TM_COMBINE = 256
ISSUE_UNROLL = 8
FF_SPLIT = 2
CARRY_ROWS = SUBLANES_V7X


def _mib(n):
    return int(n * 1024 * 1024)


def _params(semantics, vmem_mib):
    assert _mib(vmem_mib) < VMEM_BYTES_V7X
    return pltpu.CompilerParams(dimension_semantics=semantics, vmem_limit_bytes=_mib(vmem_mib))


def _rms(x, g):
    ms = jnp.mean(x * x, axis=-1, keepdims=True)
    return x * lax.rsqrt(ms + EPS) * g


def _const_spec(shape):
    nd = len(shape)
    return pl.BlockSpec(shape, lambda *_: (0,) * nd, pipeline_mode=pl.Buffered(1))


def _attn_kernel(sink_ref, q_ref, kp_ref, kc_ref, vp_ref, vc_ref, bias0_ref, bias_ref, o_ref):
    pairs_per_group = ATTN_Q_HEADS // ATTN_KV_HEADS // 2
    row = lax.broadcasted_iota(jnp.int32, (BLOCK, BLOCK), 0)
    col = lax.broadcasted_iota(jnp.int32, (BLOCK, BLOCK), 1)
    from_prev = col > row
    low_lanes = lax.broadcasted_iota(jnp.int32, (2 * BLOCK, 2 * HEAD_DIM), 1) < HEAD_DIM
    low_out = col < HEAD_DIM
    zero = jnp.zeros((), BF16)

    def block_diag(band):
        return jnp.concatenate([jnp.where(low_lanes, band, zero), jnp.where(low_lanes, zero, band)], axis=0)

    def scores(blk):
        rows = slice(blk * BLOCK, (blk + 1) * BLOCK)
        prev_rows = slice((blk - 1) * BLOCK, blk * BLOCK)
        out = []
        for g in range(ATTN_KV_HEADS):
            lanes = slice(g * 2 * HEAD_DIM, (g + 1) * 2 * HEAD_DIM)
            k_prev = kp_ref[:, lanes] if blk == 0 else kc_ref[prev_rows, lanes]
            v_prev = vp_ref[:, lanes] if blk == 0 else vc_ref[prev_rows, lanes]
            k_bd = block_diag(jnp.concatenate([k_prev, kc_ref[rows, lanes]], axis=0))
            v_bd = block_diag(jnp.concatenate([v_prev, vc_ref[rows, lanes]], axis=0))
            for pair in range(pairs_per_group):
                slab = g * pairs_per_group + pair
                q2 = q_ref[rows, slab * 2 * HEAD_DIM:(slab + 1) * 2 * HEAD_DIM]
                s = lax.dot_general(q2, k_bd, (((1,), (1,)), ((), ())), preferred_element_type=F32)
                out.append((slab, s, v_bd))
        return out

    def finish(blk, scored):
        rows = slice(blk * BLOCK, (blk + 1) * BLOCK)
        tbl_ref = bias0_ref if blk == 0 else bias_ref
        staged = []
        for slab, s, v_bd in scored:
            probs, denoms = [], []
            for side in range(2):
                h = 2 * slab + side
                sh = s[:, side * 2 * BLOCK:(side + 1) * 2 * BLOCK]
                logits = jnp.where(from_prev, sh[:, :BLOCK], sh[:, BLOCK:]) + tbl_ref[h]
                sink = sink_ref[h]
                m = jnp.maximum(jnp.max(logits, axis=-1, keepdims=True), sink)
                p = jnp.exp(logits - m)
                denoms.append(jnp.sum(p, axis=-1, keepdims=True) + jnp.exp(sink - m))
                probs += [jnp.where(from_prev, p, 0.0), jnp.where(from_prev, 0.0, p)]
            staged.append((slab, jnp.concatenate(probs, axis=-1).astype(BF16), v_bd, denoms))
        for slab, p_band, v_bd, denoms in staged:
            o = jnp.dot(p_band, v_bd, preferred_element_type=F32)
            o = o / jnp.where(low_out, denoms[0], denoms[1])
            o_ref[rows, slab * 2 * HEAD_DIM:(slab + 1) * 2 * HEAD_DIM] = o.astype(BF16)

    pending = scores(0)
    for blk in range(ATTN_BLOCKS):
        upcoming = scores(blk + 1) if blk + 1 < ATTN_BLOCKS else None
        finish(blk, pending)
        pending = upcoming


def _attention(q, k, v, sinks, bias_tbl, blocks_per_seq):
    t = q.shape[0]
    tile = ATTN_BLOCKS * BLOCK
    cur = lambda i: (i, 0)
    prev = lambda i: (jnp.maximum(i * ATTN_BLOCKS - 1, 0), 0)
    tbl = (None, ATTN_Q_HEADS, BLOCK, BLOCK)
    return pl.pallas_call(
        _attn_kernel,
        grid=(t // tile,),
        in_specs=[pl.BlockSpec(memory_space=pltpu.SMEM),
                  pl.BlockSpec((tile, ATTN_WIDTH), cur),
                  pl.BlockSpec((BLOCK, KV_DUP_WIDTH), prev),
                  pl.BlockSpec((tile, KV_DUP_WIDTH), cur),
                  pl.BlockSpec((BLOCK, KV_DUP_WIDTH), prev),
                  pl.BlockSpec((tile, KV_DUP_WIDTH), cur),
                  pl.BlockSpec(tbl, lambda i: (jnp.minimum((i * ATTN_BLOCKS) % blocks_per_seq, 1), 0, 0, 0)),
                  pl.BlockSpec(tbl, lambda i: (1, 0, 0, 0))],
        out_specs=pl.BlockSpec((tile, ATTN_WIDTH), cur),
        out_shape=jax.ShapeDtypeStruct((t, ATTN_WIDTH), BF16),
        compiler_params=_params(("parallel",), 24),
        name="swa_attention",
    )(sinks, q, k, k, v, v, bias_tbl, bias_tbl)


def _bias_table_kernel(rel_ref, bucket_ref, o_ref):
    for v in range(2):
        bucket = bucket_ref[v]
        hits = [bucket == b for b in range(N_BUCKETS)]
        for h in range(ATTN_Q_HEADS):
            tbl = jnp.full(bucket.shape, NEG_INF, F32)
            for b in range(N_BUCKETS):
                tbl = jnp.where(hits[b], rel_ref[b * ATTN_Q_HEADS + h], tbl)
            o_ref[v, h] = tbl


def _attention_bias_tables(rel_bias):
    q_idx = np.arange(BLOCK)[:, None]
    j_idx = np.arange(BLOCK)[None, :]
    from_prev = j_idx > q_idx
    n = np.where(from_prev, q_idx + BLOCK - j_idx, q_idx - j_idx)
    large = MAX_EXACT + (np.log(np.maximum(n, 1).astype(np.float32) / np.float32(MAX_EXACT))
                         / np.float32(math.log(MAX_DISTANCE / MAX_EXACT))
                         * np.float32(N_BUCKETS - MAX_EXACT)).astype(np.int32)
    bucket = np.where(n < MAX_EXACT, n, np.minimum(large, N_BUCKETS - 1))
    first = np.where(from_prev, -1, bucket)
    buckets = jnp.asarray(np.stack([first, bucket]).astype(np.int32))
    return pl.pallas_call(
        _bias_table_kernel,
        in_specs=[pl.BlockSpec(memory_space=pltpu.SMEM), pl.BlockSpec(memory_space=pltpu.VMEM)],
        out_specs=pl.BlockSpec(memory_space=pltpu.VMEM),
        out_shape=jax.ShapeDtypeStruct((2, ATTN_Q_HEADS, BLOCK, BLOCK), F32),
        name="t5_bias_table",
    )(rel_bias.astype(F32).reshape(-1), buckets)


def _shift_rows(x, s, fill):
    return jnp.concatenate([jnp.full((s, x.shape[1]), fill, x.dtype), x[:x.shape[0] - s]], axis=0)


def _in_proj_conv_kernel(x_ref, g_ref, w_ref, scw_ref, scb_ref, rgw_ref, rgb_ref, wgate_ref, bgate_ref, lam_ref,
                         q_ref, k_ref, v_ref, o_ref, r_ref, sc_ext, rg_ext, h_carry, *, tiles_per_seq):
    i = pl.program_id(0)
    ts = r_ref.shape[0]
    c0 = CARRY_ROWS

    @pl.when(i == 0)
    def _():
        r_ref[...] = jnp.zeros_like(r_ref)

    @pl.when((i == 0) | ((i + tiles_per_seq - 1) % tiles_per_seq == 0))
    def _():
        sc_ext[0:c0, :] = jnp.zeros((c0, SC_WIDTH), F32)
        rg_ext[0:c0, :] = jnp.zeros((c0, RG_WIDTH), F32)
        h_carry[...] = jnp.zeros_like(h_carry)

    attn_cols = ATTN_WIDTH + 2 * KV_DUP_WIDTH
    hx = _rms(x_ref[...], g_ref[...]).astype(BF16)
    p = jnp.dot(hx, w_ref[:, :attn_cols], preferred_element_type=F32)
    q_ref[...] = p[:, :ATTN_WIDTH].astype(BF16)
    k_ref[...] = p[:, ATTN_WIDTH:ATTN_WIDTH + KV_DUP_WIDTH].astype(BF16)
    v_ref[...] = p[:, ATTN_WIDTH + KV_DUP_WIDTH:].astype(BF16)

    sc_b = r_ref[:, 0:SC_WIDTH]
    sc_ext[c0:c0 + ts, :] = r_ref[:, SC_WIDTH:2 * SC_WIDTH] * r_ref[:, 2 * SC_WIDTH:3 * SC_WIDTH]
    rg_ext[c0:c0 + ts, :] = r_ref[:, 3 * SC_WIDTH:3 * SC_WIDTH + RG_WIDTH]
    rg_g = r_ref[:, 3 * SC_WIDTH + RG_WIDTH:]

    conv = scb_ref[...]
    for k in range(SC_CONV):
        off = c0 - (SC_CONV - 1) + k
        conv = conv + scw_ref[k:k + 1, :] * sc_ext[off:off + ts, :]
    conv_out = sc_b * conv

    rg_in = rgb_ref[...]
    for k in range(RG_CONV):
        off = c0 - (RG_CONV - 1) + k
        rg_in = rg_in + rgw_ref[k:k + 1, :] * rg_ext[off:off + ts, :]

    sc_ext[0:c0, :] = sc_ext[ts:ts + c0, :]
    rg_ext[0:c0, :] = rg_ext[ts:ts + c0, :]

    gates = jnp.dot(rg_in.astype(BF16), wgate_ref[...], preferred_element_type=F32) + bgate_ref[...]

    r_ref[...] = jnp.dot(hx, w_ref[:, attn_cols:], preferred_element_type=F32)

    r_gate = jax.nn.sigmoid(gates[:, :RG_WIDTH])
    i_gate = jax.nn.sigmoid(gates[:, RG_WIDTH:])
    neg_lam = -lam_ref[...]
    softplus = jnp.maximum(neg_lam, 0.0) + jnp.log1p(jnp.exp(-jnp.abs(neg_lam)))
    log_a = -RG_C * r_gate * softplus
    a = jnp.exp(log_a)
    u = jnp.sqrt(jnp.tanh(-log_a) * (1.0 + a * a)) * (i_gate * rg_in)

    s = 1
    while s < ts:
        u = a * _shift_rows(u, s, 0.0) + u
        a = a * _shift_rows(a, s, 1.0)
        s *= 2
    h = a * h_carry[...] + u
    h_carry[...] = h[ts - 1:ts, :]

    c = math.sqrt(2.0 / math.pi)
    gelu = 0.5 * rg_g * (1.0 + jnp.tanh(c * (rg_g + 0.044715 * (rg_g * rg_g * rg_g))))
    o_ref[:, 0:SC_WIDTH] = conv_out.astype(BF16)
    o_ref[:, SC_WIDTH:] = (h * gelu).astype(BF16)


def _in_proj_conv(x, g, w, sc_w, sc_b, rg_w, rg_b, w_gate, b_gate, lam, seq):
    t = x.shape[0]
    n = w.shape[1]
    tiles = t // TS
    assert n == ATTN_WIDTH + 2 * KV_DUP_WIDTH + REST_WIDTH
    proj = lambda width: pl.BlockSpec((TS, width), lambda i: (jnp.minimum(i, tiles - 1), 0))
    lagged = pl.BlockSpec((TS, SC_WIDTH + RG_WIDTH), lambda i: (jnp.maximum(i - 1, 0), 0))
    return pl.pallas_call(
        functools.partial(_in_proj_conv_kernel, tiles_per_seq=seq // TS),
        grid=(tiles + 1,),
        in_specs=[proj(D_MODEL), _const_spec((1, D_MODEL)), _const_spec((D_MODEL, n)),
                  _const_spec((SC_CONV, SC_WIDTH)), _const_spec((1, SC_WIDTH)),
                  _const_spec((RG_CONV, RG_WIDTH)), _const_spec((1, RG_WIDTH)),
                  _const_spec((RG_WIDTH, 2 * RG_WIDTH)), _const_spec((1, 2 * RG_WIDTH)),
                  _const_spec((1, RG_WIDTH))],
        out_specs=[proj(ATTN_WIDTH), proj(KV_DUP_WIDTH), proj(KV_DUP_WIDTH), lagged],
        out_shape=[jax.ShapeDtypeStruct((t, ATTN_WIDTH), BF16),
                   jax.ShapeDtypeStruct((t, KV_DUP_WIDTH), BF16),
                   jax.ShapeDtypeStruct((t, KV_DUP_WIDTH), BF16),
                   jax.ShapeDtypeStruct((t, SC_WIDTH + RG_WIDTH), BF16)],
        scratch_shapes=[pltpu.VMEM((TS, REST_WIDTH), F32),
                        pltpu.VMEM((TS + 2 * CARRY_ROWS, SC_WIDTH), F32),
                        pltpu.VMEM((TS + 2 * CARRY_ROWS, RG_WIDTH), F32),
                        pltpu.VMEM((1, RG_WIDTH), F32)],
        compiler_params=_params(("arbitrary",), 48),
        name="in_proj_conv_rglru",
    )(x, g, w, sc_w, sc_b, rg_w, rg_b, w_gate, b_gate, lam)


def _mem_kv_kernel(m_ref, g_ref, w_ref, k_ref, v_ref):
    h = _rms(m_ref[...], g_ref[...]).astype(BF16)
    p = jnp.dot(h, w_ref[...], preferred_element_type=F32)
    k_ref[...] = p[:, :XA_WIDTH].astype(BF16)
    v_ref[...] = p[:, XA_WIDTH:].astype(BF16)


def _mem_kv(mem, g, wkv):
    t = mem.shape[0]
    row = lambda width: pl.BlockSpec((MEM_LEN, width), lambda i: (i, 0))
    return pl.pallas_call(
        _mem_kv_kernel,
        grid=(t // MEM_LEN,),
        in_specs=[row(D_MODEL), _const_spec((1, D_MODEL)), _const_spec((D_MODEL, 2 * XA_WIDTH))],
        out_specs=[row(XA_WIDTH), row(XA_WIDTH)],
        out_shape=[jax.ShapeDtypeStruct((t, XA_WIDTH), BF16)] * 2,
        compiler_params=_params(("parallel",), 24),
        name="mem_kv",
    )(mem, g, wkv)


def _post_mixer_kernel(x_ref, a_ref, c_ref, wout_ref, g_ref, wq_ref, k_ref, v_ref, wo_ref, *rest):
    o_ref = rest[-1] if len(rest) == 1 else rest[4]
    halves = [slice(i * (TM // 2), (i + 1) * (TM // 2)) for i in range(2)]
    heads = [slice(hd * XA_HEAD_DIM, (hd + 1) * XA_HEAD_DIM) for hd in range(XA_HEADS)]
    nt = (((1,), (1,)), ((), ()))
    k = k_ref[...]
    v = v_ref[...]

    x1 = [x_ref[hs, :] + jnp.dot(jnp.concatenate([a_ref[hs, :], c_ref[hs, :]], axis=-1), wout_ref[...],
                                 preferred_element_type=F32) for hs in halves]
    q = [jnp.dot(_rms(xh, g_ref[...]).astype(BF16), wq_ref[...], preferred_element_type=F32).astype(BF16)
         for xh in x1]
    scores = [[lax.dot_general(qh[:, sl], k[:, sl], nt, preferred_element_type=F32) for sl in heads] for qh in q]
    x2 = []
    for xh, per_head in zip(x1, scores):
        probs, sums = [], []
        for s in per_head:
            s = s * (1.0 / math.sqrt(XA_HEAD_DIM))
            p = jnp.exp(s - jnp.max(s, axis=-1, keepdims=True))
            probs.append(p.astype(BF16))
            sums.append(jnp.sum(p, axis=-1, keepdims=True))
        att = jnp.concatenate([jnp.dot(p, v[:, sl], preferred_element_type=F32) / l
                               for p, sl, l in zip(probs, heads, sums)], axis=-1).astype(BF16)
        x2.append(xh + jnp.dot(att, wo_ref[...], preferred_element_type=F32))

    for hs, xh in zip(halves, x2):
        o_ref[hs, :] = xh
    if len(rest) > 1:
        ffn_g_ref, wr_ref, tri_ref, sel_ref, _, hf_ref, meta_ref, metat_ref, cnt_ref, carry = rest
        h = _rms(jnp.concatenate(x2, axis=0), ffn_g_ref[...])
        _route(h, wr_ref, tri_ref, sel_ref, hf_ref, meta_ref, metat_ref, cnt_ref, carry)


def _post_mixer(x, attn, cr, w_out, g, wq, k, v, wo, seq, route=None):
    t = x.shape[0]
    per_seq = seq // TM
    row = lambda width: pl.BlockSpec((TM, width), lambda i: (i, 0))
    mem_blk = pl.BlockSpec((MEM_LEN, XA_WIDTH), lambda i: (i // per_seq, 0))
    in_specs = [row(D_MODEL), row(ATTN_WIDTH), row(SC_WIDTH + RG_WIDTH), _const_spec((D_MODEL, D_MODEL)),
                _const_spec((1, D_MODEL)), _const_spec((D_MODEL, XA_WIDTH)), mem_blk, mem_blk,
                _const_spec((XA_WIDTH, D_MODEL))]
    out_specs = [row(D_MODEL)]
    out_shape = [jax.ShapeDtypeStruct((t, D_MODEL), F32)]
    scratch = []
    args = [x, attn, cr, w_out, g, wq, k, v, wo]
    if route is not None:
        assert TM % ROUTE_CHUNK == 0
        in_specs += [_const_spec((1, D_MODEL)), _const_spec((3 * D_MODEL, LANES_V7X)), _const_spec((ROUTE_CHUNK, ROUTE_CHUNK)),
                     _const_spec((SUBLANES_V7X, LANES_V7X))]
        out_specs += [pl.BlockSpec((TM * ROW_CHUNKS, LANES_V7X), lambda i: (i, 0)), row(LANES_V7X),
                      pl.BlockSpec((SUBLANES_V7X, TM), lambda i: (0, i)),
                      pl.BlockSpec((1, LANES_V7X), lambda i: (0, 0))]
        out_shape += [jax.ShapeDtypeStruct((t * ROW_CHUNKS, LANES_V7X), F32),
                      jax.ShapeDtypeStruct((t, LANES_V7X), F32),
                      jax.ShapeDtypeStruct((SUBLANES_V7X, t), F32),
                      jax.ShapeDtypeStruct((1, LANES_V7X), F32)]
        scratch = [pltpu.VMEM((1, LANES_V7X), F32)]
        args += list(route)
    return pl.pallas_call(
        _post_mixer_kernel,
        grid=(t // TM,),
        in_specs=in_specs,
        out_specs=out_specs,
        out_shape=out_shape,
        scratch_shapes=scratch,
        compiler_params=_params(("arbitrary",), 40),
        name="post_mixer_route" if route is not None else "post_mixer",
    )(*args)


def _ffn_kernel(x_ref, g_ref, wg_ref, wu_ref, wd_ref, *rest):
    n_cast = (len(rest) - 1) // 2
    o_ref = rest[n_cast]
    x = x_ref[...]
    h = _rms(x, g_ref[...]).astype(BF16)
    gate = jnp.dot(h, wg_ref[...], preferred_element_type=F32)
    up = jnp.dot(h, wu_ref[...], preferred_element_type=F32)
    act = (gate * jax.nn.sigmoid(gate) * up).astype(BF16)
    o_ref[...] = x + jnp.dot(act, wd_ref[...], preferred_element_type=F32)
    for src, dst in zip(rest[:n_cast], rest[n_cast + 1:]):
        dst[...] = src[...].astype(BF16)


def _dense_ffn(x, g, wg, wu, wd, cast_along=()):
    t = x.shape[0]
    steps = t // TM
    d_ff = wg.shape[1]
    row = pl.BlockSpec((TM, D_MODEL), lambda i: (i, 0))
    flat = [w.reshape(-1, w.shape[-1]) for w in cast_along]
    for w in flat:
        assert w.shape[0] % (steps * 2 * SUBLANES_V7X) == 0
    slabs = [pl.BlockSpec((w.shape[0] // steps, w.shape[1]), lambda i: (i, 0)) for w in flat]
    outs = pl.pallas_call(
        _ffn_kernel,
        grid=(steps,),
        in_specs=[row, _const_spec((1, D_MODEL)), _const_spec((D_MODEL, d_ff)), _const_spec((D_MODEL, d_ff)),
                  _const_spec((d_ff, D_MODEL))] + slabs,
        out_specs=[row] + slabs,
        out_shape=[jax.ShapeDtypeStruct((t, D_MODEL), F32)] + [jax.ShapeDtypeStruct(w.shape, BF16) for w in flat],
        compiler_params=_params(("parallel",), 60),
        name="dense_swiglu",
    )(x, g, wg, wu, wd, *flat)
    return outs[0], [o.reshape(w.shape) for o, w in zip(outs[1:], cast_along)]


META_E1, META_E2, META_R1, META_R2, META_W1, META_W2 = range(6)


def _to_token_tiles(ref, rows):
    m = rows.shape[0]
    for c in range(ROW_CHUNKS):
        ref[pl.ds(c, m, stride=ROW_CHUNKS), :] = rows[:, c * LANES_V7X:(c + 1) * LANES_V7X]


def _from_token_tiles(ref):
    m = ref.shape[0] // ROW_CHUNKS
    return jnp.concatenate([ref[pl.ds(c, m, stride=ROW_CHUNKS), :] for c in range(ROW_CHUNKS)], axis=-1)


def _route(h, wr_ref, tri_ref, sel_ref, hf_ref, meta_ref, metat_ref, cnt_ref, carry):
    @pl.when(pl.program_id(0) == 0)
    def _():
        carry[...] = jnp.zeros_like(carry)

    _to_token_tiles(hf_ref, h)
    chunks = [slice(c * ROUTE_CHUNK, (c + 1) * ROUTE_CHUNK) for c in range(h.shape[0] // ROUTE_CHUNK)]
    each = lambda fn, *lists: [fn(*vals) for vals in zip(*lists)]
    rowmax = lambda a: jnp.max(a, axis=-1, keepdims=True)
    rowsum = lambda a: jnp.sum(a, axis=-1, keepdims=True)
    lane = lax.broadcasted_iota(jnp.int32, (ROUTE_CHUNK, LANES_V7X), 1)
    first_hit = lambda lg, m: jnp.min(jnp.where(lg == m, lane, LANES_V7X), axis=-1, keepdims=True)

    h_hi = h.astype(BF16)
    h_lo = (h - h_hi.astype(F32)).astype(BF16)
    lhs = jnp.concatenate([h_hi, h_lo, h_hi], axis=-1)
    logits = [jnp.dot(lhs[c, :], wr_ref[...], preferred_element_type=F32) for c in chunks]
    lg = each(lambda l: jnp.where(lane < N_EXPERTS, l, -jnp.inf), logits)
    m1 = each(rowmax, lg)
    e1 = each(first_hit, lg, m1)
    lg2 = each(lambda l, e: jnp.where(lane == e, -jnp.inf, l), lg, e1)
    m2 = each(rowmax, lg2)
    e2 = each(first_hit, lg2, m2)
    ex = each(lambda a, b: jnp.exp(b - a), m1, m2)
    w1 = each(lambda e: 1.0 / (1.0 + e), ex)
    w2 = each(lambda e: e / (1.0 + e), ex)
    hit1 = each(lambda e: lane == e, e1)
    hit2 = each(lambda e: lane == e, e2)
    onehot = each(lambda a, b: (a | b).astype(BF16), hit1, hit2)
    within = each(lambda o: jnp.dot(tri_ref[...], o, preferred_element_type=F32), onehot)
    totals = each(lambda o: jnp.sum(o.astype(F32), axis=0, keepdims=True), onehot)
    ahead, base = [], carry[...]
    for w, tot in zip(within, totals):
        ahead.append(w + base)
        base = base + tot
    carry[...] = base
    cnt_ref[...] = base
    r1 = each(lambda hit, a: rowsum(jnp.where(hit, a, 0.0)), hit1, ahead)
    r2 = each(lambda hit, a: rowsum(jnp.where(hit, a, 0.0)), hit2, ahead)

    def record(*vals):
        meta = jnp.zeros((ROUTE_CHUNK, LANES_V7X), F32)
        for col, val in zip((META_E1, META_E2, META_R1, META_R2, META_W1, META_W2), vals):
            meta = jnp.where(lane == col, val.astype(F32), meta)
        return meta

    meta = each(record, e1, e2, r1, r2, w1, w2)
    metat = each(lambda mt: lax.dot_general(sel_ref[...], mt, (((1,), (1,)), ((), ())), preferred_element_type=F32,
                                            precision=lax.Precision.HIGHEST), meta)
    for c, mt, mtt in zip(chunks, meta, metat):
        meta_ref[c, :] = mt
        metat_ref[:, c] = mtt


def _token_rows(ref, first_token, n_tokens):
    start = pl.multiple_of(first_token * ROW_CHUNKS, ROW_CHUNKS)
    return ref.at[pl.ds(start, n_tokens * ROW_CHUNKS)]


def _row_copy(src_ref, src_token, dst_ref, dst_token, sem):
    return pltpu.make_async_copy(_token_rows(src_ref, src_token, 1), _token_rows(dst_ref, dst_token, 1), sem)


def _dispatch_kernel(fill_ref, pos1_ref, pos2_ref, hf_ref, xs_ref, zeros, sem_z, sem):
    @pl.when(pl.program_id(0) == 0)
    def _():
        zeros[...] = jnp.zeros_like(zeros)

        def tile_fill(e):
            return pltpu.make_async_copy(zeros, _token_rows(xs_ref, pl.multiple_of(fill_ref[e], TM_GROUP), TM_GROUP),
                                         sem_z)

        for e in range(2 * N_EXPERTS):
            @pl.when(fill_ref[e] >= 0)
            def _():
                tile_fill(e).start()
        for e in range(2 * N_EXPERTS):
            @pl.when(fill_ref[e] >= 0)
            def _():
                tile_fill(e).wait()

    def issue(r, carry):
        _row_copy(hf_ref, r, xs_ref, pos1_ref[r], sem).start(priority=0)
        _row_copy(hf_ref, r, xs_ref, pos2_ref[r], sem).start(priority=1)
        return carry

    lax.fori_loop(0, TM_MOVE, issue, 0, unroll=ISSUE_UNROLL)
    whole_step = pltpu.make_async_copy(hf_ref, _token_rows(xs_ref, 0, TM_MOVE), sem)
    whole_step.wait()
    whole_step.wait()


def _dispatch(fill_start, pos1, pos2, hf, n_rows):
    t = hf.shape[0] // ROW_CHUNKS
    idx = pl.BlockSpec((TM_MOVE,), lambda i, fill: (i,), memory_space=pltpu.SMEM)
    return pl.pallas_call(
        _dispatch_kernel,
        grid_spec=pltpu.PrefetchScalarGridSpec(
            num_scalar_prefetch=1,
            grid=(t // TM_MOVE,),
            in_specs=[idx, idx, pl.BlockSpec((TM_MOVE * ROW_CHUNKS, LANES_V7X), lambda i, fill: (i, 0))],
            out_specs=pl.BlockSpec(memory_space=pl.ANY),
            scratch_shapes=[pltpu.VMEM((TM_GROUP * ROW_CHUNKS, LANES_V7X), F32), pltpu.SemaphoreType.DMA(()),
                            pltpu.SemaphoreType.DMA(())]),
        out_shape=jax.ShapeDtypeStruct((n_rows * ROW_CHUNKS, LANES_V7X), F32),
        compiler_params=_params(("arbitrary",), 24),
        name="moe_dispatch",
    )(fill_start, pos1, pos2, hf)


def _grouped_kernel(te_ref, used_ref, x_ref, wg_ref, wu_ref, wd_ref, y_ref):
    i = pl.program_id(0)

    @pl.when(i < used_ref[0])
    def _():
        h = _from_token_tiles(x_ref).astype(BF16)
        ff = wg_ref.shape[1] // FF_SPLIT
        cols = [slice(c * ff, (c + 1) * ff) for c in range(FF_SPLIT)]
        gates = [jnp.dot(h, wg_ref[:, c], preferred_element_type=F32) for c in cols]
        ups = [jnp.dot(h, wu_ref[:, c], preferred_element_type=F32) for c in cols]
        total = None
        for c, gate, up in zip(cols, gates, ups):
            act = (gate * jax.nn.sigmoid(gate) * up).astype(BF16)
            part = jnp.dot(act, wd_ref[c, :], preferred_element_type=F32)
            total = part if total is None else total + part
        _to_token_tiles(y_ref, total)

    @pl.when(i >= used_ref[0])
    def _():
        y_ref[...] = jnp.zeros_like(y_ref)


def _grouped_swiglu(tile_expert, n_used, xs, wg, wu, wd):
    n_rows = xs.shape[0] // ROW_CHUNKS
    d_ff = wg.shape[2]
    tile = (TM_GROUP * ROW_CHUNKS, LANES_V7X)
    src = lambda i, te, used: (jnp.maximum(jnp.minimum(i, used[0] - 1), 0), 0)
    expert = lambda shape: pl.BlockSpec((None,) + shape, lambda i, te, used: (te[i], 0, 0),
                                        pipeline_mode=pl.Buffered(1))
    return pl.pallas_call(
        _grouped_kernel,
        grid_spec=pltpu.PrefetchScalarGridSpec(
            num_scalar_prefetch=2,
            grid=(n_rows // TM_GROUP,),
            in_specs=[pl.BlockSpec(tile, src), expert((D_MODEL, d_ff)), expert((D_MODEL, d_ff)),
                      expert((d_ff, D_MODEL))],
            out_specs=pl.BlockSpec(tile, lambda i, te, used: (i, 0))),
        out_shape=jax.ShapeDtypeStruct((n_rows * ROW_CHUNKS, LANES_V7X), F32),
        compiler_params=_params(("arbitrary",), 58),
        name="moe_grouped_swiglu",
    )(tile_expert, n_used, xs, wg, wu, wd)


def _combine_kernel(pos1_ref, pos2_ref, pos1_next_ref, pos2_next_ref, x_ref, meta_ref, g_ref, y_ref, o_ref,
                    y1, y2, sems):
    i = pl.program_id(0)

    def gather(p1_ref, p2_ref, slot):
        def issue(r, carry):
            _row_copy(y_ref, p1_ref[r], y1.at[slot], r, sems.at[slot]).start(priority=0)
            _row_copy(y_ref, p2_ref[r], y2.at[slot], r, sems.at[slot]).start(priority=1)
            return carry
        lax.fori_loop(0, TM_COMBINE, issue, 0, unroll=ISSUE_UNROLL)

    def finish(slot):
        for buf in (y1, y2):
            pltpu.make_async_copy(_token_rows(y_ref, 0, TM_COMBINE), buf.at[slot], sems.at[slot]).wait()
        w1 = meta_ref[:, META_W1:META_W1 + 1]
        w2 = meta_ref[:, META_W2:META_W2 + 1]
        out = x_ref[...] + (w1 * _from_token_tiles(y1.at[slot]) + w2 * _from_token_tiles(y2.at[slot]))
        o_ref[...] = _rms(out, g_ref[...])

    @pl.when(i == 0)
    def _():
        gather(pos1_ref, pos2_ref, 0)

    for slot in range(2):
        @pl.when(i % 2 == slot)
        def _():
            @pl.when(i + 1 < pl.num_programs(0))
            def _():
                gather(pos1_next_ref, pos2_next_ref, 1 - slot)
            finish(slot)


def _combine(pos1, pos2, x, meta, g, y):
    t = x.shape[0]
    steps = t // TM_COMBINE
    idx = pl.BlockSpec((TM_COMBINE,), lambda i: (i,), memory_space=pltpu.SMEM)
    idx_next = pl.BlockSpec((TM_COMBINE,), lambda i: (jnp.minimum(i + 1, steps - 1),), memory_space=pltpu.SMEM)
    row = lambda width: pl.BlockSpec((TM_COMBINE, width), lambda i: (i, 0))
    slots = pltpu.VMEM((2, TM_COMBINE * ROW_CHUNKS, LANES_V7X), F32)
    return pl.pallas_call(
        _combine_kernel,
        grid=(steps,),
        in_specs=[idx, idx, idx_next, idx_next, row(D_MODEL), row(LANES_V7X), _const_spec((1, D_MODEL)),
                  pl.BlockSpec(memory_space=pl.ANY)],
        out_specs=row(D_MODEL),
        out_shape=jax.ShapeDtypeStruct((t, D_MODEL), F32),
        scratch_shapes=[slots, slots, pltpu.SemaphoreType.DMA((2,))],
        compiler_params=_params(("arbitrary",), 24),
        name="moe_combine_norm",
    )(pos1, pos2, pos1, pos2, x, meta, g, y)


def _route_operands(ffn_g, router_w):
    wr_pad = jnp.zeros((D_MODEL, LANES_V7X), F32).at[:, :N_EXPERTS].set(router_w)
    wr_hi = wr_pad.astype(BF16)
    wr_lo = (wr_pad - wr_hi.astype(F32)).astype(BF16)
    wr_split = jnp.concatenate([wr_hi, wr_hi, wr_lo], axis=0)
    tri = jnp.tril(jnp.ones((ROUTE_CHUNK, ROUTE_CHUNK), BF16), -1)
    sel = jnp.eye(SUBLANES_V7X, LANES_V7X, dtype=F32)
    return ffn_g, wr_split, tri, sel


def _moe_layer(x, hf, meta, metat, counts, wg, wu, wd, final_g):
    t = x.shape[0]
    n_rows = 2 * t + N_EXPERTS * TM_GROUP

    cnt = counts[0, :N_EXPERTS].astype(jnp.int32)
    padded = (cnt + TM_GROUP - 1) // TM_GROUP * TM_GROUP
    ends = jnp.cumsum(padded)
    starts = ends - padded
    experts = jnp.arange(N_EXPERTS, dtype=jnp.int32)[:, None]
    group_start = lambda e: jnp.sum(jnp.where(e[None, :] == experts, starts[:, None], 0), axis=0)
    pos1 = group_start(metat[META_E1].astype(jnp.int32)) + metat[META_R1].astype(jnp.int32)
    pos2 = group_start(metat[META_E2].astype(jnp.int32)) + metat[META_R2].astype(jnp.int32)
    n_used = (ends[-1] // TM_GROUP).astype(jnp.int32).reshape(1)
    tile_row = jnp.minimum(jnp.arange(n_rows // TM_GROUP, dtype=jnp.int32), n_used[0] - 1) * TM_GROUP
    tile_expert = jnp.sum(ends[None, :] <= tile_row[:, None], axis=1).astype(jnp.int32)
    tail_start = jnp.where(padded > 0, ends - TM_GROUP, -1)
    spare = ends[-1] + TM_GROUP * jnp.arange(N_EXPERTS, dtype=jnp.int32)
    fill_start = jnp.concatenate([tail_start, jnp.where(spare < n_rows, spare, -1)]).astype(jnp.int32)

    xs = _dispatch(fill_start, pos1, pos2, hf, n_rows)
    y = _grouped_swiglu(tile_expert, n_used, xs, wg, wu, wd)
    return _combine(pos1, pos2, x, meta, final_g, y)


def _arrange_in_proj(w):
    q = w[:, :ATTN_WIDTH] * (1.0 / math.sqrt(HEAD_DIM))
    dup = lambda start: [w[:, start + h * HEAD_DIM:start + (h + 1) * HEAD_DIM]
                         for h in range(ATTN_KV_HEADS) for _ in range(2)]
    return jnp.concatenate([q] + dup(ATTN_WIDTH) + dup(ATTN_WIDTH + KV_WIDTH) + [w[:, ATTN_WIDTH + 2 * KV_WIDTH:]],
                           axis=1)


def _block_diag(w):
    heads, d, _ = w.shape
    eye = jnp.eye(heads, dtype=w.dtype)
    return jnp.einsum('hij,hg->higj', w, eye).reshape(heads * d, heads * d)


def kernel(x, mem, rel_bias, mix_norm, w_in, attn_sinks, sc_conv_w, sc_conv_b, rg_conv_w, rg_conv_b, rg_w_a,
           rg_b_a, rg_w_x, rg_b_x, rg_lambda, w_out, xa_norm, mem_norm, xa_wq, xa_wk, xa_wv, xa_wo, ffn_norm,
           dense_wg, dense_wu, dense_wd, moe_router, moe_wg, moe_wu, moe_wd, final_norm):
    batch, seq, _ = x.shape
    depth = w_in.shape[0]
    assert depth == 2 and seq % TS == 0 and seq % TM == 0
    xt = x.reshape(batch * seq, D_MODEL)
    memt = mem.reshape(batch * MEM_LEN, D_MODEL)
    bias_tbl = _attention_bias_tables(rel_bias)
    vec = lambda a: a.reshape(1, -1)

    for layer in range(depth):
        w_gate = jnp.concatenate([_block_diag(rg_w_a[layer]), _block_diag(rg_w_x[layer])], axis=1).astype(BF16)
        b_gate = jnp.concatenate([rg_b_a[layer], rg_b_x[layer]]).reshape(1, -1)
        q, k, v, cr = _in_proj_conv(xt, vec(mix_norm[layer]), _arrange_in_proj(w_in[layer]).astype(BF16),
                                    sc_conv_w[layer], vec(sc_conv_b[layer]), rg_conv_w[layer],
                                    vec(rg_conv_b[layer]), w_gate, b_gate, vec(rg_lambda[layer]), seq)
        attn = _attention(q, k, v, attn_sinks[layer], bias_tbl, seq // BLOCK)
        wkv = jnp.concatenate([xa_wk[layer], xa_wv[layer]], axis=1).astype(BF16)
        mk, mv = _mem_kv(memt, vec(mem_norm[layer]), wkv)
        post = functools.partial(_post_mixer, xt, attn, cr, w_out[layer].astype(BF16), vec(xa_norm[layer]),
                                 xa_wq[layer].astype(BF16), mk, mv, xa_wo[layer].astype(BF16), seq)

        j = layer // 2
        if layer % 2 == 0:
            (xt,) = post()
            xt, expert_w = _dense_ffn(xt, vec(ffn_norm[layer]), dense_wg[j].astype(BF16), dense_wu[j].astype(BF16),
                                      dense_wd[j].astype(BF16), cast_along=(moe_wg[j], moe_wu[j], moe_wd[j]))
        else:
            xt, hf, meta, metat, counts = post(route=_route_operands(vec(ffn_norm[layer]), moe_router[j]))
            xt = _moe_layer(xt, hf, meta, metat, counts, *expert_w, vec(final_norm))
    return xt.reshape(batch, seq, D_MODEL)
```

```python
import functools
import math

import jax
import jax.numpy as jnp
import numpy as np
from jax import lax
from jax.experimental import pallas as pl
from jax.experimental.pallas import tpu as pltpu

F32 = jnp.float32
BF16 = jnp.bfloat16

D_MODEL = 1024
MEM_LEN = 256
HEAD_DIM = 64
ATTN_Q_HEADS = 8
ATTN_KV_HEADS = 2
ATTN_WIDTH = ATTN_Q_HEADS * HEAD_DIM
KV_WIDTH = ATTN_KV_HEADS * HEAD_DIM
KV_DUP_WIDTH = 2 * KV_WIDTH
BLOCK = 128
SC_WIDTH = 256
SC_CONV = 3
RG_WIDTH = 256
RG_HEADS = 4
RG_HEAD_DIM = RG_WIDTH // RG_HEADS
RG_CONV = 4
RG_C = 8.0
N_BUCKETS = 32
MAX_EXACT = N_BUCKETS // 2
MAX_DISTANCE = 128
XA_HEADS = 4
XA_HEAD_DIM = 128
XA_WIDTH = XA_HEADS * XA_HEAD_DIM
N_EXPERTS = 8
EPS = 1e-6
NEG_INF = -1e30
REST_WIDTH = 3 * SC_WIDTH + 2 * RG_WIDTH

LANES_V7X = 128
SUBLANES_V7X = 8
VMEM_BYTES_V7X = 64 * 1024 * 1024
ROW_CHUNKS = D_MODEL // LANES_V7X
assert ROW_CHUNKS == SUBLANES_V7X

TM = 512
ATTN_BLOCKS = 8
TS = 512
ROUTE_CHUNK = 128
TM_GROUP = 512
TM_MOVE = 1024
TM_COMBINE = 512
ISSUE_UNROLL = 8
FF_SPLIT = 2
CARRY_ROWS = SUBLANES_V7X


def _mib(n):
    return int(n * 1024 * 1024)


def _params(semantics, vmem_mib):
    assert _mib(vmem_mib) < VMEM_BYTES_V7X
    return pltpu.CompilerParams(dimension_semantics=semantics, vmem_limit_bytes=_mib(vmem_mib))


def _rms(x, g):
    ms = jnp.mean(x * x, axis=-1, keepdims=True)
    return x * lax.rsqrt(ms + EPS) * g


def _const_spec(shape):
    nd = len(shape)
    return pl.BlockSpec(shape, lambda *_: (0,) * nd, pipeline_mode=pl.Buffered(1))


def _attn_kernel(sink_ref, q_ref, kp_ref, kc_ref, vp_ref, vc_ref, bias0_ref, bias_ref, o_ref):
    pairs_per_group = ATTN_Q_HEADS // ATTN_KV_HEADS // 2
    row = lax.broadcasted_iota(jnp.int32, (BLOCK, BLOCK), 0)
    col = lax.broadcasted_iota(jnp.int32, (BLOCK, BLOCK), 1)
    from_prev = col > row
    low_lanes = lax.broadcasted_iota(jnp.int32, (2 * BLOCK, 2 * HEAD_DIM), 1) < HEAD_DIM
    low_out = col < HEAD_DIM
    zero = jnp.zeros((), BF16)

    def block_diag(band):
        return jnp.concatenate([jnp.where(low_lanes, band, zero), jnp.where(low_lanes, zero, band)], axis=0)

    def scores(blk):
        rows = slice(blk * BLOCK, (blk + 1) * BLOCK)
        prev_rows = slice((blk - 1) * BLOCK, blk * BLOCK)
        out = []
        for g in range(ATTN_KV_HEADS):
            lanes = slice(g * 2 * HEAD_DIM, (g + 1) * 2 * HEAD_DIM)
            k_prev = kp_ref[:, lanes] if blk == 0 else kc_ref[prev_rows, lanes]
            v_prev = vp_ref[:, lanes] if blk == 0 else vc_ref[prev_rows, lanes]
            k_bd = block_diag(jnp.concatenate([k_prev, kc_ref[rows, lanes]], axis=0))
            v_bd = block_diag(jnp.concatenate([v_prev, vc_ref[rows, lanes]], axis=0))
            for pair in range(pairs_per_group):
                slab = g * pairs_per_group + pair
                q2 = q_ref[rows, slab * 2 * HEAD_DIM:(slab + 1) * 2 * HEAD_DIM]
                s = lax.dot_general(q2, k_bd, (((1,), (1,)), ((), ())), preferred_element_type=F32)
                out.append((slab, s, v_bd))
        return out

    def finish(blk, scored):
        rows = slice(blk * BLOCK, (blk + 1) * BLOCK)
        tbl_ref = bias0_ref if blk == 0 else bias_ref
        staged = []
        for slab, s, v_bd in scored:
            probs, denoms = [], []
            for side in range(2):
                h = 2 * slab + side
                sh = s[:, side * 2 * BLOCK:(side + 1) * 2 * BLOCK]
                logits = jnp.where(from_prev, sh[:, :BLOCK], sh[:, BLOCK:]) + tbl_ref[h]
                sink = sink_ref[h]
                m = jnp.maximum(jnp.max(logits, axis=-1, keepdims=True), sink)
                p = jnp.exp(logits - m)
                denoms.append(jnp.sum(p, axis=-1, keepdims=True) + jnp.exp(sink - m))
                probs += [jnp.where(from_prev, p, 0.0), jnp.where(from_prev, 0.0, p)]
            staged.append((slab, jnp.concatenate(probs, axis=-1).astype(BF16), v_bd, denoms))
        for slab, p_band, v_bd, denoms in staged:
            o = jnp.dot(p_band, v_bd, preferred_element_type=F32)
            o = o / jnp.where(low_out, denoms[0], denoms[1])
            o_ref[rows, slab * 2 * HEAD_DIM:(slab + 1) * 2 * HEAD_DIM] = o.astype(BF16)

    pending = scores(0)
    for blk in range(ATTN_BLOCKS):
        upcoming = scores(blk + 1) if blk + 1 < ATTN_BLOCKS else None
        finish(blk, pending)
        pending = upcoming


def _attention(q, k, v, sinks, bias_tbl, blocks_per_seq):
    t = q.shape[0]
    tile = ATTN_BLOCKS * BLOCK
    cur = lambda i: (i, 0)
    prev = lambda i: (jnp.maximum(i * ATTN_BLOCKS - 1, 0), 0)
    tbl = (None, ATTN_Q_HEADS, BLOCK, BLOCK)
    return pl.pallas_call(
        _attn_kernel,
        grid=(t // tile,),
        in_specs=[pl.BlockSpec(memory_space=pltpu.SMEM),
                  pl.BlockSpec((tile, ATTN_WIDTH), cur),
                  pl.BlockSpec((BLOCK, KV_DUP_WIDTH), prev),
                  pl.BlockSpec((tile, KV_DUP_WIDTH), cur),
                  pl.BlockSpec((BLOCK, KV_DUP_WIDTH), prev),
                  pl.BlockSpec((tile, KV_DUP_WIDTH), cur),
                  pl.BlockSpec(tbl, lambda i: (jnp.minimum((i * ATTN_BLOCKS) % blocks_per_seq, 1), 0, 0, 0)),
                  pl.BlockSpec(tbl, lambda i: (1, 0, 0, 0))],
        out_specs=pl.BlockSpec((tile, ATTN_WIDTH), cur),
        out_shape=jax.ShapeDtypeStruct((t, ATTN_WIDTH), BF16),
        compiler_params=_params(("parallel",), 24),
        name="swa_attention",
    )(sinks, q, k, k, v, v, bias_tbl, bias_tbl)


def _bias_table_kernel(rel_ref, bucket_ref, o_ref):
    for v in range(2):
        bucket = bucket_ref[v]
        hits = [bucket == b for b in range(N_BUCKETS)]
        for h in range(ATTN_Q_HEADS):
            tbl = jnp.full(bucket.shape, NEG_INF, F32)
            for b in range(N_BUCKETS):
                tbl = jnp.where(hits[b], rel_ref[b * ATTN_Q_HEADS + h], tbl)
            o_ref[v, h] = tbl


def _attention_bias_tables(rel_bias):
    q_idx = np.arange(BLOCK)[:, None]
    j_idx = np.arange(BLOCK)[None, :]
    from_prev = j_idx > q_idx
    n = np.where(from_prev, q_idx + BLOCK - j_idx, q_idx - j_idx)
    large = MAX_EXACT + (np.log(np.maximum(n, 1).astype(np.float32) / np.float32(MAX_EXACT))
                         / np.float32(math.log(MAX_DISTANCE / MAX_EXACT))
                         * np.float32(N_BUCKETS - MAX_EXACT)).astype(np.int32)
    bucket = np.where(n < MAX_EXACT, n, np.minimum(large, N_BUCKETS - 1))
    first = np.where(from_prev, -1, bucket)
    buckets = jnp.asarray(np.stack([first, bucket]).astype(np.int32))
    return pl.pallas_call(
        _bias_table_kernel,
        in_specs=[pl.BlockSpec(memory_space=pltpu.SMEM), pl.BlockSpec(memory_space=pltpu.VMEM)],
        out_specs=pl.BlockSpec(memory_space=pltpu.VMEM),
        out_shape=jax.ShapeDtypeStruct((2, ATTN_Q_HEADS, BLOCK, BLOCK), F32),
        name="t5_bias_table",
    )(rel_bias.astype(F32).reshape(-1), buckets)


def _shift_rows(x, s, fill):
    return jnp.concatenate([jnp.full((s, x.shape[1]), fill, x.dtype), x[:x.shape[0] - s]], axis=0)


def _in_proj_conv_kernel(x_ref, g_ref, w_ref, scw_ref, scb_ref, rgw_ref, rgb_ref, wgate_ref, bgate_ref, lam_ref,
                         q_ref, k_ref, v_ref, o_ref, r_ref, sc_ext, rg_ext, h_carry, *, tiles_per_seq):
    i = pl.program_id(0)
    ts = r_ref.shape[0]
    c0 = CARRY_ROWS

    @pl.when(i == 0)
    def _():
        r_ref[...] = jnp.zeros_like(r_ref)

    @pl.when((i == 0) | ((i + tiles_per_seq - 1) % tiles_per_seq == 0))
    def _():
        sc_ext[0:c0, :] = jnp.zeros((c0, SC_WIDTH), F32)
        rg_ext[0:c0, :] = jnp.zeros((c0, RG_WIDTH), F32)
        h_carry[...] = jnp.zeros_like(h_carry)

    attn_cols = ATTN_WIDTH + 2 * KV_DUP_WIDTH
    hx = _rms(x_ref[...], g_ref[...]).astype(BF16)
    p = jnp.dot(hx, w_ref[:, :attn_cols], preferred_element_type=F32)
    q_ref[...] = p[:, :ATTN_WIDTH].astype(BF16)
    k_ref[...] = p[:, ATTN_WIDTH:ATTN_WIDTH + KV_DUP_WIDTH].astype(BF16)
    v_ref[...] = p[:, ATTN_WIDTH + KV_DUP_WIDTH:].astype(BF16)

    sc_b = r_ref[:, 0:SC_WIDTH]
    sc_ext[c0:c0 + ts, :] = r_ref[:, SC_WIDTH:2 * SC_WIDTH] * r_ref[:, 2 * SC_WIDTH:3 * SC_WIDTH]
    rg_ext[c0:c0 + ts, :] = r_ref[:, 3 * SC_WIDTH:3 * SC_WIDTH + RG_WIDTH]
    rg_g = r_ref[:, 3 * SC_WIDTH + RG_WIDTH:]

    conv = scb_ref[...]
    for k in range(SC_CONV):
        off = c0 - (SC_CONV - 1) + k
        conv = conv + scw_ref[k:k + 1, :] * sc_ext[off:off + ts, :]
    conv_out = sc_b * conv

    rg_in = rgb_ref[...]
    for k in range(RG_CONV):
        off = c0 - (RG_CONV - 1) + k
        rg_in = rg_in + rgw_ref[k:k + 1, :] * rg_ext[off:off + ts, :]

    sc_ext[0:c0, :] = sc_ext[ts:ts + c0, :]
    rg_ext[0:c0, :] = rg_ext[ts:ts + c0, :]

    gates = jnp.dot(rg_in.astype(BF16), wgate_ref[...], preferred_element_type=F32) + bgate_ref[...]

    r_ref[...] = jnp.dot(hx, w_ref[:, attn_cols:], preferred_element_type=F32)

    r_gate = jax.nn.sigmoid(gates[:, :RG_WIDTH])
    i_gate = jax.nn.sigmoid(gates[:, RG_WIDTH:])
    neg_lam = -lam_ref[...]
    softplus = jnp.maximum(neg_lam, 0.0) + jnp.log1p(jnp.exp(-jnp.abs(neg_lam)))
    log_a = -RG_C * r_gate * softplus
    a = jnp.exp(log_a)
    u = jnp.sqrt(jnp.tanh(-log_a) * (1.0 + a * a)) * (i_gate * rg_in)

    s = 1
    while s < ts:
        u = a * _shift_rows(u, s, 0.0) + u
        a = a * _shift_rows(a, s, 1.0)
        s *= 2
    h = a * h_carry[...] + u
    h_carry[...] = h[ts - 1:ts, :]

    c = math.sqrt(2.0 / math.pi)
    gelu = 0.5 * rg_g * (1.0 + jnp.tanh(c * (rg_g + 0.044715 * (rg_g * rg_g * rg_g))))
    o_ref[:, 0:SC_WIDTH] = conv_out.astype(BF16)
    o_ref[:, SC_WIDTH:] = (h * gelu).astype(BF16)


def _in_proj_conv(x, g, w, sc_w, sc_b, rg_w, rg_b, w_gate, b_gate, lam, seq):
    t = x.shape[0]
    n = w.shape[1]
    tiles = t // TS
    assert n == ATTN_WIDTH + 2 * KV_DUP_WIDTH + REST_WIDTH
    proj = lambda width: pl.BlockSpec((TS, width), lambda i: (jnp.minimum(i, tiles - 1), 0))
    lagged = pl.BlockSpec((TS, SC_WIDTH + RG_WIDTH), lambda i: (jnp.maximum(i - 1, 0), 0))
    return pl.pallas_call(
        functools.partial(_in_proj_conv_kernel, tiles_per_seq=seq // TS),
        grid=(tiles + 1,),
        in_specs=[proj(D_MODEL), _const_spec((1, D_MODEL)), _const_spec((D_MODEL, n)),
                  _const_spec((SC_CONV, SC_WIDTH)), _const_spec((1, SC_WIDTH)),
                  _const_spec((RG_CONV, RG_WIDTH)), _const_spec((1, RG_WIDTH)),
                  _const_spec((RG_WIDTH, 2 * RG_WIDTH)), _const_spec((1, 2 * RG_WIDTH)),
                  _const_spec((1, RG_WIDTH))],
        out_specs=[proj(ATTN_WIDTH), proj(KV_DUP_WIDTH), proj(KV_DUP_WIDTH), lagged],
        out_shape=[jax.ShapeDtypeStruct((t, ATTN_WIDTH), BF16),
                   jax.ShapeDtypeStruct((t, KV_DUP_WIDTH), BF16),
                   jax.ShapeDtypeStruct((t, KV_DUP_WIDTH), BF16),
                   jax.ShapeDtypeStruct((t, SC_WIDTH + RG_WIDTH), BF16)],
        scratch_shapes=[pltpu.VMEM((TS, REST_WIDTH), F32),
                        pltpu.VMEM((TS + 2 * CARRY_ROWS, SC_WIDTH), F32),
                        pltpu.VMEM((TS + 2 * CARRY_ROWS, RG_WIDTH), F32),
                        pltpu.VMEM((1, RG_WIDTH), F32)],
        compiler_params=_params(("arbitrary",), 48),
        name="in_proj_conv_rglru",
    )(x, g, w, sc_w, sc_b, rg_w, rg_b, w_gate, b_gate, lam)


def _mem_kv_kernel(m_ref, g_ref, w_ref, k_ref, v_ref):
    h = _rms(m_ref[...], g_ref[...]).astype(BF16)
    p = jnp.dot(h, w_ref[...], preferred_element_type=F32)
    k_ref[...] = p[:, :XA_WIDTH].astype(BF16)
    v_ref[...] = p[:, XA_WIDTH:].astype(BF16)


def _mem_kv(mem, g, wkv):
    t = mem.shape[0]
    row = lambda width: pl.BlockSpec((MEM_LEN, width), lambda i: (i, 0))
    return pl.pallas_call(
        _mem_kv_kernel,
        grid=(t // MEM_LEN,),
        in_specs=[row(D_MODEL), _const_spec((1, D_MODEL)), _const_spec((D_MODEL, 2 * XA_WIDTH))],
        out_specs=[row(XA_WIDTH), row(XA_WIDTH)],
        out_shape=[jax.ShapeDtypeStruct((t, XA_WIDTH), BF16)] * 2,
        compiler_params=_params(("parallel",), 24),
        name="mem_kv",
    )(mem, g, wkv)


def _post_mixer_kernel(x_ref, a_ref, c_ref, wout_ref, g_ref, wq_ref, k_ref, v_ref, wo_ref, *rest):
    o_ref = rest[-1] if len(rest) == 1 else rest[4]
    halves = [slice(i * (TM // 2), (i + 1) * (TM // 2)) for i in range(2)]
    heads = [slice(hd * XA_HEAD_DIM, (hd + 1) * XA_HEAD_DIM) for hd in range(XA_HEADS)]
    nt = (((1,), (1,)), ((), ()))
    k = k_ref[...]
    v = v_ref[...]

    x1 = [x_ref[hs, :] + jnp.dot(jnp.concatenate([a_ref[hs, :], c_ref[hs, :]], axis=-1), wout_ref[...],
                                 preferred_element_type=F32) for hs in halves]
    q = [jnp.dot(_rms(xh, g_ref[...]).astype(BF16), wq_ref[...], preferred_element_type=F32).astype(BF16)
         for xh in x1]
    scores = [[lax.dot_general(qh[:, sl], k[:, sl], nt, preferred_element_type=F32) for sl in heads] for qh in q]
    x2 = []
    for xh, per_head in zip(x1, scores):
        probs, sums = [], []
        for s in per_head:
            s = s * (1.0 / math.sqrt(XA_HEAD_DIM))
            p = jnp.exp(s - jnp.max(s, axis=-1, keepdims=True))
            probs.append(p.astype(BF16))
            sums.append(jnp.sum(p, axis=-1, keepdims=True))
        att = jnp.concatenate([jnp.dot(p, v[:, sl], preferred_element_type=F32) / l
                               for p, sl, l in zip(probs, heads, sums)], axis=-1).astype(BF16)
        x2.append(xh + jnp.dot(att, wo_ref[...], preferred_element_type=F32))

    for hs, xh in zip(halves, x2):
        o_ref[hs, :] = xh
    if len(rest) > 1:
        ffn_g_ref, wr_ref, tri_ref, sel_ref, _, hf_ref, meta_ref, metat_ref, cnt_ref, carry = rest
        h = _rms(jnp.concatenate(x2, axis=0), ffn_g_ref[...])
        _route(h, wr_ref, tri_ref, sel_ref, hf_ref, meta_ref, metat_ref, cnt_ref, carry)


def _post_mixer(x, attn, cr, w_out, g, wq, k, v, wo, seq, route=None):
    t = x.shape[0]
    per_seq = seq // TM
    row = lambda width: pl.BlockSpec((TM, width), lambda i: (i, 0))
    mem_blk = pl.BlockSpec((MEM_LEN, XA_WIDTH), lambda i: (i // per_seq, 0))
    in_specs = [row(D_MODEL), row(ATTN_WIDTH), row(SC_WIDTH + RG_WIDTH), _const_spec((D_MODEL, D_MODEL)),
                _const_spec((1, D_MODEL)), _const_spec((D_MODEL, XA_WIDTH)), mem_blk, mem_blk,
                _const_spec((XA_WIDTH, D_MODEL))]
    out_specs = [row(D_MODEL)]
    out_shape = [jax.ShapeDtypeStruct((t, D_MODEL), F32)]
    scratch = []
    args = [x, attn, cr, w_out, g, wq, k, v, wo]
    if route is not None:
        assert TM % ROUTE_CHUNK == 0
        in_specs += [_const_spec((1, D_MODEL)), _const_spec((2 * D_MODEL, LANES_V7X)),
                     _const_spec((ROUTE_CHUNK, ROUTE_CHUNK)), _const_spec((SUBLANES_V7X, LANES_V7X))]
        out_specs += [pl.BlockSpec((TM * ROW_CHUNKS, LANES_V7X), lambda i: (i, 0)), row(LANES_V7X),
                      pl.BlockSpec((SUBLANES_V7X, TM), lambda i: (0, i)),
                      pl.BlockSpec((1, LANES_V7X), lambda i: (0, 0))]
        out_shape += [jax.ShapeDtypeStruct((t * ROW_CHUNKS, LANES_V7X), F32),
                      jax.ShapeDtypeStruct((t, LANES_V7X), F32),
                      jax.ShapeDtypeStruct((SUBLANES_V7X, t), F32),
                      jax.ShapeDtypeStruct((1, LANES_V7X), F32)]
        scratch = [pltpu.VMEM((1, LANES_V7X), F32)]
        args += list(route)
    return pl.pallas_call(
        _post_mixer_kernel,
        grid=(t // TM,),
        in_specs=in_specs,
        out_specs=out_specs,
        out_shape=out_shape,
        scratch_shapes=scratch,
        compiler_params=_params(("arbitrary",), 40),
        name="post_mixer_route" if route is not None else "post_mixer",
    )(*args)


def _ffn_kernel(x_ref, g_ref, wg_ref, wu_ref, wd_ref, *rest):
    n_cast = (len(rest) - 1) // 2
    o_ref = rest[n_cast]
    x = x_ref[...]
    h = _rms(x, g_ref[...]).astype(BF16)
    gate = jnp.dot(h, wg_ref[...], preferred_element_type=F32)
    up = jnp.dot(h, wu_ref[...], preferred_element_type=F32)
    act = (gate * jax.nn.sigmoid(gate) * up).astype(BF16)
    o_ref[...] = x + jnp.dot(act, wd_ref[...], preferred_element_type=F32)
    for src, dst in zip(rest[:n_cast], rest[n_cast + 1:]):
        dst[...] = src[...].astype(BF16)


def _dense_ffn(x, g, wg, wu, wd, cast_along=()):
    t = x.shape[0]
    steps = t // TM
    d_ff = wg.shape[1]
    row = pl.BlockSpec((TM, D_MODEL), lambda i: (i, 0))
    flat = [w.reshape(-1, w.shape[-1]) for w in cast_along]
    for w in flat:
        assert w.shape[0] % (steps * 2 * SUBLANES_V7X) == 0
    slabs = [pl.BlockSpec((w.shape[0] // steps, w.shape[1]), lambda i: (i, 0)) for w in flat]
    outs = pl.pallas_call(
        _ffn_kernel,
        grid=(steps,),
        in_specs=[row, _const_spec((1, D_MODEL)), _const_spec((D_MODEL, d_ff)), _const_spec((D_MODEL, d_ff)),
                  _const_spec((d_ff, D_MODEL))] + slabs,
        out_specs=[row] + slabs,
        out_shape=[jax.ShapeDtypeStruct((t, D_MODEL), F32)] + [jax.ShapeDtypeStruct(w.shape, BF16) for w in flat],
        compiler_params=_params(("parallel",), 60),
        name="dense_swiglu",
    )(x, g, wg, wu, wd, *flat)
    return outs[0], [o.reshape(w.shape) for o, w in zip(outs[1:], cast_along)]


META_E1, META_E2, META_R1, META_R2, META_W1, META_W2 = range(6)


def _to_token_tiles(ref, rows):
    m = rows.shape[0]
    for c in range(ROW_CHUNKS):
        ref[pl.ds(c, m, stride=ROW_CHUNKS), :] = rows[:, c * LANES_V7X:(c + 1) * LANES_V7X]


def _from_token_tiles(ref):
    m = ref.shape[0] // ROW_CHUNKS
    return jnp.concatenate([ref[pl.ds(c, m, stride=ROW_CHUNKS), :] for c in range(ROW_CHUNKS)], axis=-1)


def _route(h, wr_ref, tri_ref, sel_ref, hf_ref, meta_ref, metat_ref, cnt_ref, carry):
    @pl.when(pl.program_id(0) == 0)
    def _():
        carry[...] = jnp.zeros_like(carry)

    _to_token_tiles(hf_ref, h)
    chunks = [slice(c * ROUTE_CHUNK, (c + 1) * ROUTE_CHUNK) for c in range(h.shape[0] // ROUTE_CHUNK)]
    each = lambda fn, *lists: [fn(*vals) for vals in zip(*lists)]
    rowmax = lambda a: jnp.max(a, axis=-1, keepdims=True)
    rowsum = lambda a: jnp.sum(a, axis=-1, keepdims=True)
    lane = lax.broadcasted_iota(jnp.int32, (ROUTE_CHUNK, LANES_V7X), 1)
    first_hit = lambda lg, m: jnp.min(jnp.where(lg == m, lane, LANES_V7X), axis=-1, keepdims=True)

    h_hi = h.astype(BF16)
    h_lo = (h - h_hi.astype(F32)).astype(BF16)
    w_hi = wr_ref[0:D_MODEL, :]
    w_lo = wr_ref[D_MODEL:2 * D_MODEL, :]
    mm = lambda a, b: jnp.dot(a, b, preferred_element_type=F32)
    logits = [mm(h_hi[c, :], w_hi) + mm(h_lo[c, :], w_hi) + mm(h_hi[c, :], w_lo) for c in chunks]
    lg = each(lambda l: jnp.where(lane < N_EXPERTS, l, -jnp.inf), logits)
    m1 = each(rowmax, lg)
    e1 = each(first_hit, lg, m1)
    lg2 = each(lambda l, e: jnp.where(lane == e, -jnp.inf, l), lg, e1)
    m2 = each(rowmax, lg2)
    e2 = each(first_hit, lg2, m2)
    ex = each(lambda a, b: jnp.exp(b - a), m1, m2)
    w1 = each(lambda e: 1.0 / (1.0 + e), ex)
    w2 = each(lambda e: e / (1.0 + e), ex)
    hit1 = each(lambda e: lane == e, e1)
    hit2 = each(lambda e: lane == e, e2)
    onehot = each(lambda a, b: (a | b).astype(BF16), hit1, hit2)
    within = each(lambda o: jnp.dot(tri_ref[...], o, preferred_element_type=F32), onehot)
    totals = each(lambda o: jnp.sum(o.astype(F32), axis=0, keepdims=True), onehot)
    ahead, base = [], carry[...]
    for w, tot in zip(within, totals):
        ahead.append(w + base)
        base = base + tot
    carry[...] = base
    cnt_ref[...] = base
    r1 = each(lambda hit, a: rowsum(jnp.where(hit, a, 0.0)), hit1, ahead)
    r2 = each(lambda hit, a: rowsum(jnp.where(hit, a, 0.0)), hit2, ahead)

    def record(*vals):
        meta = jnp.zeros((ROUTE_CHUNK, LANES_V7X), F32)
        for col, val in zip((META_E1, META_E2, META_R1, META_R2, META_W1, META_W2), vals):
            meta = jnp.where(lane == col, val.astype(F32), meta)
        return meta

    meta = each(record, e1, e2, r1, r2, w1, w2)
    metat = each(lambda mt: lax.dot_general(sel_ref[...], mt, (((1,), (1,)), ((), ())), preferred_element_type=F32,
                                            precision=lax.Precision.HIGHEST), meta)
    for c, mt, mtt in zip(chunks, meta, metat):
        meta_ref[c, :] = mt
        metat_ref[:, c] = mtt


def _token_rows(ref, first_token, n_tokens):
    start = pl.multiple_of(first_token * ROW_CHUNKS, ROW_CHUNKS)
    return ref.at[pl.ds(start, n_tokens * ROW_CHUNKS)]


def _row_copy(src_ref, src_token, dst_ref, dst_token, sem):
    return pltpu.make_async_copy(_token_rows(src_ref, src_token, 1), _token_rows(dst_ref, dst_token, 1), sem)


def _dispatch_kernel(fill_ref, pos1_ref, pos2_ref, hf_ref, xs_ref, zeros, sem_z, sem):
    @pl.when(pl.program_id(0) == 0)
    def _():
        zeros[...] = jnp.zeros_like(zeros)

        def tile_fill(e):
            return pltpu.make_async_copy(zeros, _token_rows(xs_ref, pl.multiple_of(fill_ref[e], TM_GROUP), TM_GROUP),
                                         sem_z)

        for e in range(2 * N_EXPERTS):
            @pl.when(fill_ref[e] >= 0)
            def _():
                tile_fill(e).start()
        for e in range(2 * N_EXPERTS):
            @pl.when(fill_ref[e] >= 0)
            def _():
                tile_fill(e).wait()

    def issue(r, carry):
        _row_copy(hf_ref, r, xs_ref, pos1_ref[r], sem).start(priority=0)
        _row_copy(hf_ref, r, xs_ref, pos2_ref[r], sem).start(priority=1)
        return carry

    lax.fori_loop(0, TM_MOVE, issue, 0, unroll=ISSUE_UNROLL)
    whole_step = pltpu.make_async_copy(hf_ref, _token_rows(xs_ref, 0, TM_MOVE), sem)
    whole_step.wait()
    whole_step.wait()


def _dispatch(fill_start, pos1, pos2, hf, n_rows):
    t = hf.shape[0] // ROW_CHUNKS
    idx = pl.BlockSpec((TM_MOVE,), lambda i, fill: (i,), memory_space=pltpu.SMEM)
    return pl.pallas_call(
        _dispatch_kernel,
        grid_spec=pltpu.PrefetchScalarGridSpec(
            num_scalar_prefetch=1,
            grid=(t // TM_MOVE,),
            in_specs=[idx, idx, pl.BlockSpec((TM_MOVE * ROW_CHUNKS, LANES_V7X), lambda i, fill: (i, 0))],
            out_specs=pl.BlockSpec(memory_space=pl.ANY),
            scratch_shapes=[pltpu.VMEM((TM_GROUP * ROW_CHUNKS, LANES_V7X), F32), pltpu.SemaphoreType.DMA(()),
                            pltpu.SemaphoreType.DMA(())]),
        out_shape=jax.ShapeDtypeStruct((n_rows * ROW_CHUNKS, LANES_V7X), F32),
        compiler_params=_params(("arbitrary",), 24),
        name="moe_dispatch",
    )(fill_start, pos1, pos2, hf)


def _grouped_kernel(te_ref, used_ref, x_ref, wg_ref, wu_ref, wd_ref, y_ref):
    i = pl.program_id(0)

    @pl.when(i < used_ref[0])
    def _():
        h = _from_token_tiles(x_ref).astype(BF16)
        ff = wg_ref.shape[1] // FF_SPLIT
        cols = [slice(c * ff, (c + 1) * ff) for c in range(FF_SPLIT)]
        gates = [jnp.dot(h, wg_ref[:, c], preferred_element_type=F32) for c in cols]
        ups = [jnp.dot(h, wu_ref[:, c], preferred_element_type=F32) for c in cols]
        total = None
        for c, gate, up in zip(cols, gates, ups):
            act = (gate * jax.nn.sigmoid(gate) * up).astype(BF16)
            part = jnp.dot(act, wd_ref[c, :], preferred_element_type=F32)
            total = part if total is None else total + part
        _to_token_tiles(y_ref, total)

    @pl.when(i >= used_ref[0])
    def _():
        y_ref[...] = jnp.zeros_like(y_ref)


def _grouped_swiglu(tile_expert, n_used, xs, wg, wu, wd):
    n_rows = xs.shape[0] // ROW_CHUNKS
    d_ff = wg.shape[2]
    tile = (TM_GROUP * ROW_CHUNKS, LANES_V7X)
    src = lambda i, te, used: (jnp.maximum(jnp.minimum(i, used[0] - 1), 0), 0)
    expert = lambda shape: pl.BlockSpec((None,) + shape, lambda i, te, used: (te[i], 0, 0),
                                        pipeline_mode=pl.Buffered(1))
    return pl.pallas_call(
        _grouped_kernel,
        grid_spec=pltpu.PrefetchScalarGridSpec(
            num_scalar_prefetch=2,
            grid=(n_rows // TM_GROUP,),
            in_specs=[pl.BlockSpec(tile, src), expert((D_MODEL, d_ff)), expert((D_MODEL, d_ff)),
                      expert((d_ff, D_MODEL))],
            out_specs=pl.BlockSpec(tile, lambda i, te, used: (i, 0))),
        out_shape=jax.ShapeDtypeStruct((n_rows * ROW_CHUNKS, LANES_V7X), F32),
        compiler_params=_params(("arbitrary",), 58),
        name="moe_grouped_swiglu",
    )(tile_expert, n_used, xs, wg, wu, wd)


def _combine_kernel(pos1_ref, pos2_ref, pos1_next_ref, pos2_next_ref, x_ref, meta_ref, g_ref, y_ref, o_ref,
                    y1, y2, sems):
    i = pl.program_id(0)

    def gather(p1_ref, p2_ref, slot):
        def issue(r, carry):
            _row_copy(y_ref, p1_ref[r], y1.at[slot], r, sems.at[slot]).start(priority=0)
            _row_copy(y_ref, p2_ref[r], y2.at[slot], r, sems.at[slot]).start(priority=1)
            return carry
        lax.fori_loop(0, TM_COMBINE, issue, 0, unroll=ISSUE_UNROLL)

    def finish(slot):
        for buf in (y1, y2):
            pltpu.make_async_copy(_token_rows(y_ref, 0, TM_COMBINE), buf.at[slot], sems.at[slot]).wait()
        w1 = meta_ref[:, META_W1:META_W1 + 1]
        w2 = meta_ref[:, META_W2:META_W2 + 1]
        out = x_ref[...] + (w1 * _from_token_tiles(y1.at[slot]) + w2 * _from_token_tiles(y2.at[slot]))
        o_ref[...] = _rms(out, g_ref[...])

    @pl.when(i == 0)
    def _():
        gather(pos1_ref, pos2_ref, 0)

    for slot in range(2):
        @pl.when(i % 2 == slot)
        def _():
            @pl.when(i + 1 < pl.num_programs(0))
            def _():
                gather(pos1_next_ref, pos2_next_ref, 1 - slot)
            finish(slot)


def _combine(pos1, pos2, x, meta, g, y):
    t = x.shape[0]
    steps = t // TM_COMBINE
    idx = pl.BlockSpec((TM_COMBINE,), lambda i: (i,), memory_space=pltpu.SMEM)
    idx_next = pl.BlockSpec((TM_COMBINE,), lambda i: (jnp.minimum(i + 1, steps - 1),), memory_space=pltpu.SMEM)
    row = lambda width: pl.BlockSpec((TM_COMBINE, width), lambda i: (i, 0))
    slots = pltpu.VMEM((2, TM_COMBINE * ROW_CHUNKS, LANES_V7X), F32)
    return pl.pallas_call(
        _combine_kernel,
        grid=(steps,),
        in_specs=[idx, idx, idx_next, idx_next, row(D_MODEL), row(LANES_V7X), _const_spec((1, D_MODEL)),
                  pl.BlockSpec(memory_space=pl.ANY)],
        out_specs=row(D_MODEL),
        out_shape=jax.ShapeDtypeStruct((t, D_MODEL), F32),
        scratch_shapes=[slots, slots, pltpu.SemaphoreType.DMA((2,))],
        compiler_params=_params(("arbitrary",), 24),
        name="moe_combine_norm",
    )(pos1, pos2, pos1, pos2, x, meta, g, y)


def _route_operands(ffn_g, router_w):
    wr_pad = jnp.zeros((D_MODEL, LANES_V7X), F32).at[:, :N_EXPERTS].set(router_w)
    wr_hi = wr_pad.astype(BF16)
    wr_lo = (wr_pad - wr_hi.astype(F32)).astype(BF16)
    wr_split = jnp.concatenate([wr_hi, wr_lo], axis=0)
    tri = jnp.tril(jnp.ones((ROUTE_CHUNK, ROUTE_CHUNK), BF16), -1)
    sel = jnp.eye(SUBLANES_V7X, LANES_V7X, dtype=F32)
    return ffn_g, wr_split, tri, sel


def _moe_layer(x, hf, meta, metat, counts, wg, wu, wd, final_g):
    t = x.shape[0]
    n_rows = 2 * t + N_EXPERTS * TM_GROUP

    cnt = counts[0, :N_EXPERTS].astype(jnp.int32)
    padded = (cnt + TM_GROUP - 1) // TM_GROUP * TM_GROUP
    ends = jnp.cumsum(padded)
    starts = ends - padded
    experts = jnp.arange(N_EXPERTS, dtype=jnp.int32)[:, None]
    group_start = lambda e: jnp.sum(jnp.where(e[None, :] == experts, starts[:, None], 0), axis=0)
    pos1 = group_start(metat[META_E1].astype(jnp.int32)) + metat[META_R1].astype(jnp.int32)
    pos2 = group_start(metat[META_E2].astype(jnp.int32)) + metat[META_R2].astype(jnp.int32)
    n_used = (ends[-1] // TM_GROUP).astype(jnp.int32).reshape(1)
    tile_row = jnp.minimum(jnp.arange(n_rows // TM_GROUP, dtype=jnp.int32), n_used[0] - 1) * TM_GROUP
    tile_expert = jnp.sum(ends[None, :] <= tile_row[:, None], axis=1).astype(jnp.int32)
    tail_start = jnp.where(padded > 0, ends - TM_GROUP, -1)
    spare = ends[-1] + TM_GROUP * jnp.arange(N_EXPERTS, dtype=jnp.int32)
    fill_start = jnp.concatenate([tail_start, jnp.where(spare < n_rows, spare, -1)]).astype(jnp.int32)

    xs = _dispatch(fill_start, pos1, pos2, hf, n_rows)
    y = _grouped_swiglu(tile_expert, n_used, xs, wg, wu, wd)
    return _combine(pos1, pos2, x, meta, final_g, y)


def _arrange_in_proj(w):
    q = (w[:, :ATTN_WIDTH] * (1.0 / math.sqrt(HEAD_DIM))).astype(BF16)
    w = w.astype(BF16)
    dup = lambda start: [w[:, start + h * HEAD_DIM:start + (h + 1) * HEAD_DIM]
                         for h in range(ATTN_KV_HEADS) for _ in range(2)]
    return jnp.concatenate([q] + dup(ATTN_WIDTH) + dup(ATTN_WIDTH + KV_WIDTH) + [w[:, ATTN_WIDTH + 2 * KV_WIDTH:]],
                           axis=1)


def _block_diag(w):
    heads, d, _ = w.shape
    eye = jnp.eye(heads, dtype=w.dtype)
    return jnp.einsum('hij,hg->higj', w, eye).reshape(heads * d, heads * d)


def kernel(x, mem, rel_bias, mix_norm, w_in, attn_sinks, sc_conv_w, sc_conv_b, rg_conv_w, rg_conv_b, rg_w_a,
           rg_b_a, rg_w_x, rg_b_x, rg_lambda, w_out, xa_norm, mem_norm, xa_wq, xa_wk, xa_wv, xa_wo, ffn_norm,
           dense_wg, dense_wu, dense_wd, moe_router, moe_wg, moe_wu, moe_wd, final_norm):
    batch, seq, _ = x.shape
    depth = w_in.shape[0]
    assert depth == 2 and seq % TS == 0 and seq % TM == 0
    xt = x.reshape(batch * seq, D_MODEL)
    memt = mem.reshape(batch * MEM_LEN, D_MODEL)
    bias_tbl = _attention_bias_tables(rel_bias)
    vec = lambda a: a.reshape(1, -1)

    for layer in range(depth):
        w_gate = jnp.concatenate([_block_diag(rg_w_a[layer]), _block_diag(rg_w_x[layer])], axis=1).astype(BF16)
        b_gate = jnp.concatenate([rg_b_a[layer], rg_b_x[layer]]).reshape(1, -1)
        q, k, v, cr = _in_proj_conv(xt, vec(mix_norm[layer]), _arrange_in_proj(w_in[layer]),
                                    sc_conv_w[layer], vec(sc_conv_b[layer]), rg_conv_w[layer],
                                    vec(rg_conv_b[layer]), w_gate, b_gate, vec(rg_lambda[layer]), seq)
        attn = _attention(q, k, v, attn_sinks[layer], bias_tbl, seq // BLOCK)
        wkv = jnp.concatenate([xa_wk[layer], xa_wv[layer]], axis=1).astype(BF16)
        mk, mv = _mem_kv(memt, vec(mem_norm[layer]), wkv)
        post = functools.partial(_post_mixer, xt, attn, cr, w_out[layer].astype(BF16), vec(xa_norm[layer]),
                                 xa_wq[layer].astype(BF16), mk, mv, xa_wo[layer].astype(BF16), seq)

        j = layer // 2
        if layer % 2 == 0:
            (xt,) = post()
            xt, expert_w = _dense_ffn(xt, vec(ffn_norm[layer]), dense_wg[j].astype(BF16), dense_wu[j].astype(BF16),
                                      dense_wd[j].astype(BF16), cast_along=(moe_wg[j], moe_wu[j], moe_wd[j]))
        else:
            xt, hf, meta, metat, counts = post(route=_route_operands(vec(ffn_norm[layer]), moe_router[j]))
            xt = _moe_layer(xt, hf, meta, metat, counts, *expert_w, vec(final_norm))
    return xt.reshape(batch, seq, D_MODEL)
```

```python
import functools
import math

import jax
import jax.numpy as jnp
import numpy as np
from jax import lax
from jax.experimental import pallas as pl
from jax.experimental.pallas import tpu as pltpu

F32 = jnp.float32
BF16 = jnp.bfloat16

D_MODEL = 1024
MEM_LEN = 256
HEAD_DIM = 64
ATTN_Q_HEADS = 8
ATTN_KV_HEADS = 2
ATTN_WIDTH = ATTN_Q_HEADS * HEAD_DIM
KV_WIDTH = ATTN_KV_HEADS * HEAD_DIM
KV_DUP_WIDTH = 2 * KV_WIDTH
BLOCK = 128
SC_WIDTH = 256
SC_CONV = 3
RG_WIDTH = 256
RG_HEADS = 4
RG_HEAD_DIM = RG_WIDTH // RG_HEADS
RG_CONV = 4
RG_C = 8.0
N_BUCKETS = 32
MAX_EXACT = N_BUCKETS // 2
MAX_DISTANCE = 128
XA_HEADS = 4
XA_HEAD_DIM = 128
XA_WIDTH = XA_HEADS * XA_HEAD_DIM
N_EXPERTS = 8
EPS = 1e-6
NEG_INF = -1e30
REST_WIDTH = 3 * SC_WIDTH + 2 * RG_WIDTH

LANES_V7X = 128
SUBLANES_V7X = 8
VMEM_BYTES_V7X = 64 * 1024 * 1024
ROW_CHUNKS = D_MODEL // LANES_V7X
assert ROW_CHUNKS == SUBLANES_V7X

TM = 512
ATTN_BLOCKS = 8
TS = 512
ROUTE_CHUNK = 128
TM_GROUP = 512
TM_MOVE = 1024
TM_COMBINE = 256
ISSUE_UNROLL = 8
FF_SPLIT = 2
CARRY_ROWS = SUBLANES_V7X


def _mib(n):
    return int(n * 1024 * 1024)


def _params(semantics, vmem_mib):
    assert _mib(vmem_mib) < VMEM_BYTES_V7X
    return pltpu.CompilerParams(dimension_semantics=semantics, vmem_limit_bytes=_mib(vmem_mib))


def _rms(x, g):
    ms = jnp.mean(x * x, axis=-1, keepdims=True)
    return x * lax.rsqrt(ms + EPS) * g


def _const_spec(shape):
    nd = len(shape)
    return pl.BlockSpec(shape, lambda *_: (0,) * nd, pipeline_mode=pl.Buffered(1))


def _attn_kernel(sink_ref, q_ref, kp_ref, kc_ref, vp_ref, vc_ref, bias0_ref, bias_ref, o_ref):
    pairs_per_group = ATTN_Q_HEADS // ATTN_KV_HEADS // 2
    row = lax.broadcasted_iota(jnp.int32, (BLOCK, BLOCK), 0)
    col = lax.broadcasted_iota(jnp.int32, (BLOCK, BLOCK), 1)
    from_prev = col > row
    low_lanes = lax.broadcasted_iota(jnp.int32, (2 * BLOCK, 2 * HEAD_DIM), 1) < HEAD_DIM
    low_out = col < HEAD_DIM
    zero = jnp.zeros((), BF16)

    def block_diag(band):
        return jnp.concatenate([jnp.where(low_lanes, band, zero), jnp.where(low_lanes, zero, band)], axis=0)

    def scores(blk):
        rows = slice(blk * BLOCK, (blk + 1) * BLOCK)
        prev_rows = slice((blk - 1) * BLOCK, blk * BLOCK)
        out = []
        for g in range(ATTN_KV_HEADS):
            lanes = slice(g * 2 * HEAD_DIM, (g + 1) * 2 * HEAD_DIM)
            k_prev = kp_ref[:, lanes] if blk == 0 else kc_ref[prev_rows, lanes]
            v_prev = vp_ref[:, lanes] if blk == 0 else vc_ref[prev_rows, lanes]
            k_bd = block_diag(jnp.concatenate([k_prev, kc_ref[rows, lanes]], axis=0))
            v_bd = block_diag(jnp.concatenate([v_prev, vc_ref[rows, lanes]], axis=0))
            for pair in range(pairs_per_group):
                slab = g * pairs_per_group + pair
                q2 = q_ref[rows, slab * 2 * HEAD_DIM:(slab + 1) * 2 * HEAD_DIM]
                s = lax.dot_general(q2, k_bd, (((1,), (1,)), ((), ())), preferred_element_type=F32)
                out.append((slab, s, v_bd))
        return out

    def finish(blk, scored):
        rows = slice(blk * BLOCK, (blk + 1) * BLOCK)
        tbl_ref = bias0_ref if blk == 0 else bias_ref
        staged = []
        for slab, s, v_bd in scored:
            probs, denoms = [], []
            for side in range(2):
                h = 2 * slab + side
                sh = s[:, side * 2 * BLOCK:(side + 1) * 2 * BLOCK]
                logits = jnp.where(from_prev, sh[:, :BLOCK], sh[:, BLOCK:]) + tbl_ref[h]
                sink = sink_ref[h]
                m = jnp.maximum(jnp.max(logits, axis=-1, keepdims=True), sink)
                p = jnp.exp(logits - m)
                denoms.append(jnp.sum(p, axis=-1, keepdims=True) + jnp.exp(sink - m))
                probs += [jnp.where(from_prev, p, 0.0), jnp.where(from_prev, 0.0, p)]
            staged.append((slab, jnp.concatenate(probs, axis=-1).astype(BF16), v_bd, denoms))
        for slab, p_band, v_bd, denoms in staged:
            o = jnp.dot(p_band, v_bd, preferred_element_type=F32)
            o = o / jnp.where(low_out, denoms[0], denoms[1])
            o_ref[rows, slab * 2 * HEAD_DIM:(slab + 1) * 2 * HEAD_DIM] = o.astype(BF16)

    pending = scores(0)
    for blk in range(ATTN_BLOCKS):
        upcoming = scores(blk + 1) if blk + 1 < ATTN_BLOCKS else None
        finish(blk, pending)
        pending = upcoming


def _attention(q, k, v, sinks, bias_tbl, blocks_per_seq):
    t = q.shape[0]
    tile = ATTN_BLOCKS * BLOCK
    cur = lambda i: (i, 0)
    prev = lambda i: (jnp.maximum(i * ATTN_BLOCKS - 1, 0), 0)
    tbl = (None, ATTN_Q_HEADS, BLOCK, BLOCK)
    return pl.pallas_call(
        _attn_kernel,
        grid=(t // tile,),
        in_specs=[pl.BlockSpec(memory_space=pltpu.SMEM),
                  pl.BlockSpec((tile, ATTN_WIDTH), cur),
                  pl.BlockSpec((BLOCK, KV_DUP_WIDTH), prev),
                  pl.BlockSpec((tile, KV_DUP_WIDTH), cur),
                  pl.BlockSpec((BLOCK, KV_DUP_WIDTH), prev),
                  pl.BlockSpec((tile, KV_DUP_WIDTH), cur),
                  pl.BlockSpec(tbl, lambda i: (jnp.minimum((i * ATTN_BLOCKS) % blocks_per_seq, 1), 0, 0, 0)),
                  pl.BlockSpec(tbl, lambda i: (1, 0, 0, 0))],
        out_specs=pl.BlockSpec((tile, ATTN_WIDTH), cur),
        out_shape=jax.ShapeDtypeStruct((t, ATTN_WIDTH), BF16),
        compiler_params=_params(("parallel",), 24),
        name="swa_attention",
    )(sinks, q, k, k, v, v, bias_tbl, bias_tbl)


def _bias_table_kernel(rel_ref, bucket_ref, o_ref):
    for v in range(2):
        bucket = bucket_ref[v]
        hits = [bucket == b for b in range(N_BUCKETS)]
        for h in range(ATTN_Q_HEADS):
            tbl = jnp.full(bucket.shape, NEG_INF, F32)
            for b in range(N_BUCKETS):
                tbl = jnp.where(hits[b], rel_ref[b * ATTN_Q_HEADS + h], tbl)
            o_ref[v, h] = tbl


def _attention_bias_tables(rel_bias):
    q_idx = np.arange(BLOCK)[:, None]
    j_idx = np.arange(BLOCK)[None, :]
    from_prev = j_idx > q_idx
    n = np.where(from_prev, q_idx + BLOCK - j_idx, q_idx - j_idx)
    large = MAX_EXACT + (np.log(np.maximum(n, 1).astype(np.float32) / np.float32(MAX_EXACT))
                         / np.float32(math.log(MAX_DISTANCE / MAX_EXACT))
                         * np.float32(N_BUCKETS - MAX_EXACT)).astype(np.int32)
    bucket = np.where(n < MAX_EXACT, n, np.minimum(large, N_BUCKETS - 1))
    first = np.where(from_prev, -1, bucket)
    buckets = jnp.asarray(np.stack([first, bucket]).astype(np.int32))
    return pl.pallas_call(
        _bias_table_kernel,
        in_specs=[pl.BlockSpec(memory_space=pltpu.SMEM), pl.BlockSpec(memory_space=pltpu.VMEM)],
        out_specs=pl.BlockSpec(memory_space=pltpu.VMEM),
        out_shape=jax.ShapeDtypeStruct((2, ATTN_Q_HEADS, BLOCK, BLOCK), F32),
        name="t5_bias_table",
    )(rel_bias.astype(F32).reshape(-1), buckets)


def _shift_rows(x, s, fill):
    return jnp.concatenate([jnp.full((s, x.shape[1]), fill, x.dtype), x[:x.shape[0] - s]], axis=0)


def _in_proj_conv_kernel(x_ref, g_ref, w_ref, scw_ref, scb_ref, rgw_ref, rgb_ref, wgate_ref, bgate_ref, lam_ref,
                         q_ref, k_ref, v_ref, o_ref, r_ref, sc_ext, rg_ext, h_carry, *, tiles_per_seq):
    i = pl.program_id(0)
    ts = r_ref.shape[0]
    c0 = CARRY_ROWS

    @pl.when(i == 0)
    def _():
        r_ref[...] = jnp.zeros_like(r_ref)

    @pl.when((i == 0) | ((i + tiles_per_seq - 1) % tiles_per_seq == 0))
    def _():
        sc_ext[0:c0, :] = jnp.zeros((c0, SC_WIDTH), F32)
        rg_ext[0:c0, :] = jnp.zeros((c0, RG_WIDTH), F32)
        h_carry[...] = jnp.zeros_like(h_carry)

    attn_cols = ATTN_WIDTH + 2 * KV_DUP_WIDTH
    hx = _rms(x_ref[...], g_ref[...]).astype(BF16)
    p = jnp.dot(hx, w_ref[:, :attn_cols], preferred_element_type=F32)
    q_ref[...] = p[:, :ATTN_WIDTH].astype(BF16)
    k_ref[...] = p[:, ATTN_WIDTH:ATTN_WIDTH + KV_DUP_WIDTH].astype(BF16)
    v_ref[...] = p[:, ATTN_WIDTH + KV_DUP_WIDTH:].astype(BF16)

    sc_b = r_ref[:, 0:SC_WIDTH]
    sc_ext[c0:c0 + ts, :] = r_ref[:, SC_WIDTH:2 * SC_WIDTH] * r_ref[:, 2 * SC_WIDTH:3 * SC_WIDTH]
    rg_ext[c0:c0 + ts, :] = r_ref[:, 3 * SC_WIDTH:3 * SC_WIDTH + RG_WIDTH]
    rg_g = r_ref[:, 3 * SC_WIDTH + RG_WIDTH:]

    conv = scb_ref[...]
    for k in range(SC_CONV):
        off = c0 - (SC_CONV - 1) + k
        conv = conv + scw_ref[k:k + 1, :] * sc_ext[off:off + ts, :]
    conv_out = sc_b * conv

    rg_in = rgb_ref[...]
    for k in range(RG_CONV):
        off = c0 - (RG_CONV - 1) + k
        rg_in = rg_in + rgw_ref[k:k + 1, :] * rg_ext[off:off + ts, :]

    sc_ext[0:c0, :] = sc_ext[ts:ts + c0, :]
    rg_ext[0:c0, :] = rg_ext[ts:ts + c0, :]

    gates = jnp.dot(rg_in.astype(BF16), wgate_ref[...], preferred_element_type=F32) + bgate_ref[...]

    r_ref[...] = jnp.dot(hx, w_ref[:, attn_cols:], preferred_element_type=F32)

    r_gate = jax.nn.sigmoid(gates[:, :RG_WIDTH])
    i_gate = jax.nn.sigmoid(gates[:, RG_WIDTH:])
    neg_lam = -lam_ref[...]
    softplus = jnp.maximum(neg_lam, 0.0) + jnp.log1p(jnp.exp(-jnp.abs(neg_lam)))
    log_a = -RG_C * r_gate * softplus
    a = jnp.exp(log_a)
    u = jnp.sqrt(jnp.tanh(-log_a) * (1.0 + a * a)) * (i_gate * rg_in)

    s = 1
    while s < ts:
        u = a * _shift_rows(u, s, 0.0) + u
        a = a * _shift_rows(a, s, 1.0)
        s *= 2
    h = a * h_carry[...] + u
    h_carry[...] = h[ts - 1:ts, :]

    c = math.sqrt(2.0 / math.pi)
    gelu = 0.5 * rg_g * (1.0 + jnp.tanh(c * (rg_g + 0.044715 * (rg_g * rg_g * rg_g))))
    o_ref[:, 0:SC_WIDTH] = conv_out.astype(BF16)
    o_ref[:, SC_WIDTH:] = (h * gelu).astype(BF16)


def _in_proj_conv(x, g, w, sc_w, sc_b, rg_w, rg_b, w_gate, b_gate, lam, seq):
    t = x.shape[0]
    n = w.shape[1]
    tiles = t // TS
    assert n == ATTN_WIDTH + 2 * KV_DUP_WIDTH + REST_WIDTH
    proj = lambda width: pl.BlockSpec((TS, width), lambda i: (jnp.minimum(i, tiles - 1), 0))
    lagged = pl.BlockSpec((TS, SC_WIDTH + RG_WIDTH), lambda i: (jnp.maximum(i - 1, 0), 0))
    return pl.pallas_call(
        functools.partial(_in_proj_conv_kernel, tiles_per_seq=seq // TS),
        grid=(tiles + 1,),
        in_specs=[proj(D_MODEL), _const_spec((1, D_MODEL)), _const_spec((D_MODEL, n)),
                  _const_spec((SC_CONV, SC_WIDTH)), _const_spec((1, SC_WIDTH)),
                  _const_spec((RG_CONV, RG_WIDTH)), _const_spec((1, RG_WIDTH)),
                  _const_spec((RG_WIDTH, 2 * RG_WIDTH)), _const_spec((1, 2 * RG_WIDTH)),
                  _const_spec((1, RG_WIDTH))],
        out_specs=[proj(ATTN_WIDTH), proj(KV_DUP_WIDTH), proj(KV_DUP_WIDTH), lagged],
        out_shape=[jax.ShapeDtypeStruct((t, ATTN_WIDTH), BF16),
                   jax.ShapeDtypeStruct((t, KV_DUP_WIDTH), BF16),
                   jax.ShapeDtypeStruct((t, KV_DUP_WIDTH), BF16),
                   jax.ShapeDtypeStruct((t, SC_WIDTH + RG_WIDTH), BF16)],
        scratch_shapes=[pltpu.VMEM((TS, REST_WIDTH), F32),
                        pltpu.VMEM((TS + 2 * CARRY_ROWS, SC_WIDTH), F32),
                        pltpu.VMEM((TS + 2 * CARRY_ROWS, RG_WIDTH), F32),
                        pltpu.VMEM((1, RG_WIDTH), F32)],
        compiler_params=_params(("arbitrary",), 48),
        name="in_proj_conv_rglru",
    )(x, g, w, sc_w, sc_b, rg_w, rg_b, w_gate, b_gate, lam)


def _mem_kv_kernel(m_ref, g_ref, w_ref, k_ref, v_ref):
    h = _rms(m_ref[...], g_ref[...]).astype(BF16)
    p = jnp.dot(h, w_ref[...], preferred_element_type=F32)
    k_ref[...] = p[:, :XA_WIDTH].astype(BF16)
    v_ref[...] = p[:, XA_WIDTH:].astype(BF16)


def _mem_kv(mem, g, wkv):
    t = mem.shape[0]
    row = lambda width: pl.BlockSpec((MEM_LEN, width), lambda i: (i, 0))
    return pl.pallas_call(
        _mem_kv_kernel,
        grid=(t // MEM_LEN,),
        in_specs=[row(D_MODEL), _const_spec((1, D_MODEL)), _const_spec((D_MODEL, 2 * XA_WIDTH))],
        out_specs=[row(XA_WIDTH), row(XA_WIDTH)],
        out_shape=[jax.ShapeDtypeStruct((t, XA_WIDTH), BF16)] * 2,
        compiler_params=_params(("parallel",), 24),
        name="mem_kv",
    )(mem, g, wkv)


def _post_mixer_kernel(x_ref, a_ref, c_ref, wout_ref, g_ref, wq_ref, k_ref, v_ref, wo_ref, *rest):
    o_ref = rest[-1] if len(rest) == 1 else rest[4]
    halves = [slice(i * (TM // 2), (i + 1) * (TM // 2)) for i in range(2)]
    heads = [slice(hd * XA_HEAD_DIM, (hd + 1) * XA_HEAD_DIM) for hd in range(XA_HEADS)]
    nt = (((1,), (1,)), ((), ()))
    k = k_ref[...]
    v = v_ref[...]

    x1 = [x_ref[hs, :] + jnp.dot(jnp.concatenate([a_ref[hs, :], c_ref[hs, :]], axis=-1), wout_ref[...],
                                 preferred_element_type=F32) for hs in halves]
    q = [jnp.dot(_rms(xh, g_ref[...]).astype(BF16), wq_ref[...], preferred_element_type=F32).astype(BF16)
         for xh in x1]
    scores = [[lax.dot_general(qh[:, sl], k[:, sl], nt, preferred_element_type=F32) for sl in heads] for qh in q]
    x2 = []
    for xh, per_head in zip(x1, scores):
        probs, sums = [], []
        for s in per_head:
            s = s * (1.0 / math.sqrt(XA_HEAD_DIM))
            p = jnp.exp(s - jnp.max(s, axis=-1, keepdims=True))
            probs.append(p.astype(BF16))
            sums.append(jnp.sum(p, axis=-1, keepdims=True))
        att = jnp.concatenate([jnp.dot(p, v[:, sl], preferred_element_type=F32) / l
                               for p, sl, l in zip(probs, heads, sums)], axis=-1).astype(BF16)
        x2.append(xh + jnp.dot(att, wo_ref[...], preferred_element_type=F32))

    for hs, xh in zip(halves, x2):
        o_ref[hs, :] = xh
    if len(rest) > 1:
        ffn_g_ref, wr_ref, tri_ref, sel_ref, _, hf_ref, meta_ref, metat_ref, cnt_ref, carry = rest
        h = _rms(jnp.concatenate(x2, axis=0), ffn_g_ref[...])
        _route(h, wr_ref, tri_ref, sel_ref, hf_ref, meta_ref, metat_ref, cnt_ref, carry)


def _post_mixer(x, attn, cr, w_out, g, wq, k, v, wo, seq, route=None):
    t = x.shape[0]
    per_seq = seq // TM
    row = lambda width: pl.BlockSpec((TM, width), lambda i: (i, 0))
    mem_blk = pl.BlockSpec((MEM_LEN, XA_WIDTH), lambda i: (i // per_seq, 0))
    in_specs = [row(D_MODEL), row(ATTN_WIDTH), row(SC_WIDTH + RG_WIDTH), _const_spec((D_MODEL, D_MODEL)),
                _const_spec((1, D_MODEL)), _const_spec((D_MODEL, XA_WIDTH)), mem_blk, mem_blk,
                _const_spec((XA_WIDTH, D_MODEL))]
    out_specs = [row(D_MODEL)]
    out_shape = [jax.ShapeDtypeStruct((t, D_MODEL), F32)]
    scratch = []
    args = [x, attn, cr, w_out, g, wq, k, v, wo]
    if route is not None:
        assert TM % ROUTE_CHUNK == 0
        in_specs += [_const_spec((1, D_MODEL)), _const_spec((D_MODEL, LANES_V7X)),
                     _const_spec((ROUTE_CHUNK, ROUTE_CHUNK)), _const_spec((SUBLANES_V7X, LANES_V7X))]
        out_specs += [pl.BlockSpec((TM * ROW_CHUNKS, LANES_V7X), lambda i: (i, 0)), row(LANES_V7X),
                      pl.BlockSpec((SUBLANES_V7X, TM), lambda i: (0, i)),
                      pl.BlockSpec((1, LANES_V7X), lambda i: (0, 0))]
        out_shape += [jax.ShapeDtypeStruct((t * ROW_CHUNKS, LANES_V7X), F32),
                      jax.ShapeDtypeStruct((t, LANES_V7X), F32),
                      jax.ShapeDtypeStruct((SUBLANES_V7X, t), F32),
                      jax.ShapeDtypeStruct((1, LANES_V7X), F32)]
        scratch = [pltpu.VMEM((1, LANES_V7X), F32)]
        args += list(route)
    return pl.pallas_call(
        _post_mixer_kernel,
        grid=(t // TM,),
        in_specs=in_specs,
        out_specs=out_specs,
        out_shape=out_shape,
        scratch_shapes=scratch,
        compiler_params=_params(("arbitrary",), 40),
        name="post_mixer_route" if route is not None else "post_mixer",
    )(*args)


def _ffn_kernel(x_ref, g_ref, wg_ref, wu_ref, wd_ref, *rest):
    n_cast = (len(rest) - 1) // 2
    o_ref = rest[n_cast]
    x = x_ref[...]
    h = _rms(x, g_ref[...]).astype(BF16)
    gate = jnp.dot(h, wg_ref[...], preferred_element_type=F32)
    up = jnp.dot(h, wu_ref[...], preferred_element_type=F32)
    act = (gate * jax.nn.sigmoid(gate) * up).astype(BF16)
    o_ref[...] = x + jnp.dot(act, wd_ref[...], preferred_element_type=F32)
    for src, dst in zip(rest[:n_cast], rest[n_cast + 1:]):
        dst[...] = src[...].astype(BF16)


def _dense_ffn(x, g, wg, wu, wd, cast_along=()):
    t = x.shape[0]
    steps = t // TM
    d_ff = wg.shape[1]
    row = pl.BlockSpec((TM, D_MODEL), lambda i: (i, 0))
    flat = [w.reshape(-1, w.shape[-1]) for w in cast_along]
    for w in flat:
        assert w.shape[0] % (steps * 2 * SUBLANES_V7X) == 0
    slabs = [pl.BlockSpec((w.shape[0] // steps, w.shape[1]), lambda i: (i, 0)) for w in flat]
    outs = pl.pallas_call(
        _ffn_kernel,
        grid=(steps,),
        in_specs=[row, _const_spec((1, D_MODEL)), _const_spec((D_MODEL, d_ff)), _const_spec((D_MODEL, d_ff)),
                  _const_spec((d_ff, D_MODEL))] + slabs,
        out_specs=[row] + slabs,
        out_shape=[jax.ShapeDtypeStruct((t, D_MODEL), F32)] + [jax.ShapeDtypeStruct(w.shape, BF16) for w in flat],
        compiler_params=_params(("parallel",), 60),
        name="dense_swiglu",
    )(x, g, wg, wu, wd, *flat)
    return outs[0], [o.reshape(w.shape) for o, w in zip(outs[1:], cast_along)]


META_E1, META_E2, META_R1, META_R2, META_W1, META_W2 = range(6)


def _to_token_tiles(ref, rows):
    m = rows.shape[0]
    for c in range(ROW_CHUNKS):
        ref[pl.ds(c, m, stride=ROW_CHUNKS), :] = rows[:, c * LANES_V7X:(c + 1) * LANES_V7X]


def _from_token_tiles(ref):
    m = ref.shape[0] // ROW_CHUNKS
    return jnp.concatenate([ref[pl.ds(c, m, stride=ROW_CHUNKS), :] for c in range(ROW_CHUNKS)], axis=-1)


def _route(h, wr_ref, tri_ref, sel_ref, hf_ref, meta_ref, metat_ref, cnt_ref, carry):
    @pl.when(pl.program_id(0) == 0)
    def _():
        carry[...] = jnp.zeros_like(carry)

    _to_token_tiles(hf_ref, h)
    chunks = [slice(c * ROUTE_CHUNK, (c + 1) * ROUTE_CHUNK) for c in range(h.shape[0] // ROUTE_CHUNK)]
    each = lambda fn, *lists: [fn(*vals) for vals in zip(*lists)]
    rowmax = lambda a: jnp.max(a, axis=-1, keepdims=True)
    rowsum = lambda a: jnp.sum(a, axis=-1, keepdims=True)
    lane = lax.broadcasted_iota(jnp.int32, (ROUTE_CHUNK, LANES_V7X), 1)
    first_hit = lambda lg, m: jnp.min(jnp.where(lg == m, lane, LANES_V7X), axis=-1, keepdims=True)

    hb = h.astype(BF16)
    logits = [jnp.dot(hb[c, :], wr_ref[...], preferred_element_type=F32) for c in chunks]
    lg = each(lambda l: jnp.where(lane < N_EXPERTS, l, -jnp.inf), logits)
    m1 = each(rowmax, lg)
    e1 = each(first_hit, lg, m1)
    lg2 = each(lambda l, e: jnp.where(lane == e, -jnp.inf, l), lg, e1)
    m2 = each(rowmax, lg2)
    e2 = each(first_hit, lg2, m2)
    ex = each(lambda a, b: jnp.exp(b - a), m1, m2)
    w1 = each(lambda e: 1.0 / (1.0 + e), ex)
    w2 = each(lambda e: e / (1.0 + e), ex)
    hit1 = each(lambda e: lane == e, e1)
    hit2 = each(lambda e: lane == e, e2)
    onehot = each(lambda a, b: (a | b).astype(BF16), hit1, hit2)
    within = each(lambda o: jnp.dot(tri_ref[...], o, preferred_element_type=F32), onehot)
    totals = each(lambda o: jnp.sum(o.astype(F32), axis=0, keepdims=True), onehot)
    ahead, base = [], carry[...]
    for w, tot in zip(within, totals):
        ahead.append(w + base)
        base = base + tot
    carry[...] = base
    cnt_ref[...] = base
    r1 = each(lambda hit, a: rowsum(jnp.where(hit, a, 0.0)), hit1, ahead)
    r2 = each(lambda hit, a: rowsum(jnp.where(hit, a, 0.0)), hit2, ahead)

    def record(*vals):
        meta = jnp.zeros((ROUTE_CHUNK, LANES_V7X), F32)
        for col, val in zip((META_E1, META_E2, META_R1, META_R2, META_W1, META_W2), vals):
            meta = jnp.where(lane == col, val.astype(F32), meta)
        return meta

    meta = each(record, e1, e2, r1, r2, w1, w2)
    metat = each(lambda mt: lax.dot_general(sel_ref[...], mt, (((1,), (1,)), ((), ())), preferred_element_type=F32,
                                            precision=lax.Precision.HIGHEST), meta)
    for c, mt, mtt in zip(chunks, meta, metat):
        meta_ref[c, :] = mt
        metat_ref[:, c] = mtt


def _token_rows(ref, first_token, n_tokens):
    start = pl.multiple_of(first_token * ROW_CHUNKS, ROW_CHUNKS)
    return ref.at[pl.ds(start, n_tokens * ROW_CHUNKS)]


def _row_copy(src_ref, src_token, dst_ref, dst_token, sem):
    return pltpu.make_async_copy(_token_rows(src_ref, src_token, 1), _token_rows(dst_ref, dst_token, 1), sem)


def _dispatch_kernel(fill_ref, pos1_ref, pos2_ref, hf_ref, xs_ref, zeros, sem_z, sem):
    @pl.when(pl.program_id(0) == 0)
    def _():
        zeros[...] = jnp.zeros_like(zeros)

        def tile_fill(e):
            return pltpu.make_async_copy(zeros, _token_rows(xs_ref, pl.multiple_of(fill_ref[e], TM_GROUP), TM_GROUP),
                                         sem_z)

        for e in range(2 * N_EXPERTS):
            @pl.when(fill_ref[e] >= 0)
            def _():
                tile_fill(e).start()
        for e in range(2 * N_EXPERTS):
            @pl.when(fill_ref[e] >= 0)
            def _():
                tile_fill(e).wait()

    def issue(r, carry):
        _row_copy(hf_ref, r, xs_ref, pos1_ref[r], sem).start(priority=0)
        _row_copy(hf_ref, r, xs_ref, pos2_ref[r], sem).start(priority=1)
        return carry

    lax.fori_loop(0, TM_MOVE, issue, 0, unroll=ISSUE_UNROLL)
    whole_step = pltpu.make_async_copy(hf_ref, _token_rows(xs_ref, 0, TM_MOVE), sem)
    whole_step.wait()
    whole_step.wait()


def _dispatch(fill_start, pos1, pos2, hf, n_rows):
    t = hf.shape[0] // ROW_CHUNKS
    idx = pl.BlockSpec((TM_MOVE,), lambda i, fill: (i,), memory_space=pltpu.SMEM)
    return pl.pallas_call(
        _dispatch_kernel,
        grid_spec=pltpu.PrefetchScalarGridSpec(
            num_scalar_prefetch=1,
            grid=(t // TM_MOVE,),
            in_specs=[idx, idx, pl.BlockSpec((TM_MOVE * ROW_CHUNKS, LANES_V7X), lambda i, fill: (i, 0))],
            out_specs=pl.BlockSpec(memory_space=pl.ANY),
            scratch_shapes=[pltpu.VMEM((TM_GROUP * ROW_CHUNKS, LANES_V7X), F32), pltpu.SemaphoreType.DMA(()),
                            pltpu.SemaphoreType.DMA(())]),
        out_shape=jax.ShapeDtypeStruct((n_rows * ROW_CHUNKS, LANES_V7X), F32),
        compiler_params=_params(("arbitrary",), 24),
        name="moe_dispatch",
    )(fill_start, pos1, pos2, hf)


def _grouped_kernel(te_ref, used_ref, x_ref, wg_ref, wu_ref, wd_ref, y_ref):
    i = pl.program_id(0)

    @pl.when(i < used_ref[0])
    def _():
        h = _from_token_tiles(x_ref).astype(BF16)
        ff = wg_ref.shape[1] // FF_SPLIT
        cols = [slice(c * ff, (c + 1) * ff) for c in range(FF_SPLIT)]
        gates = [jnp.dot(h, wg_ref[:, c], preferred_element_type=F32) for c in cols]
        ups = [jnp.dot(h, wu_ref[:, c], preferred_element_type=F32) for c in cols]
        total = None
        for c, gate, up in zip(cols, gates, ups):
            act = (gate * jax.nn.sigmoid(gate) * up).astype(BF16)
            part = jnp.dot(act, wd_ref[c, :], preferred_element_type=F32)
            total = part if total is None else total + part
        _to_token_tiles(y_ref, total)

    @pl.when(i >= used_ref[0])
    def _():
        y_ref[...] = jnp.zeros_like(y_ref)


def _grouped_swiglu(tile_expert, n_used, xs, wg, wu, wd):
    n_rows = xs.shape[0] // ROW_CHUNKS
    d_ff = wg.shape[2]
    tile = (TM_GROUP * ROW_CHUNKS, LANES_V7X)
    src = lambda i, te, used: (jnp.maximum(jnp.minimum(i, used[0] - 1), 0), 0)
    expert = lambda shape: pl.BlockSpec((None,) + shape, lambda i, te, used: (te[i], 0, 0),
                                        pipeline_mode=pl.Buffered(1))
    return pl.pallas_call(
        _grouped_kernel,
        grid_spec=pltpu.PrefetchScalarGridSpec(
            num_scalar_prefetch=2,
            grid=(n_rows // TM_GROUP,),
            in_specs=[pl.BlockSpec(tile, src), expert((D_MODEL, d_ff)), expert((D_MODEL, d_ff)),
                      expert((d_ff, D_MODEL))],
            out_specs=pl.BlockSpec(tile, lambda i, te, used: (i, 0))),
        out_shape=jax.ShapeDtypeStruct((n_rows * ROW_CHUNKS, LANES_V7X), F32),
        compiler_params=_params(("arbitrary",), 58),
        name="moe_grouped_swiglu",
    )(tile_expert, n_used, xs, wg, wu, wd)


def _combine_kernel(pos1_ref, pos2_ref, pos1_next_ref, pos2_next_ref, x_ref, meta_ref, g_ref, y_ref, o_ref,
                    y1, y2, sems):
    i = pl.program_id(0)

    def gather(p1_ref, p2_ref, slot):
        def issue(r, carry):
            _row_copy(y_ref, p1_ref[r], y1.at[slot], r, sems.at[slot]).start(priority=0)
            _row_copy(y_ref, p2_ref[r], y2.at[slot], r, sems.at[slot]).start(priority=1)
            return carry
        lax.fori_loop(0, TM_COMBINE, issue, 0, unroll=ISSUE_UNROLL)

    def finish(slot):
        for buf in (y1, y2):
            pltpu.make_async_copy(_token_rows(y_ref, 0, TM_COMBINE), buf.at[slot], sems.at[slot]).wait()
        w1 = meta_ref[:, META_W1:META_W1 + 1]
        w2 = meta_ref[:, META_W2:META_W2 + 1]
        out = x_ref[...] + (w1 * _from_token_tiles(y1.at[slot]) + w2 * _from_token_tiles(y2.at[slot]))
        o_ref[...] = _rms(out, g_ref[...])

    @pl.when(i == 0)
    def _():
        gather(pos1_ref, pos2_ref, 0)

    for slot in range(2):
        @pl.when(i % 2 == slot)
        def _():
            @pl.when(i + 1 < pl.num_programs(0))
            def _():
                gather(pos1_next_ref, pos2_next_ref, 1 - slot)
            finish(slot)


def _combine(pos1, pos2, x, meta, g, y):
    t = x.shape[0]
    steps = t // TM_COMBINE
    idx = pl.BlockSpec((TM_COMBINE,), lambda i: (i,), memory_space=pltpu.SMEM)
    idx_next = pl.BlockSpec((TM_COMBINE,), lambda i: (jnp.minimum(i + 1, steps - 1),), memory_space=pltpu.SMEM)
    row = lambda width: pl.BlockSpec((TM_COMBINE, width), lambda i: (i, 0))
    slots = pltpu.VMEM((2, TM_COMBINE * ROW_CHUNKS, LANES_V7X), F32)
    return pl.pallas_call(
        _combine_kernel,
        grid=(steps,),
        in_specs=[idx, idx, idx_next, idx_next, row(D_MODEL), row(LANES_V7X), _const_spec((1, D_MODEL)),
                  pl.BlockSpec(memory_space=pl.ANY)],
        out_specs=row(D_MODEL),
        out_shape=jax.ShapeDtypeStruct((t, D_MODEL), F32),
        scratch_shapes=[slots, slots, pltpu.SemaphoreType.DMA((2,))],
        compiler_params=_params(("arbitrary",), 24),
        name="moe_combine_norm",
    )(pos1, pos2, pos1, pos2, x, meta, g, y)


def _route_operands(ffn_g, router_w):
    wr_pad = jnp.zeros((D_MODEL, LANES_V7X), BF16).at[:, :N_EXPERTS].set(router_w.astype(BF16))
    tri = jnp.tril(jnp.ones((ROUTE_CHUNK, ROUTE_CHUNK), BF16), -1)
    sel = jnp.eye(SUBLANES_V7X, LANES_V7X, dtype=F32)
    return ffn_g, wr_pad, tri, sel


def _moe_layer(x, hf, meta, metat, counts, wg, wu, wd, final_g):
    t = x.shape[0]
    n_rows = 2 * t + N_EXPERTS * TM_GROUP

    cnt = counts[0, :N_EXPERTS].astype(jnp.int32)
    padded = (cnt + TM_GROUP - 1) // TM_GROUP * TM_GROUP
    ends = jnp.cumsum(padded)
    starts = ends - padded
    experts = jnp.arange(N_EXPERTS, dtype=jnp.int32)[:, None]
    group_start = lambda e: jnp.sum(jnp.where(e[None, :] == experts, starts[:, None], 0), axis=0)
    pos1 = group_start(metat[META_E1].astype(jnp.int32)) + metat[META_R1].astype(jnp.int32)
    pos2 = group_start(metat[META_E2].astype(jnp.int32)) + metat[META_R2].astype(jnp.int32)
    n_used = (ends[-1] // TM_GROUP).astype(jnp.int32).reshape(1)
    tile_row = jnp.minimum(jnp.arange(n_rows // TM_GROUP, dtype=jnp.int32), n_used[0] - 1) * TM_GROUP
    tile_expert = jnp.sum(ends[None, :] <= tile_row[:, None], axis=1).astype(jnp.int32)
    tail_start = jnp.where(padded > 0, ends - TM_GROUP, -1)
    spare = ends[-1] + TM_GROUP * jnp.arange(N_EXPERTS, dtype=jnp.int32)
    fill_start = jnp.concatenate([tail_start, jnp.where(spare < n_rows, spare, -1)]).astype(jnp.int32)

    xs = _dispatch(fill_start, pos1, pos2, hf, n_rows)
    y = _grouped_swiglu(tile_expert, n_used, xs, wg, wu, wd)
    return _combine(pos1, pos2, x, meta, final_g, y)


def _arrange_in_proj(w):
    q = (w[:, :ATTN_WIDTH] * (1.0 / math.sqrt(HEAD_DIM))).astype(BF16)
    w = w.astype(BF16)
    dup = lambda start: [w[:, start + h * HEAD_DIM:start + (h + 1) * HEAD_DIM]
                         for h in range(ATTN_KV_HEADS) for _ in range(2)]
    return jnp.concatenate([q] + dup(ATTN_WIDTH) + dup(ATTN_WIDTH + KV_WIDTH) + [w[:, ATTN_WIDTH + 2 * KV_WIDTH:]],
                           axis=1)


def _block_diag(w):
    heads, d, _ = w.shape
    eye = jnp.eye(heads, dtype=w.dtype)
    return jnp.einsum('hij,hg->higj', w, eye).reshape(heads * d, heads * d)


def kernel(x, mem, rel_bias, mix_norm, w_in, attn_sinks, sc_conv_w, sc_conv_b, rg_conv_w, rg_conv_b, rg_w_a,
           rg_b_a, rg_w_x, rg_b_x, rg_lambda, w_out, xa_norm, mem_norm, xa_wq, xa_wk, xa_wv, xa_wo, ffn_norm,
           dense_wg, dense_wu, dense_wd, moe_router, moe_wg, moe_wu, moe_wd, final_norm):
    batch, seq, _ = x.shape
    depth = w_in.shape[0]
    assert depth == 2 and seq % TS == 0 and seq % TM == 0
    xt = x.reshape(batch * seq, D_MODEL)
    memt = mem.reshape(batch * MEM_LEN, D_MODEL)
    bias_tbl = _attention_bias_tables(rel_bias)
    vec = lambda a: a.reshape(1, -1)

    for layer in range(depth):
        w_gate = jnp.concatenate([_block_diag(rg_w_a[layer]), _block_diag(rg_w_x[layer])], axis=1).astype(BF16)
        b_gate = jnp.concatenate([rg_b_a[layer], rg_b_x[layer]]).reshape(1, -1)
        q, k, v, cr = _in_proj_conv(xt, vec(mix_norm[layer]), _arrange_in_proj(w_in[layer]),
                                    sc_conv_w[layer], vec(sc_conv_b[layer]), rg_conv_w[layer],
                                    vec(rg_conv_b[layer]), w_gate, b_gate, vec(rg_lambda[layer]), seq)
        attn = _attention(q, k, v, attn_sinks[layer], bias_tbl, seq // BLOCK)
        wkv = jnp.concatenate([xa_wk[layer], xa_wv[layer]], axis=1).astype(BF16)
        mk, mv = _mem_kv(memt, vec(mem_norm[layer]), wkv)
        post = functools.partial(_post_mixer, xt, attn, cr, w_out[layer].astype(BF16), vec(xa_norm[layer]),
                                 xa_wq[layer].astype(BF16), mk, mv, xa_wo[layer].astype(BF16), seq)

        j = layer // 2
        if layer % 2 == 0:
            (xt,) = post()
            xt, expert_w = _dense_ffn(xt, vec(ffn_norm[layer]), dense_wg[j].astype(BF16), dense_wu[j].astype(BF16),
                                      dense_wd[j].astype(BF16), cast_along=(moe_wg[j], moe_wu[j], moe_wd[j]))
        else:
            xt, hf, meta, metat, counts = post(route=_route_operands(vec(ffn_norm[layer]), moe_router[j]))
            xt = _moe_layer(xt, hf, meta, metat, counts, *expert_w, vec(final_norm))
    return xt.reshape(batch, seq, D_MODEL)
```

```python
import functools
import math

import jax
import jax.numpy as jnp
import numpy as np
from jax import lax
from jax.experimental import pallas as pl
from jax.experimental.pallas import tpu as pltpu

F32 = jnp.float32
BF16 = jnp.bfloat16

D_MODEL = 1024
MEM_LEN = 256
HEAD_DIM = 64
ATTN_Q_HEADS = 8
ATTN_KV_HEADS = 2
ATTN_WIDTH = ATTN_Q_HEADS * HEAD_DIM
KV_WIDTH = ATTN_KV_HEADS * HEAD_DIM
KV_DUP_WIDTH = 2 * KV_WIDTH
BLOCK = 128
SC_WIDTH = 256
SC_CONV = 3
RG_WIDTH = 256
RG_HEADS = 4
RG_HEAD_DIM = RG_WIDTH // RG_HEADS
RG_CONV = 4
RG_C = 8.0
N_BUCKETS = 32
MAX_EXACT = N_BUCKETS // 2
MAX_DISTANCE = 128
XA_HEADS = 4
XA_HEAD_DIM = 128
XA_WIDTH = XA_HEADS * XA_HEAD_DIM
N_EXPERTS = 8
EPS = 1e-6
NEG_INF = -1e30
REST_WIDTH = 3 * SC_WIDTH + 2 * RG_WIDTH

LANES_V7X = 128
SUBLANES_V7X = 8
VMEM_BYTES_V7X = 64 * 1024 * 1024
ROW_CHUNKS = D_MODEL // LANES_V7X
assert ROW_CHUNKS == SUBLANES_V7X

TM = 512
ATTN_BLOCKS = 8
TS = 512
ROUTE_CHUNK = 128
TM_GROUP = 512
TM_MOVE = 1024
TM_COMBINE = 256
ISSUE_UNROLL = 8
FF_SPLIT = 2
CARRY_ROWS = SUBLANES_V7X


def _mib(n):
    return int(n * 1024 * 1024)


def _params(semantics, vmem_mib):
    assert _mib(vmem_mib) < VMEM_BYTES_V7X
    return pltpu.CompilerParams(dimension_semantics=semantics, vmem_limit_bytes=_mib(vmem_mib))


def _rms(x, g):
    ms = jnp.mean(x * x, axis=-1, keepdims=True)
    return x * lax.rsqrt(ms + EPS) * g


def _const_spec(shape):
    nd = len(shape)
    return pl.BlockSpec(shape, lambda *_: (0,) * nd, pipeline_mode=pl.Buffered(1))


def _attn_kernel(sink_ref, q_ref, kp_ref, kc_ref, vp_ref, vc_ref, bias0_ref, bias_ref, o_ref):
    pairs_per_group = ATTN_Q_HEADS // ATTN_KV_HEADS // 2
    row = lax.broadcasted_iota(jnp.int32, (BLOCK, BLOCK), 0)
    col = lax.broadcasted_iota(jnp.int32, (BLOCK, BLOCK), 1)
    from_prev = col > row
    low_lanes = lax.broadcasted_iota(jnp.int32, (2 * BLOCK, 2 * HEAD_DIM), 1) < HEAD_DIM
    low_out = col < HEAD_DIM
    zero = jnp.zeros((), BF16)

    def block_diag(band):
        return jnp.concatenate([jnp.where(low_lanes, band, zero), jnp.where(low_lanes, zero, band)], axis=0)

    def scores(blk):
        rows = slice(blk * BLOCK, (blk + 1) * BLOCK)
        prev_rows = slice((blk - 1) * BLOCK, blk * BLOCK)
        out = []
        for g in range(ATTN_KV_HEADS):
            lanes = slice(g * 2 * HEAD_DIM, (g + 1) * 2 * HEAD_DIM)
            k_prev = kp_ref[:, lanes] if blk == 0 else kc_ref[prev_rows, lanes]
            v_prev = vp_ref[:, lanes] if blk == 0 else vc_ref[prev_rows, lanes]
            k_bd = block_diag(jnp.concatenate([k_prev, kc_ref[rows, lanes]], axis=0))
            v_bd = block_diag(jnp.concatenate([v_prev, vc_ref[rows, lanes]], axis=0))
            for pair in range(pairs_per_group):
                slab = g * pairs_per_group + pair
                q2 = q_ref[rows, slab * 2 * HEAD_DIM:(slab + 1) * 2 * HEAD_DIM]
                s = lax.dot_general(q2, k_bd, (((1,), (1,)), ((), ())), preferred_element_type=F32)
                out.append((slab, s, v_bd))
        return out

    def finish(blk, scored):
        rows = slice(blk * BLOCK, (blk + 1) * BLOCK)
        tbl_ref = bias0_ref if blk == 0 else bias_ref
        staged = []
        for slab, s, v_bd in scored:
            probs, denoms = [], []
            for side in range(2):
                h = 2 * slab + side
                sh = s[:, side * 2 * BLOCK:(side + 1) * 2 * BLOCK]
                logits = jnp.where(from_prev, sh[:, :BLOCK], sh[:, BLOCK:]) + tbl_ref[h]
                sink = sink_ref[h]
                m = jnp.maximum(jnp.max(logits, axis=-1, keepdims=True), sink)
                p = jnp.exp(logits - m)
                denoms.append(jnp.sum(p, axis=-1, keepdims=True) + jnp.exp(sink - m))
                probs += [jnp.where(from_prev, p, 0.0), jnp.where(from_prev, 0.0, p)]
            staged.append((slab, jnp.concatenate(probs, axis=-1).astype(BF16), v_bd, denoms))
        for slab, p_band, v_bd, denoms in staged:
            o = jnp.dot(p_band, v_bd, preferred_element_type=F32)
            o = o / jnp.where(low_out, denoms[0], denoms[1])
            o_ref[rows, slab * 2 * HEAD_DIM:(slab + 1) * 2 * HEAD_DIM] = o.astype(BF16)

    pending = scores(0)
    for blk in range(ATTN_BLOCKS):
        upcoming = scores(blk + 1) if blk + 1 < ATTN_BLOCKS else None
        finish(blk, pending)
        pending = upcoming


def _attention(q, k, v, sinks, bias_tbl, blocks_per_seq):
    t = q.shape[0]
    tile = ATTN_BLOCKS * BLOCK
    cur = lambda i: (i, 0)
    prev = lambda i: (jnp.maximum(i * ATTN_BLOCKS - 1, 0), 0)
    tbl = (None, ATTN_Q_HEADS, BLOCK, BLOCK)
    return pl.pallas_call(
        _attn_kernel,
        grid=(t // tile,),
        in_specs=[pl.BlockSpec(memory_space=pltpu.SMEM),
                  pl.BlockSpec((tile, ATTN_WIDTH), cur),
                  pl.BlockSpec((BLOCK, KV_DUP_WIDTH), prev),
                  pl.BlockSpec((tile, KV_DUP_WIDTH), cur),
                  pl.BlockSpec((BLOCK, KV_DUP_WIDTH), prev),
                  pl.BlockSpec((tile, KV_DUP_WIDTH), cur),
                  pl.BlockSpec(tbl, lambda i: (jnp.minimum((i * ATTN_BLOCKS) % blocks_per_seq, 1), 0, 0, 0)),
                  pl.BlockSpec(tbl, lambda i: (1, 0, 0, 0))],
        out_specs=pl.BlockSpec((tile, ATTN_WIDTH), cur),
        out_shape=jax.ShapeDtypeStruct((t, ATTN_WIDTH), BF16),
        compiler_params=_params(("parallel",), 24),
        name="swa_attention",
    )(sinks, q, k, k, v, v, bias_tbl, bias_tbl)


def _bias_table_kernel(rel_ref, bucket_ref, o_ref):
    for v in range(2):
        bucket = bucket_ref[v]
        hits = [bucket == b for b in range(N_BUCKETS)]
        for h in range(ATTN_Q_HEADS):
            tbl = jnp.full(bucket.shape, NEG_INF, F32)
            for b in range(N_BUCKETS):
                tbl = jnp.where(hits[b], rel_ref[b * ATTN_Q_HEADS + h], tbl)
            o_ref[v, h] = tbl


def _attention_bias_tables(rel_bias):
    q_idx = np.arange(BLOCK)[:, None]
    j_idx = np.arange(BLOCK)[None, :]
    from_prev = j_idx > q_idx
    n = np.where(from_prev, q_idx + BLOCK - j_idx, q_idx - j_idx)
    large = MAX_EXACT + (np.log(np.maximum(n, 1).astype(np.float32) / np.float32(MAX_EXACT))
                         / np.float32(math.log(MAX_DISTANCE / MAX_EXACT))
                         * np.float32(N_BUCKETS - MAX_EXACT)).astype(np.int32)
    bucket = np.where(n < MAX_EXACT, n, np.minimum(large, N_BUCKETS - 1))
    first = np.where(from_prev, -1, bucket)
    buckets = jnp.asarray(np.stack([first, bucket]).astype(np.int32))
    return pl.pallas_call(
        _bias_table_kernel,
        in_specs=[pl.BlockSpec(memory_space=pltpu.SMEM), pl.BlockSpec(memory_space=pltpu.VMEM)],
        out_specs=pl.BlockSpec(memory_space=pltpu.VMEM),
        out_shape=jax.ShapeDtypeStruct((2, ATTN_Q_HEADS, BLOCK, BLOCK), F32),
        name="t5_bias_table",
    )(rel_bias.astype(F32).reshape(-1), buckets)


def _shift_rows(x, s, fill):
    return jnp.concatenate([jnp.full((s, x.shape[1]), fill, x.dtype), x[:x.shape[0] - s]], axis=0)


def _in_proj_conv_kernel(x_ref, g_ref, w_ref, scw_ref, scb_ref, rgw_ref, rgb_ref, wgate_ref, bgate_ref, lam_ref,
                         q_ref, k_ref, v_ref, o_ref, r_ref, sc_ext, rg_ext, h_carry, *, tiles_per_seq):
    i = pl.program_id(0)
    ts = r_ref.shape[0]
    c0 = CARRY_ROWS

    @pl.when(i == 0)
    def _():
        r_ref[...] = jnp.zeros_like(r_ref)

    @pl.when((i == 0) | ((i + tiles_per_seq - 1) % tiles_per_seq == 0))
    def _():
        sc_ext[0:c0, :] = jnp.zeros((c0, SC_WIDTH), F32)
        rg_ext[0:c0, :] = jnp.zeros((c0, RG_WIDTH), F32)
        h_carry[...] = jnp.zeros_like(h_carry)

    attn_cols = ATTN_WIDTH + 2 * KV_DUP_WIDTH
    hx = _rms(x_ref[...], g_ref[...]).astype(BF16)
    p = jnp.dot(hx, w_ref[:, :attn_cols], preferred_element_type=F32)
    q_ref[...] = p[:, :ATTN_WIDTH].astype(BF16)
    k_ref[...] = p[:, ATTN_WIDTH:ATTN_WIDTH + KV_DUP_WIDTH].astype(BF16)
    v_ref[...] = p[:, ATTN_WIDTH + KV_DUP_WIDTH:].astype(BF16)

    sc_b = r_ref[:, 0:SC_WIDTH]
    sc_ext[c0:c0 + ts, :] = r_ref[:, SC_WIDTH:2 * SC_WIDTH] * r_ref[:, 2 * SC_WIDTH:3 * SC_WIDTH]
    rg_ext[c0:c0 + ts, :] = r_ref[:, 3 * SC_WIDTH:3 * SC_WIDTH + RG_WIDTH]
    rg_g = r_ref[:, 3 * SC_WIDTH + RG_WIDTH:]

    conv = scb_ref[...]
    for k in range(SC_CONV):
        off = c0 - (SC_CONV - 1) + k
        conv = conv + scw_ref[k:k + 1, :] * sc_ext[off:off + ts, :]
    conv_out = sc_b * conv

    rg_in = rgb_ref[...]
    for k in range(RG_CONV):
        off = c0 - (RG_CONV - 1) + k
        rg_in = rg_in + rgw_ref[k:k + 1, :] * rg_ext[off:off + ts, :]

    sc_ext[0:c0, :] = sc_ext[ts:ts + c0, :]
    rg_ext[0:c0, :] = rg_ext[ts:ts + c0, :]

    gates = jnp.dot(rg_in.astype(BF16), wgate_ref[...], preferred_element_type=F32) + bgate_ref[...]

    r_ref[...] = jnp.dot(hx, w_ref[:, attn_cols:], preferred_element_type=F32)

    r_gate = jax.nn.sigmoid(gates[:, :RG_WIDTH])
    i_gate = jax.nn.sigmoid(gates[:, RG_WIDTH:])
    neg_lam = -lam_ref[...]
    softplus = jnp.maximum(neg_lam, 0.0) + jnp.log1p(jnp.exp(-jnp.abs(neg_lam)))
    log_a = -RG_C * r_gate * softplus
    a = jnp.exp(log_a)
    u = jnp.sqrt(jnp.tanh(-log_a) * (1.0 + a * a)) * (i_gate * rg_in)

    s = 1
    while s < ts:
        u = a * _shift_rows(u, s, 0.0) + u
        a = a * _shift_rows(a, s, 1.0)
        s *= 2
    h = a * h_carry[...] + u
    h_carry[...] = h[ts - 1:ts, :]

    c = math.sqrt(2.0 / math.pi)
    gelu = 0.5 * rg_g * (1.0 + jnp.tanh(c * (rg_g + 0.044715 * (rg_g * rg_g * rg_g))))
    o_ref[:, 0:SC_WIDTH] = conv_out.astype(BF16)
    o_ref[:, SC_WIDTH:] = (h * gelu).astype(BF16)


def _in_proj_conv(x, g, w, sc_w, sc_b, rg_w, rg_b, w_gate, b_gate, lam, seq):
    t = x.shape[0]
    n = w.shape[1]
    tiles = t // TS
    assert n == ATTN_WIDTH + 2 * KV_DUP_WIDTH + REST_WIDTH
    proj = lambda width: pl.BlockSpec((TS, width), lambda i: (jnp.minimum(i, tiles - 1), 0))
    lagged = pl.BlockSpec((TS, SC_WIDTH + RG_WIDTH), lambda i: (jnp.maximum(i - 1, 0), 0))
    return pl.pallas_call(
        functools.partial(_in_proj_conv_kernel, tiles_per_seq=seq // TS),
        grid=(tiles + 1,),
        in_specs=[proj(D_MODEL), _const_spec((1, D_MODEL)), _const_spec((D_MODEL, n)),
                  _const_spec((SC_CONV, SC_WIDTH)), _const_spec((1, SC_WIDTH)),
                  _const_spec((RG_CONV, RG_WIDTH)), _const_spec((1, RG_WIDTH)),
                  _const_spec((RG_WIDTH, 2 * RG_WIDTH)), _const_spec((1, 2 * RG_WIDTH)),
                  _const_spec((1, RG_WIDTH))],
        out_specs=[proj(ATTN_WIDTH), proj(KV_DUP_WIDTH), proj(KV_DUP_WIDTH), lagged],
        out_shape=[jax.ShapeDtypeStruct((t, ATTN_WIDTH), BF16),
                   jax.ShapeDtypeStruct((t, KV_DUP_WIDTH), BF16),
                   jax.ShapeDtypeStruct((t, KV_DUP_WIDTH), BF16),
                   jax.ShapeDtypeStruct((t, SC_WIDTH + RG_WIDTH), BF16)],
        scratch_shapes=[pltpu.VMEM((TS, REST_WIDTH), F32),
                        pltpu.VMEM((TS + 2 * CARRY_ROWS, SC_WIDTH), F32),
                        pltpu.VMEM((TS + 2 * CARRY_ROWS, RG_WIDTH), F32),
                        pltpu.VMEM((1, RG_WIDTH), F32)],
        compiler_params=_params(("arbitrary",), 48),
        name="in_proj_conv_rglru",
    )(x, g, w, sc_w, sc_b, rg_w, rg_b, w_gate, b_gate, lam)


def _mem_kv_kernel(m_ref, g_ref, w_ref, k_ref, v_ref):
    h = _rms(m_ref[...], g_ref[...]).astype(BF16)
    p = jnp.dot(h, w_ref[...], preferred_element_type=F32)
    k_ref[...] = p[:, :XA_WIDTH].astype(BF16)
    v_ref[...] = p[:, XA_WIDTH:].astype(BF16)


def _mem_kv(mem, g, wkv):
    t = mem.shape[0]
    row = lambda width: pl.BlockSpec((MEM_LEN, width), lambda i: (i, 0))
    return pl.pallas_call(
        _mem_kv_kernel,
        grid=(t // MEM_LEN,),
        in_specs=[row(D_MODEL), _const_spec((1, D_MODEL)), _const_spec((D_MODEL, 2 * XA_WIDTH))],
        out_specs=[row(XA_WIDTH), row(XA_WIDTH)],
        out_shape=[jax.ShapeDtypeStruct((t, XA_WIDTH), BF16)] * 2,
        compiler_params=_params(("parallel",), 24),
        name="mem_kv",
    )(mem, g, wkv)


def _post_mixer_kernel(x_ref, a_ref, c_ref, wout_ref, g_ref, wq_ref, k_ref, v_ref, wo_ref, *rest):
    o_ref = rest[-1] if len(rest) == 1 else rest[4]
    halves = [slice(i * (TM // 2), (i + 1) * (TM // 2)) for i in range(2)]
    heads = [slice(hd * XA_HEAD_DIM, (hd + 1) * XA_HEAD_DIM) for hd in range(XA_HEADS)]
    nt = (((1,), (1,)), ((), ()))
    k = k_ref[...]
    v = v_ref[...]

    x1 = [x_ref[hs, :] + jnp.dot(jnp.concatenate([a_ref[hs, :], c_ref[hs, :]], axis=-1), wout_ref[...],
                                 preferred_element_type=F32) for hs in halves]
    q = [jnp.dot(_rms(xh, g_ref[...]).astype(BF16), wq_ref[...], preferred_element_type=F32).astype(BF16)
         for xh in x1]
    scores = [[lax.dot_general(qh[:, sl], k[:, sl], nt, preferred_element_type=F32) for sl in heads] for qh in q]
    x2 = []
    for xh, per_head in zip(x1, scores):
        probs, sums = [], []
        for s in per_head:
            s = s * (1.0 / math.sqrt(XA_HEAD_DIM))
            p = jnp.exp(s - jnp.max(s, axis=-1, keepdims=True))
            probs.append(p.astype(BF16))
            sums.append(jnp.sum(p, axis=-1, keepdims=True))
        att = jnp.concatenate([jnp.dot(p, v[:, sl], preferred_element_type=F32) / l
                               for p, sl, l in zip(probs, heads, sums)], axis=-1).astype(BF16)
        x2.append(xh + jnp.dot(att, wo_ref[...], preferred_element_type=F32))

    for hs, xh in zip(halves, x2):
        o_ref[hs, :] = xh
    if len(rest) > 1:
        ffn_g_ref, wr_ref, tri_ref, sel_ref, _, hf_ref, meta_ref, metat_ref, cnt_ref, carry = rest
        h = _rms(jnp.concatenate(x2, axis=0), ffn_g_ref[...])
        _route(h, wr_ref, tri_ref, sel_ref, hf_ref, meta_ref, metat_ref, cnt_ref, carry)


def _post_mixer(x, attn, cr, w_out, g, wq, k, v, wo, seq, route=None):
    t = x.shape[0]
    per_seq = seq // TM
    row = lambda width: pl.BlockSpec((TM, width), lambda i: (i, 0))
    mem_blk = pl.BlockSpec((MEM_LEN, XA_WIDTH), lambda i: (i // per_seq, 0))
    in_specs = [row(D_MODEL), row(ATTN_WIDTH), row(SC_WIDTH + RG_WIDTH), _const_spec((D_MODEL, D_MODEL)),
                _const_spec((1, D_MODEL)), _const_spec((D_MODEL, XA_WIDTH)), mem_blk, mem_blk,
                _const_spec((XA_WIDTH, D_MODEL))]
    out_specs = [row(D_MODEL)]
    out_shape = [jax.ShapeDtypeStruct((t, D_MODEL), F32)]
    scratch = []
    args = [x, attn, cr, w_out, g, wq, k, v, wo]
    if route is not None:
        assert TM % ROUTE_CHUNK == 0
        in_specs += [_const_spec((1, D_MODEL)), _const_spec((D_MODEL, LANES_V7X)),
                     _const_spec((ROUTE_CHUNK, ROUTE_CHUNK)), _const_spec((SUBLANES_V7X, LANES_V7X))]
        out_specs += [pl.BlockSpec((TM * ROW_CHUNKS, LANES_V7X), lambda i: (i, 0)), row(LANES_V7X),
                      pl.BlockSpec((SUBLANES_V7X, TM), lambda i: (0, i)),
                      pl.BlockSpec((1, LANES_V7X), lambda i: (0, 0))]
        out_shape += [jax.ShapeDtypeStruct((t * ROW_CHUNKS, LANES_V7X), F32),
                      jax.ShapeDtypeStruct((t, LANES_V7X), F32),
                      jax.ShapeDtypeStruct((SUBLANES_V7X, t), F32),
                      jax.ShapeDtypeStruct((1, LANES_V7X), F32)]
        scratch = [pltpu.VMEM((1, LANES_V7X), F32)]
        args += list(route)
    return pl.pallas_call(
        _post_mixer_kernel,
        grid=(t // TM,),
        in_specs=in_specs,
        out_specs=out_specs,
        out_shape=out_shape,
        scratch_shapes=scratch,
        compiler_params=_params(("arbitrary",), 40),
        name="post_mixer_route" if route is not None else "post_mixer",
    )(*args)


def _ffn_kernel(x_ref, g_ref, wg_ref, wu_ref, wd_ref, *rest):
    n_cast = (len(rest) - 1) // 2
    o_ref = rest[n_cast]
    x = x_ref[...]
    h = _rms(x, g_ref[...]).astype(BF16)
    gate = jnp.dot(h, wg_ref[...], preferred_element_type=F32)
    up = jnp.dot(h, wu_ref[...], preferred_element_type=F32)
    act = (gate * jax.nn.sigmoid(gate) * up).astype(BF16)
    o_ref[...] = x + jnp.dot(act, wd_ref[...], preferred_element_type=F32)
    for src, dst in zip(rest[:n_cast], rest[n_cast + 1:]):
        dst[...] = src[...].astype(BF16)


def _dense_ffn(x, g, wg, wu, wd, cast_along=()):
    t = x.shape[0]
    steps = t // TM
    d_ff = wg.shape[1]
    row = pl.BlockSpec((TM, D_MODEL), lambda i: (i, 0))
    flat = [w.reshape(-1, w.shape[-1]) for w in cast_along]
    for w in flat:
        assert w.shape[0] % (steps * 2 * SUBLANES_V7X) == 0
    slabs = [pl.BlockSpec((w.shape[0] // steps, w.shape[1]), lambda i: (i, 0)) for w in flat]
    outs = pl.pallas_call(
        _ffn_kernel,
        grid=(steps,),
        in_specs=[row, _const_spec((1, D_MODEL)), _const_spec((D_MODEL, d_ff)), _const_spec((D_MODEL, d_ff)),
                  _const_spec((d_ff, D_MODEL))] + slabs,
        out_specs=[row] + slabs,
        out_shape=[jax.ShapeDtypeStruct((t, D_MODEL), F32)] + [jax.ShapeDtypeStruct(w.shape, BF16) for w in flat],
        compiler_params=_params(("parallel",), 60),
        name="dense_swiglu",
    )(x, g, wg, wu, wd, *flat)
    return outs[0], [o.reshape(w.shape) for o, w in zip(outs[1:], cast_along)]


META_E1, META_E2, META_R1, META_R2, META_W1, META_W2 = range(6)


def _to_token_tiles(ref, rows):
    m = rows.shape[0]
    for c in range(ROW_CHUNKS):
        ref[pl.ds(c, m, stride=ROW_CHUNKS), :] = rows[:, c * LANES_V7X:(c + 1) * LANES_V7X]


def _from_token_tiles(ref):
    m = ref.shape[0] // ROW_CHUNKS
    return jnp.concatenate([ref[pl.ds(c, m, stride=ROW_CHUNKS), :] for c in range(ROW_CHUNKS)], axis=-1)


def _route(h, wr_ref, tri_ref, sel_ref, hf_ref, meta_ref, metat_ref, cnt_ref, carry):
    @pl.when(pl.program_id(0) == 0)
    def _():
        carry[...] = jnp.zeros_like(carry)

    _to_token_tiles(hf_ref, h)
    chunks = [slice(c * ROUTE_CHUNK, (c + 1) * ROUTE_CHUNK) for c in range(h.shape[0] // ROUTE_CHUNK)]
    each = lambda fn, *lists: [fn(*vals) for vals in zip(*lists)]
    rowmax = lambda a: jnp.max(a, axis=-1, keepdims=True)
    rowsum = lambda a: jnp.sum(a, axis=-1, keepdims=True)
    lane = lax.broadcasted_iota(jnp.int32, (ROUTE_CHUNK, LANES_V7X), 1)
    first_hit = lambda lg, m: jnp.min(jnp.where(lg == m, lane, LANES_V7X), axis=-1, keepdims=True)

    hb = h.astype(BF16)
    logits = [jnp.dot(hb[c, :], wr_ref[...], preferred_element_type=F32) for c in chunks]
    lg = each(lambda l: jnp.where(lane < N_EXPERTS, l, -jnp.inf), logits)
    m1 = each(rowmax, lg)
    e1 = each(first_hit, lg, m1)
    lg2 = each(lambda l, e: jnp.where(lane == e, -jnp.inf, l), lg, e1)
    m2 = each(rowmax, lg2)
    e2 = each(first_hit, lg2, m2)
    ex = each(lambda a, b: jnp.exp(b - a), m1, m2)
    w1 = each(lambda e: 1.0 / (1.0 + e), ex)
    w2 = each(lambda e: e / (1.0 + e), ex)
    hit1 = each(lambda e: lane == e, e1)
    hit2 = each(lambda e: lane == e, e2)
    onehot = each(lambda a, b: (a | b).astype(BF16), hit1, hit2)
    within = each(lambda o: jnp.dot(tri_ref[...], o, preferred_element_type=F32), onehot)
    totals = each(lambda o: jnp.sum(o.astype(F32), axis=0, keepdims=True), onehot)
    ahead, base = [], carry[...]
    for w, tot in zip(within, totals):
        ahead.append(w + base)
        base = base + tot
    carry[...] = base
    cnt_ref[...] = base
    r1 = each(lambda hit, a: rowsum(jnp.where(hit, a, 0.0)), hit1, ahead)
    r2 = each(lambda hit, a: rowsum(jnp.where(hit, a, 0.0)), hit2, ahead)

    def record(*vals):
        meta = jnp.zeros((ROUTE_CHUNK, LANES_V7X), F32)
        for col, val in zip((META_E1, META_E2, META_R1, META_R2, META_W1, META_W2), vals):
            meta = jnp.where(lane == col, val.astype(F32), meta)
        return meta

    meta = each(record, e1, e2, r1, r2, w1, w2)
    metat = each(lambda mt: lax.dot_general(sel_ref[...], mt, (((1,), (1,)), ((), ())), preferred_element_type=F32,
                                            precision=lax.Precision.HIGHEST), meta)
    for c, mt, mtt in zip(chunks, meta, metat):
        meta_ref[c, :] = mt
        metat_ref[:, c] = mtt


def _token_rows(ref, first_token, n_tokens):
    start = pl.multiple_of(first_token * ROW_CHUNKS, ROW_CHUNKS)
    return ref.at[pl.ds(start, n_tokens * ROW_CHUNKS)]


def _row_copy(src_ref, src_token, dst_ref, dst_token, sem):
    return pltpu.make_async_copy(_token_rows(src_ref, src_token, 1), _token_rows(dst_ref, dst_token, 1), sem)


def _dispatch_kernel(fill_ref, pos1_ref, pos2_ref, hf_ref, xs_ref, zeros, sem_z, sem):
    @pl.when(pl.program_id(0) == 0)
    def _():
        zeros[...] = jnp.zeros_like(zeros)

        def tile_fill(e):
            return pltpu.make_async_copy(zeros, _token_rows(xs_ref, pl.multiple_of(fill_ref[e], TM_GROUP), TM_GROUP),
                                         sem_z)

        for e in range(2 * N_EXPERTS):
            @pl.when(fill_ref[e] >= 0)
            def _():
                tile_fill(e).start()
        for e in range(2 * N_EXPERTS):
            @pl.when(fill_ref[e] >= 0)
            def _():
                tile_fill(e).wait()

    def issue(r, carry):
        _row_copy(hf_ref, r, xs_ref, pos1_ref[r], sem).start(priority=0)
        _row_copy(hf_ref, r, xs_ref, pos2_ref[r], sem).start(priority=1)
        return carry

    lax.fori_loop(0, TM_MOVE, issue, 0, unroll=ISSUE_UNROLL)
    whole_step = pltpu.make_async_copy(hf_ref, _token_rows(xs_ref, 0, TM_MOVE), sem)
    whole_step.wait()
    whole_step.wait()


def _dispatch(fill_start, pos1, pos2, hf, n_rows):
    t = hf.shape[0] // ROW_CHUNKS
    idx = pl.BlockSpec((TM_MOVE,), lambda i, fill: (i,), memory_space=pltpu.SMEM)
    return pl.pallas_call(
        _dispatch_kernel,
        grid_spec=pltpu.PrefetchScalarGridSpec(
            num_scalar_prefetch=1,
            grid=(t // TM_MOVE,),
            in_specs=[idx, idx, pl.BlockSpec((TM_MOVE * ROW_CHUNKS, LANES_V7X), lambda i, fill: (i, 0))],
            out_specs=pl.BlockSpec(memory_space=pl.ANY),
            scratch_shapes=[pltpu.VMEM((TM_GROUP * ROW_CHUNKS, LANES_V7X), F32), pltpu.SemaphoreType.DMA(()),
                            pltpu.SemaphoreType.DMA(())]),
        out_shape=jax.ShapeDtypeStruct((n_rows * ROW_CHUNKS, LANES_V7X), F32),
        compiler_params=_params(("arbitrary",), 24),
        name="moe_dispatch",
    )(fill_start, pos1, pos2, hf)


def _grouped_kernel(te_ref, used_ref, x_ref, wg_ref, wu_ref, wd_ref, y_ref):
    i = pl.program_id(0)

    @pl.when(i < used_ref[0])
    def _():
        h = _from_token_tiles(x_ref).astype(BF16)
        ff = wg_ref.shape[1] // FF_SPLIT
        cols = [slice(c * ff, (c + 1) * ff) for c in range(FF_SPLIT)]
        gates = [jnp.dot(h, wg_ref[:, c], preferred_element_type=F32) for c in cols]
        ups = [jnp.dot(h, wu_ref[:, c], preferred_element_type=F32) for c in cols]
        total = None
        for c, gate, up in zip(cols, gates, ups):
            act = (gate * jax.nn.sigmoid(gate) * up).astype(BF16)
            part = jnp.dot(act, wd_ref[c, :], preferred_element_type=F32)
            total = part if total is None else total + part
        _to_token_tiles(y_ref, total)

    @pl.when(i >= used_ref[0])
    def _():
        y_ref[...] = jnp.zeros_like(y_ref)


def _grouped_swiglu(tile_expert, n_used, xs, wg, wu, wd):
    n_rows = xs.shape[0] // ROW_CHUNKS
    d_ff = wg.shape[2]
    tile = (TM_GROUP * ROW_CHUNKS, LANES_V7X)
    src = lambda i, te, used: (jnp.maximum(jnp.minimum(i, used[0] - 1), 0), 0)
    expert = lambda shape, buffers: pl.BlockSpec((None,) + shape, lambda i, te, used: (te[i], 0, 0),
                                                 pipeline_mode=pl.Buffered(buffers))
    return pl.pallas_call(
        _grouped_kernel,
        grid_spec=pltpu.PrefetchScalarGridSpec(
            num_scalar_prefetch=2,
            grid=(n_rows // TM_GROUP,),
            in_specs=[pl.BlockSpec(tile, src), expert((D_MODEL, d_ff), 1), expert((D_MODEL, d_ff), 2),
                      expert((d_ff, D_MODEL), 2)],
            out_specs=pl.BlockSpec(tile, lambda i, te, used: (i, 0))),
        out_shape=jax.ShapeDtypeStruct((n_rows * ROW_CHUNKS, LANES_V7X), F32),
        compiler_params=_params(("arbitrary",), 58),
        name="moe_grouped_swiglu",
    )(tile_expert, n_used, xs, wg, wu, wd)


def _combine_kernel(pos1_ref, pos2_ref, pos1_next_ref, pos2_next_ref, x_ref, meta_ref, g_ref, y_ref, o_ref,
                    y1, y2, sems):
    i = pl.program_id(0)

    def gather(p1_ref, p2_ref, slot):
        def issue(r, carry):
            _row_copy(y_ref, p1_ref[r], y1.at[slot], r, sems.at[slot]).start(priority=0)
            _row_copy(y_ref, p2_ref[r], y2.at[slot], r, sems.at[slot]).start(priority=1)
            return carry
        lax.fori_loop(0, TM_COMBINE, issue, 0, unroll=ISSUE_UNROLL)

    def finish(slot):
        for buf in (y1, y2):
            pltpu.make_async_copy(_token_rows(y_ref, 0, TM_COMBINE), buf.at[slot], sems.at[slot]).wait()
        w1 = meta_ref[:, META_W1:META_W1 + 1]
        w2 = meta_ref[:, META_W2:META_W2 + 1]
        out = x_ref[...] + (w1 * _from_token_tiles(y1.at[slot]) + w2 * _from_token_tiles(y2.at[slot]))
        o_ref[...] = _rms(out, g_ref[...])

    @pl.when(i == 0)
    def _():
        gather(pos1_ref, pos2_ref, 0)

    for slot in range(2):
        @pl.when(i % 2 == slot)
        def _():
            @pl.when(i + 1 < pl.num_programs(0))
            def _():
                gather(pos1_next_ref, pos2_next_ref, 1 - slot)
            finish(slot)


def _combine(pos1, pos2, x, meta, g, y):
    t = x.shape[0]
    steps = t // TM_COMBINE
    idx = pl.BlockSpec((TM_COMBINE,), lambda i: (i,), memory_space=pltpu.SMEM)
    idx_next = pl.BlockSpec((TM_COMBINE,), lambda i: (jnp.minimum(i + 1, steps - 1),), memory_space=pltpu.SMEM)
    row = lambda width: pl.BlockSpec((TM_COMBINE, width), lambda i: (i, 0))
    slots = pltpu.VMEM((2, TM_COMBINE * ROW_CHUNKS, LANES_V7X), F32)
    return pl.pallas_call(
        _combine_kernel,
        grid=(steps,),
        in_specs=[idx, idx, idx_next, idx_next, row(D_MODEL), row(LANES_V7X), _const_spec((1, D_MODEL)),
                  pl.BlockSpec(memory_space=pl.ANY)],
        out_specs=row(D_MODEL),
        out_shape=jax.ShapeDtypeStruct((t, D_MODEL), F32),
        scratch_shapes=[slots, slots, pltpu.SemaphoreType.DMA((2,))],
        compiler_params=_params(("arbitrary",), 24),
        name="moe_combine_norm",
    )(pos1, pos2, pos1, pos2, x, meta, g, y)


def _route_operands(ffn_g, router_w):
    wr_pad = jnp.zeros((D_MODEL, LANES_V7X), BF16).at[:, :N_EXPERTS].set(router_w.astype(BF16))
    tri = jnp.tril(jnp.ones((ROUTE_CHUNK, ROUTE_CHUNK), BF16), -1)
    sel = jnp.eye(SUBLANES_V7X, LANES_V7X, dtype=F32)
    return ffn_g, wr_pad, tri, sel


def _moe_layer(x, hf, meta, metat, counts, wg, wu, wd, final_g):
    t = x.shape[0]
    n_rows = 2 * t + N_EXPERTS * TM_GROUP

    cnt = counts[0, :N_EXPERTS].astype(jnp.int32)
    padded = (cnt + TM_GROUP - 1) // TM_GROUP * TM_GROUP
    ends = jnp.cumsum(padded)
    starts = ends - padded
    experts = jnp.arange(N_EXPERTS, dtype=jnp.int32)[:, None]
    group_start = lambda e: jnp.sum(jnp.where(e[None, :] == experts, starts[:, None], 0), axis=0)
    pos1 = group_start(metat[META_E1].astype(jnp.int32)) + metat[META_R1].astype(jnp.int32)
    pos2 = group_start(metat[META_E2].astype(jnp.int32)) + metat[META_R2].astype(jnp.int32)
    n_used = (ends[-1] // TM_GROUP).astype(jnp.int32).reshape(1)
    tile_row = jnp.minimum(jnp.arange(n_rows // TM_GROUP, dtype=jnp.int32), n_used[0] - 1) * TM_GROUP
    tile_expert = jnp.sum(ends[None, :] <= tile_row[:, None], axis=1).astype(jnp.int32)
    tail_start = jnp.where(padded > 0, ends - TM_GROUP, -1)
    spare = ends[-1] + TM_GROUP * jnp.arange(N_EXPERTS, dtype=jnp.int32)
    fill_start = jnp.concatenate([tail_start, jnp.where(spare < n_rows, spare, -1)]).astype(jnp.int32)

    xs = _dispatch(fill_start, pos1, pos2, hf, n_rows)
    y = _grouped_swiglu(tile_expert, n_used, xs, wg, wu, wd)
    return _combine(pos1, pos2, x, meta, final_g, y)


def _arrange_in_proj(w):
    q = (w[:, :ATTN_WIDTH] * (1.0 / math.sqrt(HEAD_DIM))).astype(BF16)
    w = w.astype(BF16)
    dup = lambda start: [w[:, start + h * HEAD_DIM:start + (h + 1) * HEAD_DIM]
                         for h in range(ATTN_KV_HEADS) for _ in range(2)]
    return jnp.concatenate([q] + dup(ATTN_WIDTH) + dup(ATTN_WIDTH + KV_WIDTH) + [w[:, ATTN_WIDTH + 2 * KV_WIDTH:]],
                           axis=1)


def _block_diag(w):
    heads, d, _ = w.shape
    eye = jnp.eye(heads, dtype=w.dtype)
    return jnp.einsum('hij,hg->higj', w, eye).reshape(heads * d, heads * d)


def kernel(x, mem, rel_bias, mix_norm, w_in, attn_sinks, sc_conv_w, sc_conv_b, rg_conv_w, rg_conv_b, rg_w_a,
           rg_b_a, rg_w_x, rg_b_x, rg_lambda, w_out, xa_norm, mem_norm, xa_wq, xa_wk, xa_wv, xa_wo, ffn_norm,
           dense_wg, dense_wu, dense_wd, moe_router, moe_wg, moe_wu, moe_wd, final_norm):
    batch, seq, _ = x.shape
    depth = w_in.shape[0]
    assert depth == 2 and seq % TS == 0 and seq % TM == 0
    xt = x.reshape(batch * seq, D_MODEL)
    memt = mem.reshape(batch * MEM_LEN, D_MODEL)
    bias_tbl = _attention_bias_tables(rel_bias)
    vec = lambda a: a.reshape(1, -1)

    for layer in range(depth):
        w_gate = jnp.concatenate([_block_diag(rg_w_a[layer]), _block_diag(rg_w_x[layer])], axis=1).astype(BF16)
        b_gate = jnp.concatenate([rg_b_a[layer], rg_b_x[layer]]).reshape(1, -1)
        q, k, v, cr = _in_proj_conv(xt, vec(mix_norm[layer]), _arrange_in_proj(w_in[layer]),
                                    sc_conv_w[layer], vec(sc_conv_b[layer]), rg_conv_w[layer],
                                    vec(rg_conv_b[layer]), w_gate, b_gate, vec(rg_lambda[layer]), seq)
        attn = _attention(q, k, v, attn_sinks[layer], bias_tbl, seq // BLOCK)
        wkv = jnp.concatenate([xa_wk[layer], xa_wv[layer]], axis=1).astype(BF16)
        mk, mv = _mem_kv(memt, vec(mem_norm[layer]), wkv)
        post = functools.partial(_post_mixer, xt, attn, cr, w_out[layer].astype(BF16), vec(xa_norm[layer]),
                                 xa_wq[layer].astype(BF16), mk, mv, xa_wo[layer].astype(BF16), seq)

        j = layer // 2
        if layer % 2 == 0:
            (xt,) = post()
            xt, expert_w = _dense_ffn(xt, vec(ffn_norm[layer]), dense_wg[j].astype(BF16), dense_wu[j].astype(BF16),
                                      dense_wd[j].astype(BF16), cast_along=(moe_wg[j], moe_wu[j], moe_wd[j]))
        else:
            xt, hf, meta, metat, counts = post(route=_route_operands(vec(ffn_norm[layer]), moe_router[j]))
            xt = _moe_layer(xt, hf, meta, metat, counts, *expert_w, vec(final_norm))
    return xt.reshape(batch, seq, D_MODEL)
```

```python
import functools
import math

import jax
import jax.numpy as jnp
import numpy as np
from jax import lax
from jax.experimental import pallas as pl
from jax.experimental.pallas import tpu as pltpu

F32 = jnp.float32
BF16 = jnp.bfloat16

D_MODEL = 1024
MEM_LEN = 256
HEAD_DIM = 64
ATTN_Q_HEADS = 8
ATTN_KV_HEADS = 2
ATTN_WIDTH = ATTN_Q_HEADS * HEAD_DIM
KV_WIDTH = ATTN_KV_HEADS * HEAD_DIM
KV_DUP_WIDTH = 2 * KV_WIDTH
BLOCK = 128
SC_WIDTH = 256
SC_CONV = 3
RG_WIDTH = 256
RG_HEADS = 4
RG_HEAD_DIM = RG_WIDTH // RG_HEADS
RG_CONV = 4
RG_C = 8.0
N_BUCKETS = 32
MAX_EXACT = N_BUCKETS // 2
MAX_DISTANCE = 128
XA_HEADS = 4
XA_HEAD_DIM = 128
XA_WIDTH = XA_HEADS * XA_HEAD_DIM
N_EXPERTS = 8
EPS = 1e-6
NEG_INF = -1e30
REST_WIDTH = 3 * SC_WIDTH + 2 * RG_WIDTH

LANES_V7X = 128
SUBLANES_V7X = 8
VMEM_BYTES_V7X = 64 * 1024 * 1024
ROW_CHUNKS = D_MODEL // LANES_V7X
assert ROW_CHUNKS == SUBLANES_V7X

TM = 512
ATTN_BLOCKS = 8
TS = 512
ROUTE_CHUNK = 128
TM_GROUP = 512
TM_MOVE = 1024
TM_COMBINE = 256
ISSUE_UNROLL = 8
FF_SPLIT = 2
CARRY_ROWS = SUBLANES_V7X


def _mib(n):
    return int(n * 1024 * 1024)


def _params(semantics, vmem_mib):
    assert _mib(vmem_mib) < VMEM_BYTES_V7X
    return pltpu.CompilerParams(dimension_semantics=semantics, vmem_limit_bytes=_mib(vmem_mib))


def _rms(x, g):
    ms = jnp.mean(x * x, axis=-1, keepdims=True)
    return x * lax.rsqrt(ms + EPS) * g


def _const_spec(shape):
    nd = len(shape)
    return pl.BlockSpec(shape, lambda *_: (0,) * nd, pipeline_mode=pl.Buffered(1))


def _attn_kernel(sink_ref, q_ref, kp_ref, kc_ref, vp_ref, vc_ref, bias0_ref, bias_ref, o_ref):
    pairs_per_group = ATTN_Q_HEADS // ATTN_KV_HEADS // 2
    row = lax.broadcasted_iota(jnp.int32, (BLOCK, BLOCK), 0)
    col = lax.broadcasted_iota(jnp.int32, (BLOCK, BLOCK), 1)
    from_prev = col > row
    low_lanes = lax.broadcasted_iota(jnp.int32, (2 * BLOCK, 2 * HEAD_DIM), 1) < HEAD_DIM
    low_out = col < HEAD_DIM
    zero = jnp.zeros((), BF16)

    def block_diag(band):
        return jnp.concatenate([jnp.where(low_lanes, band, zero), jnp.where(low_lanes, zero, band)], axis=0)

    def scores(blk):
        rows = slice(blk * BLOCK, (blk + 1) * BLOCK)
        prev_rows = slice((blk - 1) * BLOCK, blk * BLOCK)
        out = []
        for g in range(ATTN_KV_HEADS):
            lanes = slice(g * 2 * HEAD_DIM, (g + 1) * 2 * HEAD_DIM)
            k_prev = kp_ref[:, lanes] if blk == 0 else kc_ref[prev_rows, lanes]
            v_prev = vp_ref[:, lanes] if blk == 0 else vc_ref[prev_rows, lanes]
            k_bd = block_diag(jnp.concatenate([k_prev, kc_ref[rows, lanes]], axis=0))
            v_bd = block_diag(jnp.concatenate([v_prev, vc_ref[rows, lanes]], axis=0))
            for pair in range(pairs_per_group):
                slab = g * pairs_per_group + pair
                q2 = q_ref[rows, slab * 2 * HEAD_DIM:(slab + 1) * 2 * HEAD_DIM]
                s = lax.dot_general(q2, k_bd, (((1,), (1,)), ((), ())), preferred_element_type=F32)
                out.append((slab, s, v_bd))
        return out

    def finish(blk, scored):
        rows = slice(blk * BLOCK, (blk + 1) * BLOCK)
        tbl_ref = bias0_ref if blk == 0 else bias_ref
        staged = []
        for slab, s, v_bd in scored:
            probs, denoms = [], []
            for side in range(2):
                h = 2 * slab + side
                sh = s[:, side * 2 * BLOCK:(side + 1) * 2 * BLOCK]
                logits = jnp.where(from_prev, sh[:, :BLOCK], sh[:, BLOCK:]) + tbl_ref[h]
                sink = sink_ref[h]
                m = jnp.maximum(jnp.max(logits, axis=-1, keepdims=True), sink)
                p = jnp.exp(logits - m)
                denoms.append(jnp.sum(p, axis=-1, keepdims=True) + jnp.exp(sink - m))
                probs += [jnp.where(from_prev, p, 0.0), jnp.where(from_prev, 0.0, p)]
            staged.append((slab, jnp.concatenate(probs, axis=-1).astype(BF16), v_bd, denoms))
        for slab, p_band, v_bd, denoms in staged:
            o = jnp.dot(p_band, v_bd, preferred_element_type=F32)
            o = o / jnp.where(low_out, denoms[0], denoms[1])
            o_ref[rows, slab * 2 * HEAD_DIM:(slab + 1) * 2 * HEAD_DIM] = o.astype(BF16)

    pending = scores(0)
    for blk in range(ATTN_BLOCKS):
        upcoming = scores(blk + 1) if blk + 1 < ATTN_BLOCKS else None
        finish(blk, pending)
        pending = upcoming


def _attention(q, k, v, sinks, bias_tbl, blocks_per_seq):
    t = q.shape[0]
    tile = ATTN_BLOCKS * BLOCK
    cur = lambda i: (i, 0)
    prev = lambda i: (jnp.maximum(i * ATTN_BLOCKS - 1, 0), 0)
    tbl = (None, ATTN_Q_HEADS, BLOCK, BLOCK)
    return pl.pallas_call(
        _attn_kernel,
        grid=(t // tile,),
        in_specs=[pl.BlockSpec(memory_space=pltpu.SMEM),
                  pl.BlockSpec((tile, ATTN_WIDTH), cur),
                  pl.BlockSpec((BLOCK, KV_DUP_WIDTH), prev),
                  pl.BlockSpec((tile, KV_DUP_WIDTH), cur),
                  pl.BlockSpec((BLOCK, KV_DUP_WIDTH), prev),
                  pl.BlockSpec((tile, KV_DUP_WIDTH), cur),
                  pl.BlockSpec(tbl, lambda i: (jnp.minimum((i * ATTN_BLOCKS) % blocks_per_seq, 1), 0, 0, 0)),
                  pl.BlockSpec(tbl, lambda i: (1, 0, 0, 0))],
        out_specs=pl.BlockSpec((tile, ATTN_WIDTH), cur),
        out_shape=jax.ShapeDtypeStruct((t, ATTN_WIDTH), BF16),
        compiler_params=_params(("parallel",), 24),
        name="swa_attention",
    )(sinks, q, k, k, v, v, bias_tbl, bias_tbl)


def _bias_table_kernel(rel_ref, bucket_ref, o_ref):
    for v in range(2):
        bucket = bucket_ref[v]
        hits = [bucket == b for b in range(N_BUCKETS)]
        for h in range(ATTN_Q_HEADS):
            tbl = jnp.full(bucket.shape, NEG_INF, F32)
            for b in range(N_BUCKETS):
                tbl = jnp.where(hits[b], rel_ref[b * ATTN_Q_HEADS + h], tbl)
            o_ref[v, h] = tbl


def _attention_bias_tables(rel_bias):
    q_idx = np.arange(BLOCK)[:, None]
    j_idx = np.arange(BLOCK)[None, :]
    from_prev = j_idx > q_idx
    n = np.where(from_prev, q_idx + BLOCK - j_idx, q_idx - j_idx)
    large = MAX_EXACT + (np.log(np.maximum(n, 1).astype(np.float32) / np.float32(MAX_EXACT))
                         / np.float32(math.log(MAX_DISTANCE / MAX_EXACT))
                         * np.float32(N_BUCKETS - MAX_EXACT)).astype(np.int32)
    bucket = np.where(n < MAX_EXACT, n, np.minimum(large, N_BUCKETS - 1))
    first = np.where(from_prev, -1, bucket)
    buckets = jnp.asarray(np.stack([first, bucket]).astype(np.int32))
    return pl.pallas_call(
        _bias_table_kernel,
        in_specs=[pl.BlockSpec(memory_space=pltpu.SMEM), pl.BlockSpec(memory_space=pltpu.VMEM)],
        out_specs=pl.BlockSpec(memory_space=pltpu.VMEM),
        out_shape=jax.ShapeDtypeStruct((2, ATTN_Q_HEADS, BLOCK, BLOCK), F32),
        name="t5_bias_table",
    )(rel_bias.astype(F32).reshape(-1), buckets)


def _shift_rows(x, s, fill):
    return jnp.concatenate([jnp.full((s, x.shape[1]), fill, x.dtype), x[:x.shape[0] - s]], axis=0)


def _in_proj_conv_kernel(x_ref, g_ref, w_ref, scw_ref, scb_ref, rgw_ref, rgb_ref, wgate_ref, bgate_ref, lam_ref,
                         q_ref, k_ref, v_ref, o_ref, r_ref, sc_ext, rg_ext, h_carry, *, tiles_per_seq):
    i = pl.program_id(0)
    ts = r_ref.shape[0]
    c0 = CARRY_ROWS

    @pl.when(i == 0)
    def _():
        r_ref[...] = jnp.zeros_like(r_ref)

    @pl.when((i == 0) | ((i + tiles_per_seq - 1) % tiles_per_seq == 0))
    def _():
        sc_ext[0:c0, :] = jnp.zeros((c0, SC_WIDTH), F32)
        rg_ext[0:c0, :] = jnp.zeros((c0, RG_WIDTH), F32)
        h_carry[...] = jnp.zeros_like(h_carry)

    attn_cols = ATTN_WIDTH + 2 * KV_DUP_WIDTH
    hx = _rms(x_ref[...], g_ref[...]).astype(BF16)
    p = jnp.dot(hx, w_ref[:, :attn_cols], preferred_element_type=F32)
    q_ref[...] = p[:, :ATTN_WIDTH].astype(BF16)
    k_ref[...] = p[:, ATTN_WIDTH:ATTN_WIDTH + KV_DUP_WIDTH].astype(BF16)
    v_ref[...] = p[:, ATTN_WIDTH + KV_DUP_WIDTH:].astype(BF16)

    sc_b = r_ref[:, 0:SC_WIDTH]
    sc_ext[c0:c0 + ts, :] = r_ref[:, SC_WIDTH:2 * SC_WIDTH] * r_ref[:, 2 * SC_WIDTH:3 * SC_WIDTH]
    rg_ext[c0:c0 + ts, :] = r_ref[:, 3 * SC_WIDTH:3 * SC_WIDTH + RG_WIDTH]
    rg_g = r_ref[:, 3 * SC_WIDTH + RG_WIDTH:]

    conv = scb_ref[...]
    for k in range(SC_CONV):
        off = c0 - (SC_CONV - 1) + k
        conv = conv + scw_ref[k:k + 1, :] * sc_ext[off:off + ts, :]
    conv_out = sc_b * conv

    rg_in = rgb_ref[...]
    for k in range(RG_CONV):
        off = c0 - (RG_CONV - 1) + k
        rg_in = rg_in + rgw_ref[k:k + 1, :] * rg_ext[off:off + ts, :]

    sc_ext[0:c0, :] = sc_ext[ts:ts + c0, :]
    rg_ext[0:c0, :] = rg_ext[ts:ts + c0, :]

    gates = jnp.dot(rg_in.astype(BF16), wgate_ref[...], preferred_element_type=F32) + bgate_ref[...]

    r_ref[...] = jnp.dot(hx, w_ref[:, attn_cols:], preferred_element_type=F32)

    r_gate = jax.nn.sigmoid(gates[:, :RG_WIDTH])
    i_gate = jax.nn.sigmoid(gates[:, RG_WIDTH:])
    neg_lam = -lam_ref[...]
    softplus = jnp.maximum(neg_lam, 0.0) + jnp.log1p(jnp.exp(-jnp.abs(neg_lam)))
    log_a = -RG_C * r_gate * softplus
    a = jnp.exp(log_a)
    u = jnp.sqrt(jnp.tanh(-log_a) * (1.0 + a * a)) * (i_gate * rg_in)

    s = 1
    while s < ts:
        u = a * _shift_rows(u, s, 0.0) + u
        a = a * _shift_rows(a, s, 1.0)
        s *= 2
    h = a * h_carry[...] + u
    h_carry[...] = h[ts - 1:ts, :]

    c = math.sqrt(2.0 / math.pi)
    gelu = 0.5 * rg_g * (1.0 + jnp.tanh(c * (rg_g + 0.044715 * (rg_g * rg_g * rg_g))))
    o_ref[:, 0:SC_WIDTH] = conv_out.astype(BF16)
    o_ref[:, SC_WIDTH:] = (h * gelu).astype(BF16)


def _in_proj_conv(x, g, w, sc_w, sc_b, rg_w, rg_b, w_gate, b_gate, lam, seq):
    t = x.shape[0]
    n = w.shape[1]
    tiles = t // TS
    assert n == ATTN_WIDTH + 2 * KV_DUP_WIDTH + REST_WIDTH
    proj = lambda width: pl.BlockSpec((TS, width), lambda i: (jnp.minimum(i, tiles - 1), 0))
    lagged = pl.BlockSpec((TS, SC_WIDTH + RG_WIDTH), lambda i: (jnp.maximum(i - 1, 0), 0))
    return pl.pallas_call(
        functools.partial(_in_proj_conv_kernel, tiles_per_seq=seq // TS),
        grid=(tiles + 1,),
        in_specs=[proj(D_MODEL), _const_spec((1, D_MODEL)), _const_spec((D_MODEL, n)),
                  _const_spec((SC_CONV, SC_WIDTH)), _const_spec((1, SC_WIDTH)),
                  _const_spec((RG_CONV, RG_WIDTH)), _const_spec((1, RG_WIDTH)),
                  _const_spec((RG_WIDTH, 2 * RG_WIDTH)), _const_spec((1, 2 * RG_WIDTH)),
                  _const_spec((1, RG_WIDTH))],
        out_specs=[proj(ATTN_WIDTH), proj(KV_DUP_WIDTH), proj(KV_DUP_WIDTH), lagged],
        out_shape=[jax.ShapeDtypeStruct((t, ATTN_WIDTH), BF16),
                   jax.ShapeDtypeStruct((t, KV_DUP_WIDTH), BF16),
                   jax.ShapeDtypeStruct((t, KV_DUP_WIDTH), BF16),
                   jax.ShapeDtypeStruct((t, SC_WIDTH + RG_WIDTH), BF16)],
        scratch_shapes=[pltpu.VMEM((TS, REST_WIDTH), F32),
                        pltpu.VMEM((TS + 2 * CARRY_ROWS, SC_WIDTH), F32),
                        pltpu.VMEM((TS + 2 * CARRY_ROWS, RG_WIDTH), F32),
                        pltpu.VMEM((1, RG_WIDTH), F32)],
        compiler_params=_params(("arbitrary",), 48),
        name="in_proj_conv_rglru",
    )(x, g, w, sc_w, sc_b, rg_w, rg_b, w_gate, b_gate, lam)


def _mem_kv_kernel(m_ref, g_ref, w_ref, k_ref, v_ref):
    h = _rms(m_ref[...], g_ref[...]).astype(BF16)
    p = jnp.dot(h, w_ref[...], preferred_element_type=F32)
    k_ref[...] = p[:, :XA_WIDTH].astype(BF16)
    v_ref[...] = p[:, XA_WIDTH:].astype(BF16)


def _mem_kv(mem, g, wkv):
    t = mem.shape[0]
    row = lambda width: pl.BlockSpec((MEM_LEN, width), lambda i: (i, 0))
    return pl.pallas_call(
        _mem_kv_kernel,
        grid=(t // MEM_LEN,),
        in_specs=[row(D_MODEL), _const_spec((1, D_MODEL)), _const_spec((D_MODEL, 2 * XA_WIDTH))],
        out_specs=[row(XA_WIDTH), row(XA_WIDTH)],
        out_shape=[jax.ShapeDtypeStruct((t, XA_WIDTH), BF16)] * 2,
        compiler_params=_params(("parallel",), 24),
        name="mem_kv",
    )(mem, g, wkv)


def _post_mixer_kernel(x_ref, a_ref, c_ref, wout_ref, g_ref, wq_ref, k_ref, v_ref, wo_ref, *rest):
    o_ref = rest[-1] if len(rest) == 1 else rest[4]
    halves = [slice(i * (TM // 2), (i + 1) * (TM // 2)) for i in range(2)]
    heads = [slice(hd * XA_HEAD_DIM, (hd + 1) * XA_HEAD_DIM) for hd in range(XA_HEADS)]
    nt = (((1,), (1,)), ((), ()))
    k = k_ref[...]
    v = v_ref[...]

    x1 = [x_ref[hs, :] + jnp.dot(jnp.concatenate([a_ref[hs, :], c_ref[hs, :]], axis=-1), wout_ref[...],
                                 preferred_element_type=F32) for hs in halves]
    q = [jnp.dot(_rms(xh, g_ref[...]).astype(BF16), wq_ref[...], preferred_element_type=F32).astype(BF16)
         for xh in x1]
    scores = [[lax.dot_general(qh[:, sl], k[:, sl], nt, preferred_element_type=F32) for sl in heads] for qh in q]
    x2 = []
    for xh, per_head in zip(x1, scores):
        probs, sums = [], []
        for s in per_head:
            s = s * (1.0 / math.sqrt(XA_HEAD_DIM))
            p = jnp.exp(s - jnp.max(s, axis=-1, keepdims=True))
            probs.append(p.astype(BF16))
            sums.append(jnp.sum(p, axis=-1, keepdims=True))
        att = jnp.concatenate([jnp.dot(p, v[:, sl], preferred_element_type=F32) / l
                               for p, sl, l in zip(probs, heads, sums)], axis=-1).astype(BF16)
        x2.append(xh + jnp.dot(att, wo_ref[...], preferred_element_type=F32))

    for hs, xh in zip(halves, x2):
        o_ref[hs, :] = xh
    if len(rest) > 1:
        ffn_g_ref, wr_ref, tri_ref, sel_ref, _, hf_ref, meta_ref, metat_ref, cnt_ref, base_ref, carry = rest
        h = _rms(jnp.concatenate(x2, axis=0), ffn_g_ref[...])
        _route(h, wr_ref, tri_ref, sel_ref, hf_ref, meta_ref, metat_ref, cnt_ref, base_ref, carry)


def _post_mixer(x, attn, cr, w_out, g, wq, k, v, wo, seq, route=None):
    t = x.shape[0]
    per_seq = seq // TM
    row = lambda width: pl.BlockSpec((TM, width), lambda i: (i, 0))
    mem_blk = pl.BlockSpec((MEM_LEN, XA_WIDTH), lambda i: (i // per_seq, 0))
    in_specs = [row(D_MODEL), row(ATTN_WIDTH), row(SC_WIDTH + RG_WIDTH), _const_spec((D_MODEL, D_MODEL)),
                _const_spec((1, D_MODEL)), _const_spec((D_MODEL, XA_WIDTH)), mem_blk, mem_blk,
                _const_spec((XA_WIDTH, D_MODEL))]
    out_specs = [row(D_MODEL)]
    out_shape = [jax.ShapeDtypeStruct((t, D_MODEL), F32)]
    scratch = []
    args = [x, attn, cr, w_out, g, wq, k, v, wo]
    if route is not None:
        assert TM % ROUTE_CHUNK == 0
        in_specs += [_const_spec((1, D_MODEL)), _const_spec((D_MODEL, LANES_V7X)),
                     _const_spec((ROUTE_CHUNK, ROUTE_CHUNK)), _const_spec((SUBLANES_V7X, LANES_V7X))]
        out_specs += [row(D_MODEL), row(LANES_V7X),
                      pl.BlockSpec((SUBLANES_V7X, TM), lambda i: (0, i)),
                      pl.BlockSpec((1, LANES_V7X), lambda i: (0, 0)),
                      pl.BlockSpec((SUBLANES_V7X, LANES_V7X), lambda i: (i, 0))]
        out_shape += [jax.ShapeDtypeStruct((t, D_MODEL), BF16),
                      jax.ShapeDtypeStruct((t, LANES_V7X), F32),
                      jax.ShapeDtypeStruct((SUBLANES_V7X, t), F32),
                      jax.ShapeDtypeStruct((1, LANES_V7X), F32),
                      jax.ShapeDtypeStruct((t // TM * SUBLANES_V7X, LANES_V7X), F32)]
        scratch = [pltpu.VMEM((1, LANES_V7X), F32)]
        args += list(route)
    return pl.pallas_call(
        _post_mixer_kernel,
        grid=(t // TM,),
        in_specs=in_specs,
        out_specs=out_specs,
        out_shape=out_shape,
        scratch_shapes=scratch,
        compiler_params=_params(("arbitrary",), 40),
        name="post_mixer_route" if route is not None else "post_mixer",
    )(*args)


def _ffn_kernel(x_ref, g_ref, wg_ref, wu_ref, wd_ref, *rest):
    n_cast = (len(rest) - 1) // 2
    o_ref = rest[n_cast]
    x = x_ref[...]
    h = _rms(x, g_ref[...]).astype(BF16)
    gate = jnp.dot(h, wg_ref[...], preferred_element_type=F32)
    up = jnp.dot(h, wu_ref[...], preferred_element_type=F32)
    act = (gate * jax.nn.sigmoid(gate) * up).astype(BF16)
    o_ref[...] = x + jnp.dot(act, wd_ref[...], preferred_element_type=F32)
    for src, dst in zip(rest[:n_cast], rest[n_cast + 1:]):
        dst[...] = src[...].astype(BF16)


def _dense_ffn(x, g, wg, wu, wd, cast_along=()):
    t = x.shape[0]
    steps = t // TM
    d_ff = wg.shape[1]
    row = pl.BlockSpec((TM, D_MODEL), lambda i: (i, 0))
    flat = [w.reshape(-1, w.shape[-1]) for w in cast_along]
    for w in flat:
        assert w.shape[0] % (steps * 2 * SUBLANES_V7X) == 0
    slabs = [pl.BlockSpec((w.shape[0] // steps, w.shape[1]), lambda i: (i, 0)) for w in flat]
    outs = pl.pallas_call(
        _ffn_kernel,
        grid=(steps,),
        in_specs=[row, _const_spec((1, D_MODEL)), _const_spec((D_MODEL, d_ff)), _const_spec((D_MODEL, d_ff)),
                  _const_spec((d_ff, D_MODEL))] + slabs,
        out_specs=[row] + slabs,
        out_shape=[jax.ShapeDtypeStruct((t, D_MODEL), F32)] + [jax.ShapeDtypeStruct(w.shape, BF16) for w in flat],
        compiler_params=_params(("parallel",), 60),
        name="dense_swiglu",
    )(x, g, wg, wu, wd, *flat)
    return outs[0], [o.reshape(w.shape) for o, w in zip(outs[1:], cast_along)]


META_E1, META_E2, META_R1, META_R2, META_W1, META_W2 = range(6)


def _to_token_tiles(ref, rows):
    m = rows.shape[0]
    for c in range(ROW_CHUNKS):
        ref[pl.ds(c, m, stride=ROW_CHUNKS), :] = rows[:, c * LANES_V7X:(c + 1) * LANES_V7X]


def _from_token_tiles(ref):
    m = ref.shape[0] // ROW_CHUNKS
    return jnp.concatenate([ref[pl.ds(c, m, stride=ROW_CHUNKS), :] for c in range(ROW_CHUNKS)], axis=-1)


def _route(h, wr_ref, tri_ref, sel_ref, hf_ref, meta_ref, metat_ref, cnt_ref, base_ref, carry):
    @pl.when(pl.program_id(0) == 0)
    def _():
        carry[...] = jnp.zeros_like(carry)

    base_ref[...] = jnp.broadcast_to(carry[...], base_ref.shape)
    chunks = [slice(c * ROUTE_CHUNK, (c + 1) * ROUTE_CHUNK) for c in range(h.shape[0] // ROUTE_CHUNK)]
    each = lambda fn, *lists: [fn(*vals) for vals in zip(*lists)]
    rowmax = lambda a: jnp.max(a, axis=-1, keepdims=True)
    rowsum = lambda a: jnp.sum(a, axis=-1, keepdims=True)
    lane = lax.broadcasted_iota(jnp.int32, (ROUTE_CHUNK, LANES_V7X), 1)
    first_hit = lambda lg, m: jnp.min(jnp.where(lg == m, lane, LANES_V7X), axis=-1, keepdims=True)

    hb = h.astype(BF16)
    hf_ref[...] = hb
    logits = [jnp.dot(hb[c, :], wr_ref[...], preferred_element_type=F32) for c in chunks]
    lg = each(lambda l: jnp.where(lane < N_EXPERTS, l, -jnp.inf), logits)
    m1 = each(rowmax, lg)
    e1 = each(first_hit, lg, m1)
    lg2 = each(lambda l, e: jnp.where(lane == e, -jnp.inf, l), lg, e1)
    m2 = each(rowmax, lg2)
    e2 = each(first_hit, lg2, m2)
    ex = each(lambda a, b: jnp.exp(b - a), m1, m2)
    w1 = each(lambda e: 1.0 / (1.0 + e), ex)
    w2 = each(lambda e: e / (1.0 + e), ex)
    hit1 = each(lambda e: lane == e, e1)
    hit2 = each(lambda e: lane == e, e2)
    onehot = each(lambda a, b: (a | b).astype(BF16), hit1, hit2)
    within = each(lambda o: jnp.dot(tri_ref[...], o, preferred_element_type=F32), onehot)
    totals = each(lambda o: jnp.sum(o.astype(F32), axis=0, keepdims=True), onehot)
    ahead, base = [], carry[...]
    for w, tot in zip(within, totals):
        ahead.append(w + base)
        base = base + tot
    carry[...] = base
    cnt_ref[...] = base
    r1 = each(lambda hit, a: rowsum(jnp.where(hit, a, 0.0)), hit1, ahead)
    r2 = each(lambda hit, a: rowsum(jnp.where(hit, a, 0.0)), hit2, ahead)

    def record(*vals):
        meta = jnp.zeros((ROUTE_CHUNK, LANES_V7X), F32)
        for col, val in zip((META_E1, META_E2, META_R1, META_R2, META_W1, META_W2), vals):
            meta = jnp.where(lane == col, val.astype(F32), meta)
        return meta

    meta = each(record, e1, e2, r1, r2, w1, w2)
    metat = each(lambda mt: lax.dot_general(sel_ref[...], mt, (((1,), (1,)), ((), ())), preferred_element_type=F32,
                                            precision=lax.Precision.HIGHEST), meta)
    for c, mt, mtt in zip(chunks, meta, metat):
        meta_ref[c, :] = mt
        metat_ref[:, c] = mtt


def _token_rows(ref, first_token, n_tokens):
    start = pl.multiple_of(first_token * ROW_CHUNKS, ROW_CHUNKS)
    return ref.at[pl.ds(start, n_tokens * ROW_CHUNKS)]


def _row_copy(src_ref, src_token, dst_ref, dst_token, sem):
    return pltpu.make_async_copy(_token_rows(src_ref, src_token, 1), _token_rows(dst_ref, dst_token, 1), sem)


def _dispatch_kernel(fill_ref, n_ref, off_ref, dst_ref, loc1_ref, loc2_ref, hf_ref, xs_ref, zeros, buf, sem_z, sems):
    i = pl.program_id(0)
    steps = pl.num_programs(0)
    slot_tokens = buf.shape[1] // ROW_CHUNKS

    @pl.when(i == 0)
    def _():
        zeros[...] = jnp.zeros_like(zeros)

        def tile_fill(e):
            return pltpu.make_async_copy(zeros, _token_rows(xs_ref, pl.multiple_of(fill_ref[e], TM_GROUP), TM_GROUP),
                                         sem_z)

        for e in range(2 * N_EXPERTS):
            @pl.when(fill_ref[e] >= 0)
            def _():
                tile_fill(e).start()
        for e in range(2 * N_EXPERTS):
            @pl.when(fill_ref[e] >= 0)
            def _():
                tile_fill(e).wait()

    k = lax.broadcasted_iota(jnp.int32, (slot_tokens, hf_ref.shape[0]), 0)
    onehot = ((k == loc1_ref[...]) | (k == loc2_ref[...])).astype(BF16)
    sorted_rows = jnp.dot(onehot, hf_ref[...], preferred_element_type=F32)

    def run(slot):
        whole_slot = pltpu.make_async_copy(buf.at[slot], _token_rows(xs_ref, 0, slot_tokens), sems.at[slot])

        @pl.when(i >= 2)
        def _():
            whole_slot.wait()

        _to_token_tiles(buf.at[slot], sorted_rows)
        for e in range(N_EXPERTS):
            n = n_ref[i * N_EXPERTS + e]
            off = off_ref[i * N_EXPERTS + e]
            dst = dst_ref[i * N_EXPERTS + e]
            bit = hf_ref.shape[0]
            while bit >= 1:
                head = n & ~(2 * bit - 1)

                @pl.when((n & bit) != 0)
                def _(bit=bit, head=head):
                    pltpu.make_async_copy(_token_rows(buf.at[slot], off + head, bit),
                                          _token_rows(xs_ref, dst + head, bit), sems.at[slot]).start()
                bit //= 2

        @pl.when(i == steps - 1)
        def _():
            whole_slot.wait()
            other = pltpu.make_async_copy(buf.at[1 - slot], _token_rows(xs_ref, 0, slot_tokens), sems.at[1 - slot])

            @pl.when(steps >= 2)
            def _():
                other.wait()

    for slot in range(2):
        pl.when(i % 2 == slot)(functools.partial(run, slot))


def _dispatch(fill_start, n_tab, off_tab, dst_tab, loc1, loc2, hf, n_rows):
    t = hf.shape[0]
    loc = pl.BlockSpec((1, TM), lambda i, *_: (0, i))
    return pl.pallas_call(
        _dispatch_kernel,
        grid_spec=pltpu.PrefetchScalarGridSpec(
            num_scalar_prefetch=4,
            grid=(t // TM,),
            in_specs=[loc, loc, pl.BlockSpec((TM, D_MODEL), lambda i, *_: (i, 0))],
            out_specs=pl.BlockSpec(memory_space=pl.ANY),
            scratch_shapes=[pltpu.VMEM((TM_GROUP * ROW_CHUNKS, LANES_V7X), F32),
                            pltpu.VMEM((2, 2 * TM * ROW_CHUNKS, LANES_V7X), F32),
                            pltpu.SemaphoreType.DMA(()), pltpu.SemaphoreType.DMA((2,))]),
        out_shape=jax.ShapeDtypeStruct((n_rows * ROW_CHUNKS, LANES_V7X), F32),
        compiler_params=_params(("arbitrary",), 40),
        name="moe_dispatch",
    )(fill_start, n_tab, off_tab, dst_tab, loc1.reshape(1, t), loc2.reshape(1, t), hf)


def _grouped_kernel(te_ref, used_ref, x_ref, wg_ref, wu_ref, wd_ref, y_ref):
    i = pl.program_id(0)

    @pl.when(i < used_ref[0])
    def _():
        h = _from_token_tiles(x_ref).astype(BF16)
        ff = wg_ref.shape[1] // FF_SPLIT
        cols = [slice(c * ff, (c + 1) * ff) for c in range(FF_SPLIT)]
        gates = [jnp.dot(h, wg_ref[:, c], preferred_element_type=F32) for c in cols]
        ups = [jnp.dot(h, wu_ref[:, c], preferred_element_type=F32) for c in cols]
        total = None
        for c, gate, up in zip(cols, gates, ups):
            act = (gate * jax.nn.sigmoid(gate) * up).astype(BF16)
            part = jnp.dot(act, wd_ref[c, :], preferred_element_type=F32)
            total = part if total is None else total + part
        _to_token_tiles(y_ref, total)

    @pl.when(i >= used_ref[0])
    def _():
        y_ref[...] = jnp.zeros_like(y_ref)


def _grouped_swiglu(tile_expert, n_used, xs, wg, wu, wd):
    n_rows = xs.shape[0] // ROW_CHUNKS
    d_ff = wg.shape[2]
    tile = (TM_GROUP * ROW_CHUNKS, LANES_V7X)
    src = lambda i, te, used: (jnp.maximum(jnp.minimum(i, used[0] - 1), 0), 0)
    expert = lambda shape, buffers: pl.BlockSpec((None,) + shape, lambda i, te, used: (te[i], 0, 0),
                                                 pipeline_mode=pl.Buffered(buffers))
    return pl.pallas_call(
        _grouped_kernel,
        grid_spec=pltpu.PrefetchScalarGridSpec(
            num_scalar_prefetch=2,
            grid=(n_rows // TM_GROUP,),
            in_specs=[pl.BlockSpec(tile, src), expert((D_MODEL, d_ff), 1), expert((D_MODEL, d_ff), 2),
                      expert((d_ff, D_MODEL), 2)],
            out_specs=pl.BlockSpec(tile, lambda i, te, used: (i, 0))),
        out_shape=jax.ShapeDtypeStruct((n_rows * ROW_CHUNKS, LANES_V7X), F32),
        compiler_params=_params(("arbitrary",), 58),
        name="moe_grouped_swiglu",
    )(tile_expert, n_used, xs, wg, wu, wd)


def _combine_kernel(pos1_ref, pos2_ref, pos1_next_ref, pos2_next_ref, x_ref, meta_ref, g_ref, y_ref, o_ref,
                    y1, y2, sems):
    i = pl.program_id(0)

    def gather(p1_ref, p2_ref, slot):
        def issue(r, carry):
            _row_copy(y_ref, p1_ref[r], y1.at[slot], r, sems.at[slot]).start(priority=0)
            _row_copy(y_ref, p2_ref[r], y2.at[slot], r, sems.at[slot]).start(priority=1)
            return carry
        lax.fori_loop(0, TM_COMBINE, issue, 0, unroll=ISSUE_UNROLL)

    def finish(slot):
        for buf in (y1, y2):
            pltpu.make_async_copy(_token_rows(y_ref, 0, TM_COMBINE), buf.at[slot], sems.at[slot]).wait()
        w1 = meta_ref[:, META_W1:META_W1 + 1]
        w2 = meta_ref[:, META_W2:META_W2 + 1]
        out = x_ref[...] + (w1 * _from_token_tiles(y1.at[slot]) + w2 * _from_token_tiles(y2.at[slot]))
        o_ref[...] = _rms(out, g_ref[...])

    @pl.when(i == 0)
    def _():
        gather(pos1_ref, pos2_ref, 0)

    for slot in range(2):
        @pl.when(i % 2 == slot)
        def _():
            @pl.when(i + 1 < pl.num_programs(0))
            def _():
                gather(pos1_next_ref, pos2_next_ref, 1 - slot)
            finish(slot)


def _combine(pos1, pos2, x, meta, g, y):
    t = x.shape[0]
    steps = t // TM_COMBINE
    idx = pl.BlockSpec((TM_COMBINE,), lambda i: (i,), memory_space=pltpu.SMEM)
    idx_next = pl.BlockSpec((TM_COMBINE,), lambda i: (jnp.minimum(i + 1, steps - 1),), memory_space=pltpu.SMEM)
    row = lambda width: pl.BlockSpec((TM_COMBINE, width), lambda i: (i, 0))
    slots = pltpu.VMEM((2, TM_COMBINE * ROW_CHUNKS, LANES_V7X), F32)
    return pl.pallas_call(
        _combine_kernel,
        grid=(steps,),
        in_specs=[idx, idx, idx_next, idx_next, row(D_MODEL), row(LANES_V7X), _const_spec((1, D_MODEL)),
                  pl.BlockSpec(memory_space=pl.ANY)],
        out_specs=row(D_MODEL),
        out_shape=jax.ShapeDtypeStruct((t, D_MODEL), F32),
        scratch_shapes=[slots, slots, pltpu.SemaphoreType.DMA((2,))],
        compiler_params=_params(("arbitrary",), 24),
        name="moe_combine_norm",
    )(pos1, pos2, pos1, pos2, x, meta, g, y)


def _route_operands(ffn_g, router_w):
    wr_pad = jnp.zeros((D_MODEL, LANES_V7X), BF16).at[:, :N_EXPERTS].set(router_w.astype(BF16))
    tri = jnp.tril(jnp.ones((ROUTE_CHUNK, ROUTE_CHUNK), BF16), -1)
    sel = jnp.eye(SUBLANES_V7X, LANES_V7X, dtype=F32)
    return ffn_g, wr_pad, tri, sel


def _moe_layer(x, hf, meta, metat, counts, base, wg, wu, wd, final_g):
    t = x.shape[0]
    steps = t // TM
    n_rows = 2 * t + N_EXPERTS * TM_GROUP

    cnt = counts[0, :N_EXPERTS].astype(jnp.int32)
    padded = (cnt + TM_GROUP - 1) // TM_GROUP * TM_GROUP
    ends = jnp.cumsum(padded)
    starts = ends - padded
    experts = jnp.arange(N_EXPERTS, dtype=jnp.int32)[:, None]
    e1, e2 = metat[META_E1].astype(jnp.int32), metat[META_E2].astype(jnp.int32)
    r1, r2 = metat[META_R1].astype(jnp.int32), metat[META_R2].astype(jnp.int32)
    pick = lambda e, table: jnp.sum(jnp.where(e[None, :] == experts, table, 0), axis=0)
    pos1 = pick(e1, starts[:, None]) + r1
    pos2 = pick(e2, starts[:, None]) + r2
    before = base.reshape(steps, SUBLANES_V7X, LANES_V7X)[:, 0, :N_EXPERTS].astype(jnp.int32)
    n_tab = jnp.concatenate([before[1:], cnt[None, :]], axis=0) - before
    off_tab = jnp.cumsum(n_tab, axis=1) - n_tab
    dst_tab = starts[None, :] + before
    shift = jnp.repeat((off_tab - before).T, TM, axis=1)
    loc1 = pick(e1, shift) + r1
    loc2 = pick(e2, shift) + r2
    n_used = (ends[-1] // TM_GROUP).astype(jnp.int32).reshape(1)
    tile_row = jnp.minimum(jnp.arange(n_rows // TM_GROUP, dtype=jnp.int32), n_used[0] - 1) * TM_GROUP
    tile_expert = jnp.sum(ends[None, :] <= tile_row[:, None], axis=1).astype(jnp.int32)
    tail_start = jnp.where(padded > 0, ends - TM_GROUP, -1)
    spare = ends[-1] + TM_GROUP * jnp.arange(N_EXPERTS, dtype=jnp.int32)
    fill_start = jnp.concatenate([tail_start, jnp.where(spare < n_rows, spare, -1)]).astype(jnp.int32)

    flat = lambda a: a.reshape(-1).astype(jnp.int32)
    xs = _dispatch(fill_start, flat(n_tab), flat(off_tab), flat(dst_tab), loc1, loc2, hf, n_rows)
    y = _grouped_swiglu(tile_expert, n_used, xs, wg, wu, wd)
    return _combine(pos1, pos2, x, meta, final_g, y)


def _arrange_in_proj(w):
    q = (w[:, :ATTN_WIDTH] * (1.0 / math.sqrt(HEAD_DIM))).astype(BF16)
    w = w.astype(BF16)
    dup = lambda start: [w[:, start + h * HEAD_DIM:start + (h + 1) * HEAD_DIM]
                         for h in range(ATTN_KV_HEADS) for _ in range(2)]
    return jnp.concatenate([q] + dup(ATTN_WIDTH) + dup(ATTN_WIDTH + KV_WIDTH) + [w[:, ATTN_WIDTH + 2 * KV_WIDTH:]],
                           axis=1)


def _block_diag(w):
    heads, d, _ = w.shape
    eye = jnp.eye(heads, dtype=w.dtype)
    return jnp.einsum('hij,hg->higj', w, eye).reshape(heads * d, heads * d)


def kernel(x, mem, rel_bias, mix_norm, w_in, attn_sinks, sc_conv_w, sc_conv_b, rg_conv_w, rg_conv_b, rg_w_a,
           rg_b_a, rg_w_x, rg_b_x, rg_lambda, w_out, xa_norm, mem_norm, xa_wq, xa_wk, xa_wv, xa_wo, ffn_norm,
           dense_wg, dense_wu, dense_wd, moe_router, moe_wg, moe_wu, moe_wd, final_norm):
    batch, seq, _ = x.shape
    depth = w_in.shape[0]
    assert depth == 2 and seq % TS == 0 and seq % TM == 0
    xt = x.reshape(batch * seq, D_MODEL)
    memt = mem.reshape(batch * MEM_LEN, D_MODEL)
    bias_tbl = _attention_bias_tables(rel_bias)
    vec = lambda a: a.reshape(1, -1)

    for layer in range(depth):
        w_gate = jnp.concatenate([_block_diag(rg_w_a[layer]), _block_diag(rg_w_x[layer])], axis=1).astype(BF16)
        b_gate = jnp.concatenate([rg_b_a[layer], rg_b_x[layer]]).reshape(1, -1)
        q, k, v, cr = _in_proj_conv(xt, vec(mix_norm[layer]), _arrange_in_proj(w_in[layer]),
                                    sc_conv_w[layer], vec(sc_conv_b[layer]), rg_conv_w[layer],
                                    vec(rg_conv_b[layer]), w_gate, b_gate, vec(rg_lambda[layer]), seq)
        attn = _attention(q, k, v, attn_sinks[layer], bias_tbl, seq // BLOCK)
        wkv = jnp.concatenate([xa_wk[layer], xa_wv[layer]], axis=1).astype(BF16)
        mk, mv = _mem_kv(memt, vec(mem_norm[layer]), wkv)
        post = functools.partial(_post_mixer, xt, attn, cr, w_out[layer].astype(BF16), vec(xa_norm[layer]),
                                 xa_wq[layer].astype(BF16), mk, mv, xa_wo[layer].astype(BF16), seq)

        j = layer // 2
        if layer % 2 == 0:
            (xt,) = post()
            xt, expert_w = _dense_ffn(xt, vec(ffn_norm[layer]), dense_wg[j].astype(BF16), dense_wu[j].astype(BF16),
                                      dense_wd[j].astype(BF16), cast_along=(moe_wg[j], moe_wu[j], moe_wd[j]))
        else:
            xt, hf, meta, metat, counts, base = post(route=_route_operands(vec(ffn_norm[layer]), moe_router[j]))
            xt = _moe_layer(xt, hf, meta, metat, counts, base, *expert_w, vec(final_norm))
    return xt.reshape(batch, seq, D_MODEL)
```

```python
import functools
import math

import jax
import jax.numpy as jnp
import numpy as np
from jax import lax
from jax.experimental import pallas as pl
from jax.experimental.pallas import tpu as pltpu

F32 = jnp.float32
BF16 = jnp.bfloat16

D_MODEL = 1024
MEM_LEN = 256
HEAD_DIM = 64
ATTN_Q_HEADS = 8
ATTN_KV_HEADS = 2
ATTN_WIDTH = ATTN_Q_HEADS * HEAD_DIM
KV_WIDTH = ATTN_KV_HEADS * HEAD_DIM
KV_DUP_WIDTH = 2 * KV_WIDTH
BLOCK = 128
SC_WIDTH = 256
SC_CONV = 3
RG_WIDTH = 256
RG_HEADS = 4
RG_HEAD_DIM = RG_WIDTH // RG_HEADS
RG_CONV = 4
RG_C = 8.0
N_BUCKETS = 32
MAX_EXACT = N_BUCKETS // 2
MAX_DISTANCE = 128
XA_HEADS = 4
XA_HEAD_DIM = 128
XA_WIDTH = XA_HEADS * XA_HEAD_DIM
N_EXPERTS = 8
EPS = 1e-6
NEG_INF = -1e30
REST_WIDTH = 3 * SC_WIDTH + 2 * RG_WIDTH

LANES_V7X = 128
SUBLANES_V7X = 8
VMEM_BYTES_V7X = 64 * 1024 * 1024
ROW_CHUNKS = D_MODEL // LANES_V7X
assert ROW_CHUNKS == SUBLANES_V7X

TM = 512
ATTN_BLOCKS = 8
TS = 512
ROUTE_CHUNK = 128
TM_GROUP = 512
TM_MOVE = 1024
TM_COMBINE = 256
ISSUE_UNROLL = 8
FF_SPLIT = 2
CARRY_ROWS = SUBLANES_V7X


def _mib(n):
    return int(n * 1024 * 1024)


def _params(semantics, vmem_mib):
    assert _mib(vmem_mib) < VMEM_BYTES_V7X
    return pltpu.CompilerParams(dimension_semantics=semantics, vmem_limit_bytes=_mib(vmem_mib))


def _rms(x, g):
    ms = jnp.mean(x * x, axis=-1, keepdims=True)
    return x * lax.rsqrt(ms + EPS) * g


def _const_spec(shape):
    nd = len(shape)
    return pl.BlockSpec(shape, lambda *_: (0,) * nd, pipeline_mode=pl.Buffered(1))


def _attn_kernel(sink_ref, q_ref, kp_ref, kc_ref, vp_ref, vc_ref, bias0_ref, bias_ref, o_ref):
    pairs_per_group = ATTN_Q_HEADS // ATTN_KV_HEADS // 2
    row = lax.broadcasted_iota(jnp.int32, (BLOCK, BLOCK), 0)
    col = lax.broadcasted_iota(jnp.int32, (BLOCK, BLOCK), 1)
    from_prev = col > row
    low_lanes = lax.broadcasted_iota(jnp.int32, (2 * BLOCK, 2 * HEAD_DIM), 1) < HEAD_DIM
    low_out = col < HEAD_DIM
    zero = jnp.zeros((), BF16)

    def block_diag(band):
        return jnp.concatenate([jnp.where(low_lanes, band, zero), jnp.where(low_lanes, zero, band)], axis=0)

    def scores(blk):
        rows = slice(blk * BLOCK, (blk + 1) * BLOCK)
        prev_rows = slice((blk - 1) * BLOCK, blk * BLOCK)
        out = []
        for g in range(ATTN_KV_HEADS):
            lanes = slice(g * 2 * HEAD_DIM, (g + 1) * 2 * HEAD_DIM)
            k_prev = kp_ref[:, lanes] if blk == 0 else kc_ref[prev_rows, lanes]
            v_prev = vp_ref[:, lanes] if blk == 0 else vc_ref[prev_rows, lanes]
            k_bd = block_diag(jnp.concatenate([k_prev, kc_ref[rows, lanes]], axis=0))
            v_bd = block_diag(jnp.concatenate([v_prev, vc_ref[rows, lanes]], axis=0))
            for pair in range(pairs_per_group):
                slab = g * pairs_per_group + pair
                q2 = q_ref[rows, slab * 2 * HEAD_DIM:(slab + 1) * 2 * HEAD_DIM]
                s = lax.dot_general(q2, k_bd, (((1,), (1,)), ((), ())), preferred_element_type=F32)
                out.append((slab, s, v_bd))
        return out

    def finish(blk, scored):
        rows = slice(blk * BLOCK, (blk + 1) * BLOCK)
        tbl_ref = bias0_ref if blk == 0 else bias_ref
        staged = []
        for slab, s, v_bd in scored:
            probs, denoms = [], []
            for side in range(2):
                h = 2 * slab + side
                sh = s[:, side * 2 * BLOCK:(side + 1) * 2 * BLOCK]
                logits = jnp.where(from_prev, sh[:, :BLOCK], sh[:, BLOCK:]) + tbl_ref[h]
                sink = sink_ref[h]
                m = jnp.maximum(jnp.max(logits, axis=-1, keepdims=True), sink)
                p = jnp.exp(logits - m)
                denoms.append(jnp.sum(p, axis=-1, keepdims=True) + jnp.exp(sink - m))
                probs += [jnp.where(from_prev, p, 0.0), jnp.where(from_prev, 0.0, p)]
            staged.append((slab, jnp.concatenate(probs, axis=-1).astype(BF16), v_bd, denoms))
        for slab, p_band, v_bd, denoms in staged:
            o = jnp.dot(p_band, v_bd, preferred_element_type=F32)
            o = o / jnp.where(low_out, denoms[0], denoms[1])
            o_ref[rows, slab * 2 * HEAD_DIM:(slab + 1) * 2 * HEAD_DIM] = o.astype(BF16)

    pending = scores(0)
    for blk in range(ATTN_BLOCKS):
        upcoming = scores(blk + 1) if blk + 1 < ATTN_BLOCKS else None
        finish(blk, pending)
        pending = upcoming


def _attention(q, k, v, sinks, bias_tbl, blocks_per_seq):
    t = q.shape[0]
    tile = ATTN_BLOCKS * BLOCK
    cur = lambda i: (i, 0)
    prev = lambda i: (jnp.maximum(i * ATTN_BLOCKS - 1, 0), 0)
    tbl = (None, ATTN_Q_HEADS, BLOCK, BLOCK)
    return pl.pallas_call(
        _attn_kernel,
        grid=(t // tile,),
        in_specs=[pl.BlockSpec(memory_space=pltpu.SMEM),
                  pl.BlockSpec((tile, ATTN_WIDTH), cur),
                  pl.BlockSpec((BLOCK, KV_DUP_WIDTH), prev),
                  pl.BlockSpec((tile, KV_DUP_WIDTH), cur),
                  pl.BlockSpec((BLOCK, KV_DUP_WIDTH), prev),
                  pl.BlockSpec((tile, KV_DUP_WIDTH), cur),
                  pl.BlockSpec(tbl, lambda i: (jnp.minimum((i * ATTN_BLOCKS) % blocks_per_seq, 1), 0, 0, 0)),
                  pl.BlockSpec(tbl, lambda i: (1, 0, 0, 0))],
        out_specs=pl.BlockSpec((tile, ATTN_WIDTH), cur),
        out_shape=jax.ShapeDtypeStruct((t, ATTN_WIDTH), BF16),
        compiler_params=_params(("parallel",), 24),
        name="swa_attention",
    )(sinks, q, k, k, v, v, bias_tbl, bias_tbl)


def _bias_table_kernel(rel_ref, bucket_ref, o_ref):
    for v in range(2):
        bucket = bucket_ref[v]
        hits = [bucket == b for b in range(N_BUCKETS)]
        for h in range(ATTN_Q_HEADS):
            tbl = jnp.full(bucket.shape, NEG_INF, F32)
            for b in range(N_BUCKETS):
                tbl = jnp.where(hits[b], rel_ref[b * ATTN_Q_HEADS + h], tbl)
            o_ref[v, h] = tbl


def _attention_bias_tables(rel_bias):
    q_idx = np.arange(BLOCK)[:, None]
    j_idx = np.arange(BLOCK)[None, :]
    from_prev = j_idx > q_idx
    n = np.where(from_prev, q_idx + BLOCK - j_idx, q_idx - j_idx)
    large = MAX_EXACT + (np.log(np.maximum(n, 1).astype(np.float32) / np.float32(MAX_EXACT))
                         / np.float32(math.log(MAX_DISTANCE / MAX_EXACT))
                         * np.float32(N_BUCKETS - MAX_EXACT)).astype(np.int32)
    bucket = np.where(n < MAX_EXACT, n, np.minimum(large, N_BUCKETS - 1))
    first = np.where(from_prev, -1, bucket)
    buckets = jnp.asarray(np.stack([first, bucket]).astype(np.int32))
    return pl.pallas_call(
        _bias_table_kernel,
        in_specs=[pl.BlockSpec(memory_space=pltpu.SMEM), pl.BlockSpec(memory_space=pltpu.VMEM)],
        out_specs=pl.BlockSpec(memory_space=pltpu.VMEM),
        out_shape=jax.ShapeDtypeStruct((2, ATTN_Q_HEADS, BLOCK, BLOCK), F32),
        name="t5_bias_table",
    )(rel_bias.astype(F32).reshape(-1), buckets)


def _shift_rows(x, s, fill):
    return jnp.concatenate([jnp.full((s, x.shape[1]), fill, x.dtype), x[:x.shape[0] - s]], axis=0)


def _in_proj_conv_kernel(x_ref, g_ref, w_ref, scw_ref, scb_ref, rgw_ref, rgb_ref, wgate_ref, bgate_ref, lam_ref,
                         q_ref, k_ref, v_ref, o_ref, r_ref, sc_ext, rg_ext, h_carry, *, tiles_per_seq):
    i = pl.program_id(0)
    ts = r_ref.shape[0]
    c0 = CARRY_ROWS

    @pl.when(i == 0)
    def _():
        r_ref[...] = jnp.zeros_like(r_ref)

    @pl.when((i == 0) | ((i + tiles_per_seq - 1) % tiles_per_seq == 0))
    def _():
        sc_ext[0:c0, :] = jnp.zeros((c0, SC_WIDTH), F32)
        rg_ext[0:c0, :] = jnp.zeros((c0, RG_WIDTH), F32)
        h_carry[...] = jnp.zeros_like(h_carry)

    attn_cols = ATTN_WIDTH + 2 * KV_DUP_WIDTH
    hx = _rms(x_ref[...], g_ref[...]).astype(BF16)
    p = jnp.dot(hx, w_ref[:, :attn_cols], preferred_element_type=F32)
    q_ref[...] = p[:, :ATTN_WIDTH].astype(BF16)
    k_ref[...] = p[:, ATTN_WIDTH:ATTN_WIDTH + KV_DUP_WIDTH].astype(BF16)
    v_ref[...] = p[:, ATTN_WIDTH + KV_DUP_WIDTH:].astype(BF16)

    sc_b = r_ref[:, 0:SC_WIDTH]
    sc_ext[c0:c0 + ts, :] = r_ref[:, SC_WIDTH:2 * SC_WIDTH] * r_ref[:, 2 * SC_WIDTH:3 * SC_WIDTH]
    rg_ext[c0:c0 + ts, :] = r_ref[:, 3 * SC_WIDTH:3 * SC_WIDTH + RG_WIDTH]
    rg_g = r_ref[:, 3 * SC_WIDTH + RG_WIDTH:]

    conv = scb_ref[...]
    for k in range(SC_CONV):
        off = c0 - (SC_CONV - 1) + k
        conv = conv + scw_ref[k:k + 1, :] * sc_ext[off:off + ts, :]
    conv_out = sc_b * conv

    rg_in = rgb_ref[...]
    for k in range(RG_CONV):
        off = c0 - (RG_CONV - 1) + k
        rg_in = rg_in + rgw_ref[k:k + 1, :] * rg_ext[off:off + ts, :]

    sc_ext[0:c0, :] = sc_ext[ts:ts + c0, :]
    rg_ext[0:c0, :] = rg_ext[ts:ts + c0, :]

    gates = jnp.dot(rg_in.astype(BF16), wgate_ref[...], preferred_element_type=F32) + bgate_ref[...]

    r_ref[...] = jnp.dot(hx, w_ref[:, attn_cols:], preferred_element_type=F32)

    r_gate = jax.nn.sigmoid(gates[:, :RG_WIDTH])
    i_gate = jax.nn.sigmoid(gates[:, RG_WIDTH:])
    neg_lam = -lam_ref[...]
    softplus = jnp.maximum(neg_lam, 0.0) + jnp.log1p(jnp.exp(-jnp.abs(neg_lam)))
    log_a = -RG_C * r_gate * softplus
    a = jnp.exp(log_a)
    u = jnp.sqrt(jnp.tanh(-log_a) * (1.0 + a * a)) * (i_gate * rg_in)

    s = 1
    while s < ts:
        u = a * _shift_rows(u, s, 0.0) + u
        a = a * _shift_rows(a, s, 1.0)
        s *= 2
    h = a * h_carry[...] + u
    h_carry[...] = h[ts - 1:ts, :]

    c = math.sqrt(2.0 / math.pi)
    gelu = 0.5 * rg_g * (1.0 + jnp.tanh(c * (rg_g + 0.044715 * (rg_g * rg_g * rg_g))))
    o_ref[:, 0:SC_WIDTH] = conv_out.astype(BF16)
    o_ref[:, SC_WIDTH:] = (h * gelu).astype(BF16)


def _in_proj_conv(x, g, w, sc_w, sc_b, rg_w, rg_b, w_gate, b_gate, lam, seq):
    t = x.shape[0]
    n = w.shape[1]
    tiles = t // TS
    assert n == ATTN_WIDTH + 2 * KV_DUP_WIDTH + REST_WIDTH
    proj = lambda width: pl.BlockSpec((TS, width), lambda i: (jnp.minimum(i, tiles - 1), 0))
    lagged = pl.BlockSpec((TS, SC_WIDTH + RG_WIDTH), lambda i: (jnp.maximum(i - 1, 0), 0))
    return pl.pallas_call(
        functools.partial(_in_proj_conv_kernel, tiles_per_seq=seq // TS),
        grid=(tiles + 1,),
        in_specs=[proj(D_MODEL), _const_spec((1, D_MODEL)), _const_spec((D_MODEL, n)),
                  _const_spec((SC_CONV, SC_WIDTH)), _const_spec((1, SC_WIDTH)),
                  _const_spec((RG_CONV, RG_WIDTH)), _const_spec((1, RG_WIDTH)),
                  _const_spec((RG_WIDTH, 2 * RG_WIDTH)), _const_spec((1, 2 * RG_WIDTH)),
                  _const_spec((1, RG_WIDTH))],
        out_specs=[proj(ATTN_WIDTH), proj(KV_DUP_WIDTH), proj(KV_DUP_WIDTH), lagged],
        out_shape=[jax.ShapeDtypeStruct((t, ATTN_WIDTH), BF16),
                   jax.ShapeDtypeStruct((t, KV_DUP_WIDTH), BF16),
                   jax.ShapeDtypeStruct((t, KV_DUP_WIDTH), BF16),
                   jax.ShapeDtypeStruct((t, SC_WIDTH + RG_WIDTH), BF16)],
        scratch_shapes=[pltpu.VMEM((TS, REST_WIDTH), F32),
                        pltpu.VMEM((TS + 2 * CARRY_ROWS, SC_WIDTH), F32),
                        pltpu.VMEM((TS + 2 * CARRY_ROWS, RG_WIDTH), F32),
                        pltpu.VMEM((1, RG_WIDTH), F32)],
        compiler_params=_params(("arbitrary",), 48),
        name="in_proj_conv_rglru",
    )(x, g, w, sc_w, sc_b, rg_w, rg_b, w_gate, b_gate, lam)


def _mem_kv_kernel(m_ref, g_ref, w_ref, k_ref, v_ref):
    h = _rms(m_ref[...], g_ref[...]).astype(BF16)
    p = jnp.dot(h, w_ref[...], preferred_element_type=F32)
    k_ref[...] = p[:, :XA_WIDTH].astype(BF16)
    v_ref[...] = p[:, XA_WIDTH:].astype(BF16)


def _mem_kv(mem, g, wkv):
    t = mem.shape[0]
    row = lambda width: pl.BlockSpec((MEM_LEN, width), lambda i: (i, 0))
    return pl.pallas_call(
        _mem_kv_kernel,
        grid=(t // MEM_LEN,),
        in_specs=[row(D_MODEL), _const_spec((1, D_MODEL)), _const_spec((D_MODEL, 2 * XA_WIDTH))],
        out_specs=[row(XA_WIDTH), row(XA_WIDTH)],
        out_shape=[jax.ShapeDtypeStruct((t, XA_WIDTH), BF16)] * 2,
        compiler_params=_params(("parallel",), 24),
        name="mem_kv",
    )(mem, g, wkv)


def _post_mixer_kernel(x_ref, a_ref, c_ref, wout_ref, g_ref, wq_ref, k_ref, v_ref, wo_ref, *rest):
    o_ref = rest[-1] if len(rest) == 1 else rest[4]
    halves = [slice(i * (TM // 2), (i + 1) * (TM // 2)) for i in range(2)]
    heads = [slice(hd * XA_HEAD_DIM, (hd + 1) * XA_HEAD_DIM) for hd in range(XA_HEADS)]
    nt = (((1,), (1,)), ((), ()))
    k = k_ref[...]
    v = v_ref[...]

    x1 = [x_ref[hs, :] + jnp.dot(jnp.concatenate([a_ref[hs, :], c_ref[hs, :]], axis=-1), wout_ref[...],
                                 preferred_element_type=F32) for hs in halves]
    q = [jnp.dot(_rms(xh, g_ref[...]).astype(BF16), wq_ref[...], preferred_element_type=F32).astype(BF16)
         for xh in x1]
    scores = [[lax.dot_general(qh[:, sl], k[:, sl], nt, preferred_element_type=F32) for sl in heads] for qh in q]
    x2 = []
    for xh, per_head in zip(x1, scores):
        probs, sums = [], []
        for s in per_head:
            s = s * (1.0 / math.sqrt(XA_HEAD_DIM))
            p = jnp.exp(s - jnp.max(s, axis=-1, keepdims=True))
            probs.append(p.astype(BF16))
            sums.append(jnp.sum(p, axis=-1, keepdims=True))
        att = jnp.concatenate([jnp.dot(p, v[:, sl], preferred_element_type=F32) / l
                               for p, sl, l in zip(probs, heads, sums)], axis=-1).astype(BF16)
        x2.append(xh + jnp.dot(att, wo_ref[...], preferred_element_type=F32))

    for hs, xh in zip(halves, x2):
        o_ref[hs, :] = xh
    if len(rest) > 1:
        ffn_g_ref, wr_ref, tri_ref, sel_ref, _, hf_ref, meta_ref, metat_ref, cnt_ref, base_ref, carry = rest
        h = _rms(jnp.concatenate(x2, axis=0), ffn_g_ref[...])
        _route(h, wr_ref, tri_ref, sel_ref, hf_ref, meta_ref, metat_ref, cnt_ref, base_ref, carry)


def _post_mixer(x, attn, cr, w_out, g, wq, k, v, wo, seq, route=None):
    t = x.shape[0]
    per_seq = seq // TM
    row = lambda width: pl.BlockSpec((TM, width), lambda i: (i, 0))
    mem_blk = pl.BlockSpec((MEM_LEN, XA_WIDTH), lambda i: (i // per_seq, 0))
    in_specs = [row(D_MODEL), row(ATTN_WIDTH), row(SC_WIDTH + RG_WIDTH), _const_spec((D_MODEL, D_MODEL)),
                _const_spec((1, D_MODEL)), _const_spec((D_MODEL, XA_WIDTH)), mem_blk, mem_blk,
                _const_spec((XA_WIDTH, D_MODEL))]
    out_specs = [row(D_MODEL)]
    out_shape = [jax.ShapeDtypeStruct((t, D_MODEL), F32)]
    scratch = []
    args = [x, attn, cr, w_out, g, wq, k, v, wo]
    if route is not None:
        assert TM % ROUTE_CHUNK == 0
        in_specs += [_const_spec((1, D_MODEL)), _const_spec((D_MODEL, LANES_V7X)),
                     _const_spec((ROUTE_CHUNK, ROUTE_CHUNK)), _const_spec((SUBLANES_V7X, LANES_V7X))]
        out_specs += [row(D_MODEL), row(LANES_V7X),
                      pl.BlockSpec((SUBLANES_V7X, TM), lambda i: (0, i)),
                      pl.BlockSpec((1, LANES_V7X), lambda i: (0, 0)),
                      pl.BlockSpec((SUBLANES_V7X, LANES_V7X), lambda i: (i, 0))]
        out_shape += [jax.ShapeDtypeStruct((t, D_MODEL), BF16),
                      jax.ShapeDtypeStruct((t, LANES_V7X), F32),
                      jax.ShapeDtypeStruct((SUBLANES_V7X, t), F32),
                      jax.ShapeDtypeStruct((1, LANES_V7X), F32),
                      jax.ShapeDtypeStruct((t // TM * SUBLANES_V7X, LANES_V7X), F32)]
        scratch = [pltpu.VMEM((1, LANES_V7X), F32)]
        args += list(route)
    return pl.pallas_call(
        _post_mixer_kernel,
        grid=(t // TM,),
        in_specs=in_specs,
        out_specs=out_specs,
        out_shape=out_shape,
        scratch_shapes=scratch,
        compiler_params=_params(("arbitrary",), 40),
        name="post_mixer_route" if route is not None else "post_mixer",
    )(*args)


def _ffn_kernel(x_ref, g_ref, wg_ref, wu_ref, wd_ref, *rest):
    n_cast = (len(rest) - 1) // 2
    o_ref = rest[n_cast]
    x = x_ref[...]
    h = _rms(x, g_ref[...]).astype(BF16)
    gate = jnp.dot(h, wg_ref[...], preferred_element_type=F32)
    up = jnp.dot(h, wu_ref[...], preferred_element_type=F32)
    act = (gate * jax.nn.sigmoid(gate) * up).astype(BF16)
    o_ref[...] = x + jnp.dot(act, wd_ref[...], preferred_element_type=F32)
    for src, dst in zip(rest[:n_cast], rest[n_cast + 1:]):
        dst[...] = src[...].astype(BF16)


def _dense_ffn(x, g, wg, wu, wd, cast_along=()):
    t = x.shape[0]
    steps = t // TM
    d_ff = wg.shape[1]
    row = pl.BlockSpec((TM, D_MODEL), lambda i: (i, 0))
    flat = [w.reshape(-1, w.shape[-1]) for w in cast_along]
    for w in flat:
        assert w.shape[0] % (steps * 2 * SUBLANES_V7X) == 0
    slabs = [pl.BlockSpec((w.shape[0] // steps, w.shape[1]), lambda i: (i, 0)) for w in flat]
    outs = pl.pallas_call(
        _ffn_kernel,
        grid=(steps,),
        in_specs=[row, _const_spec((1, D_MODEL)), _const_spec((D_MODEL, d_ff)), _const_spec((D_MODEL, d_ff)),
                  _const_spec((d_ff, D_MODEL))] + slabs,
        out_specs=[row] + slabs,
        out_shape=[jax.ShapeDtypeStruct((t, D_MODEL), F32)] + [jax.ShapeDtypeStruct(w.shape, BF16) for w in flat],
        compiler_params=_params(("parallel",), 60),
        name="dense_swiglu",
    )(x, g, wg, wu, wd, *flat)
    return outs[0], [o.reshape(w.shape) for o, w in zip(outs[1:], cast_along)]


META_E1, META_E2, META_R1, META_R2, META_W1, META_W2 = range(6)


def _to_token_tiles(ref, rows):
    m = rows.shape[0]
    for c in range(ROW_CHUNKS):
        ref[pl.ds(c, m, stride=ROW_CHUNKS), :] = rows[:, c * LANES_V7X:(c + 1) * LANES_V7X]


def _from_token_tiles(ref):
    m = ref.shape[0] // ROW_CHUNKS
    return jnp.concatenate([ref[pl.ds(c, m, stride=ROW_CHUNKS), :] for c in range(ROW_CHUNKS)], axis=-1)


def _route(h, wr_ref, tri_ref, sel_ref, hf_ref, meta_ref, metat_ref, cnt_ref, base_ref, carry):
    @pl.when(pl.program_id(0) == 0)
    def _():
        carry[...] = jnp.zeros_like(carry)

    base_ref[...] = jnp.broadcast_to(carry[...], base_ref.shape)
    chunks = [slice(c * ROUTE_CHUNK, (c + 1) * ROUTE_CHUNK) for c in range(h.shape[0] // ROUTE_CHUNK)]
    each = lambda fn, *lists: [fn(*vals) for vals in zip(*lists)]
    rowmax = lambda a: jnp.max(a, axis=-1, keepdims=True)
    rowsum = lambda a: jnp.sum(a, axis=-1, keepdims=True)
    lane = lax.broadcasted_iota(jnp.int32, (ROUTE_CHUNK, LANES_V7X), 1)
    first_hit = lambda lg, m: jnp.min(jnp.where(lg == m, lane, LANES_V7X), axis=-1, keepdims=True)

    hb = h.astype(BF16)
    hf_ref[...] = hb
    logits = [jnp.dot(hb[c, :], wr_ref[...], preferred_element_type=F32) for c in chunks]
    lg = each(lambda l: jnp.where(lane < N_EXPERTS, l, -jnp.inf), logits)
    m1 = each(rowmax, lg)
    e1 = each(first_hit, lg, m1)
    lg2 = each(lambda l, e: jnp.where(lane == e, -jnp.inf, l), lg, e1)
    m2 = each(rowmax, lg2)
    e2 = each(first_hit, lg2, m2)
    ex = each(lambda a, b: jnp.exp(b - a), m1, m2)
    w1 = each(lambda e: 1.0 / (1.0 + e), ex)
    w2 = each(lambda e: e / (1.0 + e), ex)
    hit1 = each(lambda e: lane == e, e1)
    hit2 = each(lambda e: lane == e, e2)
    onehot = each(lambda a, b: (a | b).astype(BF16), hit1, hit2)
    within = each(lambda o: jnp.dot(tri_ref[...], o, preferred_element_type=F32), onehot)
    totals = each(lambda o: jnp.sum(o.astype(F32), axis=0, keepdims=True), onehot)
    ahead, base = [], carry[...]
    for w, tot in zip(within, totals):
        ahead.append(w + base)
        base = base + tot
    carry[...] = base
    cnt_ref[...] = base
    r1 = each(lambda hit, a: rowsum(jnp.where(hit, a, 0.0)), hit1, ahead)
    r2 = each(lambda hit, a: rowsum(jnp.where(hit, a, 0.0)), hit2, ahead)

    def record(*vals):
        meta = jnp.zeros((ROUTE_CHUNK, LANES_V7X), F32)
        for col, val in zip((META_E1, META_E2, META_R1, META_R2, META_W1, META_W2), vals):
            meta = jnp.where(lane == col, val.astype(F32), meta)
        return meta

    meta = each(record, e1, e2, r1, r2, w1, w2)
    metat = each(lambda mt: lax.dot_general(sel_ref[...], mt, (((1,), (1,)), ((), ())), preferred_element_type=F32,
                                            precision=lax.Precision.HIGHEST), meta)
    for c, mt, mtt in zip(chunks, meta, metat):
        meta_ref[c, :] = mt
        metat_ref[:, c] = mtt


def _token_rows(ref, first_token, n_tokens):
    start = pl.multiple_of(first_token * ROW_CHUNKS, ROW_CHUNKS)
    return ref.at[pl.ds(start, n_tokens * ROW_CHUNKS)]


def _row_copy(src_ref, src_token, dst_ref, dst_token, sem):
    return pltpu.make_async_copy(_token_rows(src_ref, src_token, 1), _token_rows(dst_ref, dst_token, 1), sem)


def _dispatch_kernel(fill_ref, n_ref, off_ref, dst_ref, loc1_ref, loc2_ref, hf_ref, xs_ref, zeros, buf, sem_z, sems):
    i = pl.program_id(0)
    steps = pl.num_programs(0)
    slot_tokens = buf.shape[1] // ROW_CHUNKS

    @pl.when(i == 0)
    def _():
        zeros[...] = jnp.zeros_like(zeros)

        def tile_fill(e):
            return pltpu.make_async_copy(zeros, _token_rows(xs_ref, pl.multiple_of(fill_ref[e], TM_GROUP), TM_GROUP),
                                         sem_z)

        for e in range(2 * N_EXPERTS):
            @pl.when(fill_ref[e] >= 0)
            def _():
                tile_fill(e).start()
        for e in range(2 * N_EXPERTS):
            @pl.when(fill_ref[e] >= 0)
            def _():
                tile_fill(e).wait()

    k = lax.broadcasted_iota(jnp.int32, (slot_tokens, hf_ref.shape[0]), 0)
    onehot = ((k == loc1_ref[...]) | (k == loc2_ref[...])).astype(BF16)
    sorted_rows = jnp.dot(onehot, hf_ref[...], preferred_element_type=F32)

    def run(slot):
        whole_slot = pltpu.make_async_copy(buf.at[slot], _token_rows(xs_ref, 0, slot_tokens), sems.at[slot])

        @pl.when(i >= 2)
        def _():
            whole_slot.wait()

        _to_token_tiles(buf.at[slot], sorted_rows)
        for e in range(N_EXPERTS):
            n = n_ref[i * N_EXPERTS + e]
            off = off_ref[i * N_EXPERTS + e]
            dst = dst_ref[i * N_EXPERTS + e]
            bit = hf_ref.shape[0]
            while bit >= 1:
                head = n & ~(2 * bit - 1)

                @pl.when((n & bit) != 0)
                def _(bit=bit, head=head):
                    pltpu.make_async_copy(_token_rows(buf.at[slot], off + head, bit),
                                          _token_rows(xs_ref, dst + head, bit), sems.at[slot]).start(priority=e % 2)
                bit //= 2

        @pl.when(i == steps - 1)
        def _():
            whole_slot.wait()
            other = pltpu.make_async_copy(buf.at[1 - slot], _token_rows(xs_ref, 0, slot_tokens), sems.at[1 - slot])

            @pl.when(steps >= 2)
            def _():
                other.wait()

    for slot in range(2):
        pl.when(i % 2 == slot)(functools.partial(run, slot))


def _dispatch(fill_start, n_tab, off_tab, dst_tab, loc1, loc2, hf, n_rows):
    t = hf.shape[0]
    loc = pl.BlockSpec((1, TM), lambda i, *_: (0, i))
    return pl.pallas_call(
        _dispatch_kernel,
        grid_spec=pltpu.PrefetchScalarGridSpec(
            num_scalar_prefetch=4,
            grid=(t // TM,),
            in_specs=[loc, loc, pl.BlockSpec((TM, D_MODEL), lambda i, *_: (i, 0))],
            out_specs=pl.BlockSpec(memory_space=pl.ANY),
            scratch_shapes=[pltpu.VMEM((TM_GROUP * ROW_CHUNKS, LANES_V7X), F32),
                            pltpu.VMEM((2, 2 * TM * ROW_CHUNKS, LANES_V7X), F32),
                            pltpu.SemaphoreType.DMA(()), pltpu.SemaphoreType.DMA((2,))]),
        out_shape=jax.ShapeDtypeStruct((n_rows * ROW_CHUNKS, LANES_V7X), F32),
        compiler_params=_params(("arbitrary",), 40),
        name="moe_dispatch",
    )(fill_start, n_tab, off_tab, dst_tab, loc1.reshape(1, t), loc2.reshape(1, t), hf)


def _grouped_kernel(te_ref, used_ref, x_ref, wg_ref, wu_ref, wd_ref, y_ref):
    i = pl.program_id(0)

    @pl.when(i < used_ref[0])
    def _():
        h = _from_token_tiles(x_ref).astype(BF16)
        ff = wg_ref.shape[1] // FF_SPLIT
        cols = [slice(c * ff, (c + 1) * ff) for c in range(FF_SPLIT)]
        gates = [jnp.dot(h, wg_ref[:, c], preferred_element_type=F32) for c in cols]
        ups = [jnp.dot(h, wu_ref[:, c], preferred_element_type=F32) for c in cols]
        total = None
        for c, gate, up in zip(cols, gates, ups):
            act = (gate * jax.nn.sigmoid(gate) * up).astype(BF16)
            part = jnp.dot(act, wd_ref[c, :], preferred_element_type=F32)
            total = part if total is None else total + part
        _to_token_tiles(y_ref, total)

    @pl.when(i >= used_ref[0])
    def _():
        y_ref[...] = jnp.zeros_like(y_ref)


def _grouped_swiglu(tile_expert, n_used, xs, wg, wu, wd):
    n_rows = xs.shape[0] // ROW_CHUNKS
    d_ff = wg.shape[2]
    tile = (TM_GROUP * ROW_CHUNKS, LANES_V7X)
    src = lambda i, te, used: (jnp.maximum(jnp.minimum(i, used[0] - 1), 0), 0)
    expert = lambda shape, buffers: pl.BlockSpec((None,) + shape, lambda i, te, used: (te[i], 0, 0),
                                                 pipeline_mode=pl.Buffered(buffers))
    return pl.pallas_call(
        _grouped_kernel,
        grid_spec=pltpu.PrefetchScalarGridSpec(
            num_scalar_prefetch=2,
            grid=(n_rows // TM_GROUP,),
            in_specs=[pl.BlockSpec(tile, src), expert((D_MODEL, d_ff), 1), expert((D_MODEL, d_ff), 2),
                      expert((d_ff, D_MODEL), 2)],
            out_specs=pl.BlockSpec(tile, lambda i, te, used: (i, 0))),
        out_shape=jax.ShapeDtypeStruct((n_rows * ROW_CHUNKS, LANES_V7X), F32),
        compiler_params=_params(("arbitrary",), 58),
        name="moe_grouped_swiglu",
    )(tile_expert, n_used, xs, wg, wu, wd)


def _combine_kernel(pos1_ref, pos2_ref, pos1_next_ref, pos2_next_ref, x_ref, meta_ref, g_ref, y_ref, o_ref,
                    y1, y2, sems):
    i = pl.program_id(0)

    def gather(p1_ref, p2_ref, slot):
        def issue(r, carry):
            _row_copy(y_ref, p1_ref[r], y1.at[slot], r, sems.at[slot]).start(priority=0)
            _row_copy(y_ref, p2_ref[r], y2.at[slot], r, sems.at[slot]).start(priority=1)
            return carry
        lax.fori_loop(0, TM_COMBINE, issue, 0, unroll=ISSUE_UNROLL)

    def finish(slot):
        for buf in (y1, y2):
            pltpu.make_async_copy(_token_rows(y_ref, 0, TM_COMBINE), buf.at[slot], sems.at[slot]).wait()
        w1 = meta_ref[:, META_W1:META_W1 + 1]
        w2 = meta_ref[:, META_W2:META_W2 + 1]
        out = x_ref[...] + (w1 * _from_token_tiles(y1.at[slot]) + w2 * _from_token_tiles(y2.at[slot]))
        o_ref[...] = _rms(out, g_ref[...])

    @pl.when(i == 0)
    def _():
        gather(pos1_ref, pos2_ref, 0)

    for slot in range(2):
        @pl.when(i % 2 == slot)
        def _():
            @pl.when(i + 1 < pl.num_programs(0))
            def _():
                gather(pos1_next_ref, pos2_next_ref, 1 - slot)
            finish(slot)


def _combine(pos1, pos2, x, meta, g, y):
    t = x.shape[0]
    steps = t // TM_COMBINE
    idx = pl.BlockSpec((TM_COMBINE,), lambda i: (i,), memory_space=pltpu.SMEM)
    idx_next = pl.BlockSpec((TM_COMBINE,), lambda i: (jnp.minimum(i + 1, steps - 1),), memory_space=pltpu.SMEM)
    row = lambda width: pl.BlockSpec((TM_COMBINE, width), lambda i: (i, 0))
    slots = pltpu.VMEM((2, TM_COMBINE * ROW_CHUNKS, LANES_V7X), F32)
    return pl.pallas_call(
        _combine_kernel,
        grid=(steps,),
        in_specs=[idx, idx, idx_next, idx_next, row(D_MODEL), row(LANES_V7X), _const_spec((1, D_MODEL)),
                  pl.BlockSpec(memory_space=pl.ANY)],
        out_specs=row(D_MODEL),
        out_shape=jax.ShapeDtypeStruct((t, D_MODEL), F32),
        scratch_shapes=[slots, slots, pltpu.SemaphoreType.DMA((2,))],
        compiler_params=_params(("arbitrary",), 24),
        name="moe_combine_norm",
    )(pos1, pos2, pos1, pos2, x, meta, g, y)


def _route_operands(ffn_g, router_w):
    wr_pad = jnp.zeros((D_MODEL, LANES_V7X), BF16).at[:, :N_EXPERTS].set(router_w.astype(BF16))
    tri = jnp.tril(jnp.ones((ROUTE_CHUNK, ROUTE_CHUNK), BF16), -1)
    sel = jnp.eye(SUBLANES_V7X, LANES_V7X, dtype=F32)
    return ffn_g, wr_pad, tri, sel


def _moe_layer(x, hf, meta, metat, counts, base, wg, wu, wd, final_g):
    t = x.shape[0]
    steps = t // TM
    n_rows = 2 * t + N_EXPERTS * TM_GROUP

    cnt = counts[0, :N_EXPERTS].astype(jnp.int32)
    padded = (cnt + TM_GROUP - 1) // TM_GROUP * TM_GROUP
    ends = jnp.cumsum(padded)
    starts = ends - padded
    experts = jnp.arange(N_EXPERTS, dtype=jnp.int32)[:, None]
    e1, e2 = metat[META_E1].astype(jnp.int32), metat[META_E2].astype(jnp.int32)
    r1, r2 = metat[META_R1].astype(jnp.int32), metat[META_R2].astype(jnp.int32)
    pick = lambda e, table: jnp.sum(jnp.where(e[None, :] == experts, table, 0), axis=0)
    pos1 = pick(e1, starts[:, None]) + r1
    pos2 = pick(e2, starts[:, None]) + r2
    before = base.reshape(steps, SUBLANES_V7X, LANES_V7X)[:, 0, :N_EXPERTS].astype(jnp.int32)
    n_tab = jnp.concatenate([before[1:], cnt[None, :]], axis=0) - before
    off_tab = jnp.cumsum(n_tab, axis=1) - n_tab
    dst_tab = starts[None, :] + before
    shift = jnp.repeat((off_tab - before).T, TM, axis=1)
    loc1 = pick(e1, shift) + r1
    loc2 = pick(e2, shift) + r2
    n_used = (ends[-1] // TM_GROUP).astype(jnp.int32).reshape(1)
    tile_row = jnp.minimum(jnp.arange(n_rows // TM_GROUP, dtype=jnp.int32), n_used[0] - 1) * TM_GROUP
    tile_expert = jnp.sum(ends[None, :] <= tile_row[:, None], axis=1).astype(jnp.int32)
    tail_start = jnp.where(padded > 0, ends - TM_GROUP, -1)
    spare = ends[-1] + TM_GROUP * jnp.arange(N_EXPERTS, dtype=jnp.int32)
    fill_start = jnp.concatenate([tail_start, jnp.where(spare < n_rows, spare, -1)]).astype(jnp.int32)

    flat = lambda a: a.reshape(-1).astype(jnp.int32)
    xs = _dispatch(fill_start, flat(n_tab), flat(off_tab), flat(dst_tab), loc1, loc2, hf, n_rows)
    y = _grouped_swiglu(tile_expert, n_used, xs, wg, wu, wd)
    return _combine(pos1, pos2, x, meta, final_g, y)


def _arrange_in_proj(w):
    q = (w[:, :ATTN_WIDTH] * (1.0 / math.sqrt(HEAD_DIM))).astype(BF16)
    w = w.astype(BF16)
    dup = lambda start: [w[:, start + h * HEAD_DIM:start + (h + 1) * HEAD_DIM]
                         for h in range(ATTN_KV_HEADS) for _ in range(2)]
    return jnp.concatenate([q] + dup(ATTN_WIDTH) + dup(ATTN_WIDTH + KV_WIDTH) + [w[:, ATTN_WIDTH + 2 * KV_WIDTH:]],
                           axis=1)


def _block_diag(w):
    heads, d, _ = w.shape
    eye = jnp.eye(heads, dtype=w.dtype)
    return jnp.einsum('hij,hg->higj', w, eye).reshape(heads * d, heads * d)


def kernel(x, mem, rel_bias, mix_norm, w_in, attn_sinks, sc_conv_w, sc_conv_b, rg_conv_w, rg_conv_b, rg_w_a,
           rg_b_a, rg_w_x, rg_b_x, rg_lambda, w_out, xa_norm, mem_norm, xa_wq, xa_wk, xa_wv, xa_wo, ffn_norm,
           dense_wg, dense_wu, dense_wd, moe_router, moe_wg, moe_wu, moe_wd, final_norm):
    batch, seq, _ = x.shape
    depth = w_in.shape[0]
    assert depth == 2 and seq % TS == 0 and seq % TM == 0
    xt = x.reshape(batch * seq, D_MODEL)
    memt = mem.reshape(batch * MEM_LEN, D_MODEL)
    bias_tbl = _attention_bias_tables(rel_bias)
    vec = lambda a: a.reshape(1, -1)

    for layer in range(depth):
        w_gate = jnp.concatenate([_block_diag(rg_w_a[layer]), _block_diag(rg_w_x[layer])], axis=1).astype(BF16)
        b_gate = jnp.concatenate([rg_b_a[layer], rg_b_x[layer]]).reshape(1, -1)
        q, k, v, cr = _in_proj_conv(xt, vec(mix_norm[layer]), _arrange_in_proj(w_in[layer]),
                                    sc_conv_w[layer], vec(sc_conv_b[layer]), rg_conv_w[layer],
                                    vec(rg_conv_b[layer]), w_gate, b_gate, vec(rg_lambda[layer]), seq)
        attn = _attention(q, k, v, attn_sinks[layer], bias_tbl, seq // BLOCK)
        wkv = jnp.concatenate([xa_wk[layer], xa_wv[layer]], axis=1).astype(BF16)
        mk, mv = _mem_kv(memt, vec(mem_norm[layer]), wkv)
        post = functools.partial(_post_mixer, xt, attn, cr, w_out[layer].astype(BF16), vec(xa_norm[layer]),
                                 xa_wq[layer].astype(BF16), mk, mv, xa_wo[layer].astype(BF16), seq)

        j = layer // 2
        if layer % 2 == 0:
            (xt,) = post()
            xt, expert_w = _dense_ffn(xt, vec(ffn_norm[layer]), dense_wg[j].astype(BF16), dense_wu[j].astype(BF16),
                                      dense_wd[j].astype(BF16), cast_along=(moe_wg[j], moe_wu[j], moe_wd[j]))
        else:
            xt, hf, meta, metat, counts, base = post(route=_route_operands(vec(ffn_norm[layer]), moe_router[j]))
            xt = _moe_layer(xt, hf, meta, metat, counts, base, *expert_w, vec(final_norm))
    return xt.reshape(batch, seq, D_MODEL)
```

```python
import functools
import math

import jax
import jax.numpy as jnp
import numpy as np
from jax import lax
from jax.experimental import pallas as pl
from jax.experimental.pallas import tpu as pltpu

F32 = jnp.float32
BF16 = jnp.bfloat16

D_MODEL = 1024
MEM_LEN = 256
HEAD_DIM = 64
ATTN_Q_HEADS = 8
ATTN_KV_HEADS = 2
ATTN_WIDTH = ATTN_Q_HEADS * HEAD_DIM
KV_WIDTH = ATTN_KV_HEADS * HEAD_DIM
KV_DUP_WIDTH = 2 * KV_WIDTH
BLOCK = 128
SC_WIDTH = 256
SC_CONV = 3
RG_WIDTH = 256
RG_HEADS = 4
RG_HEAD_DIM = RG_WIDTH // RG_HEADS
RG_CONV = 4
RG_C = 8.0
N_BUCKETS = 32
MAX_EXACT = N_BUCKETS // 2
MAX_DISTANCE = 128
XA_HEADS = 4
XA_HEAD_DIM = 128
XA_WIDTH = XA_HEADS * XA_HEAD_DIM
N_EXPERTS = 8
EPS = 1e-6
NEG_INF = -1e30
REST_WIDTH = 3 * SC_WIDTH + 2 * RG_WIDTH

LANES_V7X = 128
SUBLANES_V7X = 8
VMEM_BYTES_V7X = 64 * 1024 * 1024
ROW_CHUNKS = D_MODEL // LANES_V7X
assert ROW_CHUNKS == SUBLANES_V7X

TM = 512
ATTN_BLOCKS = 8
TS = 512
ROUTE_CHUNK = 128
TM_GROUP = 512
ISSUE_UNROLL = 8
FF_SPLIT = 2
CARRY_ROWS = SUBLANES_V7X


def _mib(n):
    return int(n * 1024 * 1024)


def _params(semantics, vmem_mib):
    assert _mib(vmem_mib) < VMEM_BYTES_V7X
    return pltpu.CompilerParams(dimension_semantics=semantics, vmem_limit_bytes=_mib(vmem_mib))


def _rms(x, g):
    ms = jnp.mean(x * x, axis=-1, keepdims=True)
    return x * lax.rsqrt(ms + EPS) * g


def _const_spec(shape):
    nd = len(shape)
    return pl.BlockSpec(shape, lambda *_: (0,) * nd, pipeline_mode=pl.Buffered(1))


def _attn_kernel(sink_ref, q_ref, kp_ref, kc_ref, vp_ref, vc_ref, bias0_ref, bias_ref, o_ref):
    pairs_per_group = ATTN_Q_HEADS // ATTN_KV_HEADS // 2
    row = lax.broadcasted_iota(jnp.int32, (BLOCK, BLOCK), 0)
    col = lax.broadcasted_iota(jnp.int32, (BLOCK, BLOCK), 1)
    from_prev = col > row
    low_lanes = lax.broadcasted_iota(jnp.int32, (2 * BLOCK, 2 * HEAD_DIM), 1) < HEAD_DIM
    low_out = col < HEAD_DIM
    zero = jnp.zeros((), BF16)

    def block_diag(band):
        return jnp.concatenate([jnp.where(low_lanes, band, zero), jnp.where(low_lanes, zero, band)], axis=0)

    def scores(blk):
        rows = slice(blk * BLOCK, (blk + 1) * BLOCK)
        prev_rows = slice((blk - 1) * BLOCK, blk * BLOCK)
        out = []
        for g in range(ATTN_KV_HEADS):
            lanes = slice(g * 2 * HEAD_DIM, (g + 1) * 2 * HEAD_DIM)
            k_prev = kp_ref[:, lanes] if blk == 0 else kc_ref[prev_rows, lanes]
            v_prev = vp_ref[:, lanes] if blk == 0 else vc_ref[prev_rows, lanes]
            k_bd = block_diag(jnp.concatenate([k_prev, kc_ref[rows, lanes]], axis=0))
            v_bd = block_diag(jnp.concatenate([v_prev, vc_ref[rows, lanes]], axis=0))
            for pair in range(pairs_per_group):
                slab = g * pairs_per_group + pair
                q2 = q_ref[rows, slab * 2 * HEAD_DIM:(slab + 1) * 2 * HEAD_DIM]
                s = lax.dot_general(q2, k_bd, (((1,), (1,)), ((), ())), preferred_element_type=F32)
                out.append((slab, s, v_bd))
        return out

    def finish(blk, scored):
        rows = slice(blk * BLOCK, (blk + 1) * BLOCK)
        tbl_ref = bias0_ref if blk == 0 else bias_ref
        staged = []
        for slab, s, v_bd in scored:
            probs, denoms = [], []
            for side in range(2):
                h = 2 * slab + side
                sh = s[:, side * 2 * BLOCK:(side + 1) * 2 * BLOCK]
                logits = jnp.where(from_prev, sh[:, :BLOCK], sh[:, BLOCK:]) + tbl_ref[h]
                sink = sink_ref[h]
                m = jnp.maximum(jnp.max(logits, axis=-1, keepdims=True), sink)
                p = jnp.exp(logits - m)
                denoms.append(jnp.sum(p, axis=-1, keepdims=True) + jnp.exp(sink - m))
                probs += [jnp.where(from_prev, p, 0.0), jnp.where(from_prev, 0.0, p)]
            staged.append((slab, jnp.concatenate(probs, axis=-1).astype(BF16), v_bd, denoms))
        for slab, p_band, v_bd, denoms in staged:
            o = jnp.dot(p_band, v_bd, preferred_element_type=F32)
            o = o / jnp.where(low_out, denoms[0], denoms[1])
            o_ref[rows, slab * 2 * HEAD_DIM:(slab + 1) * 2 * HEAD_DIM] = o.astype(BF16)

    pending = scores(0)
    for blk in range(ATTN_BLOCKS):
        upcoming = scores(blk + 1) if blk + 1 < ATTN_BLOCKS else None
        finish(blk, pending)
        pending = upcoming


def _attention(q, k, v, sinks, bias_tbl, blocks_per_seq):
    t = q.shape[0]
    tile = ATTN_BLOCKS * BLOCK
    cur = lambda i: (i, 0)
    prev = lambda i: (jnp.maximum(i * ATTN_BLOCKS - 1, 0), 0)
    tbl = (None, ATTN_Q_HEADS, BLOCK, BLOCK)
    return pl.pallas_call(
        _attn_kernel,
        grid=(t // tile,),
        in_specs=[pl.BlockSpec(memory_space=pltpu.SMEM),
                  pl.BlockSpec((tile, ATTN_WIDTH), cur),
                  pl.BlockSpec((BLOCK, KV_DUP_WIDTH), prev),
                  pl.BlockSpec((tile, KV_DUP_WIDTH), cur),
                  pl.BlockSpec((BLOCK, KV_DUP_WIDTH), prev),
                  pl.BlockSpec((tile, KV_DUP_WIDTH), cur),
                  pl.BlockSpec(tbl, lambda i: (jnp.minimum((i * ATTN_BLOCKS) % blocks_per_seq, 1), 0, 0, 0)),
                  pl.BlockSpec(tbl, lambda i: (1, 0, 0, 0))],
        out_specs=pl.BlockSpec((tile, ATTN_WIDTH), cur),
        out_shape=jax.ShapeDtypeStruct((t, ATTN_WIDTH), BF16),
        compiler_params=_params(("parallel",), 24),
        name="swa_attention",
    )(sinks, q, k, k, v, v, bias_tbl, bias_tbl)


def _bias_table_kernel(rel_ref, bucket_ref, o_ref):
    for v in range(2):
        bucket = bucket_ref[v]
        hits = [bucket == b for b in range(N_BUCKETS)]
        for h in range(ATTN_Q_HEADS):
            tbl = jnp.full(bucket.shape, NEG_INF, F32)
            for b in range(N_BUCKETS):
                tbl = jnp.where(hits[b], rel_ref[b * ATTN_Q_HEADS + h], tbl)
            o_ref[v, h] = tbl


def _attention_bias_tables(rel_bias):
    q_idx = np.arange(BLOCK)[:, None]
    j_idx = np.arange(BLOCK)[None, :]
    from_prev = j_idx > q_idx
    n = np.where(from_prev, q_idx + BLOCK - j_idx, q_idx - j_idx)
    large = MAX_EXACT + (np.log(np.maximum(n, 1).astype(np.float32) / np.float32(MAX_EXACT))
                         / np.float32(math.log(MAX_DISTANCE / MAX_EXACT))
                         * np.float32(N_BUCKETS - MAX_EXACT)).astype(np.int32)
    bucket = np.where(n < MAX_EXACT, n, np.minimum(large, N_BUCKETS - 1))
    first = np.where(from_prev, -1, bucket)
    buckets = jnp.asarray(np.stack([first, bucket]).astype(np.int32))
    return pl.pallas_call(
        _bias_table_kernel,
        in_specs=[pl.BlockSpec(memory_space=pltpu.SMEM), pl.BlockSpec(memory_space=pltpu.VMEM)],
        out_specs=pl.BlockSpec(memory_space=pltpu.VMEM),
        out_shape=jax.ShapeDtypeStruct((2, ATTN_Q_HEADS, BLOCK, BLOCK), F32),
        name="t5_bias_table",
    )(rel_bias.astype(F32).reshape(-1), buckets)


def _shift_rows(x, s, fill):
    return jnp.concatenate([jnp.full((s, x.shape[1]), fill, x.dtype), x[:x.shape[0] - s]], axis=0)


def _in_proj_conv_kernel(x_ref, g_ref, w_ref, scw_ref, scb_ref, rgw_ref, rgb_ref, wgate_ref, bgate_ref, lam_ref,
                         q_ref, k_ref, v_ref, o_ref, r_ref, sc_ext, rg_ext, h_carry, *, tiles_per_seq):
    i = pl.program_id(0)
    ts = r_ref.shape[0]
    c0 = CARRY_ROWS

    @pl.when(i == 0)
    def _():
        r_ref[...] = jnp.zeros_like(r_ref)

    @pl.when((i == 0) | ((i + tiles_per_seq - 1) % tiles_per_seq == 0))
    def _():
        sc_ext[0:c0, :] = jnp.zeros((c0, SC_WIDTH), F32)
        rg_ext[0:c0, :] = jnp.zeros((c0, RG_WIDTH), F32)
        h_carry[...] = jnp.zeros_like(h_carry)

    attn_cols = ATTN_WIDTH + 2 * KV_DUP_WIDTH
    hx = _rms(x_ref[...], g_ref[...]).astype(BF16)
    p = jnp.dot(hx, w_ref[:, :attn_cols], preferred_element_type=F32)
    q_ref[...] = p[:, :ATTN_WIDTH].astype(BF16)
    k_ref[...] = p[:, ATTN_WIDTH:ATTN_WIDTH + KV_DUP_WIDTH].astype(BF16)
    v_ref[...] = p[:, ATTN_WIDTH + KV_DUP_WIDTH:].astype(BF16)

    sc_b = r_ref[:, 0:SC_WIDTH]
    sc_ext[c0:c0 + ts, :] = r_ref[:, SC_WIDTH:2 * SC_WIDTH] * r_ref[:, 2 * SC_WIDTH:3 * SC_WIDTH]
    rg_ext[c0:c0 + ts, :] = r_ref[:, 3 * SC_WIDTH:3 * SC_WIDTH + RG_WIDTH]
    rg_g = r_ref[:, 3 * SC_WIDTH + RG_WIDTH:]

    conv = scb_ref[...]
    for k in range(SC_CONV):
        off = c0 - (SC_CONV - 1) + k
        conv = conv + scw_ref[k:k + 1, :] * sc_ext[off:off + ts, :]
    conv_out = sc_b * conv

    rg_in = rgb_ref[...]
    for k in range(RG_CONV):
        off = c0 - (RG_CONV - 1) + k
        rg_in = rg_in + rgw_ref[k:k + 1, :] * rg_ext[off:off + ts, :]

    sc_ext[0:c0, :] = sc_ext[ts:ts + c0, :]
    rg_ext[0:c0, :] = rg_ext[ts:ts + c0, :]

    gates = jnp.dot(rg_in.astype(BF16), wgate_ref[...], preferred_element_type=F32) + bgate_ref[...]

    r_ref[...] = jnp.dot(hx, w_ref[:, attn_cols:], preferred_element_type=F32)

    r_gate = jax.nn.sigmoid(gates[:, :RG_WIDTH])
    i_gate = jax.nn.sigmoid(gates[:, RG_WIDTH:])
    neg_lam = -lam_ref[...]
    softplus = jnp.maximum(neg_lam, 0.0) + jnp.log1p(jnp.exp(-jnp.abs(neg_lam)))
    log_a = -RG_C * r_gate * softplus
    a = jnp.exp(log_a)
    u = jnp.sqrt(jnp.tanh(-log_a) * (1.0 + a * a)) * (i_gate * rg_in)

    s = 1
    while s < ts:
        u = a * _shift_rows(u, s, 0.0) + u
        a = a * _shift_rows(a, s, 1.0)
        s *= 2
    h = a * h_carry[...] + u
    h_carry[...] = h[ts - 1:ts, :]

    c = math.sqrt(2.0 / math.pi)
    gelu = 0.5 * rg_g * (1.0 + jnp.tanh(c * (rg_g + 0.044715 * (rg_g * rg_g * rg_g))))
    o_ref[:, 0:SC_WIDTH] = conv_out.astype(BF16)
    o_ref[:, SC_WIDTH:] = (h * gelu).astype(BF16)


def _in_proj_conv(x, g, w, sc_w, sc_b, rg_w, rg_b, w_gate, b_gate, lam, seq):
    t = x.shape[0]
    n = w.shape[1]
    tiles = t // TS
    assert n == ATTN_WIDTH + 2 * KV_DUP_WIDTH + REST_WIDTH
    proj = lambda width: pl.BlockSpec((TS, width), lambda i: (jnp.minimum(i, tiles - 1), 0))
    lagged = pl.BlockSpec((TS, SC_WIDTH + RG_WIDTH), lambda i: (jnp.maximum(i - 1, 0), 0))
    return pl.pallas_call(
        functools.partial(_in_proj_conv_kernel, tiles_per_seq=seq // TS),
        grid=(tiles + 1,),
        in_specs=[proj(D_MODEL), _const_spec((1, D_MODEL)), _const_spec((D_MODEL, n)),
                  _const_spec((SC_CONV, SC_WIDTH)), _const_spec((1, SC_WIDTH)),
                  _const_spec((RG_CONV, RG_WIDTH)), _const_spec((1, RG_WIDTH)),
                  _const_spec((RG_WIDTH, 2 * RG_WIDTH)), _const_spec((1, 2 * RG_WIDTH)),
                  _const_spec((1, RG_WIDTH))],
        out_specs=[proj(ATTN_WIDTH), proj(KV_DUP_WIDTH), proj(KV_DUP_WIDTH), lagged],
        out_shape=[jax.ShapeDtypeStruct((t, ATTN_WIDTH), BF16),
                   jax.ShapeDtypeStruct((t, KV_DUP_WIDTH), BF16),
                   jax.ShapeDtypeStruct((t, KV_DUP_WIDTH), BF16),
                   jax.ShapeDtypeStruct((t, SC_WIDTH + RG_WIDTH), BF16)],
        scratch_shapes=[pltpu.VMEM((TS, REST_WIDTH), F32),
                        pltpu.VMEM((TS + 2 * CARRY_ROWS, SC_WIDTH), F32),
                        pltpu.VMEM((TS + 2 * CARRY_ROWS, RG_WIDTH), F32),
                        pltpu.VMEM((1, RG_WIDTH), F32)],
        compiler_params=_params(("arbitrary",), 48),
        name="in_proj_conv_rglru",
    )(x, g, w, sc_w, sc_b, rg_w, rg_b, w_gate, b_gate, lam)


def _mem_kv_kernel(m_ref, g_ref, w_ref, k_ref, v_ref):
    h = _rms(m_ref[...], g_ref[...]).astype(BF16)
    p = jnp.dot(h, w_ref[...], preferred_element_type=F32)
    k_ref[...] = p[:, :XA_WIDTH].astype(BF16)
    v_ref[...] = p[:, XA_WIDTH:].astype(BF16)


def _mem_kv(mem, g, wkv):
    t = mem.shape[0]
    row = lambda width: pl.BlockSpec((MEM_LEN, width), lambda i: (i, 0))
    return pl.pallas_call(
        _mem_kv_kernel,
        grid=(t // MEM_LEN,),
        in_specs=[row(D_MODEL), _const_spec((1, D_MODEL)), _const_spec((D_MODEL, 2 * XA_WIDTH))],
        out_specs=[row(XA_WIDTH), row(XA_WIDTH)],
        out_shape=[jax.ShapeDtypeStruct((t, XA_WIDTH), BF16)] * 2,
        compiler_params=_params(("parallel",), 24),
        name="mem_kv",
    )(mem, g, wkv)


def _post_mixer_kernel(x_ref, a_ref, c_ref, wout_ref, g_ref, wq_ref, k_ref, v_ref, wo_ref, *rest):
    o_ref = rest[-1] if len(rest) == 1 else rest[4]
    halves = [slice(i * (TM // 2), (i + 1) * (TM // 2)) for i in range(2)]
    heads = [slice(hd * XA_HEAD_DIM, (hd + 1) * XA_HEAD_DIM) for hd in range(XA_HEADS)]
    nt = (((1,), (1,)), ((), ()))
    k = k_ref[...]
    v = v_ref[...]

    x1 = [x_ref[hs, :] + jnp.dot(jnp.concatenate([a_ref[hs, :], c_ref[hs, :]], axis=-1), wout_ref[...],
                                 preferred_element_type=F32) for hs in halves]
    q = [jnp.dot(_rms(xh, g_ref[...]).astype(BF16), wq_ref[...], preferred_element_type=F32).astype(BF16)
         for xh in x1]
    scores = [[lax.dot_general(qh[:, sl], k[:, sl], nt, preferred_element_type=F32) for sl in heads] for qh in q]
    x2 = []
    for xh, per_head in zip(x1, scores):
        probs, sums = [], []
        for s in per_head:
            s = s * (1.0 / math.sqrt(XA_HEAD_DIM))
            p = jnp.exp(s - jnp.max(s, axis=-1, keepdims=True))
            probs.append(p.astype(BF16))
            sums.append(jnp.sum(p, axis=-1, keepdims=True))
        att = jnp.concatenate([jnp.dot(p, v[:, sl], preferred_element_type=F32) / l
                               for p, sl, l in zip(probs, heads, sums)], axis=-1).astype(BF16)
        x2.append(xh + jnp.dot(att, wo_ref[...], preferred_element_type=F32))

    for hs, xh in zip(halves, x2):
        o_ref[hs, :] = xh
    if len(rest) > 1:
        ffn_g_ref, wr_ref, tri_ref, sel_ref, _, hf_ref, meta_ref, metat_ref, cnt_ref, base_ref, carry = rest
        h = _rms(jnp.concatenate(x2, axis=0), ffn_g_ref[...])
        _route(h, wr_ref, tri_ref, sel_ref, hf_ref, meta_ref, metat_ref, cnt_ref, base_ref, carry)


def _post_mixer(x, attn, cr, w_out, g, wq, k, v, wo, seq, route=None):
    t = x.shape[0]
    per_seq = seq // TM
    row = lambda width: pl.BlockSpec((TM, width), lambda i: (i, 0))
    mem_blk = pl.BlockSpec((MEM_LEN, XA_WIDTH), lambda i: (i // per_seq, 0))
    in_specs = [row(D_MODEL), row(ATTN_WIDTH), row(SC_WIDTH + RG_WIDTH), _const_spec((D_MODEL, D_MODEL)),
                _const_spec((1, D_MODEL)), _const_spec((D_MODEL, XA_WIDTH)), mem_blk, mem_blk,
                _const_spec((XA_WIDTH, D_MODEL))]
    out_specs = [row(D_MODEL)]
    out_shape = [jax.ShapeDtypeStruct((t, D_MODEL), F32)]
    scratch = []
    args = [x, attn, cr, w_out, g, wq, k, v, wo]
    if route is not None:
        assert TM % ROUTE_CHUNK == 0
        in_specs += [_const_spec((1, D_MODEL)), _const_spec((D_MODEL, LANES_V7X)),
                     _const_spec((ROUTE_CHUNK, ROUTE_CHUNK)), _const_spec((SUBLANES_V7X, LANES_V7X))]
        out_specs += [row(D_MODEL), row(LANES_V7X),
                      pl.BlockSpec((SUBLANES_V7X, TM), lambda i: (0, i)),
                      pl.BlockSpec((1, LANES_V7X), lambda i: (0, 0)),
                      pl.BlockSpec((SUBLANES_V7X, LANES_V7X), lambda i: (i, 0))]
        out_shape += [jax.ShapeDtypeStruct((t, D_MODEL), BF16),
                      jax.ShapeDtypeStruct((t, LANES_V7X), F32),
                      jax.ShapeDtypeStruct((SUBLANES_V7X, t), F32),
                      jax.ShapeDtypeStruct((1, LANES_V7X), F32),
                      jax.ShapeDtypeStruct((t // TM * SUBLANES_V7X, LANES_V7X), F32)]
        scratch = [pltpu.VMEM((1, LANES_V7X), F32)]
        args += list(route)
    return pl.pallas_call(
        _post_mixer_kernel,
        grid=(t // TM,),
        in_specs=in_specs,
        out_specs=out_specs,
        out_shape=out_shape,
        scratch_shapes=scratch,
        compiler_params=_params(("arbitrary",), 40),
        name="post_mixer_route" if route is not None else "post_mixer",
    )(*args)


def _ffn_kernel(x_ref, g_ref, wg_ref, wu_ref, wd_ref, *rest):
    n_cast = (len(rest) - 1) // 2
    o_ref = rest[n_cast]
    x = x_ref[...]
    h = _rms(x, g_ref[...]).astype(BF16)
    gate = jnp.dot(h, wg_ref[...], preferred_element_type=F32)
    up = jnp.dot(h, wu_ref[...], preferred_element_type=F32)
    act = (gate * jax.nn.sigmoid(gate) * up).astype(BF16)
    o_ref[...] = x + jnp.dot(act, wd_ref[...], preferred_element_type=F32)
    for src, dst in zip(rest[:n_cast], rest[n_cast + 1:]):
        dst[...] = src[...].astype(BF16)


def _dense_ffn(x, g, wg, wu, wd, cast_along=()):
    t = x.shape[0]
    steps = t // TM
    d_ff = wg.shape[1]
    row = pl.BlockSpec((TM, D_MODEL), lambda i: (i, 0))
    flat = [w.reshape(-1, w.shape[-1]) for w in cast_along]
    for w in flat:
        assert w.shape[0] % (steps * 2 * SUBLANES_V7X) == 0
    slabs = [pl.BlockSpec((w.shape[0] // steps, w.shape[1]), lambda i: (i, 0)) for w in flat]
    outs = pl.pallas_call(
        _ffn_kernel,
        grid=(steps,),
        in_specs=[row, _const_spec((1, D_MODEL)), _const_spec((D_MODEL, d_ff)), _const_spec((D_MODEL, d_ff)),
                  _const_spec((d_ff, D_MODEL))] + slabs,
        out_specs=[row] + slabs,
        out_shape=[jax.ShapeDtypeStruct((t, D_MODEL), F32)] + [jax.ShapeDtypeStruct(w.shape, BF16) for w in flat],
        compiler_params=_params(("parallel",), 60),
        name="dense_swiglu",
    )(x, g, wg, wu, wd, *flat)
    return outs[0], [o.reshape(w.shape) for o, w in zip(outs[1:], cast_along)]


META_E1, META_E2, META_R1, META_R2, META_W1, META_W2 = range(6)


def _to_token_tiles(ref, rows):
    m = rows.shape[0]
    for c in range(ROW_CHUNKS):
        ref[pl.ds(c, m, stride=ROW_CHUNKS), :] = rows[:, c * LANES_V7X:(c + 1) * LANES_V7X]


def _from_token_tiles(ref):
    m = ref.shape[0] // ROW_CHUNKS
    return jnp.concatenate([ref[pl.ds(c, m, stride=ROW_CHUNKS), :] for c in range(ROW_CHUNKS)], axis=-1)


def _route(h, wr_ref, tri_ref, sel_ref, hf_ref, meta_ref, metat_ref, cnt_ref, base_ref, carry):
    @pl.when(pl.program_id(0) == 0)
    def _():
        carry[...] = jnp.zeros_like(carry)

    base_ref[...] = jnp.broadcast_to(carry[...], base_ref.shape)
    chunks = [slice(c * ROUTE_CHUNK, (c + 1) * ROUTE_CHUNK) for c in range(h.shape[0] // ROUTE_CHUNK)]
    each = lambda fn, *lists: [fn(*vals) for vals in zip(*lists)]
    rowmax = lambda a: jnp.max(a, axis=-1, keepdims=True)
    rowsum = lambda a: jnp.sum(a, axis=-1, keepdims=True)
    lane = lax.broadcasted_iota(jnp.int32, (ROUTE_CHUNK, LANES_V7X), 1)
    first_hit = lambda lg, m: jnp.min(jnp.where(lg == m, lane, LANES_V7X), axis=-1, keepdims=True)

    hb = h.astype(BF16)
    hf_ref[...] = hb
    logits = [jnp.dot(hb[c, :], wr_ref[...], preferred_element_type=F32) for c in chunks]
    lg = each(lambda l: jnp.where(lane < N_EXPERTS, l, -jnp.inf), logits)
    m1 = each(rowmax, lg)
    e1 = each(first_hit, lg, m1)
    lg2 = each(lambda l, e: jnp.where(lane == e, -jnp.inf, l), lg, e1)
    m2 = each(rowmax, lg2)
    e2 = each(first_hit, lg2, m2)
    ex = each(lambda a, b: jnp.exp(b - a), m1, m2)
    w1 = each(lambda e: 1.0 / (1.0 + e), ex)
    w2 = each(lambda e: e / (1.0 + e), ex)
    hit1 = each(lambda e: lane == e, e1)
    hit2 = each(lambda e: lane == e, e2)
    onehot = each(lambda a, b: (a | b).astype(BF16), hit1, hit2)
    within = each(lambda o: jnp.dot(tri_ref[...], o, preferred_element_type=F32), onehot)
    totals = each(lambda o: jnp.sum(o.astype(F32), axis=0, keepdims=True), onehot)
    ahead, base = [], carry[...]
    for w, tot in zip(within, totals):
        ahead.append(w + base)
        base = base + tot
    carry[...] = base
    cnt_ref[...] = base
    r1 = each(lambda hit, a: rowsum(jnp.where(hit, a, 0.0)), hit1, ahead)
    r2 = each(lambda hit, a: rowsum(jnp.where(hit, a, 0.0)), hit2, ahead)

    def record(*vals):
        meta = jnp.zeros((ROUTE_CHUNK, LANES_V7X), F32)
        for col, val in zip((META_E1, META_E2, META_R1, META_R2, META_W1, META_W2), vals):
            meta = jnp.where(lane == col, val.astype(F32), meta)
        return meta

    meta = each(record, e1, e2, r1, r2, w1, w2)
    metat = each(lambda mt: lax.dot_general(sel_ref[...], mt, (((1,), (1,)), ((), ())), preferred_element_type=F32,
                                            precision=lax.Precision.HIGHEST), meta)
    for c, mt, mtt in zip(chunks, meta, metat):
        meta_ref[c, :] = mt
        metat_ref[:, c] = mtt


def _token_rows(ref, first_token, n_tokens):
    start = pl.multiple_of(first_token * ROW_CHUNKS, ROW_CHUNKS)
    return ref.at[pl.ds(start, n_tokens * ROW_CHUNKS)]


def _dispatch_kernel(fill_ref, n_ref, off_ref, dst_ref, loc1_ref, loc2_ref, hf_ref, xs_ref, zeros, buf, sem_z, sems):
    i = pl.program_id(0)
    steps = pl.num_programs(0)
    slot_tokens = buf.shape[1] // ROW_CHUNKS

    @pl.when(i == 0)
    def _():
        zeros[...] = jnp.zeros_like(zeros)

        def tile_fill(e):
            return pltpu.make_async_copy(zeros, _token_rows(xs_ref, pl.multiple_of(fill_ref[e], TM_GROUP), TM_GROUP),
                                         sem_z)

        for e in range(2 * N_EXPERTS):
            @pl.when(fill_ref[e] >= 0)
            def _():
                tile_fill(e).start()
        for e in range(2 * N_EXPERTS):
            @pl.when(fill_ref[e] >= 0)
            def _():
                tile_fill(e).wait()

    k = lax.broadcasted_iota(jnp.int32, (slot_tokens, hf_ref.shape[0]), 0)
    onehot = ((k == loc1_ref[...]) | (k == loc2_ref[...])).astype(BF16)
    sorted_rows = jnp.dot(onehot, hf_ref[...], preferred_element_type=F32)

    def run(slot):
        whole_slot = pltpu.make_async_copy(buf.at[slot], _token_rows(xs_ref, 0, slot_tokens), sems.at[slot])

        @pl.when(i >= 2)
        def _():
            whole_slot.wait()

        _to_token_tiles(buf.at[slot], sorted_rows)
        for e in range(N_EXPERTS):
            n = n_ref[i * N_EXPERTS + e]
            off = off_ref[i * N_EXPERTS + e]
            dst = dst_ref[i * N_EXPERTS + e]
            bit = hf_ref.shape[0]
            while bit >= 1:
                head = n & ~(2 * bit - 1)

                @pl.when((n & bit) != 0)
                def _(bit=bit, head=head):
                    pltpu.make_async_copy(_token_rows(buf.at[slot], off + head, bit),
                                          _token_rows(xs_ref, dst + head, bit), sems.at[slot]).start()
                bit //= 2

        @pl.when(i == steps - 1)
        def _():
            whole_slot.wait()
            other = pltpu.make_async_copy(buf.at[1 - slot], _token_rows(xs_ref, 0, slot_tokens), sems.at[1 - slot])

            @pl.when(steps >= 2)
            def _():
                other.wait()

    for slot in range(2):
        pl.when(i % 2 == slot)(functools.partial(run, slot))


def _dispatch(fill_start, n_tab, off_tab, dst_tab, loc1, loc2, hf, n_rows):
    t = hf.shape[0]
    loc = pl.BlockSpec((1, TM), lambda i, *_: (0, i))
    return pl.pallas_call(
        _dispatch_kernel,
        grid_spec=pltpu.PrefetchScalarGridSpec(
            num_scalar_prefetch=4,
            grid=(t // TM,),
            in_specs=[loc, loc, pl.BlockSpec((TM, D_MODEL), lambda i, *_: (i, 0))],
            out_specs=pl.BlockSpec(memory_space=pl.ANY),
            scratch_shapes=[pltpu.VMEM((TM_GROUP * ROW_CHUNKS, LANES_V7X), F32),
                            pltpu.VMEM((2, 2 * TM * ROW_CHUNKS, LANES_V7X), F32),
                            pltpu.SemaphoreType.DMA(()), pltpu.SemaphoreType.DMA((2,))]),
        out_shape=jax.ShapeDtypeStruct((n_rows * ROW_CHUNKS, LANES_V7X), F32),
        compiler_params=_params(("arbitrary",), 40),
        name="moe_dispatch",
    )(fill_start, n_tab, off_tab, dst_tab, loc1.reshape(1, t), loc2.reshape(1, t), hf)


def _grouped_kernel(te_ref, used_ref, x_ref, wg_ref, wu_ref, wd_ref, y_ref):
    i = pl.program_id(0)

    @pl.when(i < used_ref[0])
    def _():
        h = _from_token_tiles(x_ref).astype(BF16)
        ff = wg_ref.shape[1] // FF_SPLIT
        cols = [slice(c * ff, (c + 1) * ff) for c in range(FF_SPLIT)]
        gates = [jnp.dot(h, wg_ref[:, c], preferred_element_type=F32) for c in cols]
        ups = [jnp.dot(h, wu_ref[:, c], preferred_element_type=F32) for c in cols]
        total = None
        for c, gate, up in zip(cols, gates, ups):
            act = (gate * jax.nn.sigmoid(gate) * up).astype(BF16)
            part = jnp.dot(act, wd_ref[c, :], preferred_element_type=F32)
            total = part if total is None else total + part
        _to_token_tiles(y_ref, total)

    @pl.when(i >= used_ref[0])
    def _():
        y_ref[...] = jnp.zeros_like(y_ref)


def _grouped_swiglu(tile_expert, n_used, xs, wg, wu, wd):
    n_rows = xs.shape[0] // ROW_CHUNKS
    d_ff = wg.shape[2]
    tile = (TM_GROUP * ROW_CHUNKS, LANES_V7X)
    src = lambda i, te, used: (jnp.maximum(jnp.minimum(i, used[0] - 1), 0), 0)
    expert = lambda shape, buffers: pl.BlockSpec((None,) + shape, lambda i, te, used: (te[i], 0, 0),
                                                 pipeline_mode=pl.Buffered(buffers))
    return pl.pallas_call(
        _grouped_kernel,
        grid_spec=pltpu.PrefetchScalarGridSpec(
            num_scalar_prefetch=2,
            grid=(n_rows // TM_GROUP,),
            in_specs=[pl.BlockSpec(tile, src), expert((D_MODEL, d_ff), 1), expert((D_MODEL, d_ff), 2),
                      expert((d_ff, D_MODEL), 2)],
            out_specs=pl.BlockSpec(tile, lambda i, te, used: (i, 0))),
        out_shape=jax.ShapeDtypeStruct((n_rows * ROW_CHUNKS, LANES_V7X), F32),
        compiler_params=_params(("arbitrary",), 58),
        name="moe_grouped_swiglu",
    )(tile_expert, n_used, xs, wg, wu, wd)


def _combine_kernel(n_ref, off_ref, dst_ref, loc1_ref, loc2_ref, x_ref, meta_ref, g_ref, y_ref, o_ref,
                    slab, y1, y2, sems):
    i = pl.program_id(0)
    tm = x_ref.shape[0]

    def fetch(tile, slot):
        for e in range(N_EXPERTS):
            n = n_ref[tile * N_EXPERTS + e]
            off = off_ref[tile * N_EXPERTS + e]
            dst = dst_ref[tile * N_EXPERTS + e]
            bit = tm
            while bit >= 1:
                head = n & ~(2 * bit - 1)

                @pl.when((n & bit) != 0)
                def _(bit=bit, head=head):
                    pltpu.make_async_copy(_token_rows(y_ref, dst + head, bit),
                                          _token_rows(slab.at[slot], off + head, bit), sems.at[slot]).start()
                bit //= 2

    def finish(slot):
        pltpu.make_async_copy(_token_rows(y_ref, 0, 2 * tm), slab.at[slot], sems.at[slot]).wait()

        def pick(r, carry):
            y1[pl.ds(pl.multiple_of(r * ROW_CHUNKS, ROW_CHUNKS), ROW_CHUNKS), :] = (
                _token_rows(slab.at[slot], loc1_ref[r], 1)[...])
            y2[pl.ds(pl.multiple_of(r * ROW_CHUNKS, ROW_CHUNKS), ROW_CHUNKS), :] = (
                _token_rows(slab.at[slot], loc2_ref[r], 1)[...])
            return carry

        lax.fori_loop(0, tm, pick, 0, unroll=ISSUE_UNROLL)
        w1 = meta_ref[:, META_W1:META_W1 + 1]
        w2 = meta_ref[:, META_W2:META_W2 + 1]
        out = x_ref[...] + (w1 * _from_token_tiles(y1) + w2 * _from_token_tiles(y2))
        o_ref[...] = _rms(out, g_ref[...])

    @pl.when(i == 0)
    def _():
        fetch(0, 0)

    for slot in range(2):
        @pl.when(i % 2 == slot)
        def _():
            @pl.when(i + 1 < pl.num_programs(0))
            def _():
                fetch(i + 1, 1 - slot)
            finish(slot)


def _combine(n_tab, off_tab, dst_tab, loc1, loc2, x, meta, g, y):
    t = x.shape[0]
    idx = pl.BlockSpec((TM,), lambda i, *_: (i,), memory_space=pltpu.SMEM)
    row = lambda width: pl.BlockSpec((TM, width), lambda i, *_: (i, 0))
    return pl.pallas_call(
        _combine_kernel,
        grid_spec=pltpu.PrefetchScalarGridSpec(
            num_scalar_prefetch=3,
            grid=(t // TM,),
            in_specs=[idx, idx, row(D_MODEL), row(LANES_V7X),
                      pl.BlockSpec((1, D_MODEL), lambda i, *_: (0, 0), pipeline_mode=pl.Buffered(1)),
                      pl.BlockSpec(memory_space=pl.ANY)],
            out_specs=row(D_MODEL),
            scratch_shapes=[pltpu.VMEM((2, 2 * TM * ROW_CHUNKS, LANES_V7X), F32),
                            pltpu.VMEM((TM * ROW_CHUNKS, LANES_V7X), F32),
                            pltpu.VMEM((TM * ROW_CHUNKS, LANES_V7X), F32),
                            pltpu.SemaphoreType.DMA((2,))]),
        out_shape=jax.ShapeDtypeStruct((t, D_MODEL), F32),
        compiler_params=_params(("arbitrary",), 40),
        name="moe_combine_norm",
    )(n_tab, off_tab, dst_tab, loc1, loc2, x, meta, g, y)


def _route_operands(ffn_g, router_w):
    wr_pad = jnp.zeros((D_MODEL, LANES_V7X), BF16).at[:, :N_EXPERTS].set(router_w.astype(BF16))
    tri = jnp.tril(jnp.ones((ROUTE_CHUNK, ROUTE_CHUNK), BF16), -1)
    sel = jnp.eye(SUBLANES_V7X, LANES_V7X, dtype=F32)
    return ffn_g, wr_pad, tri, sel


def _moe_layer(x, hf, meta, metat, counts, base, wg, wu, wd, final_g):
    t = x.shape[0]
    steps = t // TM
    n_rows = 2 * t + N_EXPERTS * TM_GROUP

    cnt = counts[0, :N_EXPERTS].astype(jnp.int32)
    padded = (cnt + TM_GROUP - 1) // TM_GROUP * TM_GROUP
    ends = jnp.cumsum(padded)
    starts = ends - padded
    experts = jnp.arange(N_EXPERTS, dtype=jnp.int32)[:, None]
    e1, e2 = metat[META_E1].astype(jnp.int32), metat[META_E2].astype(jnp.int32)
    r1, r2 = metat[META_R1].astype(jnp.int32), metat[META_R2].astype(jnp.int32)
    pick = lambda e, table: jnp.sum(jnp.where(e[None, :] == experts, table, 0), axis=0)
    before = base.reshape(steps, SUBLANES_V7X, LANES_V7X)[:, 0, :N_EXPERTS].astype(jnp.int32)
    n_tab = jnp.concatenate([before[1:], cnt[None, :]], axis=0) - before
    off_tab = jnp.cumsum(n_tab, axis=1) - n_tab
    dst_tab = starts[None, :] + before
    shift = jnp.repeat((off_tab - before).T, TM, axis=1)
    loc1 = pick(e1, shift) + r1
    loc2 = pick(e2, shift) + r2
    n_used = (ends[-1] // TM_GROUP).astype(jnp.int32).reshape(1)
    tile_row = jnp.minimum(jnp.arange(n_rows // TM_GROUP, dtype=jnp.int32), n_used[0] - 1) * TM_GROUP
    tile_expert = jnp.sum(ends[None, :] <= tile_row[:, None], axis=1).astype(jnp.int32)
    tail_start = jnp.where(padded > 0, ends - TM_GROUP, -1)
    spare = ends[-1] + TM_GROUP * jnp.arange(N_EXPERTS, dtype=jnp.int32)
    fill_start = jnp.concatenate([tail_start, jnp.where(spare < n_rows, spare, -1)]).astype(jnp.int32)

    tables = [a.reshape(-1).astype(jnp.int32) for a in (n_tab, off_tab, dst_tab)]
    xs = _dispatch(fill_start, *tables, loc1, loc2, hf, n_rows)
    y = _grouped_swiglu(tile_expert, n_used, xs, wg, wu, wd)
    return _combine(*tables, loc1, loc2, x, meta, final_g, y)


def _arrange_in_proj(w):
    q = (w[:, :ATTN_WIDTH] * (1.0 / math.sqrt(HEAD_DIM))).astype(BF16)
    w = w.astype(BF16)
    dup = lambda start: [w[:, start + h * HEAD_DIM:start + (h + 1) * HEAD_DIM]
                         for h in range(ATTN_KV_HEADS) for _ in range(2)]
    return jnp.concatenate([q] + dup(ATTN_WIDTH) + dup(ATTN_WIDTH + KV_WIDTH) + [w[:, ATTN_WIDTH + 2 * KV_WIDTH:]],
                           axis=1)


def _block_diag(w):
    heads, d, _ = w.shape
    eye = jnp.eye(heads, dtype=w.dtype)
    return jnp.einsum('hij,hg->higj', w, eye).reshape(heads * d, heads * d)


def kernel(x, mem, rel_bias, mix_norm, w_in, attn_sinks, sc_conv_w, sc_conv_b, rg_conv_w, rg_conv_b, rg_w_a,
           rg_b_a, rg_w_x, rg_b_x, rg_lambda, w_out, xa_norm, mem_norm, xa_wq, xa_wk, xa_wv, xa_wo, ffn_norm,
           dense_wg, dense_wu, dense_wd, moe_router, moe_wg, moe_wu, moe_wd, final_norm):
    batch, seq, _ = x.shape
    depth = w_in.shape[0]
    assert depth == 2 and seq % TS == 0 and seq % TM == 0
    xt = x.reshape(batch * seq, D_MODEL)
    memt = mem.reshape(batch * MEM_LEN, D_MODEL)
    bias_tbl = _attention_bias_tables(rel_bias)
    vec = lambda a: a.reshape(1, -1)

    for layer in range(depth):
        w_gate = jnp.concatenate([_block_diag(rg_w_a[layer]), _block_diag(rg_w_x[layer])], axis=1).astype(BF16)
        b_gate = jnp.concatenate([rg_b_a[layer], rg_b_x[layer]]).reshape(1, -1)
        q, k, v, cr = _in_proj_conv(xt, vec(mix_norm[layer]), _arrange_in_proj(w_in[layer]),
                                    sc_conv_w[layer], vec(sc_conv_b[layer]), rg_conv_w[layer],
                                    vec(rg_conv_b[layer]), w_gate, b_gate, vec(rg_lambda[layer]), seq)
        attn = _attention(q, k, v, attn_sinks[layer], bias_tbl, seq // BLOCK)
        wkv = jnp.concatenate([xa_wk[layer], xa_wv[layer]], axis=1).astype(BF16)
        mk, mv = _mem_kv(memt, vec(mem_norm[layer]), wkv)
        post = functools.partial(_post_mixer, xt, attn, cr, w_out[layer].astype(BF16), vec(xa_norm[layer]),
                                 xa_wq[layer].astype(BF16), mk, mv, xa_wo[layer].astype(BF16), seq)

        j = layer // 2
        if layer % 2 == 0:
            (xt,) = post()
            xt, expert_w = _dense_ffn(xt, vec(ffn_norm[layer]), dense_wg[j].astype(BF16), dense_wu[j].astype(BF16),
                                      dense_wd[j].astype(BF16), cast_along=(moe_wg[j], moe_wu[j], moe_wd[j]))
        else:
            xt, hf, meta, metat, counts, base = post(route=_route_operands(vec(ffn_norm[layer]), moe_router[j]))
            xt = _moe_layer(xt, hf, meta, metat, counts, base, *expert_w, vec(final_norm))
    return xt.reshape(batch, seq, D_MODEL)
```

```python
import functools
import math

import jax
import jax.numpy as jnp
import numpy as np
from jax import lax
from jax.experimental import pallas as pl
from jax.experimental.pallas import tpu as pltpu

F32 = jnp.float32
BF16 = jnp.bfloat16

D_MODEL = 1024
MEM_LEN = 256
HEAD_DIM = 64
ATTN_Q_HEADS = 8
ATTN_KV_HEADS = 2
ATTN_WIDTH = ATTN_Q_HEADS * HEAD_DIM
KV_WIDTH = ATTN_KV_HEADS * HEAD_DIM
KV_DUP_WIDTH = 2 * KV_WIDTH
BLOCK = 128
SC_WIDTH = 256
SC_CONV = 3
RG_WIDTH = 256
RG_HEADS = 4
RG_HEAD_DIM = RG_WIDTH // RG_HEADS
RG_CONV = 4
RG_C = 8.0
N_BUCKETS = 32
MAX_EXACT = N_BUCKETS // 2
MAX_DISTANCE = 128
XA_HEADS = 4
XA_HEAD_DIM = 128
XA_WIDTH = XA_HEADS * XA_HEAD_DIM
N_EXPERTS = 8
EPS = 1e-6
NEG_INF = -1e30
REST_WIDTH = 3 * SC_WIDTH + 2 * RG_WIDTH

LANES_V7X = 128
SUBLANES_V7X = 8
VMEM_BYTES_V7X = 64 * 1024 * 1024
ROW_CHUNKS = D_MODEL // LANES_V7X
assert ROW_CHUNKS == SUBLANES_V7X

TM = 512
ATTN_BLOCKS = 8
TS = 512
ROUTE_CHUNK = 128
TM_GROUP = 512
ISSUE_UNROLL = 8
PACKED_ROWS = ROW_CHUNKS // 2
SORTED_TOKENS = 2 * TM + N_EXPERTS
FF_SPLIT = 2
CARRY_ROWS = SUBLANES_V7X


def _mib(n):
    return int(n * 1024 * 1024)


def _params(semantics, vmem_mib):
    assert _mib(vmem_mib) < VMEM_BYTES_V7X
    return pltpu.CompilerParams(dimension_semantics=semantics, vmem_limit_bytes=_mib(vmem_mib))


def _rms(x, g):
    ms = jnp.mean(x * x, axis=-1, keepdims=True)
    return x * lax.rsqrt(ms + EPS) * g


def _const_spec(shape):
    nd = len(shape)
    return pl.BlockSpec(shape, lambda *_: (0,) * nd, pipeline_mode=pl.Buffered(1))


def _attn_kernel(sink_ref, q_ref, kp_ref, kc_ref, vp_ref, vc_ref, bias0_ref, bias_ref, o_ref):
    pairs_per_group = ATTN_Q_HEADS // ATTN_KV_HEADS // 2
    row = lax.broadcasted_iota(jnp.int32, (BLOCK, BLOCK), 0)
    col = lax.broadcasted_iota(jnp.int32, (BLOCK, BLOCK), 1)
    from_prev = col > row
    low_lanes = lax.broadcasted_iota(jnp.int32, (2 * BLOCK, 2 * HEAD_DIM), 1) < HEAD_DIM
    low_out = col < HEAD_DIM
    zero = jnp.zeros((), BF16)

    def block_diag(band):
        return jnp.concatenate([jnp.where(low_lanes, band, zero), jnp.where(low_lanes, zero, band)], axis=0)

    def scores(blk):
        rows = slice(blk * BLOCK, (blk + 1) * BLOCK)
        prev_rows = slice((blk - 1) * BLOCK, blk * BLOCK)
        out = []
        for g in range(ATTN_KV_HEADS):
            lanes = slice(g * 2 * HEAD_DIM, (g + 1) * 2 * HEAD_DIM)
            k_prev = kp_ref[:, lanes] if blk == 0 else kc_ref[prev_rows, lanes]
            v_prev = vp_ref[:, lanes] if blk == 0 else vc_ref[prev_rows, lanes]
            k_bd = block_diag(jnp.concatenate([k_prev, kc_ref[rows, lanes]], axis=0))
            v_bd = block_diag(jnp.concatenate([v_prev, vc_ref[rows, lanes]], axis=0))
            for pair in range(pairs_per_group):
                slab = g * pairs_per_group + pair
                q2 = q_ref[rows, slab * 2 * HEAD_DIM:(slab + 1) * 2 * HEAD_DIM]
                s = lax.dot_general(q2, k_bd, (((1,), (1,)), ((), ())), preferred_element_type=F32)
                out.append((slab, s, v_bd))
        return out

    def finish(blk, scored):
        rows = slice(blk * BLOCK, (blk + 1) * BLOCK)
        tbl_ref = bias0_ref if blk == 0 else bias_ref
        staged = []
        for slab, s, v_bd in scored:
            probs, denoms = [], []
            for side in range(2):
                h = 2 * slab + side
                sh = s[:, side * 2 * BLOCK:(side + 1) * 2 * BLOCK]
                logits = jnp.where(from_prev, sh[:, :BLOCK], sh[:, BLOCK:]) + tbl_ref[h]
                sink = sink_ref[h]
                m = jnp.maximum(jnp.max(logits, axis=-1, keepdims=True), sink)
                p = jnp.exp(logits - m)
                denoms.append(jnp.sum(p, axis=-1, keepdims=True) + jnp.exp(sink - m))
                probs += [jnp.where(from_prev, p, 0.0), jnp.where(from_prev, 0.0, p)]
            staged.append((slab, jnp.concatenate(probs, axis=-1).astype(BF16), v_bd, denoms))
        for slab, p_band, v_bd, denoms in staged:
            o = jnp.dot(p_band, v_bd, preferred_element_type=F32)
            o = o / jnp.where(low_out, denoms[0], denoms[1])
            o_ref[rows, slab * 2 * HEAD_DIM:(slab + 1) * 2 * HEAD_DIM] = o.astype(BF16)

    pending = scores(0)
    for blk in range(ATTN_BLOCKS):
        upcoming = scores(blk + 1) if blk + 1 < ATTN_BLOCKS else None
        finish(blk, pending)
        pending = upcoming


def _attention(q, k, v, sinks, bias_tbl, blocks_per_seq):
    t = q.shape[0]
    tile = ATTN_BLOCKS * BLOCK
    cur = lambda i: (i, 0)
    prev = lambda i: (jnp.maximum(i * ATTN_BLOCKS - 1, 0), 0)
    tbl = (None, ATTN_Q_HEADS, BLOCK, BLOCK)
    return pl.pallas_call(
        _attn_kernel,
        grid=(t // tile,),
        in_specs=[pl.BlockSpec(memory_space=pltpu.SMEM),
                  pl.BlockSpec((tile, ATTN_WIDTH), cur),
                  pl.BlockSpec((BLOCK, KV_DUP_WIDTH), prev),
                  pl.BlockSpec((tile, KV_DUP_WIDTH), cur),
                  pl.BlockSpec((BLOCK, KV_DUP_WIDTH), prev),
                  pl.BlockSpec((tile, KV_DUP_WIDTH), cur),
                  pl.BlockSpec(tbl, lambda i: (jnp.minimum((i * ATTN_BLOCKS) % blocks_per_seq, 1), 0, 0, 0)),
                  pl.BlockSpec(tbl, lambda i: (1, 0, 0, 0))],
        out_specs=pl.BlockSpec((tile, ATTN_WIDTH), cur),
        out_shape=jax.ShapeDtypeStruct((t, ATTN_WIDTH), BF16),
        compiler_params=_params(("parallel",), 24),
        name="swa_attention",
    )(sinks, q, k, k, v, v, bias_tbl, bias_tbl)


def _bias_table_kernel(rel_ref, bucket_ref, o_ref):
    for v in range(2):
        bucket = bucket_ref[v]
        hits = [bucket == b for b in range(N_BUCKETS)]
        for h in range(ATTN_Q_HEADS):
            tbl = jnp.full(bucket.shape, NEG_INF, F32)
            for b in range(N_BUCKETS):
                tbl = jnp.where(hits[b], rel_ref[b * ATTN_Q_HEADS + h], tbl)
            o_ref[v, h] = tbl


def _attention_bias_tables(rel_bias):
    q_idx = np.arange(BLOCK)[:, None]
    j_idx = np.arange(BLOCK)[None, :]
    from_prev = j_idx > q_idx
    n = np.where(from_prev, q_idx + BLOCK - j_idx, q_idx - j_idx)
    large = MAX_EXACT + (np.log(np.maximum(n, 1).astype(np.float32) / np.float32(MAX_EXACT))
                         / np.float32(math.log(MAX_DISTANCE / MAX_EXACT))
                         * np.float32(N_BUCKETS - MAX_EXACT)).astype(np.int32)
    bucket = np.where(n < MAX_EXACT, n, np.minimum(large, N_BUCKETS - 1))
    first = np.where(from_prev, -1, bucket)
    buckets = jnp.asarray(np.stack([first, bucket]).astype(np.int32))
    return pl.pallas_call(
        _bias_table_kernel,
        in_specs=[pl.BlockSpec(memory_space=pltpu.SMEM), pl.BlockSpec(memory_space=pltpu.VMEM)],
        out_specs=pl.BlockSpec(memory_space=pltpu.VMEM),
        out_shape=jax.ShapeDtypeStruct((2, ATTN_Q_HEADS, BLOCK, BLOCK), F32),
        name="t5_bias_table",
    )(rel_bias.astype(F32).reshape(-1), buckets)


def _shift_rows(x, s, fill):
    return jnp.concatenate([jnp.full((s, x.shape[1]), fill, x.dtype), x[:x.shape[0] - s]], axis=0)


def _in_proj_conv_kernel(x_ref, g_ref, w_ref, scw_ref, scb_ref, rgw_ref, rgb_ref, wgate_ref, bgate_ref, lam_ref,
                         q_ref, k_ref, v_ref, o_ref, r_ref, sc_ext, rg_ext, h_carry, *, tiles_per_seq):
    i = pl.program_id(0)
    ts = r_ref.shape[0]
    c0 = CARRY_ROWS

    @pl.when(i == 0)
    def _():
        r_ref[...] = jnp.zeros_like(r_ref)

    @pl.when((i == 0) | ((i + tiles_per_seq - 1) % tiles_per_seq == 0))
    def _():
        sc_ext[0:c0, :] = jnp.zeros((c0, SC_WIDTH), F32)
        rg_ext[0:c0, :] = jnp.zeros((c0, RG_WIDTH), F32)
        h_carry[...] = jnp.zeros_like(h_carry)

    attn_cols = ATTN_WIDTH + 2 * KV_DUP_WIDTH
    hx = _rms(x_ref[...], g_ref[...]).astype(BF16)
    p = jnp.dot(hx, w_ref[:, :attn_cols], preferred_element_type=F32)
    q_ref[...] = p[:, :ATTN_WIDTH].astype(BF16)
    k_ref[...] = p[:, ATTN_WIDTH:ATTN_WIDTH + KV_DUP_WIDTH].astype(BF16)
    v_ref[...] = p[:, ATTN_WIDTH + KV_DUP_WIDTH:].astype(BF16)

    sc_b = r_ref[:, 0:SC_WIDTH]
    sc_ext[c0:c0 + ts, :] = r_ref[:, SC_WIDTH:2 * SC_WIDTH] * r_ref[:, 2 * SC_WIDTH:3 * SC_WIDTH]
    rg_ext[c0:c0 + ts, :] = r_ref[:, 3 * SC_WIDTH:3 * SC_WIDTH + RG_WIDTH]
    rg_g = r_ref[:, 3 * SC_WIDTH + RG_WIDTH:]

    conv = scb_ref[...]
    for k in range(SC_CONV):
        off = c0 - (SC_CONV - 1) + k
        conv = conv + scw_ref[k:k + 1, :] * sc_ext[off:off + ts, :]
    conv_out = sc_b * conv

    rg_in = rgb_ref[...]
    for k in range(RG_CONV):
        off = c0 - (RG_CONV - 1) + k
        rg_in = rg_in + rgw_ref[k:k + 1, :] * rg_ext[off:off + ts, :]

    sc_ext[0:c0, :] = sc_ext[ts:ts + c0, :]
    rg_ext[0:c0, :] = rg_ext[ts:ts + c0, :]

    gates = jnp.dot(rg_in.astype(BF16), wgate_ref[...], preferred_element_type=F32) + bgate_ref[...]

    r_ref[...] = jnp.dot(hx, w_ref[:, attn_cols:], preferred_element_type=F32)

    r_gate = jax.nn.sigmoid(gates[:, :RG_WIDTH])
    i_gate = jax.nn.sigmoid(gates[:, RG_WIDTH:])
    neg_lam = -lam_ref[...]
    softplus = jnp.maximum(neg_lam, 0.0) + jnp.log1p(jnp.exp(-jnp.abs(neg_lam)))
    log_a = -RG_C * r_gate * softplus
    a = jnp.exp(log_a)
    u = jnp.sqrt(jnp.tanh(-log_a) * (1.0 + a * a)) * (i_gate * rg_in)

    s = 1
    while s < ts:
        u = a * _shift_rows(u, s, 0.0) + u
        a = a * _shift_rows(a, s, 1.0)
        s *= 2
    h = a * h_carry[...] + u
    h_carry[...] = h[ts - 1:ts, :]

    c = math.sqrt(2.0 / math.pi)
    gelu = 0.5 * rg_g * (1.0 + jnp.tanh(c * (rg_g + 0.044715 * (rg_g * rg_g * rg_g))))
    o_ref[:, 0:SC_WIDTH] = conv_out.astype(BF16)
    o_ref[:, SC_WIDTH:] = (h * gelu).astype(BF16)


def _in_proj_conv(x, g, w, sc_w, sc_b, rg_w, rg_b, w_gate, b_gate, lam, seq):
    t = x.shape[0]
    n = w.shape[1]
    tiles = t // TS
    assert n == ATTN_WIDTH + 2 * KV_DUP_WIDTH + REST_WIDTH
    proj = lambda width: pl.BlockSpec((TS, width), lambda i: (jnp.minimum(i, tiles - 1), 0))
    lagged = pl.BlockSpec((TS, SC_WIDTH + RG_WIDTH), lambda i: (jnp.maximum(i - 1, 0), 0))
    return pl.pallas_call(
        functools.partial(_in_proj_conv_kernel, tiles_per_seq=seq // TS),
        grid=(tiles + 1,),
        in_specs=[proj(D_MODEL), _const_spec((1, D_MODEL)), _const_spec((D_MODEL, n)),
                  _const_spec((SC_CONV, SC_WIDTH)), _const_spec((1, SC_WIDTH)),
                  _const_spec((RG_CONV, RG_WIDTH)), _const_spec((1, RG_WIDTH)),
                  _const_spec((RG_WIDTH, 2 * RG_WIDTH)), _const_spec((1, 2 * RG_WIDTH)),
                  _const_spec((1, RG_WIDTH))],
        out_specs=[proj(ATTN_WIDTH), proj(KV_DUP_WIDTH), proj(KV_DUP_WIDTH), lagged],
        out_shape=[jax.ShapeDtypeStruct((t, ATTN_WIDTH), BF16),
                   jax.ShapeDtypeStruct((t, KV_DUP_WIDTH), BF16),
                   jax.ShapeDtypeStruct((t, KV_DUP_WIDTH), BF16),
                   jax.ShapeDtypeStruct((t, SC_WIDTH + RG_WIDTH), BF16)],
        scratch_shapes=[pltpu.VMEM((TS, REST_WIDTH), F32),
                        pltpu.VMEM((TS + 2 * CARRY_ROWS, SC_WIDTH), F32),
                        pltpu.VMEM((TS + 2 * CARRY_ROWS, RG_WIDTH), F32),
                        pltpu.VMEM((1, RG_WIDTH), F32)],
        compiler_params=_params(("arbitrary",), 48),
        name="in_proj_conv_rglru",
    )(x, g, w, sc_w, sc_b, rg_w, rg_b, w_gate, b_gate, lam)


def _mem_kv_kernel(m_ref, g_ref, w_ref, k_ref, v_ref):
    h = _rms(m_ref[...], g_ref[...]).astype(BF16)
    p = jnp.dot(h, w_ref[...], preferred_element_type=F32)
    k_ref[...] = p[:, :XA_WIDTH].astype(BF16)
    v_ref[...] = p[:, XA_WIDTH:].astype(BF16)


def _mem_kv(mem, g, wkv):
    t = mem.shape[0]
    row = lambda width: pl.BlockSpec((MEM_LEN, width), lambda i: (i, 0))
    return pl.pallas_call(
        _mem_kv_kernel,
        grid=(t // MEM_LEN,),
        in_specs=[row(D_MODEL), _const_spec((1, D_MODEL)), _const_spec((D_MODEL, 2 * XA_WIDTH))],
        out_specs=[row(XA_WIDTH), row(XA_WIDTH)],
        out_shape=[jax.ShapeDtypeStruct((t, XA_WIDTH), BF16)] * 2,
        compiler_params=_params(("parallel",), 24),
        name="mem_kv",
    )(mem, g, wkv)


def _post_mixer_kernel(x_ref, a_ref, c_ref, wout_ref, g_ref, wq_ref, k_ref, v_ref, wo_ref, *rest):
    o_ref = rest[-1] if len(rest) == 1 else rest[4]
    halves = [slice(i * (TM // 2), (i + 1) * (TM // 2)) for i in range(2)]
    heads = [slice(hd * XA_HEAD_DIM, (hd + 1) * XA_HEAD_DIM) for hd in range(XA_HEADS)]
    nt = (((1,), (1,)), ((), ()))
    k = k_ref[...]
    v = v_ref[...]

    x1 = [x_ref[hs, :] + jnp.dot(jnp.concatenate([a_ref[hs, :], c_ref[hs, :]], axis=-1), wout_ref[...],
                                 preferred_element_type=F32) for hs in halves]
    q = [jnp.dot(_rms(xh, g_ref[...]).astype(BF16), wq_ref[...], preferred_element_type=F32).astype(BF16)
         for xh in x1]
    scores = [[lax.dot_general(qh[:, sl], k[:, sl], nt, preferred_element_type=F32) for sl in heads] for qh in q]
    x2 = []
    for xh, per_head in zip(x1, scores):
        probs, sums = [], []
        for s in per_head:
            s = s * (1.0 / math.sqrt(XA_HEAD_DIM))
            p = jnp.exp(s - jnp.max(s, axis=-1, keepdims=True))
            probs.append(p.astype(BF16))
            sums.append(jnp.sum(p, axis=-1, keepdims=True))
        att = jnp.concatenate([jnp.dot(p, v[:, sl], preferred_element_type=F32) / l
                               for p, sl, l in zip(probs, heads, sums)], axis=-1).astype(BF16)
        x2.append(xh + jnp.dot(att, wo_ref[...], preferred_element_type=F32))

    for hs, xh in zip(halves, x2):
        o_ref[hs, :] = xh
    if len(rest) > 1:
        ffn_g_ref, wr_ref, tri_ref, sel_ref, _, hf_ref, meta_ref, metat_ref, cnt_ref, base_ref, carry = rest
        h = _rms(jnp.concatenate(x2, axis=0), ffn_g_ref[...])
        _route(h, wr_ref, tri_ref, sel_ref, hf_ref, meta_ref, metat_ref, cnt_ref, base_ref, carry)


def _post_mixer(x, attn, cr, w_out, g, wq, k, v, wo, seq, route=None):
    t = x.shape[0]
    per_seq = seq // TM
    row = lambda width: pl.BlockSpec((TM, width), lambda i: (i, 0))
    mem_blk = pl.BlockSpec((MEM_LEN, XA_WIDTH), lambda i: (i // per_seq, 0))
    in_specs = [row(D_MODEL), row(ATTN_WIDTH), row(SC_WIDTH + RG_WIDTH), _const_spec((D_MODEL, D_MODEL)),
                _const_spec((1, D_MODEL)), _const_spec((D_MODEL, XA_WIDTH)), mem_blk, mem_blk,
                _const_spec((XA_WIDTH, D_MODEL))]
    out_specs = [row(D_MODEL)]
    out_shape = [jax.ShapeDtypeStruct((t, D_MODEL), F32)]
    scratch = []
    args = [x, attn, cr, w_out, g, wq, k, v, wo]
    if route is not None:
        assert TM % ROUTE_CHUNK == 0
        in_specs += [_const_spec((1, D_MODEL)), _const_spec((D_MODEL, LANES_V7X)),
                     _const_spec((ROUTE_CHUNK, ROUTE_CHUNK)), _const_spec((SUBLANES_V7X, LANES_V7X))]
        out_specs += [row(D_MODEL), row(LANES_V7X),
                      pl.BlockSpec((SUBLANES_V7X, TM), lambda i: (0, i)),
                      pl.BlockSpec((1, LANES_V7X), lambda i: (0, 0)),
                      pl.BlockSpec((SUBLANES_V7X, LANES_V7X), lambda i: (i, 0))]
        out_shape += [jax.ShapeDtypeStruct((t, D_MODEL), BF16),
                      jax.ShapeDtypeStruct((t, LANES_V7X), F32),
                      jax.ShapeDtypeStruct((SUBLANES_V7X, t), F32),
                      jax.ShapeDtypeStruct((1, LANES_V7X), F32),
                      jax.ShapeDtypeStruct((t // TM * SUBLANES_V7X, LANES_V7X), F32)]
        scratch = [pltpu.VMEM((1, LANES_V7X), F32)]
        args += list(route)
    return pl.pallas_call(
        _post_mixer_kernel,
        grid=(t // TM,),
        in_specs=in_specs,
        out_specs=out_specs,
        out_shape=out_shape,
        scratch_shapes=scratch,
        compiler_params=_params(("arbitrary",), 40),
        name="post_mixer_route" if route is not None else "post_mixer",
    )(*args)


def _ffn_kernel(x_ref, g_ref, wg_ref, wu_ref, wd_ref, *rest):
    n_cast = (len(rest) - 1) // 2
    o_ref = rest[n_cast]
    x = x_ref[...]
    h = _rms(x, g_ref[...]).astype(BF16)
    gate = jnp.dot(h, wg_ref[...], preferred_element_type=F32)
    up = jnp.dot(h, wu_ref[...], preferred_element_type=F32)
    act = (gate * jax.nn.sigmoid(gate) * up).astype(BF16)
    o_ref[...] = x + jnp.dot(act, wd_ref[...], preferred_element_type=F32)
    for src, dst in zip(rest[:n_cast], rest[n_cast + 1:]):
        dst[...] = src[...].astype(BF16)


def _dense_ffn(x, g, wg, wu, wd, cast_along=()):
    t = x.shape[0]
    steps = t // TM
    d_ff = wg.shape[1]
    row = pl.BlockSpec((TM, D_MODEL), lambda i: (i, 0))
    flat = [w.reshape(-1, w.shape[-1]) for w in cast_along]
    for w in flat:
        assert w.shape[0] % (steps * 2 * SUBLANES_V7X) == 0
    slabs = [pl.BlockSpec((w.shape[0] // steps, w.shape[1]), lambda i: (i, 0)) for w in flat]
    outs = pl.pallas_call(
        _ffn_kernel,
        grid=(steps,),
        in_specs=[row, _const_spec((1, D_MODEL)), _const_spec((D_MODEL, d_ff)), _const_spec((D_MODEL, d_ff)),
                  _const_spec((d_ff, D_MODEL))] + slabs,
        out_specs=[row] + slabs,
        out_shape=[jax.ShapeDtypeStruct((t, D_MODEL), F32)] + [jax.ShapeDtypeStruct(w.shape, BF16) for w in flat],
        compiler_params=_params(("parallel",), 60),
        name="dense_swiglu",
    )(x, g, wg, wu, wd, *flat)
    return outs[0], [o.reshape(w.shape) for o, w in zip(outs[1:], cast_along)]


META_E1, META_E2, META_R1, META_R2, META_W1, META_W2 = range(6)


def _to_token_tiles(ref, rows):
    m = rows.shape[0]
    for c in range(ROW_CHUNKS):
        ref[pl.ds(c, m, stride=ROW_CHUNKS), :] = rows[:, c * LANES_V7X:(c + 1) * LANES_V7X]


def _from_token_tiles(ref):
    m = ref.shape[0] // ROW_CHUNKS
    return jnp.concatenate([ref[pl.ds(c, m, stride=ROW_CHUNKS), :] for c in range(ROW_CHUNKS)], axis=-1)


def _route(h, wr_ref, tri_ref, sel_ref, hf_ref, meta_ref, metat_ref, cnt_ref, base_ref, carry):
    @pl.when(pl.program_id(0) == 0)
    def _():
        carry[...] = jnp.zeros_like(carry)

    base_ref[...] = jnp.broadcast_to(carry[...], base_ref.shape)
    chunks = [slice(c * ROUTE_CHUNK, (c + 1) * ROUTE_CHUNK) for c in range(h.shape[0] // ROUTE_CHUNK)]
    each = lambda fn, *lists: [fn(*vals) for vals in zip(*lists)]
    rowmax = lambda a: jnp.max(a, axis=-1, keepdims=True)
    rowsum = lambda a: jnp.sum(a, axis=-1, keepdims=True)
    lane = lax.broadcasted_iota(jnp.int32, (ROUTE_CHUNK, LANES_V7X), 1)
    first_hit = lambda lg, m: jnp.min(jnp.where(lg == m, lane, LANES_V7X), axis=-1, keepdims=True)

    hb = h.astype(BF16)
    hf_ref[...] = hb
    logits = [jnp.dot(hb[c, :], wr_ref[...], preferred_element_type=F32) for c in chunks]
    lg = each(lambda l: jnp.where(lane < N_EXPERTS, l, -jnp.inf), logits)
    m1 = each(rowmax, lg)
    e1 = each(first_hit, lg, m1)
    lg2 = each(lambda l, e: jnp.where(lane == e, -jnp.inf, l), lg, e1)
    m2 = each(rowmax, lg2)
    e2 = each(first_hit, lg2, m2)
    ex = each(lambda a, b: jnp.exp(b - a), m1, m2)
    w1 = each(lambda e: 1.0 / (1.0 + e), ex)
    w2 = each(lambda e: e / (1.0 + e), ex)
    hit1 = each(lambda e: lane == e, e1)
    hit2 = each(lambda e: lane == e, e2)
    onehot = each(lambda a, b: (a | b).astype(BF16), hit1, hit2)
    within = each(lambda o: jnp.dot(tri_ref[...], o, preferred_element_type=F32), onehot)
    totals = each(lambda o: jnp.sum(o.astype(F32), axis=0, keepdims=True), onehot)
    ahead, base = [], carry[...]
    for w, tot in zip(within, totals):
        ahead.append(w + base)
        base = base + tot
    carry[...] = base
    cnt_ref[...] = base
    r1 = each(lambda hit, a: rowsum(jnp.where(hit, a, 0.0)), hit1, ahead)
    r2 = each(lambda hit, a: rowsum(jnp.where(hit, a, 0.0)), hit2, ahead)

    def record(*vals):
        meta = jnp.zeros((ROUTE_CHUNK, LANES_V7X), F32)
        for col, val in zip((META_E1, META_E2, META_R1, META_R2, META_W1, META_W2), vals):
            meta = jnp.where(lane == col, val.astype(F32), meta)
        return meta

    meta = each(record, e1, e2, r1, r2, w1, w2)
    metat = each(lambda mt: lax.dot_general(sel_ref[...], mt, (((1,), (1,)), ((), ())), preferred_element_type=F32,
                                            precision=lax.Precision.HIGHEST), meta)
    for c, mt, mtt in zip(chunks, meta, metat):
        meta_ref[c, :] = mt
        metat_ref[:, c] = mtt


def _token_rows(ref, first_token, n_tokens):
    start = pl.multiple_of(first_token * ROW_CHUNKS, ROW_CHUNKS)
    return ref.at[pl.ds(start, n_tokens * ROW_CHUNKS)]


def _packed_rows(ref, first_token, n_tokens):
    start = pl.multiple_of(first_token * PACKED_ROWS, SUBLANES_V7X)
    return ref.at[pl.ds(start, n_tokens * PACKED_ROWS)]


def _dispatch_kernel(fill_ref, n_ref, off_ref, dst_ref, loc1_ref, loc2_ref, hf_ref, xs_ref, zeros, buf, sem_z, sems):
    i = pl.program_id(0)
    steps = pl.num_programs(0)
    slot_tokens = buf.shape[1] // PACKED_ROWS
    runs = N_EXPERTS + 1

    @pl.when(i == 0)
    def _():
        zeros[...] = jnp.zeros_like(zeros)

        def tile_fill(e):
            return pltpu.make_async_copy(zeros, _packed_rows(xs_ref, pl.multiple_of(fill_ref[e], TM_GROUP), TM_GROUP),
                                         sem_z)

        for e in range(fill_ref.shape[0]):
            @pl.when(fill_ref[e] >= 0)
            def _():
                tile_fill(e).start()
        for e in range(fill_ref.shape[0]):
            @pl.when(fill_ref[e] >= 0)
            def _():
                tile_fill(e).wait()

    k = lax.broadcasted_iota(jnp.int32, (slot_tokens, hf_ref.shape[0]), 0)
    onehot = ((k == loc1_ref[...]) | (k == loc2_ref[...])).astype(BF16)
    sorted_rows = jnp.dot(onehot, hf_ref[...], preferred_element_type=F32)
    bits = pltpu.bitcast(sorted_rows, jnp.uint32)
    half = D_MODEL // 2
    packed = (bits[:, :half] & jnp.uint32(0xFFFF0000)) | (bits[:, half:] >> 16)

    def run(slot):
        whole_slot = pltpu.make_async_copy(buf.at[slot], _packed_rows(xs_ref, 0, slot_tokens), sems.at[slot])

        @pl.when(i >= 2)
        def _():
            whole_slot.wait()

        for c in range(PACKED_ROWS):
            buf.at[slot][pl.ds(c, slot_tokens, stride=PACKED_ROWS), :] = packed[:, c * LANES_V7X:(c + 1) * LANES_V7X]
        for e in range(runs):
            n = n_ref[i * runs + e]
            off = off_ref[i * runs + e]
            dst = dst_ref[i * runs + e]
            bit = hf_ref.shape[0]
            while bit >= 2:
                head = n & ~(2 * bit - 1)

                @pl.when((n & bit) != 0)
                def _(bit=bit, head=head):
                    pltpu.make_async_copy(_packed_rows(buf.at[slot], off + head, bit),
                                          _packed_rows(xs_ref, dst + head, bit), sems.at[slot]).start()
                bit //= 2

        @pl.when(i == steps - 1)
        def _():
            whole_slot.wait()
            other = pltpu.make_async_copy(buf.at[1 - slot], _packed_rows(xs_ref, 0, slot_tokens), sems.at[1 - slot])

            @pl.when(steps >= 2)
            def _():
                other.wait()

    for slot in range(2):
        pl.when(i % 2 == slot)(functools.partial(run, slot))


def _dispatch(fill_start, n_tab, off_tab, dst_tab, loc1, loc2, hf, total_rows):
    t = hf.shape[0]
    loc = pl.BlockSpec((1, TM), lambda i, *_: (0, i))
    u32 = jnp.uint32
    return pl.pallas_call(
        _dispatch_kernel,
        grid_spec=pltpu.PrefetchScalarGridSpec(
            num_scalar_prefetch=4,
            grid=(t // TM,),
            in_specs=[loc, loc, pl.BlockSpec((TM, D_MODEL), lambda i, *_: (i, 0))],
            out_specs=pl.BlockSpec(memory_space=pl.ANY),
            scratch_shapes=[pltpu.VMEM((TM_GROUP * PACKED_ROWS, LANES_V7X), u32),
                            pltpu.VMEM((2, SORTED_TOKENS * PACKED_ROWS, LANES_V7X), u32),
                            pltpu.SemaphoreType.DMA(()), pltpu.SemaphoreType.DMA((2,))]),
        out_shape=jax.ShapeDtypeStruct((total_rows * PACKED_ROWS, LANES_V7X), u32),
        compiler_params=_params(("arbitrary",), 40),
        name="moe_dispatch",
    )(fill_start, n_tab, off_tab, dst_tab, loc1.reshape(1, t), loc2.reshape(1, t), hf)


def _grouped_kernel(te_ref, used_ref, x_ref, wg_ref, wu_ref, wd_ref, y_ref):
    i = pl.program_id(0)

    @pl.when(i < used_ref[0])
    def _():
        tm = x_ref.shape[0] // PACKED_ROWS
        words = jnp.concatenate([x_ref[pl.ds(c, tm, stride=PACKED_ROWS), :] for c in range(PACKED_ROWS)], axis=-1)
        high = pltpu.bitcast(words & jnp.uint32(0xFFFF0000), F32).astype(BF16)
        low = pltpu.bitcast(words << 16, F32).astype(BF16)
        h = jnp.concatenate([high, low], axis=-1)
        ff = wg_ref.shape[1] // FF_SPLIT
        cols = [slice(c * ff, (c + 1) * ff) for c in range(FF_SPLIT)]
        gates = [jnp.dot(h, wg_ref[:, c], preferred_element_type=F32) for c in cols]
        ups = [jnp.dot(h, wu_ref[:, c], preferred_element_type=F32) for c in cols]
        total = None
        for c, gate, up in zip(cols, gates, ups):
            act = (gate * jax.nn.sigmoid(gate) * up).astype(BF16)
            part = jnp.dot(act, wd_ref[c, :], preferred_element_type=F32)
            total = part if total is None else total + part
        _to_token_tiles(y_ref, total)

    @pl.when(i >= used_ref[0])
    def _():
        y_ref[...] = jnp.zeros_like(y_ref)


def _grouped_swiglu(tile_expert, n_used, xs, wg, wu, wd, n_rows):
    d_ff = wg.shape[2]
    tile = (TM_GROUP * ROW_CHUNKS, LANES_V7X)
    packed_tile = (TM_GROUP * PACKED_ROWS, LANES_V7X)
    src = lambda i, te, used: (jnp.maximum(jnp.minimum(i, used[0] - 1), 0), 0)
    expert = lambda shape, buffers: pl.BlockSpec((None,) + shape, lambda i, te, used: (te[i], 0, 0),
                                                 pipeline_mode=pl.Buffered(buffers))
    return pl.pallas_call(
        _grouped_kernel,
        grid_spec=pltpu.PrefetchScalarGridSpec(
            num_scalar_prefetch=2,
            grid=(n_rows // TM_GROUP,),
            in_specs=[pl.BlockSpec(packed_tile, src), expert((D_MODEL, d_ff), 1), expert((D_MODEL, d_ff), 2),
                      expert((d_ff, D_MODEL), 2)],
            out_specs=pl.BlockSpec(tile, lambda i, te, used: (i, 0))),
        out_shape=jax.ShapeDtypeStruct((n_rows * ROW_CHUNKS, LANES_V7X), F32),
        compiler_params=_params(("arbitrary",), 58),
        name="moe_grouped_swiglu",
    )(tile_expert, n_used, xs, wg, wu, wd)


def _combine_kernel(n_ref, off_ref, dst_ref, loc1_ref, loc2_ref, x_ref, meta_ref, g_ref, y_ref, o_ref,
                    slab, y1, y2, sems):
    i = pl.program_id(0)
    tm = x_ref.shape[0]

    def fetch(tile, slot):
        for e in range(N_EXPERTS):
            n = n_ref[tile * N_EXPERTS + e]
            off = off_ref[tile * N_EXPERTS + e]
            dst = dst_ref[tile * N_EXPERTS + e]
            bit = tm
            while bit >= 1:
                head = n & ~(2 * bit - 1)

                @pl.when((n & bit) != 0)
                def _(bit=bit, head=head):
                    pltpu.make_async_copy(_token_rows(y_ref, dst + head, bit),
                                          _token_rows(slab.at[slot], off + head, bit), sems.at[slot]).start()
                bit //= 2

    def finish(slot):
        pltpu.make_async_copy(_token_rows(y_ref, 0, 2 * tm), _token_rows(slab.at[slot], 0, 2 * tm),
                              sems.at[slot]).wait()

        def pick(r, carry):
            y1[pl.ds(pl.multiple_of(r * ROW_CHUNKS, ROW_CHUNKS), ROW_CHUNKS), :] = (
                _token_rows(slab.at[slot], loc1_ref[r], 1)[...])
            y2[pl.ds(pl.multiple_of(r * ROW_CHUNKS, ROW_CHUNKS), ROW_CHUNKS), :] = (
                _token_rows(slab.at[slot], loc2_ref[r], 1)[...])
            return carry

        lax.fori_loop(0, tm, pick, 0, unroll=ISSUE_UNROLL)
        w1 = meta_ref[:, META_W1:META_W1 + 1]
        w2 = meta_ref[:, META_W2:META_W2 + 1]
        out = x_ref[...] + (w1 * _from_token_tiles(y1) + w2 * _from_token_tiles(y2))
        o_ref[...] = _rms(out, g_ref[...])

    @pl.when(i == 0)
    def _():
        fetch(0, 0)

    for slot in range(2):
        @pl.when(i % 2 == slot)
        def _():
            @pl.when(i + 1 < pl.num_programs(0))
            def _():
                fetch(i + 1, 1 - slot)
            finish(slot)


def _combine(n_tab, off_tab, dst_tab, loc1, loc2, x, meta, g, y):
    t = x.shape[0]
    idx = pl.BlockSpec((TM,), lambda i, *_: (i,), memory_space=pltpu.SMEM)
    row = lambda width: pl.BlockSpec((TM, width), lambda i, *_: (i, 0))
    return pl.pallas_call(
        _combine_kernel,
        grid_spec=pltpu.PrefetchScalarGridSpec(
            num_scalar_prefetch=3,
            grid=(t // TM,),
            in_specs=[idx, idx, row(D_MODEL), row(LANES_V7X),
                      pl.BlockSpec((1, D_MODEL), lambda i, *_: (0, 0), pipeline_mode=pl.Buffered(1)),
                      pl.BlockSpec(memory_space=pl.ANY)],
            out_specs=row(D_MODEL),
            scratch_shapes=[pltpu.VMEM((2, SORTED_TOKENS * ROW_CHUNKS, LANES_V7X), F32),
                            pltpu.VMEM((TM * ROW_CHUNKS, LANES_V7X), F32),
                            pltpu.VMEM((TM * ROW_CHUNKS, LANES_V7X), F32),
                            pltpu.SemaphoreType.DMA((2,))]),
        out_shape=jax.ShapeDtypeStruct((t, D_MODEL), F32),
        compiler_params=_params(("arbitrary",), 40),
        name="moe_combine_norm",
    )(n_tab, off_tab, dst_tab, loc1, loc2, x, meta, g, y)


def _route_operands(ffn_g, router_w):
    wr_pad = jnp.zeros((D_MODEL, LANES_V7X), BF16).at[:, :N_EXPERTS].set(router_w.astype(BF16))
    tri = jnp.tril(jnp.ones((ROUTE_CHUNK, ROUTE_CHUNK), BF16), -1)
    sel = jnp.eye(SUBLANES_V7X, LANES_V7X, dtype=F32)
    return ffn_g, wr_pad, tri, sel


def _moe_layer(x, hf, meta, metat, counts, base, wg, wu, wd, final_g):
    t = x.shape[0]
    steps = t // TM
    n_rows = -(-(2 * t + N_EXPERTS * steps) // TM_GROUP) * TM_GROUP + N_EXPERTS * TM_GROUP

    cnt = counts[0, :N_EXPERTS].astype(jnp.int32)
    experts = jnp.arange(N_EXPERTS, dtype=jnp.int32)[:, None]
    e1, e2 = metat[META_E1].astype(jnp.int32), metat[META_E2].astype(jnp.int32)
    r1, r2 = metat[META_R1].astype(jnp.int32), metat[META_R2].astype(jnp.int32)
    pick = lambda e, table: jnp.sum(jnp.where(e[None, :] == experts, table, 0), axis=0)
    before = base.reshape(steps, SUBLANES_V7X, LANES_V7X)[:, 0, :N_EXPERTS].astype(jnp.int32)
    n_tab = jnp.concatenate([before[1:], cnt[None, :]], axis=0) - before
    n_even = n_tab + (n_tab & 1)
    padded = (jnp.sum(n_even, axis=0) + TM_GROUP - 1) // TM_GROUP * TM_GROUP
    ends = jnp.cumsum(padded)
    starts = ends - padded
    off_tab = jnp.cumsum(n_even, axis=1) - n_even
    dst_tab = starts[None, :] + jnp.cumsum(n_even, axis=0) - n_even
    shift = jnp.repeat((off_tab - before).T, TM, axis=1)
    loc1 = pick(e1, shift) + r1
    loc2 = pick(e2, shift) + r2
    sent = jnp.sum(n_even, axis=1, keepdims=True)
    scratch = n_rows + N_EXPERTS * jnp.arange(steps, dtype=jnp.int32)[:, None]
    send = [jnp.concatenate(pair, axis=1) for pair in ((n_even, SORTED_TOKENS - sent), (off_tab, sent),
                                                       (dst_tab, scratch))]
    n_used = (ends[-1] // TM_GROUP).astype(jnp.int32).reshape(1)
    tile_row = jnp.minimum(jnp.arange(n_rows // TM_GROUP, dtype=jnp.int32), n_used[0] - 1) * TM_GROUP
    tile_expert = jnp.sum(ends[None, :] <= tile_row[:, None], axis=1).astype(jnp.int32)
    tail_start = jnp.where(padded > 0, ends - TM_GROUP, -1)
    spare = ends[-1] + TM_GROUP * jnp.arange(2 * N_EXPERTS, dtype=jnp.int32)
    fill_start = jnp.concatenate([tail_start, jnp.where(spare < n_rows, spare, -1)]).astype(jnp.int32)

    flat = lambda tables: [a.reshape(-1).astype(jnp.int32) for a in tables]
    xs = _dispatch(fill_start, *flat(send), loc1, loc2, hf, n_rows + N_EXPERTS * steps)
    y = _grouped_swiglu(tile_expert, n_used, xs, wg, wu, wd, n_rows)
    return _combine(*flat((n_tab, off_tab, dst_tab)), loc1, loc2, x, meta, final_g, y)


def _arrange_in_proj(w):
    q = (w[:, :ATTN_WIDTH] * (1.0 / math.sqrt(HEAD_DIM))).astype(BF16)
    w = w.astype(BF16)
    dup = lambda start: [w[:, start + h * HEAD_DIM:start + (h + 1) * HEAD_DIM]
                         for h in range(ATTN_KV_HEADS) for _ in range(2)]
    return jnp.concatenate([q] + dup(ATTN_WIDTH) + dup(ATTN_WIDTH + KV_WIDTH) + [w[:, ATTN_WIDTH + 2 * KV_WIDTH:]],
                           axis=1)


def _block_diag(w):
    heads, d, _ = w.shape
    eye = jnp.eye(heads, dtype=w.dtype)
    return jnp.einsum('hij,hg->higj', w, eye).reshape(heads * d, heads * d)


def kernel(x, mem, rel_bias, mix_norm, w_in, attn_sinks, sc_conv_w, sc_conv_b, rg_conv_w, rg_conv_b, rg_w_a,
           rg_b_a, rg_w_x, rg_b_x, rg_lambda, w_out, xa_norm, mem_norm, xa_wq, xa_wk, xa_wv, xa_wo, ffn_norm,
           dense_wg, dense_wu, dense_wd, moe_router, moe_wg, moe_wu, moe_wd, final_norm):
    batch, seq, _ = x.shape
    depth = w_in.shape[0]
    assert depth == 2 and seq % TS == 0 and seq % TM == 0
    xt = x.reshape(batch * seq, D_MODEL)
    memt = mem.reshape(batch * MEM_LEN, D_MODEL)
    bias_tbl = _attention_bias_tables(rel_bias)
    vec = lambda a: a.reshape(1, -1)

    for layer in range(depth):
        w_gate = jnp.concatenate([_block_diag(rg_w_a[layer]), _block_diag(rg_w_x[layer])], axis=1).astype(BF16)
        b_gate = jnp.concatenate([rg_b_a[layer], rg_b_x[layer]]).reshape(1, -1)
        q, k, v, cr = _in_proj_conv(xt, vec(mix_norm[layer]), _arrange_in_proj(w_in[layer]),
                                    sc_conv_w[layer], vec(sc_conv_b[layer]), rg_conv_w[layer],
                                    vec(rg_conv_b[layer]), w_gate, b_gate, vec(rg_lambda[layer]), seq)
        attn = _attention(q, k, v, attn_sinks[layer], bias_tbl, seq // BLOCK)
        wkv = jnp.concatenate([xa_wk[layer], xa_wv[layer]], axis=1).astype(BF16)
        mk, mv = _mem_kv(memt, vec(mem_norm[layer]), wkv)
        post = functools.partial(_post_mixer, xt, attn, cr, w_out[layer].astype(BF16), vec(xa_norm[layer]),
                                 xa_wq[layer].astype(BF16), mk, mv, xa_wo[layer].astype(BF16), seq)

        j = layer // 2
        if layer % 2 == 0:
            (xt,) = post()
            xt, expert_w = _dense_ffn(xt, vec(ffn_norm[layer]), dense_wg[j].astype(BF16), dense_wu[j].astype(BF16),
                                      dense_wd[j].astype(BF16), cast_along=(moe_wg[j], moe_wu[j], moe_wd[j]))
        else:
            xt, hf, meta, metat, counts, base = post(route=_route_operands(vec(ffn_norm[layer]), moe_router[j]))
            xt = _moe_layer(xt, hf, meta, metat, counts, base, *expert_w, vec(final_norm))
    return xt.reshape(batch, seq, D_MODEL)
```

```python
import functools
import math

import jax
import jax.numpy as jnp
import numpy as np
from jax import lax
from jax.experimental import pallas as pl
from jax.experimental.pallas import tpu as pltpu

F32 = jnp.float32
BF16 = jnp.bfloat16

D_MODEL = 1024
MEM_LEN = 256
HEAD_DIM = 64
ATTN_Q_HEADS = 8
ATTN_KV_HEADS = 2
ATTN_WIDTH = ATTN_Q_HEADS * HEAD_DIM
KV_WIDTH = ATTN_KV_HEADS * HEAD_DIM
KV_DUP_WIDTH = 2 * KV_WIDTH
BLOCK = 128
SC_WIDTH = 256
SC_CONV = 3
RG_WIDTH = 256
RG_HEADS = 4
RG_HEAD_DIM = RG_WIDTH // RG_HEADS
RG_CONV = 4
RG_C = 8.0
N_BUCKETS = 32
MAX_EXACT = N_BUCKETS // 2
MAX_DISTANCE = 128
XA_HEADS = 4
XA_HEAD_DIM = 128
XA_WIDTH = XA_HEADS * XA_HEAD_DIM
N_EXPERTS = 8
EPS = 1e-6
NEG_INF = -1e30
REST_WIDTH = 3 * SC_WIDTH + 2 * RG_WIDTH

LANES_V7X = 128
SUBLANES_V7X = 8
VMEM_BYTES_V7X = 64 * 1024 * 1024
ROW_CHUNKS = D_MODEL // LANES_V7X
assert ROW_CHUNKS == SUBLANES_V7X

TM = 512
ATTN_BLOCKS = 8
TS = 512
ROUTE_CHUNK = 128
TM_GROUP = 512
ISSUE_UNROLL = 8
PACKED_ROWS = ROW_CHUNKS // 2
SORTED_TOKENS = 2 * TM + N_EXPERTS
FF_SPLIT = 2
CARRY_ROWS = SUBLANES_V7X


def _mib(n):
    return int(n * 1024 * 1024)


def _params(semantics, vmem_mib):
    assert _mib(vmem_mib) < VMEM_BYTES_V7X
    return pltpu.CompilerParams(dimension_semantics=semantics, vmem_limit_bytes=_mib(vmem_mib))


def _rms(x, g):
    ms = jnp.mean(x * x, axis=-1, keepdims=True)
    return x * lax.rsqrt(ms + EPS) * g


def _const_spec(shape):
    nd = len(shape)
    return pl.BlockSpec(shape, lambda *_: (0,) * nd, pipeline_mode=pl.Buffered(1))


def _attn_kernel(sink_ref, q_ref, kp_ref, kc_ref, vp_ref, vc_ref, bias0_ref, bias_ref, o_ref):
    pairs_per_group = ATTN_Q_HEADS // ATTN_KV_HEADS // 2
    row = lax.broadcasted_iota(jnp.int32, (BLOCK, BLOCK), 0)
    col = lax.broadcasted_iota(jnp.int32, (BLOCK, BLOCK), 1)
    from_prev = col > row
    low_lanes = lax.broadcasted_iota(jnp.int32, (2 * BLOCK, 2 * HEAD_DIM), 1) < HEAD_DIM
    low_out = col < HEAD_DIM
    zero = jnp.zeros((), BF16)

    def block_diag(band):
        return jnp.concatenate([jnp.where(low_lanes, band, zero), jnp.where(low_lanes, zero, band)], axis=0)

    def scores(blk):
        rows = slice(blk * BLOCK, (blk + 1) * BLOCK)
        prev_rows = slice((blk - 1) * BLOCK, blk * BLOCK)
        out = []
        for g in range(ATTN_KV_HEADS):
            lanes = slice(g * 2 * HEAD_DIM, (g + 1) * 2 * HEAD_DIM)
            k_prev = kp_ref[:, lanes] if blk == 0 else kc_ref[prev_rows, lanes]
            v_prev = vp_ref[:, lanes] if blk == 0 else vc_ref[prev_rows, lanes]
            k_bd = block_diag(jnp.concatenate([k_prev, kc_ref[rows, lanes]], axis=0))
            v_bd = block_diag(jnp.concatenate([v_prev, vc_ref[rows, lanes]], axis=0))
            for pair in range(pairs_per_group):
                slab = g * pairs_per_group + pair
                q2 = q_ref[rows, slab * 2 * HEAD_DIM:(slab + 1) * 2 * HEAD_DIM]
                s = lax.dot_general(q2, k_bd, (((1,), (1,)), ((), ())), preferred_element_type=F32)
                out.append((slab, s, v_bd))
        return out

    def finish(blk, scored):
        rows = slice(blk * BLOCK, (blk + 1) * BLOCK)
        tbl_ref = bias0_ref if blk == 0 else bias_ref
        staged = []
        for slab, s, v_bd in scored:
            probs, denoms = [], []
            for side in range(2):
                h = 2 * slab + side
                sh = s[:, side * 2 * BLOCK:(side + 1) * 2 * BLOCK]
                logits = jnp.where(from_prev, sh[:, :BLOCK], sh[:, BLOCK:]) + tbl_ref[h]
                sink = sink_ref[h]
                m = jnp.maximum(jnp.max(logits, axis=-1, keepdims=True), sink)
                p = jnp.exp(logits - m)
                denoms.append(jnp.sum(p, axis=-1, keepdims=True) + jnp.exp(sink - m))
                probs += [jnp.where(from_prev, p, 0.0), jnp.where(from_prev, 0.0, p)]
            staged.append((slab, jnp.concatenate(probs, axis=-1).astype(BF16), v_bd, denoms))
        for slab, p_band, v_bd, denoms in staged:
            o = jnp.dot(p_band, v_bd, preferred_element_type=F32)
            o = o / jnp.where(low_out, denoms[0], denoms[1])
            o_ref[rows, slab * 2 * HEAD_DIM:(slab + 1) * 2 * HEAD_DIM] = o.astype(BF16)

    pending = scores(0)
    for blk in range(ATTN_BLOCKS):
        upcoming = scores(blk + 1) if blk + 1 < ATTN_BLOCKS else None
        finish(blk, pending)
        pending = upcoming


def _attention(q, k, v, sinks, bias_tbl, blocks_per_seq):
    t = q.shape[0]
    tile = ATTN_BLOCKS * BLOCK
    cur = lambda i: (i, 0)
    prev = lambda i: (jnp.maximum(i * ATTN_BLOCKS - 1, 0), 0)
    tbl = (None, ATTN_Q_HEADS, BLOCK, BLOCK)
    return pl.pallas_call(
        _attn_kernel,
        grid=(t // tile,),
        in_specs=[pl.BlockSpec(memory_space=pltpu.SMEM),
                  pl.BlockSpec((tile, ATTN_WIDTH), cur),
                  pl.BlockSpec((BLOCK, KV_DUP_WIDTH), prev),
                  pl.BlockSpec((tile, KV_DUP_WIDTH), cur),
                  pl.BlockSpec((BLOCK, KV_DUP_WIDTH), prev),
                  pl.BlockSpec((tile, KV_DUP_WIDTH), cur),
                  pl.BlockSpec(tbl, lambda i: (jnp.minimum((i * ATTN_BLOCKS) % blocks_per_seq, 1), 0, 0, 0)),
                  pl.BlockSpec(tbl, lambda i: (1, 0, 0, 0))],
        out_specs=pl.BlockSpec((tile, ATTN_WIDTH), cur),
        out_shape=jax.ShapeDtypeStruct((t, ATTN_WIDTH), BF16),
        compiler_params=_params(("parallel",), 24),
        name="swa_attention",
    )(sinks, q, k, k, v, v, bias_tbl, bias_tbl)


def _bias_table_kernel(rel_ref, bucket_ref, o_ref):
    for v in range(2):
        bucket = bucket_ref[v]
        hits = [bucket == b for b in range(N_BUCKETS)]
        for h in range(ATTN_Q_HEADS):
            tbl = jnp.full(bucket.shape, NEG_INF, F32)
            for b in range(N_BUCKETS):
                tbl = jnp.where(hits[b], rel_ref[b * ATTN_Q_HEADS + h], tbl)
            o_ref[v, h] = tbl


def _attention_bias_tables(rel_bias):
    q_idx = np.arange(BLOCK)[:, None]
    j_idx = np.arange(BLOCK)[None, :]
    from_prev = j_idx > q_idx
    n = np.where(from_prev, q_idx + BLOCK - j_idx, q_idx - j_idx)
    large = MAX_EXACT + (np.log(np.maximum(n, 1).astype(np.float32) / np.float32(MAX_EXACT))
                         / np.float32(math.log(MAX_DISTANCE / MAX_EXACT))
                         * np.float32(N_BUCKETS - MAX_EXACT)).astype(np.int32)
    bucket = np.where(n < MAX_EXACT, n, np.minimum(large, N_BUCKETS - 1))
    first = np.where(from_prev, -1, bucket)
    buckets = jnp.asarray(np.stack([first, bucket]).astype(np.int32))
    return pl.pallas_call(
        _bias_table_kernel,
        in_specs=[pl.BlockSpec(memory_space=pltpu.SMEM), pl.BlockSpec(memory_space=pltpu.VMEM)],
        out_specs=pl.BlockSpec(memory_space=pltpu.VMEM),
        out_shape=jax.ShapeDtypeStruct((2, ATTN_Q_HEADS, BLOCK, BLOCK), F32),
        name="t5_bias_table",
    )(rel_bias.astype(F32).reshape(-1), buckets)


def _shift_rows(x, s, fill):
    return jnp.concatenate([jnp.full((s, x.shape[1]), fill, x.dtype), x[:x.shape[0] - s]], axis=0)


def _in_proj_conv_kernel(x_ref, g_ref, w_ref, scw_ref, scb_ref, rgw_ref, rgb_ref, wgate_ref, bgate_ref, lam_ref,
                         q_ref, k_ref, v_ref, o_ref, r_ref, sc_ext, rg_ext, h_carry, *, tiles_per_seq):
    i = pl.program_id(0)
    ts = r_ref.shape[0]
    c0 = CARRY_ROWS

    @pl.when(i == 0)
    def _():
        r_ref[...] = jnp.zeros_like(r_ref)

    @pl.when((i == 0) | ((i + tiles_per_seq - 1) % tiles_per_seq == 0))
    def _():
        sc_ext[0:c0, :] = jnp.zeros((c0, SC_WIDTH), F32)
        rg_ext[0:c0, :] = jnp.zeros((c0, RG_WIDTH), F32)
        h_carry[...] = jnp.zeros_like(h_carry)

    attn_cols = ATTN_WIDTH + 2 * KV_DUP_WIDTH
    hx = _rms(x_ref[...], g_ref[...]).astype(BF16)
    p = jnp.dot(hx, w_ref[:, :attn_cols], preferred_element_type=F32)
    q_ref[...] = p[:, :ATTN_WIDTH].astype(BF16)
    k_ref[...] = p[:, ATTN_WIDTH:ATTN_WIDTH + KV_DUP_WIDTH].astype(BF16)
    v_ref[...] = p[:, ATTN_WIDTH + KV_DUP_WIDTH:].astype(BF16)

    sc_b = r_ref[:, 0:SC_WIDTH]
    sc_ext[c0:c0 + ts, :] = r_ref[:, SC_WIDTH:2 * SC_WIDTH] * r_ref[:, 2 * SC_WIDTH:3 * SC_WIDTH]
    rg_ext[c0:c0 + ts, :] = r_ref[:, 3 * SC_WIDTH:3 * SC_WIDTH + RG_WIDTH]
    rg_g = r_ref[:, 3 * SC_WIDTH + RG_WIDTH:]

    conv = scb_ref[...]
    for k in range(SC_CONV):
        off = c0 - (SC_CONV - 1) + k
        conv = conv + scw_ref[k:k + 1, :] * sc_ext[off:off + ts, :]
    conv_out = sc_b * conv

    rg_in = rgb_ref[...]
    for k in range(RG_CONV):
        off = c0 - (RG_CONV - 1) + k
        rg_in = rg_in + rgw_ref[k:k + 1, :] * rg_ext[off:off + ts, :]

    sc_ext[0:c0, :] = sc_ext[ts:ts + c0, :]
    rg_ext[0:c0, :] = rg_ext[ts:ts + c0, :]

    gates = jnp.dot(rg_in.astype(BF16), wgate_ref[...], preferred_element_type=F32) + bgate_ref[...]

    r_ref[...] = jnp.dot(hx, w_ref[:, attn_cols:], preferred_element_type=F32)

    r_gate = jax.nn.sigmoid(gates[:, :RG_WIDTH])
    i_gate = jax.nn.sigmoid(gates[:, RG_WIDTH:])
    neg_lam = -lam_ref[...]
    softplus = jnp.maximum(neg_lam, 0.0) + jnp.log1p(jnp.exp(-jnp.abs(neg_lam)))
    log_a = -RG_C * r_gate * softplus
    a = jnp.exp(log_a)
    u = jnp.sqrt(jnp.tanh(-log_a) * (1.0 + a * a)) * (i_gate * rg_in)

    s = 1
    while s < ts:
        u = a * _shift_rows(u, s, 0.0) + u
        a = a * _shift_rows(a, s, 1.0)
        s *= 2
    h = a * h_carry[...] + u
    h_carry[...] = h[ts - 1:ts, :]

    c = math.sqrt(2.0 / math.pi)
    gelu = 0.5 * rg_g * (1.0 + jnp.tanh(c * (rg_g + 0.044715 * (rg_g * rg_g * rg_g))))
    o_ref[:, 0:SC_WIDTH] = conv_out.astype(BF16)
    o_ref[:, SC_WIDTH:] = (h * gelu).astype(BF16)


def _in_proj_conv(x, g, w, sc_w, sc_b, rg_w, rg_b, w_gate, b_gate, lam, seq):
    t = x.shape[0]
    n = w.shape[1]
    tiles = t // TS
    assert n == ATTN_WIDTH + 2 * KV_DUP_WIDTH + REST_WIDTH
    proj = lambda width: pl.BlockSpec((TS, width), lambda i: (jnp.minimum(i, tiles - 1), 0))
    lagged = pl.BlockSpec((TS, SC_WIDTH + RG_WIDTH), lambda i: (jnp.maximum(i - 1, 0), 0))
    return pl.pallas_call(
        functools.partial(_in_proj_conv_kernel, tiles_per_seq=seq // TS),
        grid=(tiles + 1,),
        in_specs=[proj(D_MODEL), _const_spec((1, D_MODEL)), _const_spec((D_MODEL, n)),
                  _const_spec((SC_CONV, SC_WIDTH)), _const_spec((1, SC_WIDTH)),
                  _const_spec((RG_CONV, RG_WIDTH)), _const_spec((1, RG_WIDTH)),
                  _const_spec((RG_WIDTH, 2 * RG_WIDTH)), _const_spec((1, 2 * RG_WIDTH)),
                  _const_spec((1, RG_WIDTH))],
        out_specs=[proj(ATTN_WIDTH), proj(KV_DUP_WIDTH), proj(KV_DUP_WIDTH), lagged],
        out_shape=[jax.ShapeDtypeStruct((t, ATTN_WIDTH), BF16),
                   jax.ShapeDtypeStruct((t, KV_DUP_WIDTH), BF16),
                   jax.ShapeDtypeStruct((t, KV_DUP_WIDTH), BF16),
                   jax.ShapeDtypeStruct((t, SC_WIDTH + RG_WIDTH), BF16)],
        scratch_shapes=[pltpu.VMEM((TS, REST_WIDTH), F32),
                        pltpu.VMEM((TS + 2 * CARRY_ROWS, SC_WIDTH), F32),
                        pltpu.VMEM((TS + 2 * CARRY_ROWS, RG_WIDTH), F32),
                        pltpu.VMEM((1, RG_WIDTH), F32)],
        compiler_params=_params(("arbitrary",), 48),
        name="in_proj_conv_rglru",
    )(x, g, w, sc_w, sc_b, rg_w, rg_b, w_gate, b_gate, lam)


def _mem_kv_kernel(m_ref, g_ref, w_ref, k_ref, v_ref):
    h = _rms(m_ref[...], g_ref[...]).astype(BF16)
    p = jnp.dot(h, w_ref[...], preferred_element_type=F32)
    k_ref[...] = p[:, :XA_WIDTH].astype(BF16)
    v_ref[...] = p[:, XA_WIDTH:].astype(BF16)


def _mem_kv(mem, g, wkv):
    t = mem.shape[0]
    row = lambda width: pl.BlockSpec((MEM_LEN, width), lambda i: (i, 0))
    return pl.pallas_call(
        _mem_kv_kernel,
        grid=(t // MEM_LEN,),
        in_specs=[row(D_MODEL), _const_spec((1, D_MODEL)), _const_spec((D_MODEL, 2 * XA_WIDTH))],
        out_specs=[row(XA_WIDTH), row(XA_WIDTH)],
        out_shape=[jax.ShapeDtypeStruct((t, XA_WIDTH), BF16)] * 2,
        compiler_params=_params(("parallel",), 24),
        name="mem_kv",
    )(mem, g, wkv)


def _post_mixer_kernel(x_ref, a_ref, c_ref, wout_ref, g_ref, wq_ref, k_ref, v_ref, wo_ref, *rest):
    o_ref = rest[-1] if len(rest) == 1 else rest[4]
    halves = [slice(i * (TM // 2), (i + 1) * (TM // 2)) for i in range(2)]
    heads = [slice(hd * XA_HEAD_DIM, (hd + 1) * XA_HEAD_DIM) for hd in range(XA_HEADS)]
    nt = (((1,), (1,)), ((), ()))
    k = k_ref[...]
    v = v_ref[...]

    x1 = [x_ref[hs, :] + jnp.dot(jnp.concatenate([a_ref[hs, :], c_ref[hs, :]], axis=-1), wout_ref[...],
                                 preferred_element_type=F32) for hs in halves]
    q = [jnp.dot(_rms(xh, g_ref[...]).astype(BF16), wq_ref[...], preferred_element_type=F32).astype(BF16)
         for xh in x1]
    scores = [[lax.dot_general(qh[:, sl], k[:, sl], nt, preferred_element_type=F32) for sl in heads] for qh in q]
    x2 = []
    for xh, per_head in zip(x1, scores):
        probs, sums = [], []
        for s in per_head:
            s = s * (1.0 / math.sqrt(XA_HEAD_DIM))
            p = jnp.exp(s - jnp.max(s, axis=-1, keepdims=True))
            probs.append(p.astype(BF16))
            sums.append(jnp.sum(p, axis=-1, keepdims=True))
        att = jnp.concatenate([jnp.dot(p, v[:, sl], preferred_element_type=F32) / l
                               for p, sl, l in zip(probs, heads, sums)], axis=-1).astype(BF16)
        x2.append(xh + jnp.dot(att, wo_ref[...], preferred_element_type=F32))

    for hs, xh in zip(halves, x2):
        o_ref[hs, :] = xh
    if len(rest) > 1:
        ffn_g_ref, wr_ref, tri_ref, sel_ref, _, hf_ref, meta_ref, metat_ref, cnt_ref, base_ref, carry = rest
        h = _rms(jnp.concatenate(x2, axis=0), ffn_g_ref[...])
        _route(h, wr_ref, tri_ref, sel_ref, hf_ref, meta_ref, metat_ref, cnt_ref, base_ref, carry)


def _post_mixer(x, attn, cr, w_out, g, wq, k, v, wo, seq, route=None):
    t = x.shape[0]
    per_seq = seq // TM
    row = lambda width: pl.BlockSpec((TM, width), lambda i: (i, 0))
    mem_blk = pl.BlockSpec((MEM_LEN, XA_WIDTH), lambda i: (i // per_seq, 0))
    in_specs = [row(D_MODEL), row(ATTN_WIDTH), row(SC_WIDTH + RG_WIDTH), _const_spec((D_MODEL, D_MODEL)),
                _const_spec((1, D_MODEL)), _const_spec((D_MODEL, XA_WIDTH)), mem_blk, mem_blk,
                _const_spec((XA_WIDTH, D_MODEL))]
    out_specs = [row(D_MODEL)]
    out_shape = [jax.ShapeDtypeStruct((t, D_MODEL), F32)]
    scratch = []
    args = [x, attn, cr, w_out, g, wq, k, v, wo]
    if route is not None:
        assert TM % ROUTE_CHUNK == 0
        in_specs += [_const_spec((1, D_MODEL)), _const_spec((D_MODEL, LANES_V7X)),
                     _const_spec((ROUTE_CHUNK, ROUTE_CHUNK)), _const_spec((SUBLANES_V7X, LANES_V7X))]
        out_specs += [row(D_MODEL), row(LANES_V7X),
                      pl.BlockSpec((SUBLANES_V7X, TM), lambda i: (0, i)),
                      pl.BlockSpec((1, LANES_V7X), lambda i: (0, 0)),
                      pl.BlockSpec((SUBLANES_V7X, LANES_V7X), lambda i: (i, 0))]
        out_shape += [jax.ShapeDtypeStruct((t, D_MODEL), BF16),
                      jax.ShapeDtypeStruct((t, LANES_V7X), F32),
                      jax.ShapeDtypeStruct((SUBLANES_V7X, t), F32),
                      jax.ShapeDtypeStruct((1, LANES_V7X), F32),
                      jax.ShapeDtypeStruct((t // TM * SUBLANES_V7X, LANES_V7X), F32)]
        scratch = [pltpu.VMEM((1, LANES_V7X), F32)]
        args += list(route)
    return pl.pallas_call(
        _post_mixer_kernel,
        grid=(t // TM,),
        in_specs=in_specs,
        out_specs=out_specs,
        out_shape=out_shape,
        scratch_shapes=scratch,
        compiler_params=_params(("arbitrary",), 40),
        name="post_mixer_route" if route is not None else "post_mixer",
    )(*args)


def _ffn_kernel(x_ref, g_ref, wg_ref, wu_ref, wd_ref, *rest):
    n_cast = (len(rest) - 1) // 2
    o_ref = rest[n_cast]
    x = x_ref[...]
    h = _rms(x, g_ref[...]).astype(BF16)
    gate = jnp.dot(h, wg_ref[...], preferred_element_type=F32)
    up = jnp.dot(h, wu_ref[...], preferred_element_type=F32)
    act = (gate * jax.nn.sigmoid(gate) * up).astype(BF16)
    o_ref[...] = x + jnp.dot(act, wd_ref[...], preferred_element_type=F32)
    for src, dst in zip(rest[:n_cast], rest[n_cast + 1:]):
        dst[...] = src[...].astype(BF16)


def _dense_ffn(x, g, wg, wu, wd, cast_along=()):
    t = x.shape[0]
    steps = t // TM
    d_ff = wg.shape[1]
    row = pl.BlockSpec((TM, D_MODEL), lambda i: (i, 0))
    flat = [w.reshape(-1, w.shape[-1]) for w in cast_along]
    for w in flat:
        assert w.shape[0] % (steps * 2 * SUBLANES_V7X) == 0
    slabs = [pl.BlockSpec((w.shape[0] // steps, w.shape[1]), lambda i: (i, 0)) for w in flat]
    outs = pl.pallas_call(
        _ffn_kernel,
        grid=(steps,),
        in_specs=[row, _const_spec((1, D_MODEL)), _const_spec((D_MODEL, d_ff)), _const_spec((D_MODEL, d_ff)),
                  _const_spec((d_ff, D_MODEL))] + slabs,
        out_specs=[row] + slabs,
        out_shape=[jax.ShapeDtypeStruct((t, D_MODEL), F32)] + [jax.ShapeDtypeStruct(w.shape, BF16) for w in flat],
        compiler_params=_params(("parallel",), 60),
        name="dense_swiglu",
    )(x, g, wg, wu, wd, *flat)
    return outs[0], [o.reshape(w.shape) for o, w in zip(outs[1:], cast_along)]


META_E1, META_E2, META_R1, META_R2, META_W1, META_W2 = range(6)


def _to_token_tiles(ref, rows):
    m = rows.shape[0]
    for c in range(ROW_CHUNKS):
        ref[pl.ds(c, m, stride=ROW_CHUNKS), :] = rows[:, c * LANES_V7X:(c + 1) * LANES_V7X]


def _from_token_tiles(ref):
    m = ref.shape[0] // ROW_CHUNKS
    return jnp.concatenate([ref[pl.ds(c, m, stride=ROW_CHUNKS), :] for c in range(ROW_CHUNKS)], axis=-1)


def _route(h, wr_ref, tri_ref, sel_ref, hf_ref, meta_ref, metat_ref, cnt_ref, base_ref, carry):
    @pl.when(pl.program_id(0) == 0)
    def _():
        carry[...] = jnp.zeros_like(carry)

    base_ref[...] = jnp.broadcast_to(carry[...], base_ref.shape)
    chunks = [slice(c * ROUTE_CHUNK, (c + 1) * ROUTE_CHUNK) for c in range(h.shape[0] // ROUTE_CHUNK)]
    each = lambda fn, *lists: [fn(*vals) for vals in zip(*lists)]
    rowmax = lambda a: jnp.max(a, axis=-1, keepdims=True)
    rowsum = lambda a: jnp.sum(a, axis=-1, keepdims=True)
    lane = lax.broadcasted_iota(jnp.int32, (ROUTE_CHUNK, LANES_V7X), 1)
    first_hit = lambda lg, m: jnp.min(jnp.where(lg == m, lane, LANES_V7X), axis=-1, keepdims=True)

    hb = h.astype(BF16)
    hf_ref[...] = hb
    logits = [jnp.dot(hb[c, :], wr_ref[...], preferred_element_type=F32) for c in chunks]
    lg = each(lambda l: jnp.where(lane < N_EXPERTS, l, -jnp.inf), logits)
    m1 = each(rowmax, lg)
    e1 = each(first_hit, lg, m1)
    lg2 = each(lambda l, e: jnp.where(lane == e, -jnp.inf, l), lg, e1)
    m2 = each(rowmax, lg2)
    e2 = each(first_hit, lg2, m2)
    ex = each(lambda a, b: jnp.exp(b - a), m1, m2)
    w1 = each(lambda e: 1.0 / (1.0 + e), ex)
    w2 = each(lambda e: e / (1.0 + e), ex)
    hit1 = each(lambda e: lane == e, e1)
    hit2 = each(lambda e: lane == e, e2)
    onehot = each(lambda a, b: (a | b).astype(BF16), hit1, hit2)
    within = each(lambda o: jnp.dot(tri_ref[...], o, preferred_element_type=F32), onehot)
    totals = each(lambda o: jnp.sum(o.astype(F32), axis=0, keepdims=True), onehot)
    ahead, base = [], carry[...]
    for w, tot in zip(within, totals):
        ahead.append(w + base)
        base = base + tot
    carry[...] = base
    cnt_ref[...] = base
    r1 = each(lambda hit, a: rowsum(jnp.where(hit, a, 0.0)), hit1, ahead)
    r2 = each(lambda hit, a: rowsum(jnp.where(hit, a, 0.0)), hit2, ahead)

    def record(*vals):
        meta = jnp.zeros((ROUTE_CHUNK, LANES_V7X), F32)
        for col, val in zip((META_E1, META_E2, META_R1, META_R2, META_W1, META_W2), vals):
            meta = jnp.where(lane == col, val.astype(F32), meta)
        return meta

    meta = each(record, e1, e2, r1, r2, w1, w2)
    metat = each(lambda mt: lax.dot_general(sel_ref[...], mt, (((1,), (1,)), ((), ())), preferred_element_type=F32,
                                            precision=lax.Precision.HIGHEST), meta)
    for c, mt, mtt in zip(chunks, meta, metat):
        meta_ref[c, :] = mt
        metat_ref[:, c] = mtt


def _token_rows(ref, first_token, n_tokens):
    start = pl.multiple_of(first_token * ROW_CHUNKS, ROW_CHUNKS)
    return ref.at[pl.ds(start, n_tokens * ROW_CHUNKS)]


def _packed_rows(ref, first_token, n_tokens):
    start = pl.multiple_of(first_token * PACKED_ROWS, SUBLANES_V7X)
    return ref.at[pl.ds(start, n_tokens * PACKED_ROWS)]


def _dispatch_kernel(fill_ref, n_ref, off_ref, dst_ref, loc1_ref, loc2_ref, hf_ref, xs_ref, zeros, buf, sem_z, sems):
    i = pl.program_id(0)
    steps = pl.num_programs(0)
    slot_tokens = buf.shape[1] // PACKED_ROWS
    runs = N_EXPERTS + 1

    @pl.when(i == 0)
    def _():
        zeros[...] = jnp.zeros_like(zeros)

        def tile_fill(e):
            return pltpu.make_async_copy(zeros, _packed_rows(xs_ref, pl.multiple_of(fill_ref[e], TM_GROUP), TM_GROUP),
                                         sem_z)

        for e in range(fill_ref.shape[0]):
            @pl.when(fill_ref[e] >= 0)
            def _():
                tile_fill(e).start()
        for e in range(fill_ref.shape[0]):
            @pl.when(fill_ref[e] >= 0)
            def _():
                tile_fill(e).wait()

    k = lax.broadcasted_iota(jnp.int32, (slot_tokens, hf_ref.shape[0]), 0)
    onehot = ((k == loc1_ref[...]) | (k == loc2_ref[...])).astype(BF16)
    sorted_rows = jnp.dot(onehot, hf_ref[...], preferred_element_type=F32)
    bits = pltpu.bitcast(sorted_rows, jnp.uint32)
    half = D_MODEL // 2
    packed = (bits[:, :half] & jnp.uint32(0xFFFF0000)) | (bits[:, half:] >> 16)

    def run(slot):
        whole_slot = pltpu.make_async_copy(buf.at[slot], _packed_rows(xs_ref, 0, slot_tokens), sems.at[slot])

        @pl.when(i >= 2)
        def _():
            whole_slot.wait()

        for c in range(PACKED_ROWS):
            buf.at[slot][pl.ds(c, slot_tokens, stride=PACKED_ROWS), :] = packed[:, c * LANES_V7X:(c + 1) * LANES_V7X]
        for e in range(runs):
            n = n_ref[i * runs + e]
            off = off_ref[i * runs + e]
            dst = dst_ref[i * runs + e]
            bit = hf_ref.shape[0]
            while bit >= 2:
                head = n & ~(2 * bit - 1)

                @pl.when((n & bit) != 0)
                def _(bit=bit, head=head):
                    pltpu.make_async_copy(_packed_rows(buf.at[slot], off + head, bit),
                                          _packed_rows(xs_ref, dst + head, bit), sems.at[slot]).start()
                bit //= 2

        @pl.when(i == steps - 1)
        def _():
            whole_slot.wait()
            other = pltpu.make_async_copy(buf.at[1 - slot], _packed_rows(xs_ref, 0, slot_tokens), sems.at[1 - slot])

            @pl.when(steps >= 2)
            def _():
                other.wait()

    for slot in range(2):
        pl.when(i % 2 == slot)(functools.partial(run, slot))


def _dispatch(fill_start, n_tab, off_tab, dst_tab, loc1, loc2, hf, total_rows):
    t = hf.shape[0]
    loc = pl.BlockSpec((1, TM), lambda i, *_: (0, i))
    u32 = jnp.uint32
    return pl.pallas_call(
        _dispatch_kernel,
        grid_spec=pltpu.PrefetchScalarGridSpec(
            num_scalar_prefetch=4,
            grid=(t // TM,),
            in_specs=[loc, loc, pl.BlockSpec((TM, D_MODEL), lambda i, *_: (i, 0))],
            out_specs=pl.BlockSpec(memory_space=pl.ANY),
            scratch_shapes=[pltpu.VMEM((TM_GROUP * PACKED_ROWS, LANES_V7X), u32),
                            pltpu.VMEM((2, SORTED_TOKENS * PACKED_ROWS, LANES_V7X), u32),
                            pltpu.SemaphoreType.DMA(()), pltpu.SemaphoreType.DMA((2,))]),
        out_shape=jax.ShapeDtypeStruct((total_rows * PACKED_ROWS, LANES_V7X), u32),
        compiler_params=_params(("arbitrary",), 40),
        name="moe_dispatch",
    )(fill_start, n_tab, off_tab, dst_tab, loc1.reshape(1, t), loc2.reshape(1, t), hf)


def _grouped_kernel(te_ref, used_ref, x_ref, wg_ref, wu_ref, wd_ref, y_ref):
    i = pl.program_id(0)

    @pl.when(i < used_ref[0])
    def _():
        tm = x_ref.shape[0] // PACKED_ROWS
        words = jnp.concatenate([x_ref[pl.ds(c, tm, stride=PACKED_ROWS), :] for c in range(PACKED_ROWS)], axis=-1)
        high = pltpu.bitcast(words & jnp.uint32(0xFFFF0000), F32).astype(BF16)
        low = pltpu.bitcast(words << 16, F32).astype(BF16)
        h = jnp.concatenate([high, low], axis=-1)
        ff = wg_ref.shape[1] // FF_SPLIT
        cols = [slice(c * ff, (c + 1) * ff) for c in range(FF_SPLIT)]
        gates = [jnp.dot(h, wg_ref[:, c], preferred_element_type=F32) for c in cols]
        ups = [jnp.dot(h, wu_ref[:, c], preferred_element_type=F32) for c in cols]
        total = None
        for c, gate, up in zip(cols, gates, ups):
            act = (gate * jax.nn.sigmoid(gate) * up).astype(BF16)
            part = jnp.dot(act, wd_ref[c, :], preferred_element_type=F32)
            total = part if total is None else total + part
        _to_token_tiles(y_ref, total)

    @pl.when(i >= used_ref[0])
    def _():
        y_ref[...] = jnp.zeros_like(y_ref)


def _grouped_swiglu(tile_expert, n_used, xs, wg, wu, wd, n_rows):
    d_ff = wg.shape[2]
    tile = (TM_GROUP * ROW_CHUNKS, LANES_V7X)
    packed_tile = (TM_GROUP * PACKED_ROWS, LANES_V7X)
    src = lambda i, te, used: (jnp.maximum(jnp.minimum(i, used[0] - 1), 0), 0)
    expert = lambda shape, buffers: pl.BlockSpec((None,) + shape, lambda i, te, used: (te[i], 0, 0),
                                                 pipeline_mode=pl.Buffered(buffers))
    return pl.pallas_call(
        _grouped_kernel,
        grid_spec=pltpu.PrefetchScalarGridSpec(
            num_scalar_prefetch=2,
            grid=(n_rows // TM_GROUP,),
            in_specs=[pl.BlockSpec(packed_tile, src), expert((D_MODEL, d_ff), 1), expert((D_MODEL, d_ff), 2),
                      expert((d_ff, D_MODEL), 2)],
            out_specs=pl.BlockSpec(tile, lambda i, te, used: (i, 0))),
        out_shape=jax.ShapeDtypeStruct((n_rows * ROW_CHUNKS, LANES_V7X), F32),
        compiler_params=_params(("arbitrary",), 58),
        name="moe_grouped_swiglu",
    )(tile_expert, n_used, xs, wg, wu, wd)


def _combine_kernel(n_ref, off_ref, dst_ref, loc1_ref, loc2_ref, x_ref, meta_ref, g_ref, y_ref, o_ref,
                    slab, y1, y2, sems):
    i = pl.program_id(0)
    tm = x_ref.shape[0]

    def fetch(tile, slot):
        for e in range(N_EXPERTS):
            n = n_ref[tile * N_EXPERTS + e]
            off = off_ref[tile * N_EXPERTS + e]
            dst = dst_ref[tile * N_EXPERTS + e]
            bit = tm
            while bit >= 1:
                head = n & ~(2 * bit - 1)

                @pl.when((n & bit) != 0)
                def _(bit=bit, head=head):
                    pltpu.make_async_copy(_token_rows(y_ref, dst + head, bit),
                                          _token_rows(slab.at[slot], off + head, bit), sems.at[slot]).start()
                bit //= 2

    def finish(slot):
        pltpu.make_async_copy(_token_rows(y_ref, 0, 2 * tm), _token_rows(slab.at[slot], 0, 2 * tm),
                              sems.at[slot]).wait()

        def pick(r, carry):
            y1[pl.ds(pl.multiple_of(r * ROW_CHUNKS, ROW_CHUNKS), ROW_CHUNKS), :] = (
                _token_rows(slab.at[slot], loc1_ref[r], 1)[...])
            y2[pl.ds(pl.multiple_of(r * ROW_CHUNKS, ROW_CHUNKS), ROW_CHUNKS), :] = (
                _token_rows(slab.at[slot], loc2_ref[r], 1)[...])
            return carry

        lax.fori_loop(0, tm, pick, 0, unroll=ISSUE_UNROLL)
        w1 = meta_ref[:, META_W1:META_W1 + 1]
        w2 = meta_ref[:, META_W2:META_W2 + 1]
        out = x_ref[...] + (w1 * _from_token_tiles(y1) + w2 * _from_token_tiles(y2))
        o_ref[...] = _rms(out, g_ref[...])

    @pl.when(i == 0)
    def _():
        fetch(0, 0)

    for slot in range(2):
        @pl.when(i % 2 == slot)
        def _():
            @pl.when(i + 1 < pl.num_programs(0))
            def _():
                fetch(i + 1, 1 - slot)
            finish(slot)


def _combine(n_tab, off_tab, dst_tab, loc1, loc2, x, meta, g, y):
    t = x.shape[0]
    idx = pl.BlockSpec((TM,), lambda i, *_: (i,), memory_space=pltpu.SMEM)
    row = lambda width: pl.BlockSpec((TM, width), lambda i, *_: (i, 0))
    return pl.pallas_call(
        _combine_kernel,
        grid_spec=pltpu.PrefetchScalarGridSpec(
            num_scalar_prefetch=3,
            grid=(t // TM,),
            in_specs=[idx, idx, row(D_MODEL), row(LANES_V7X),
                      pl.BlockSpec((1, D_MODEL), lambda i, *_: (0, 0), pipeline_mode=pl.Buffered(1)),
                      pl.BlockSpec(memory_space=pl.ANY)],
            out_specs=row(D_MODEL),
            scratch_shapes=[pltpu.VMEM((2, SORTED_TOKENS * ROW_CHUNKS, LANES_V7X), F32),
                            pltpu.VMEM((TM * ROW_CHUNKS, LANES_V7X), F32),
                            pltpu.VMEM((TM * ROW_CHUNKS, LANES_V7X), F32),
                            pltpu.SemaphoreType.DMA((2,))]),
        out_shape=jax.ShapeDtypeStruct((t, D_MODEL), F32),
        compiler_params=_params(("arbitrary",), 40),
        name="moe_combine_norm",
    )(n_tab, off_tab, dst_tab, loc1, loc2, x, meta, g, y)


def _route_operands(ffn_g, router_w):
    wr_pad = jnp.zeros((D_MODEL, LANES_V7X), BF16).at[:, :N_EXPERTS].set(router_w.astype(BF16))
    tri = jnp.tril(jnp.ones((ROUTE_CHUNK, ROUTE_CHUNK), BF16), -1)
    sel = jnp.eye(SUBLANES_V7X, LANES_V7X, dtype=F32)
    return ffn_g, wr_pad, tri, sel


def _moe_layer(x, hf, meta, metat, counts, base, wg, wu, wd, final_g):
    t = x.shape[0]
    steps = t // TM
    n_rows = -(-(2 * t + N_EXPERTS * steps) // TM_GROUP) * TM_GROUP + N_EXPERTS * TM_GROUP

    cnt = counts[0, :N_EXPERTS].astype(jnp.int32)
    experts = jnp.arange(N_EXPERTS, dtype=jnp.int32)[:, None]
    e1, e2 = metat[META_E1].astype(jnp.int32), metat[META_E2].astype(jnp.int32)
    r1, r2 = metat[META_R1].astype(jnp.int32), metat[META_R2].astype(jnp.int32)
    pick = lambda e, table: jnp.sum(jnp.where(e[None, :] == experts, table, 0), axis=0)
    before = base.reshape(steps, SUBLANES_V7X, LANES_V7X)[:, 0, :N_EXPERTS].astype(jnp.int32)
    n_tab = jnp.concatenate([before[1:], cnt[None, :]], axis=0) - before
    n_even = n_tab + (n_tab & 1)
    padded = (jnp.sum(n_even, axis=0) + TM_GROUP - 1) // TM_GROUP * TM_GROUP
    ends = jnp.cumsum(padded)
    starts = ends - padded
    off_tab = jnp.cumsum(n_even, axis=1) - n_even
    dst_tab = starts[None, :] + jnp.cumsum(n_even, axis=0) - n_even
    shift = jnp.repeat((off_tab - before).T, TM, axis=1)
    loc1 = pick(e1, shift) + r1
    loc2 = pick(e2, shift) + r2
    sent = jnp.sum(n_even, axis=1, keepdims=True)
    scratch = n_rows + N_EXPERTS * jnp.arange(steps, dtype=jnp.int32)[:, None]
    send = [jnp.concatenate(pair, axis=1) for pair in ((n_even, SORTED_TOKENS - sent), (off_tab, sent),
                                                       (dst_tab, scratch))]
    n_used = (ends[-1] // TM_GROUP).astype(jnp.int32).reshape(1)
    tile_row = jnp.minimum(jnp.arange(n_rows // TM_GROUP, dtype=jnp.int32), n_used[0] - 1) * TM_GROUP
    tile_expert = jnp.sum(ends[None, :] <= tile_row[:, None], axis=1).astype(jnp.int32)
    tail_start = jnp.where(padded > 0, ends - TM_GROUP, -1)
    spare = ends[-1] + TM_GROUP * jnp.arange(2 * N_EXPERTS, dtype=jnp.int32)
    scratch_tiles = -(-N_EXPERTS * steps // TM_GROUP)
    scratch_fill = n_rows + TM_GROUP * jnp.arange(scratch_tiles, dtype=jnp.int32)
    fill_start = jnp.concatenate([tail_start, jnp.where(spare < n_rows, spare, -1), scratch_fill]).astype(jnp.int32)

    flat = lambda tables: [a.reshape(-1).astype(jnp.int32) for a in tables]
    xs = _dispatch(fill_start, *flat(send), loc1, loc2, hf, n_rows + scratch_tiles * TM_GROUP)
    y = _grouped_swiglu(tile_expert, n_used, xs, wg, wu, wd, n_rows)
    return _combine(*flat((n_tab, off_tab, dst_tab)), loc1, loc2, x, meta, final_g, y)


def _arrange_in_proj(w):
    q = (w[:, :ATTN_WIDTH] * (1.0 / math.sqrt(HEAD_DIM))).astype(BF16)
    w = w.astype(BF16)
    dup = lambda start: [w[:, start + h * HEAD_DIM:start + (h + 1) * HEAD_DIM]
                         for h in range(ATTN_KV_HEADS) for _ in range(2)]
    return jnp.concatenate([q] + dup(ATTN_WIDTH) + dup(ATTN_WIDTH + KV_WIDTH) + [w[:, ATTN_WIDTH + 2 * KV_WIDTH:]],
                           axis=1)


def _block_diag(w):
    heads, d, _ = w.shape
    eye = jnp.eye(heads, dtype=w.dtype)
    return jnp.einsum('hij,hg->higj', w, eye).reshape(heads * d, heads * d)


def kernel(x, mem, rel_bias, mix_norm, w_in, attn_sinks, sc_conv_w, sc_conv_b, rg_conv_w, rg_conv_b, rg_w_a,
           rg_b_a, rg_w_x, rg_b_x, rg_lambda, w_out, xa_norm, mem_norm, xa_wq, xa_wk, xa_wv, xa_wo, ffn_norm,
           dense_wg, dense_wu, dense_wd, moe_router, moe_wg, moe_wu, moe_wd, final_norm):
    batch, seq, _ = x.shape
    depth = w_in.shape[0]
    assert depth == 2 and seq % TS == 0 and seq % TM == 0
    xt = x.reshape(batch * seq, D_MODEL)
    memt = mem.reshape(batch * MEM_LEN, D_MODEL)
    bias_tbl = _attention_bias_tables(rel_bias)
    vec = lambda a: a.reshape(1, -1)

    for layer in range(depth):
        w_gate = jnp.concatenate([_block_diag(rg_w_a[layer]), _block_diag(rg_w_x[layer])], axis=1).astype(BF16)
        b_gate = jnp.concatenate([rg_b_a[layer], rg_b_x[layer]]).reshape(1, -1)
        q, k, v, cr = _in_proj_conv(xt, vec(mix_norm[layer]), _arrange_in_proj(w_in[layer]),
                                    sc_conv_w[layer], vec(sc_conv_b[layer]), rg_conv_w[layer],
                                    vec(rg_conv_b[layer]), w_gate, b_gate, vec(rg_lambda[layer]), seq)
        attn = _attention(q, k, v, attn_sinks[layer], bias_tbl, seq // BLOCK)
        wkv = jnp.concatenate([xa_wk[layer], xa_wv[layer]], axis=1).astype(BF16)
        mk, mv = _mem_kv(memt, vec(mem_norm[layer]), wkv)
        post = functools.partial(_post_mixer, xt, attn, cr, w_out[layer].astype(BF16), vec(xa_norm[layer]),
                                 xa_wq[layer].astype(BF16), mk, mv, xa_wo[layer].astype(BF16), seq)

        j = layer // 2
        if layer % 2 == 0:
            (xt,) = post()
            xt, expert_w = _dense_ffn(xt, vec(ffn_norm[layer]), dense_wg[j].astype(BF16), dense_wu[j].astype(BF16),
                                      dense_wd[j].astype(BF16), cast_along=(moe_wg[j], moe_wu[j], moe_wd[j]))
        else:
            xt, hf, meta, metat, counts, base = post(route=_route_operands(vec(ffn_norm[layer]), moe_router[j]))
            xt = _moe_layer(xt, hf, meta, metat, counts, base, *expert_w, vec(final_norm))
    return xt.reshape(batch, seq, D_MODEL)
```

```python
import functools
import math

import jax
import jax.numpy as jnp
import numpy as np
from jax import lax
from jax.experimental import pallas as pl
from jax.experimental.pallas import tpu as pltpu

F32 = jnp.float32
BF16 = jnp.bfloat16

D_MODEL = 1024
MEM_LEN = 256
HEAD_DIM = 64
ATTN_Q_HEADS = 8
ATTN_KV_HEADS = 2
ATTN_WIDTH = ATTN_Q_HEADS * HEAD_DIM
KV_WIDTH = ATTN_KV_HEADS * HEAD_DIM
KV_DUP_WIDTH = 2 * KV_WIDTH
BLOCK = 128
SC_WIDTH = 256
SC_CONV = 3
RG_WIDTH = 256
RG_HEADS = 4
RG_HEAD_DIM = RG_WIDTH // RG_HEADS
RG_CONV = 4
RG_C = 8.0
N_BUCKETS = 32
MAX_EXACT = N_BUCKETS // 2
MAX_DISTANCE = 128
XA_HEADS = 4
XA_HEAD_DIM = 128
XA_WIDTH = XA_HEADS * XA_HEAD_DIM
N_EXPERTS = 8
EPS = 1e-6
NEG_INF = -1e30
REST_WIDTH = 3 * SC_WIDTH + 2 * RG_WIDTH

LANES_V7X = 128
SUBLANES_V7X = 8
VMEM_BYTES_V7X = 64 * 1024 * 1024
ROW_CHUNKS = D_MODEL // LANES_V7X
assert ROW_CHUNKS == SUBLANES_V7X

TM = 512
ATTN_BLOCKS = 8
TS = 512
ROUTE_CHUNK = 128
TM_GROUP = 512
ISSUE_UNROLL = 8
PACKED_ROWS = ROW_CHUNKS // 2
SORTED_TOKENS = 2 * TM + N_EXPERTS
FF_SPLIT = 2
CARRY_ROWS = SUBLANES_V7X


def _mib(n):
    return int(n * 1024 * 1024)


def _params(semantics, vmem_mib):
    assert _mib(vmem_mib) < VMEM_BYTES_V7X
    return pltpu.CompilerParams(dimension_semantics=semantics, vmem_limit_bytes=_mib(vmem_mib))


def _rms(x, g):
    ms = jnp.mean(x * x, axis=-1, keepdims=True)
    return x * lax.rsqrt(ms + EPS) * g


def _const_spec(shape):
    nd = len(shape)
    return pl.BlockSpec(shape, lambda *_: (0,) * nd, pipeline_mode=pl.Buffered(1))


def _attn_kernel(sink_ref, q_ref, kp_ref, kc_ref, vp_ref, vc_ref, bias0_ref, bias_ref, o_ref):
    pairs_per_group = ATTN_Q_HEADS // ATTN_KV_HEADS // 2
    row = lax.broadcasted_iota(jnp.int32, (BLOCK, BLOCK), 0)
    col = lax.broadcasted_iota(jnp.int32, (BLOCK, BLOCK), 1)
    from_prev = col > row
    low_lanes = lax.broadcasted_iota(jnp.int32, (2 * BLOCK, 2 * HEAD_DIM), 1) < HEAD_DIM
    low_out = col < HEAD_DIM
    zero = jnp.zeros((), BF16)

    def block_diag(band):
        return jnp.concatenate([jnp.where(low_lanes, band, zero), jnp.where(low_lanes, zero, band)], axis=0)

    def scores(blk):
        rows = slice(blk * BLOCK, (blk + 1) * BLOCK)
        prev_rows = slice((blk - 1) * BLOCK, blk * BLOCK)
        out = []
        for g in range(ATTN_KV_HEADS):
            lanes = slice(g * 2 * HEAD_DIM, (g + 1) * 2 * HEAD_DIM)
            k_prev = kp_ref[:, lanes] if blk == 0 else kc_ref[prev_rows, lanes]
            v_prev = vp_ref[:, lanes] if blk == 0 else vc_ref[prev_rows, lanes]
            k_bd = block_diag(jnp.concatenate([k_prev, kc_ref[rows, lanes]], axis=0))
            v_bd = block_diag(jnp.concatenate([v_prev, vc_ref[rows, lanes]], axis=0))
            for pair in range(pairs_per_group):
                slab = g * pairs_per_group + pair
                q2 = q_ref[rows, slab * 2 * HEAD_DIM:(slab + 1) * 2 * HEAD_DIM]
                s = lax.dot_general(q2, k_bd, (((1,), (1,)), ((), ())), preferred_element_type=F32)
                out.append((slab, s, v_bd))
        return out

    def finish(blk, scored):
        rows = slice(blk * BLOCK, (blk + 1) * BLOCK)
        tbl_ref = bias0_ref if blk == 0 else bias_ref
        staged = []
        for slab, s, v_bd in scored:
            probs, denoms = [], []
            for side in range(2):
                h = 2 * slab + side
                sh = s[:, side * 2 * BLOCK:(side + 1) * 2 * BLOCK]
                logits = jnp.where(from_prev, sh[:, :BLOCK], sh[:, BLOCK:]) + tbl_ref[h]
                sink = sink_ref[h]
                m = jnp.maximum(jnp.max(logits, axis=-1, keepdims=True), sink)
                p = jnp.exp(logits - m)
                denoms.append(jnp.sum(p, axis=-1, keepdims=True) + jnp.exp(sink - m))
                probs += [jnp.where(from_prev, p, 0.0), jnp.where(from_prev, 0.0, p)]
            staged.append((slab, jnp.concatenate(probs, axis=-1).astype(BF16), v_bd, denoms))
        for slab, p_band, v_bd, denoms in staged:
            o = jnp.dot(p_band, v_bd, preferred_element_type=F32)
            o = o / jnp.where(low_out, denoms[0], denoms[1])
            o_ref[rows, slab * 2 * HEAD_DIM:(slab + 1) * 2 * HEAD_DIM] = o.astype(BF16)

    pending = scores(0)
    for blk in range(ATTN_BLOCKS):
        upcoming = scores(blk + 1) if blk + 1 < ATTN_BLOCKS else None
        finish(blk, pending)
        pending = upcoming


def _attention(q, k, v, sinks, bias_tbl, blocks_per_seq):
    t = q.shape[0]
    tile = ATTN_BLOCKS * BLOCK
    cur = lambda i: (i, 0)
    prev = lambda i: (jnp.maximum(i * ATTN_BLOCKS - 1, 0), 0)
    tbl = (None, ATTN_Q_HEADS, BLOCK, BLOCK)
    return pl.pallas_call(
        _attn_kernel,
        grid=(t // tile,),
        in_specs=[pl.BlockSpec(memory_space=pltpu.SMEM),
                  pl.BlockSpec((tile, ATTN_WIDTH), cur),
                  pl.BlockSpec((BLOCK, KV_DUP_WIDTH), prev),
                  pl.BlockSpec((tile, KV_DUP_WIDTH), cur),
                  pl.BlockSpec((BLOCK, KV_DUP_WIDTH), prev),
                  pl.BlockSpec((tile, KV_DUP_WIDTH), cur),
                  pl.BlockSpec(tbl, lambda i: (jnp.minimum((i * ATTN_BLOCKS) % blocks_per_seq, 1), 0, 0, 0)),
                  pl.BlockSpec(tbl, lambda i: (1, 0, 0, 0))],
        out_specs=pl.BlockSpec((tile, ATTN_WIDTH), cur),
        out_shape=jax.ShapeDtypeStruct((t, ATTN_WIDTH), BF16),
        compiler_params=_params(("parallel",), 24),
        name="swa_attention",
    )(sinks, q, k, k, v, v, bias_tbl, bias_tbl)


def _bias_table_kernel(rel_ref, bucket_ref, o_ref):
    for v in range(2):
        bucket = bucket_ref[v]
        hits = [bucket == b for b in range(N_BUCKETS)]
        for h in range(ATTN_Q_HEADS):
            tbl = jnp.full(bucket.shape, NEG_INF, F32)
            for b in range(N_BUCKETS):
                tbl = jnp.where(hits[b], rel_ref[b * ATTN_Q_HEADS + h], tbl)
            o_ref[v, h] = tbl


def _attention_bias_tables(rel_bias):
    q_idx = np.arange(BLOCK)[:, None]
    j_idx = np.arange(BLOCK)[None, :]
    from_prev = j_idx > q_idx
    n = np.where(from_prev, q_idx + BLOCK - j_idx, q_idx - j_idx)
    large = MAX_EXACT + (np.log(np.maximum(n, 1).astype(np.float32) / np.float32(MAX_EXACT))
                         / np.float32(math.log(MAX_DISTANCE / MAX_EXACT))
                         * np.float32(N_BUCKETS - MAX_EXACT)).astype(np.int32)
    bucket = np.where(n < MAX_EXACT, n, np.minimum(large, N_BUCKETS - 1))
    first = np.where(from_prev, -1, bucket)
    buckets = jnp.asarray(np.stack([first, bucket]).astype(np.int32))
    return pl.pallas_call(
        _bias_table_kernel,
        in_specs=[pl.BlockSpec(memory_space=pltpu.SMEM), pl.BlockSpec(memory_space=pltpu.VMEM)],
        out_specs=pl.BlockSpec(memory_space=pltpu.VMEM),
        out_shape=jax.ShapeDtypeStruct((2, ATTN_Q_HEADS, BLOCK, BLOCK), F32),
        name="t5_bias_table",
    )(rel_bias.astype(F32).reshape(-1), buckets)


def _shift_rows(x, s, fill):
    return jnp.concatenate([jnp.full((s, x.shape[1]), fill, x.dtype), x[:x.shape[0] - s]], axis=0)


def _in_proj_conv_kernel(x_ref, g_ref, w_ref, scw_ref, scb_ref, rgw_ref, rgb_ref, wgate_ref, bgate_ref, lam_ref,
                         q_ref, k_ref, v_ref, o_ref, r_ref, sc_ext, rg_ext, h_carry, *, tiles_per_seq):
    i = pl.program_id(0)
    ts = r_ref.shape[0]
    c0 = CARRY_ROWS

    @pl.when(i == 0)
    def _():
        r_ref[...] = jnp.zeros_like(r_ref)

    @pl.when((i == 0) | ((i + tiles_per_seq - 1) % tiles_per_seq == 0))
    def _():
        sc_ext[0:c0, :] = jnp.zeros((c0, SC_WIDTH), F32)
        rg_ext[0:c0, :] = jnp.zeros((c0, RG_WIDTH), F32)
        h_carry[...] = jnp.zeros_like(h_carry)

    attn_cols = ATTN_WIDTH + 2 * KV_DUP_WIDTH
    hx = _rms(x_ref[...], g_ref[...]).astype(BF16)
    p = jnp.dot(hx, w_ref[:, :attn_cols], preferred_element_type=F32)
    q_ref[...] = p[:, :ATTN_WIDTH].astype(BF16)
    k_ref[...] = p[:, ATTN_WIDTH:ATTN_WIDTH + KV_DUP_WIDTH].astype(BF16)
    v_ref[...] = p[:, ATTN_WIDTH + KV_DUP_WIDTH:].astype(BF16)

    sc_b = r_ref[:, 0:SC_WIDTH]
    sc_ext[c0:c0 + ts, :] = r_ref[:, SC_WIDTH:2 * SC_WIDTH] * r_ref[:, 2 * SC_WIDTH:3 * SC_WIDTH]
    rg_ext[c0:c0 + ts, :] = r_ref[:, 3 * SC_WIDTH:3 * SC_WIDTH + RG_WIDTH]
    rg_g = r_ref[:, 3 * SC_WIDTH + RG_WIDTH:]

    conv = scb_ref[...]
    for k in range(SC_CONV):
        off = c0 - (SC_CONV - 1) + k
        conv = conv + scw_ref[k:k + 1, :] * sc_ext[off:off + ts, :]
    conv_out = sc_b * conv

    rg_in = rgb_ref[...]
    for k in range(RG_CONV):
        off = c0 - (RG_CONV - 1) + k
        rg_in = rg_in + rgw_ref[k:k + 1, :] * rg_ext[off:off + ts, :]

    sc_ext[0:c0, :] = sc_ext[ts:ts + c0, :]
    rg_ext[0:c0, :] = rg_ext[ts:ts + c0, :]

    gates = jnp.dot(rg_in.astype(BF16), wgate_ref[...], preferred_element_type=F32) + bgate_ref[...]

    r_ref[...] = jnp.dot(hx, w_ref[:, attn_cols:], preferred_element_type=F32)

    r_gate = jax.nn.sigmoid(gates[:, :RG_WIDTH])
    i_gate = jax.nn.sigmoid(gates[:, RG_WIDTH:])
    neg_lam = -lam_ref[...]
    softplus = jnp.maximum(neg_lam, 0.0) + jnp.log1p(jnp.exp(-jnp.abs(neg_lam)))
    log_a = -RG_C * r_gate * softplus
    a = jnp.exp(log_a)
    u = jnp.sqrt(jnp.tanh(-log_a) * (1.0 + a * a)) * (i_gate * rg_in)

    s = 1
    while s < ts:
        u = a * _shift_rows(u, s, 0.0) + u
        a = a * _shift_rows(a, s, 1.0)
        s *= 2
    h = a * h_carry[...] + u
    h_carry[...] = h[ts - 1:ts, :]

    c = math.sqrt(2.0 / math.pi)
    gelu = 0.5 * rg_g * (1.0 + jnp.tanh(c * (rg_g + 0.044715 * (rg_g * rg_g * rg_g))))
    o_ref[:, 0:SC_WIDTH] = conv_out.astype(BF16)
    o_ref[:, SC_WIDTH:] = (h * gelu).astype(BF16)


def _in_proj_conv(x, g, w, sc_w, sc_b, rg_w, rg_b, w_gate, b_gate, lam, seq):
    t = x.shape[0]
    n = w.shape[1]
    tiles = t // TS
    assert n == ATTN_WIDTH + 2 * KV_DUP_WIDTH + REST_WIDTH
    proj = lambda width: pl.BlockSpec((TS, width), lambda i: (jnp.minimum(i, tiles - 1), 0))
    lagged = pl.BlockSpec((TS, SC_WIDTH + RG_WIDTH), lambda i: (jnp.maximum(i - 1, 0), 0))
    return pl.pallas_call(
        functools.partial(_in_proj_conv_kernel, tiles_per_seq=seq // TS),
        grid=(tiles + 1,),
        in_specs=[proj(D_MODEL), _const_spec((1, D_MODEL)), _const_spec((D_MODEL, n)),
                  _const_spec((SC_CONV, SC_WIDTH)), _const_spec((1, SC_WIDTH)),
                  _const_spec((RG_CONV, RG_WIDTH)), _const_spec((1, RG_WIDTH)),
                  _const_spec((RG_WIDTH, 2 * RG_WIDTH)), _const_spec((1, 2 * RG_WIDTH)),
                  _const_spec((1, RG_WIDTH))],
        out_specs=[proj(ATTN_WIDTH), proj(KV_DUP_WIDTH), proj(KV_DUP_WIDTH), lagged],
        out_shape=[jax.ShapeDtypeStruct((t, ATTN_WIDTH), BF16),
                   jax.ShapeDtypeStruct((t, KV_DUP_WIDTH), BF16),
                   jax.ShapeDtypeStruct((t, KV_DUP_WIDTH), BF16),
                   jax.ShapeDtypeStruct((t, SC_WIDTH + RG_WIDTH), BF16)],
        scratch_shapes=[pltpu.VMEM((TS, REST_WIDTH), F32),
                        pltpu.VMEM((TS + 2 * CARRY_ROWS, SC_WIDTH), F32),
                        pltpu.VMEM((TS + 2 * CARRY_ROWS, RG_WIDTH), F32),
                        pltpu.VMEM((1, RG_WIDTH), F32)],
        compiler_params=_params(("arbitrary",), 48),
        name="in_proj_conv_rglru",
    )(x, g, w, sc_w, sc_b, rg_w, rg_b, w_gate, b_gate, lam)


def _mem_kv_kernel(m_ref, g_ref, w_ref, k_ref, v_ref):
    h = _rms(m_ref[...], g_ref[...]).astype(BF16)
    p = jnp.dot(h, w_ref[...], preferred_element_type=F32)
    k_ref[...] = p[:, :XA_WIDTH].astype(BF16)
    v_ref[...] = p[:, XA_WIDTH:].astype(BF16)


def _mem_kv(mem, g, wkv):
    t = mem.shape[0]
    row = lambda width: pl.BlockSpec((MEM_LEN, width), lambda i: (i, 0))
    return pl.pallas_call(
        _mem_kv_kernel,
        grid=(t // MEM_LEN,),
        in_specs=[row(D_MODEL), _const_spec((1, D_MODEL)), _const_spec((D_MODEL, 2 * XA_WIDTH))],
        out_specs=[row(XA_WIDTH), row(XA_WIDTH)],
        out_shape=[jax.ShapeDtypeStruct((t, XA_WIDTH), BF16)] * 2,
        compiler_params=_params(("parallel",), 24),
        name="mem_kv",
    )(mem, g, wkv)


def _post_mixer_kernel(x_ref, a_ref, c_ref, wout_ref, g_ref, wq_ref, k_ref, v_ref, wo_ref, *rest):
    o_ref = rest[-1] if len(rest) == 1 else rest[4]
    halves = [slice(i * (TM // 2), (i + 1) * (TM // 2)) for i in range(2)]
    heads = [slice(hd * XA_HEAD_DIM, (hd + 1) * XA_HEAD_DIM) for hd in range(XA_HEADS)]
    nt = (((1,), (1,)), ((), ()))
    k = k_ref[...]
    v = v_ref[...]

    x1 = [x_ref[hs, :] + jnp.dot(jnp.concatenate([a_ref[hs, :], c_ref[hs, :]], axis=-1), wout_ref[...],
                                 preferred_element_type=F32) for hs in halves]
    q = [jnp.dot(_rms(xh, g_ref[...]).astype(BF16), wq_ref[...], preferred_element_type=F32).astype(BF16)
         for xh in x1]
    scores = [[lax.dot_general(qh[:, sl], k[:, sl], nt, preferred_element_type=F32) for sl in heads] for qh in q]
    x2 = []
    for xh, per_head in zip(x1, scores):
        probs, sums = [], []
        for s in per_head:
            s = s * (1.0 / math.sqrt(XA_HEAD_DIM))
            p = jnp.exp(s - jnp.max(s, axis=-1, keepdims=True))
            probs.append(p.astype(BF16))
            sums.append(jnp.sum(p, axis=-1, keepdims=True))
        att = jnp.concatenate([jnp.dot(p, v[:, sl], preferred_element_type=F32) / l
                               for p, sl, l in zip(probs, heads, sums)], axis=-1).astype(BF16)
        x2.append(xh + jnp.dot(att, wo_ref[...], preferred_element_type=F32))

    for hs, xh in zip(halves, x2):
        o_ref[hs, :] = xh
    if len(rest) > 1:
        ffn_g_ref, wr_ref, tri_ref, sel_ref, _, hf_ref, meta_ref, metat_ref, cnt_ref, base_ref, carry = rest
        h = _rms(jnp.concatenate(x2, axis=0), ffn_g_ref[...])
        _route(h, wr_ref, tri_ref, sel_ref, hf_ref, meta_ref, metat_ref, cnt_ref, base_ref, carry)


def _post_mixer(x, attn, cr, w_out, g, wq, k, v, wo, seq, route=None):
    t = x.shape[0]
    per_seq = seq // TM
    row = lambda width: pl.BlockSpec((TM, width), lambda i: (i, 0))
    mem_blk = pl.BlockSpec((MEM_LEN, XA_WIDTH), lambda i: (i // per_seq, 0))
    in_specs = [row(D_MODEL), row(ATTN_WIDTH), row(SC_WIDTH + RG_WIDTH), _const_spec((D_MODEL, D_MODEL)),
                _const_spec((1, D_MODEL)), _const_spec((D_MODEL, XA_WIDTH)), mem_blk, mem_blk,
                _const_spec((XA_WIDTH, D_MODEL))]
    out_specs = [row(D_MODEL)]
    out_shape = [jax.ShapeDtypeStruct((t, D_MODEL), F32)]
    scratch = []
    args = [x, attn, cr, w_out, g, wq, k, v, wo]
    if route is not None:
        assert TM % ROUTE_CHUNK == 0
        in_specs += [_const_spec((1, D_MODEL)), _const_spec((D_MODEL, LANES_V7X)),
                     _const_spec((ROUTE_CHUNK, ROUTE_CHUNK)), _const_spec((SUBLANES_V7X, LANES_V7X))]
        out_specs += [row(D_MODEL), row(LANES_V7X),
                      pl.BlockSpec((SUBLANES_V7X, TM), lambda i: (0, i)),
                      pl.BlockSpec((1, LANES_V7X), lambda i: (0, 0)),
                      pl.BlockSpec((SUBLANES_V7X, LANES_V7X), lambda i: (i, 0))]
        out_shape += [jax.ShapeDtypeStruct((t, D_MODEL), BF16),
                      jax.ShapeDtypeStruct((t, LANES_V7X), F32),
                      jax.ShapeDtypeStruct((SUBLANES_V7X, t), F32),
                      jax.ShapeDtypeStruct((1, LANES_V7X), F32),
                      jax.ShapeDtypeStruct((t // TM * SUBLANES_V7X, LANES_V7X), F32)]
        scratch = [pltpu.VMEM((1, LANES_V7X), F32)]
        args += list(route)
    return pl.pallas_call(
        _post_mixer_kernel,
        grid=(t // TM,),
        in_specs=in_specs,
        out_specs=out_specs,
        out_shape=out_shape,
        scratch_shapes=scratch,
        compiler_params=_params(("arbitrary",), 40),
        name="post_mixer_route" if route is not None else "post_mixer",
    )(*args)


def _ffn_kernel(x_ref, g_ref, wg_ref, wu_ref, wd_ref, *rest):
    n_cast = (len(rest) - 1) // 2
    o_ref = rest[n_cast]
    x = x_ref[...]
    h = _rms(x, g_ref[...]).astype(BF16)
    gate = jnp.dot(h, wg_ref[...], preferred_element_type=F32)
    up = jnp.dot(h, wu_ref[...], preferred_element_type=F32)
    act = (gate * jax.nn.sigmoid(gate) * up).astype(BF16)
    o_ref[...] = x + jnp.dot(act, wd_ref[...], preferred_element_type=F32)
    for src, dst in zip(rest[:n_cast], rest[n_cast + 1:]):
        dst[...] = src[...].astype(BF16)


def _dense_ffn(x, g, wg, wu, wd, cast_along=()):
    t = x.shape[0]
    steps = t // TM
    d_ff = wg.shape[1]
    row = pl.BlockSpec((TM, D_MODEL), lambda i: (i, 0))
    flat = [w.reshape(-1, w.shape[-1]) for w in cast_along]
    for w in flat:
        assert w.shape[0] % (steps * 2 * SUBLANES_V7X) == 0
    slabs = [pl.BlockSpec((w.shape[0] // steps, w.shape[1]), lambda i: (i, 0)) for w in flat]
    outs = pl.pallas_call(
        _ffn_kernel,
        grid=(steps,),
        in_specs=[row, _const_spec((1, D_MODEL)), _const_spec((D_MODEL, d_ff)), _const_spec((D_MODEL, d_ff)),
                  _const_spec((d_ff, D_MODEL))] + slabs,
        out_specs=[row] + slabs,
        out_shape=[jax.ShapeDtypeStruct((t, D_MODEL), F32)] + [jax.ShapeDtypeStruct(w.shape, BF16) for w in flat],
        compiler_params=_params(("parallel",), 60),
        name="dense_swiglu",
    )(x, g, wg, wu, wd, *flat)
    return outs[0], [o.reshape(w.shape) for o, w in zip(outs[1:], cast_along)]


META_E1, META_E2, META_R1, META_R2, META_W1, META_W2 = range(6)


def _to_token_tiles(ref, rows):
    m = rows.shape[0]
    for c in range(ROW_CHUNKS):
        ref[pl.ds(c, m, stride=ROW_CHUNKS), :] = rows[:, c * LANES_V7X:(c + 1) * LANES_V7X]


def _from_token_tiles(ref):
    m = ref.shape[0] // ROW_CHUNKS
    return jnp.concatenate([ref[pl.ds(c, m, stride=ROW_CHUNKS), :] for c in range(ROW_CHUNKS)], axis=-1)


def _route(h, wr_ref, tri_ref, sel_ref, hf_ref, meta_ref, metat_ref, cnt_ref, base_ref, carry):
    @pl.when(pl.program_id(0) == 0)
    def _():
        carry[...] = jnp.zeros_like(carry)

    base_ref[...] = jnp.broadcast_to(carry[...], base_ref.shape)
    chunks = [slice(c * ROUTE_CHUNK, (c + 1) * ROUTE_CHUNK) for c in range(h.shape[0] // ROUTE_CHUNK)]
    each = lambda fn, *lists: [fn(*vals) for vals in zip(*lists)]
    rowmax = lambda a: jnp.max(a, axis=-1, keepdims=True)
    rowsum = lambda a: jnp.sum(a, axis=-1, keepdims=True)
    lane = lax.broadcasted_iota(jnp.int32, (ROUTE_CHUNK, LANES_V7X), 1)
    first_hit = lambda lg, m: jnp.min(jnp.where(lg == m, lane, LANES_V7X), axis=-1, keepdims=True)

    hb = h.astype(BF16)
    hf_ref[...] = hb
    logits = [jnp.dot(hb[c, :], wr_ref[...], preferred_element_type=F32) for c in chunks]
    lg = each(lambda l: jnp.where(lane < N_EXPERTS, l, -jnp.inf), logits)
    m1 = each(rowmax, lg)
    e1 = each(first_hit, lg, m1)
    lg2 = each(lambda l, e: jnp.where(lane == e, -jnp.inf, l), lg, e1)
    m2 = each(rowmax, lg2)
    e2 = each(first_hit, lg2, m2)
    ex = each(lambda a, b: jnp.exp(b - a), m1, m2)
    w1 = each(lambda e: 1.0 / (1.0 + e), ex)
    w2 = each(lambda e: e / (1.0 + e), ex)
    hit1 = each(lambda e: lane == e, e1)
    hit2 = each(lambda e: lane == e, e2)
    onehot = each(lambda a, b: (a | b).astype(BF16), hit1, hit2)
    within = each(lambda o: jnp.dot(tri_ref[...], o, preferred_element_type=F32), onehot)
    totals = each(lambda o: jnp.sum(o.astype(F32), axis=0, keepdims=True), onehot)
    ahead, base = [], carry[...]
    for w, tot in zip(within, totals):
        ahead.append(w + base)
        base = base + tot
    carry[...] = base
    cnt_ref[...] = base
    r1 = each(lambda hit, a: rowsum(jnp.where(hit, a, 0.0)), hit1, ahead)
    r2 = each(lambda hit, a: rowsum(jnp.where(hit, a, 0.0)), hit2, ahead)

    def record(*vals):
        meta = jnp.zeros((ROUTE_CHUNK, LANES_V7X), F32)
        for col, val in zip((META_E1, META_E2, META_R1, META_R2, META_W1, META_W2), vals):
            meta = jnp.where(lane == col, val.astype(F32), meta)
        return meta

    meta = each(record, e1, e2, r1, r2, w1, w2)
    metat = each(lambda mt: lax.dot_general(sel_ref[...], mt, (((1,), (1,)), ((), ())), preferred_element_type=F32,
                                            precision=lax.Precision.HIGHEST), meta)
    for c, mt, mtt in zip(chunks, meta, metat):
        meta_ref[c, :] = mt
        metat_ref[:, c] = mtt


def _token_rows(ref, first_token, n_tokens):
    start = pl.multiple_of(first_token * ROW_CHUNKS, ROW_CHUNKS)
    return ref.at[pl.ds(start, n_tokens * ROW_CHUNKS)]


def _packed_rows(ref, first_token, n_tokens):
    start = pl.multiple_of(first_token * PACKED_ROWS, SUBLANES_V7X)
    return ref.at[pl.ds(start, n_tokens * PACKED_ROWS)]


def _dispatch_kernel(fill_ref, n_ref, off_ref, dst_ref, loc1_ref, loc2_ref, hf_ref, xs_ref, zeros, buf, sem_z, sems):
    i = pl.program_id(0)
    steps = pl.num_programs(0)
    slot_tokens = buf.shape[1] // PACKED_ROWS
    runs = N_EXPERTS + 1

    @pl.when(i == 0)
    def _():
        zeros[...] = jnp.zeros_like(zeros)

        def tile_fill(e):
            return pltpu.make_async_copy(zeros, _packed_rows(xs_ref, pl.multiple_of(fill_ref[e], TM_GROUP), TM_GROUP),
                                         sem_z)

        for e in range(fill_ref.shape[0]):
            @pl.when(fill_ref[e] >= 0)
            def _():
                tile_fill(e).start()
        for e in range(fill_ref.shape[0]):
            @pl.when(fill_ref[e] >= 0)
            def _():
                tile_fill(e).wait()

    k = lax.broadcasted_iota(jnp.int32, (slot_tokens, hf_ref.shape[0]), 0)
    onehot = ((k == loc1_ref[...]) | (k == loc2_ref[...])).astype(BF16)
    sorted_rows = jnp.dot(onehot, hf_ref[...], preferred_element_type=F32)
    bits = pltpu.bitcast(sorted_rows, jnp.uint32)
    half = D_MODEL // 2
    packed = (bits[:, :half] & jnp.uint32(0xFFFF0000)) | (bits[:, half:] >> 16)

    def run(slot):
        whole_slot = pltpu.make_async_copy(buf.at[slot], _packed_rows(xs_ref, 0, slot_tokens), sems.at[slot])

        @pl.when(i >= 2)
        def _():
            whole_slot.wait()

        for c in range(PACKED_ROWS):
            buf.at[slot][pl.ds(c, slot_tokens, stride=PACKED_ROWS), :] = packed[:, c * LANES_V7X:(c + 1) * LANES_V7X]
        for e in range(runs):
            n = n_ref[i * runs + e]
            off = off_ref[i * runs + e]
            dst = dst_ref[i * runs + e]
            bit = hf_ref.shape[0]
            while bit >= 2:
                head = n & ~(2 * bit - 1)

                @pl.when((n & bit) != 0)
                def _(bit=bit, head=head):
                    pltpu.make_async_copy(_packed_rows(buf.at[slot], off + head, bit),
                                          _packed_rows(xs_ref, dst + head, bit), sems.at[slot]).start()
                bit //= 2

        @pl.when(i == steps - 1)
        def _():
            whole_slot.wait()
            other = pltpu.make_async_copy(buf.at[1 - slot], _packed_rows(xs_ref, 0, slot_tokens), sems.at[1 - slot])

            @pl.when(steps >= 2)
            def _():
                other.wait()

    for slot in range(2):
        pl.when(i % 2 == slot)(functools.partial(run, slot))


def _dispatch(fill_start, n_tab, off_tab, dst_tab, loc1, loc2, hf, total_rows):
    t = hf.shape[0]
    loc = pl.BlockSpec((1, TM), lambda i, *_: (0, i))
    u32 = jnp.uint32
    return pl.pallas_call(
        _dispatch_kernel,
        grid_spec=pltpu.PrefetchScalarGridSpec(
            num_scalar_prefetch=4,
            grid=(t // TM,),
            in_specs=[loc, loc, pl.BlockSpec((TM, D_MODEL), lambda i, *_: (i, 0))],
            out_specs=pl.BlockSpec(memory_space=pl.ANY),
            scratch_shapes=[pltpu.VMEM((TM_GROUP * PACKED_ROWS, LANES_V7X), u32),
                            pltpu.VMEM((2, SORTED_TOKENS * PACKED_ROWS, LANES_V7X), u32),
                            pltpu.SemaphoreType.DMA(()), pltpu.SemaphoreType.DMA((2,))]),
        out_shape=jax.ShapeDtypeStruct((total_rows * PACKED_ROWS, LANES_V7X), u32),
        compiler_params=_params(("arbitrary",), 40),
        name="moe_dispatch",
    )(fill_start, n_tab, off_tab, dst_tab, loc1.reshape(1, t), loc2.reshape(1, t), hf)


def _grouped_kernel(te_ref, used_ref, x_ref, wg_ref, wu_ref, wd_ref, y_ref):
    i = pl.program_id(0)

    @pl.when(i < used_ref[0])
    def _():
        tm = x_ref.shape[0] // PACKED_ROWS
        words = jnp.concatenate([x_ref[pl.ds(c, tm, stride=PACKED_ROWS), :] for c in range(PACKED_ROWS)], axis=-1)
        high = pltpu.bitcast(words & jnp.uint32(0xFFFF0000), F32).astype(BF16)
        low = pltpu.bitcast(words << 16, F32).astype(BF16)
        h = jnp.concatenate([high, low], axis=-1)
        ff = wg_ref.shape[1] // FF_SPLIT
        cols = [slice(c * ff, (c + 1) * ff) for c in range(FF_SPLIT)]
        gates = [jnp.dot(h, wg_ref[:, c], preferred_element_type=F32) for c in cols]
        ups = [jnp.dot(h, wu_ref[:, c], preferred_element_type=F32) for c in cols]
        total = None
        for c, gate, up in zip(cols, gates, ups):
            act = (gate * jax.nn.sigmoid(gate) * up).astype(BF16)
            part = jnp.dot(act, wd_ref[c, :], preferred_element_type=F32)
            total = part if total is None else total + part
        _to_token_tiles(y_ref, total)

    @pl.when(i >= used_ref[0])
    def _():
        y_ref[...] = jnp.zeros_like(y_ref)


def _grouped_swiglu(tile_expert, n_used, xs, wg, wu, wd, n_rows):
    d_ff = wg.shape[2]
    tile = (TM_GROUP * ROW_CHUNKS, LANES_V7X)
    packed_tile = (TM_GROUP * PACKED_ROWS, LANES_V7X)
    src = lambda i, te, used: (jnp.maximum(jnp.minimum(i, used[0] - 1), 0), 0)
    expert = lambda shape: pl.BlockSpec((None,) + shape, lambda i, te, used: (te[i], 0, 0))
    return pl.pallas_call(
        _grouped_kernel,
        grid_spec=pltpu.PrefetchScalarGridSpec(
            num_scalar_prefetch=2,
            grid=(n_rows // TM_GROUP,),
            in_specs=[pl.BlockSpec(packed_tile, src), expert((D_MODEL, d_ff)), expert((D_MODEL, d_ff)),
                      expert((d_ff, D_MODEL))],
            out_specs=pl.BlockSpec(tile, lambda i, te, used: (i, 0))),
        out_shape=jax.ShapeDtypeStruct((n_rows * ROW_CHUNKS, LANES_V7X), F32),
        compiler_params=_params(("arbitrary",), 58),
        name="moe_grouped_swiglu",
    )(tile_expert, n_used, xs, wg, wu, wd)


def _combine_kernel(n_ref, off_ref, dst_ref, loc1_ref, loc2_ref, x_ref, meta_ref, g_ref, y_ref, o_ref,
                    slab, y1, y2, sems):
    i = pl.program_id(0)
    tm = x_ref.shape[0]

    def fetch(tile, slot):
        for e in range(N_EXPERTS):
            n = n_ref[tile * N_EXPERTS + e]
            off = off_ref[tile * N_EXPERTS + e]
            dst = dst_ref[tile * N_EXPERTS + e]
            bit = tm
            while bit >= 1:
                head = n & ~(2 * bit - 1)

                @pl.when((n & bit) != 0)
                def _(bit=bit, head=head):
                    pltpu.make_async_copy(_token_rows(y_ref, dst + head, bit),
                                          _token_rows(slab.at[slot], off + head, bit), sems.at[slot]).start()
                bit //= 2

    def finish(slot):
        pltpu.make_async_copy(_token_rows(y_ref, 0, 2 * tm), _token_rows(slab.at[slot], 0, 2 * tm),
                              sems.at[slot]).wait()

        def pick(r, carry):
            y1[pl.ds(pl.multiple_of(r * ROW_CHUNKS, ROW_CHUNKS), ROW_CHUNKS), :] = (
                _token_rows(slab.at[slot], loc1_ref[r], 1)[...])
            y2[pl.ds(pl.multiple_of(r * ROW_CHUNKS, ROW_CHUNKS), ROW_CHUNKS), :] = (
                _token_rows(slab.at[slot], loc2_ref[r], 1)[...])
            return carry

        lax.fori_loop(0, tm, pick, 0, unroll=ISSUE_UNROLL)
        w1 = meta_ref[:, META_W1:META_W1 + 1]
        w2 = meta_ref[:, META_W2:META_W2 + 1]
        out = x_ref[...] + (w1 * _from_token_tiles(y1) + w2 * _from_token_tiles(y2))
        o_ref[...] = _rms(out, g_ref[...])

    @pl.when(i == 0)
    def _():
        fetch(0, 0)

    for slot in range(2):
        @pl.when(i % 2 == slot)
        def _():
            @pl.when(i + 1 < pl.num_programs(0))
            def _():
                fetch(i + 1, 1 - slot)
            finish(slot)


def _combine(n_tab, off_tab, dst_tab, loc1, loc2, x, meta, g, y):
    t = x.shape[0]
    idx = pl.BlockSpec((TM,), lambda i, *_: (i,), memory_space=pltpu.SMEM)
    row = lambda width: pl.BlockSpec((TM, width), lambda i, *_: (i, 0))
    return pl.pallas_call(
        _combine_kernel,
        grid_spec=pltpu.PrefetchScalarGridSpec(
            num_scalar_prefetch=3,
            grid=(t // TM,),
            in_specs=[idx, idx, row(D_MODEL), row(LANES_V7X),
                      pl.BlockSpec((1, D_MODEL), lambda i, *_: (0, 0), pipeline_mode=pl.Buffered(1)),
                      pl.BlockSpec(memory_space=pl.ANY)],
            out_specs=row(D_MODEL),
            scratch_shapes=[pltpu.VMEM((2, SORTED_TOKENS * ROW_CHUNKS, LANES_V7X), F32),
                            pltpu.VMEM((TM * ROW_CHUNKS, LANES_V7X), F32),
                            pltpu.VMEM((TM * ROW_CHUNKS, LANES_V7X), F32),
                            pltpu.SemaphoreType.DMA((2,))]),
        out_shape=jax.ShapeDtypeStruct((t, D_MODEL), F32),
        compiler_params=_params(("arbitrary",), 40),
        name="moe_combine_norm",
    )(n_tab, off_tab, dst_tab, loc1, loc2, x, meta, g, y)


def _route_operands(ffn_g, router_w):
    wr_pad = jnp.zeros((D_MODEL, LANES_V7X), BF16).at[:, :N_EXPERTS].set(router_w.astype(BF16))
    tri = jnp.tril(jnp.ones((ROUTE_CHUNK, ROUTE_CHUNK), BF16), -1)
    sel = jnp.eye(SUBLANES_V7X, LANES_V7X, dtype=F32)
    return ffn_g, wr_pad, tri, sel


def _moe_layer(x, hf, meta, metat, counts, base, wg, wu, wd, final_g):
    t = x.shape[0]
    steps = t // TM
    n_rows = -(-(2 * t + N_EXPERTS * steps) // TM_GROUP) * TM_GROUP + N_EXPERTS * TM_GROUP

    cnt = counts[0, :N_EXPERTS].astype(jnp.int32)
    experts = jnp.arange(N_EXPERTS, dtype=jnp.int32)[:, None]
    e1, e2 = metat[META_E1].astype(jnp.int32), metat[META_E2].astype(jnp.int32)
    r1, r2 = metat[META_R1].astype(jnp.int32), metat[META_R2].astype(jnp.int32)
    pick = lambda e, table: jnp.sum(jnp.where(e[None, :] == experts, table, 0), axis=0)
    before = base.reshape(steps, SUBLANES_V7X, LANES_V7X)[:, 0, :N_EXPERTS].astype(jnp.int32)
    n_tab = jnp.concatenate([before[1:], cnt[None, :]], axis=0) - before
    n_even = n_tab + (n_tab & 1)
    padded = (jnp.sum(n_even, axis=0) + TM_GROUP - 1) // TM_GROUP * TM_GROUP
    ends = jnp.cumsum(padded)
    starts = ends - padded
    off_tab = jnp.cumsum(n_even, axis=1) - n_even
    dst_tab = starts[None, :] + jnp.cumsum(n_even, axis=0) - n_even
    shift = jnp.repeat((off_tab - before).T, TM, axis=1)
    loc1 = pick(e1, shift) + r1
    loc2 = pick(e2, shift) + r2
    sent = jnp.sum(n_even, axis=1, keepdims=True)
    scratch = n_rows + N_EXPERTS * jnp.arange(steps, dtype=jnp.int32)[:, None]
    send = [jnp.concatenate(pair, axis=1) for pair in ((n_even, SORTED_TOKENS - sent), (off_tab, sent),
                                                       (dst_tab, scratch))]
    n_used = (ends[-1] // TM_GROUP).astype(jnp.int32).reshape(1)
    tile_row = jnp.minimum(jnp.arange(n_rows // TM_GROUP, dtype=jnp.int32), n_used[0] - 1) * TM_GROUP
    tile_expert = jnp.sum(ends[None, :] <= tile_row[:, None], axis=1).astype(jnp.int32)
    tail_start = jnp.where(padded > 0, ends - TM_GROUP, -1)
    spare = ends[-1] + TM_GROUP * jnp.arange(2 * N_EXPERTS, dtype=jnp.int32)
    scratch_tiles = -(-N_EXPERTS * steps // TM_GROUP)
    scratch_fill = n_rows + TM_GROUP * jnp.arange(scratch_tiles, dtype=jnp.int32)
    fill_start = jnp.concatenate([tail_start, jnp.where(spare < n_rows, spare, -1), scratch_fill]).astype(jnp.int32)

    flat = lambda tables: [a.reshape(-1).astype(jnp.int32) for a in tables]
    xs = _dispatch(fill_start, *flat(send), loc1, loc2, hf, n_rows + scratch_tiles * TM_GROUP)
    y = _grouped_swiglu(tile_expert, n_used, xs, wg, wu, wd, n_rows)
    return _combine(*flat((n_tab, off_tab, dst_tab)), loc1, loc2, x, meta, final_g, y)


def _arrange_in_proj(w):
    q = (w[:, :ATTN_WIDTH] * (1.0 / math.sqrt(HEAD_DIM))).astype(BF16)
    w = w.astype(BF16)
    dup = lambda start: [w[:, start + h * HEAD_DIM:start + (h + 1) * HEAD_DIM]
                         for h in range(ATTN_KV_HEADS) for _ in range(2)]
    return jnp.concatenate([q] + dup(ATTN_WIDTH) + dup(ATTN_WIDTH + KV_WIDTH) + [w[:, ATTN_WIDTH + 2 * KV_WIDTH:]],
                           axis=1)


def _block_diag(w):
    heads, d, _ = w.shape
    eye = jnp.eye(heads, dtype=w.dtype)
    return jnp.einsum('hij,hg->higj', w, eye).reshape(heads * d, heads * d)


def kernel(x, mem, rel_bias, mix_norm, w_in, attn_sinks, sc_conv_w, sc_conv_b, rg_conv_w, rg_conv_b, rg_w_a,
           rg_b_a, rg_w_x, rg_b_x, rg_lambda, w_out, xa_norm, mem_norm, xa_wq, xa_wk, xa_wv, xa_wo, ffn_norm,
           dense_wg, dense_wu, dense_wd, moe_router, moe_wg, moe_wu, moe_wd, final_norm):
    batch, seq, _ = x.shape
    depth = w_in.shape[0]
    assert depth == 2 and seq % TS == 0 and seq % TM == 0
    xt = x.reshape(batch * seq, D_MODEL)
    memt = mem.reshape(batch * MEM_LEN, D_MODEL)
    bias_tbl = _attention_bias_tables(rel_bias)
    vec = lambda a: a.reshape(1, -1)

    for layer in range(depth):
        w_gate = jnp.concatenate([_block_diag(rg_w_a[layer]), _block_diag(rg_w_x[layer])], axis=1).astype(BF16)
        b_gate = jnp.concatenate([rg_b_a[layer], rg_b_x[layer]]).reshape(1, -1)
        q, k, v, cr = _in_proj_conv(xt, vec(mix_norm[layer]), _arrange_in_proj(w_in[layer]),
                                    sc_conv_w[layer], vec(sc_conv_b[layer]), rg_conv_w[layer],
                                    vec(rg_conv_b[layer]), w_gate, b_gate, vec(rg_lambda[layer]), seq)
        attn = _attention(q, k, v, attn_sinks[layer], bias_tbl, seq // BLOCK)
        wkv = jnp.concatenate([xa_wk[layer], xa_wv[layer]], axis=1).astype(BF16)
        mk, mv = _mem_kv(memt, vec(mem_norm[layer]), wkv)
        post = functools.partial(_post_mixer, xt, attn, cr, w_out[layer].astype(BF16), vec(xa_norm[layer]),
                                 xa_wq[layer].astype(BF16), mk, mv, xa_wo[layer].astype(BF16), seq)

        j = layer // 2
        if layer % 2 == 0:
            (xt,) = post()
            xt, expert_w = _dense_ffn(xt, vec(ffn_norm[layer]), dense_wg[j].astype(BF16), dense_wu[j].astype(BF16),
                                      dense_wd[j].astype(BF16), cast_along=(moe_wg[j], moe_wu[j], moe_wd[j]))
        else:
            xt, hf, meta, metat, counts, base = post(route=_route_operands(vec(ffn_norm[layer]), moe_router[j]))
            xt = _moe_layer(xt, hf, meta, metat, counts, base, *expert_w, vec(final_norm))
    return xt.reshape(batch, seq, D_MODEL)
```

```python
import functools
import math

import jax
import jax.numpy as jnp
import numpy as np
from jax import lax
from jax.experimental import pallas as pl
from jax.experimental.pallas import tpu as pltpu

F32 = jnp.float32
BF16 = jnp.bfloat16

D_MODEL = 1024
MEM_LEN = 256
HEAD_DIM = 64
ATTN_Q_HEADS = 8
ATTN_KV_HEADS = 2
ATTN_WIDTH = ATTN_Q_HEADS * HEAD_DIM
KV_WIDTH = ATTN_KV_HEADS * HEAD_DIM
KV_DUP_WIDTH = 2 * KV_WIDTH
BLOCK = 128
SC_WIDTH = 256
SC_CONV = 3
RG_WIDTH = 256
RG_CONV = 4
RG_C = 8.0
N_BUCKETS = 32
MAX_EXACT = N_BUCKETS // 2
MAX_DISTANCE = 128
XA_HEADS = 4
XA_HEAD_DIM = 128
XA_WIDTH = XA_HEADS * XA_HEAD_DIM
N_EXPERTS = 8
EPS = 1e-6
NEG_INF = -1e30
REST_WIDTH = 3 * SC_WIDTH + 2 * RG_WIDTH

LANES_V7X = 128
SUBLANES_V7X = 8
VMEM_BYTES_V7X = 64 * 1024 * 1024
ROW_CHUNKS = D_MODEL // LANES_V7X
assert ROW_CHUNKS == SUBLANES_V7X

TM = 512
ATTN_BLOCKS = 8
TS = 512
ROUTE_CHUNK = 128
TM_GROUP = 512
ISSUE_UNROLL = 8
PACKED_ROWS = ROW_CHUNKS // 2
SORTED_TOKENS = 2 * TM + N_EXPERTS
FF_SPLIT = 2
CARRY_ROWS = SUBLANES_V7X


def _mib(n):
    return int(n * 1024 * 1024)


def _params(semantics, vmem_mib):
    assert _mib(vmem_mib) < VMEM_BYTES_V7X
    return pltpu.CompilerParams(dimension_semantics=semantics, vmem_limit_bytes=_mib(vmem_mib))


def _rms(x, g):
    ms = jnp.mean(x * x, axis=-1, keepdims=True)
    return x * lax.rsqrt(ms + EPS) * g


def _const_spec(shape):
    nd = len(shape)
    return pl.BlockSpec(shape, lambda *_: (0,) * nd, pipeline_mode=pl.Buffered(1))


def _attn_kernel(sink_ref, q_ref, kp_ref, kc_ref, vp_ref, vc_ref, bias0_ref, bias_ref, o_ref):
    pairs_per_group = ATTN_Q_HEADS // ATTN_KV_HEADS // 2
    row = lax.broadcasted_iota(jnp.int32, (BLOCK, BLOCK), 0)
    col = lax.broadcasted_iota(jnp.int32, (BLOCK, BLOCK), 1)
    from_prev = col > row
    low_lanes = lax.broadcasted_iota(jnp.int32, (2 * BLOCK, 2 * HEAD_DIM), 1) < HEAD_DIM
    low_out = col < HEAD_DIM
    zero = jnp.zeros((), BF16)

    def block_diag(band):
        return jnp.concatenate([jnp.where(low_lanes, band, zero), jnp.where(low_lanes, zero, band)], axis=0)

    def scores(blk):
        rows = slice(blk * BLOCK, (blk + 1) * BLOCK)
        prev_rows = slice((blk - 1) * BLOCK, blk * BLOCK)
        out = []
        for g in range(ATTN_KV_HEADS):
            lanes = slice(g * 2 * HEAD_DIM, (g + 1) * 2 * HEAD_DIM)
            k_prev = kp_ref[:, lanes] if blk == 0 else kc_ref[prev_rows, lanes]
            v_prev = vp_ref[:, lanes] if blk == 0 else vc_ref[prev_rows, lanes]
            k_bd = block_diag(jnp.concatenate([k_prev, kc_ref[rows, lanes]], axis=0))
            v_bd = block_diag(jnp.concatenate([v_prev, vc_ref[rows, lanes]], axis=0))
            for pair in range(pairs_per_group):
                slab = g * pairs_per_group + pair
                q2 = q_ref[rows, slab * 2 * HEAD_DIM:(slab + 1) * 2 * HEAD_DIM]
                s = lax.dot_general(q2, k_bd, (((1,), (1,)), ((), ())), preferred_element_type=F32)
                out.append((slab, s, v_bd))
        return out

    def finish(blk, scored):
        rows = slice(blk * BLOCK, (blk + 1) * BLOCK)
        tbl_ref = bias0_ref if blk == 0 else bias_ref
        staged = []
        for slab, s, v_bd in scored:
            probs, denoms = [], []
            for side in range(2):
                h = 2 * slab + side
                sh = s[:, side * 2 * BLOCK:(side + 1) * 2 * BLOCK]
                logits = jnp.where(from_prev, sh[:, :BLOCK], sh[:, BLOCK:]) + tbl_ref[h]
                sink = sink_ref[h]
                m = jnp.maximum(jnp.max(logits, axis=-1, keepdims=True), sink)
                p = jnp.exp(logits - m)
                denoms.append(jnp.sum(p, axis=-1, keepdims=True) + jnp.exp(sink - m))
                probs += [jnp.where(from_prev, p, 0.0), jnp.where(from_prev, 0.0, p)]
            staged.append((slab, jnp.concatenate(probs, axis=-1).astype(BF16), v_bd, denoms))
        for slab, p_band, v_bd, denoms in staged:
            o = jnp.dot(p_band, v_bd, preferred_element_type=F32)
            o = o / jnp.where(low_out, denoms[0], denoms[1])
            o_ref[rows, slab * 2 * HEAD_DIM:(slab + 1) * 2 * HEAD_DIM] = o.astype(BF16)

    pending = scores(0)
    for blk in range(ATTN_BLOCKS):
        upcoming = scores(blk + 1) if blk + 1 < ATTN_BLOCKS else None
        finish(blk, pending)
        pending = upcoming


def _attention(q, k, v, sinks, bias_tbl, blocks_per_seq):
    t = q.shape[0]
    tile = ATTN_BLOCKS * BLOCK
    cur = lambda i: (i, 0)
    prev = lambda i: (jnp.maximum(i * ATTN_BLOCKS - 1, 0), 0)
    tbl = (None, ATTN_Q_HEADS, BLOCK, BLOCK)
    return pl.pallas_call(
        _attn_kernel,
        grid=(t // tile,),
        in_specs=[pl.BlockSpec(memory_space=pltpu.SMEM),
                  pl.BlockSpec((tile, ATTN_WIDTH), cur),
                  pl.BlockSpec((BLOCK, KV_DUP_WIDTH), prev),
                  pl.BlockSpec((tile, KV_DUP_WIDTH), cur),
                  pl.BlockSpec((BLOCK, KV_DUP_WIDTH), prev),
                  pl.BlockSpec((tile, KV_DUP_WIDTH), cur),
                  pl.BlockSpec(tbl, lambda i: (jnp.minimum((i * ATTN_BLOCKS) % blocks_per_seq, 1), 0, 0, 0)),
                  pl.BlockSpec(tbl, lambda i: (1, 0, 0, 0))],
        out_specs=pl.BlockSpec((tile, ATTN_WIDTH), cur),
        out_shape=jax.ShapeDtypeStruct((t, ATTN_WIDTH), BF16),
        compiler_params=_params(("parallel",), 24),
        name="swa_attention",
    )(sinks, q, k, k, v, v, bias_tbl, bias_tbl)


def _bias_table_kernel(rel_ref, bucket_ref, o_ref):
    for v in range(2):
        bucket = bucket_ref[v]
        hits = [bucket == b for b in range(N_BUCKETS)]
        for h in range(ATTN_Q_HEADS):
            tbl = jnp.full(bucket.shape, NEG_INF, F32)
            for b in range(N_BUCKETS):
                tbl = jnp.where(hits[b], rel_ref[b * ATTN_Q_HEADS + h], tbl)
            o_ref[v, h] = tbl


def _attention_bias_tables(rel_bias):
    q_idx = np.arange(BLOCK)[:, None]
    j_idx = np.arange(BLOCK)[None, :]
    from_prev = j_idx > q_idx
    n = np.where(from_prev, q_idx + BLOCK - j_idx, q_idx - j_idx)
    large = MAX_EXACT + (np.log(np.maximum(n, 1).astype(np.float32) / np.float32(MAX_EXACT))
                         / np.float32(math.log(MAX_DISTANCE / MAX_EXACT))
                         * np.float32(N_BUCKETS - MAX_EXACT)).astype(np.int32)
    bucket = np.where(n < MAX_EXACT, n, np.minimum(large, N_BUCKETS - 1))
    first = np.where(from_prev, -1, bucket)
    buckets = jnp.asarray(np.stack([first, bucket]).astype(np.int32))
    return pl.pallas_call(
        _bias_table_kernel,
        in_specs=[pl.BlockSpec(memory_space=pltpu.SMEM), pl.BlockSpec(memory_space=pltpu.VMEM)],
        out_specs=pl.BlockSpec(memory_space=pltpu.VMEM),
        out_shape=jax.ShapeDtypeStruct((2, ATTN_Q_HEADS, BLOCK, BLOCK), F32),
        name="t5_bias_table",
    )(rel_bias.astype(F32).reshape(-1), buckets)


def _shift_rows(x, s, fill):
    return jnp.concatenate([jnp.full((s, x.shape[1]), fill, x.dtype), x[:x.shape[0] - s]], axis=0)


def _in_proj_conv_kernel(x_ref, g_ref, w_ref, scw_ref, scb_ref, rgw_ref, rgb_ref, wgate_ref, bgate_ref, lam_ref,
                         q_ref, k_ref, v_ref, o_ref, r_ref, sc_ext, rg_ext, h_carry, *, tiles_per_seq):
    i = pl.program_id(0)
    ts = r_ref.shape[0]
    c0 = CARRY_ROWS

    @pl.when(i == 0)
    def _():
        r_ref[...] = jnp.zeros_like(r_ref)

    @pl.when((i == 0) | ((i + tiles_per_seq - 1) % tiles_per_seq == 0))
    def _():
        sc_ext[0:c0, :] = jnp.zeros((c0, SC_WIDTH), F32)
        rg_ext[0:c0, :] = jnp.zeros((c0, RG_WIDTH), F32)
        h_carry[...] = jnp.zeros_like(h_carry)

    attn_cols = ATTN_WIDTH + 2 * KV_DUP_WIDTH
    hx = _rms(x_ref[...], g_ref[...]).astype(BF16)
    p = jnp.dot(hx, w_ref[:, :attn_cols], preferred_element_type=F32)
    q_ref[...] = p[:, :ATTN_WIDTH].astype(BF16)
    k_ref[...] = p[:, ATTN_WIDTH:ATTN_WIDTH + KV_DUP_WIDTH].astype(BF16)
    v_ref[...] = p[:, ATTN_WIDTH + KV_DUP_WIDTH:].astype(BF16)

    sc_b = r_ref[:, 0:SC_WIDTH]
    sc_ext[c0:c0 + ts, :] = r_ref[:, SC_WIDTH:2 * SC_WIDTH] * r_ref[:, 2 * SC_WIDTH:3 * SC_WIDTH]
    rg_ext[c0:c0 + ts, :] = r_ref[:, 3 * SC_WIDTH:3 * SC_WIDTH + RG_WIDTH]
    rg_g = r_ref[:, 3 * SC_WIDTH + RG_WIDTH:]

    conv = scb_ref[...]
    for k in range(SC_CONV):
        off = c0 - (SC_CONV - 1) + k
        conv = conv + scw_ref[k:k + 1, :] * sc_ext[off:off + ts, :]
    conv_out = sc_b * conv

    rg_in = rgb_ref[...]
    for k in range(RG_CONV):
        off = c0 - (RG_CONV - 1) + k
        rg_in = rg_in + rgw_ref[k:k + 1, :] * rg_ext[off:off + ts, :]

    sc_ext[0:c0, :] = sc_ext[ts:ts + c0, :]
    rg_ext[0:c0, :] = rg_ext[ts:ts + c0, :]

    gates = jnp.dot(rg_in.astype(BF16), wgate_ref[...], preferred_element_type=F32) + bgate_ref[...]

    r_ref[...] = jnp.dot(hx, w_ref[:, attn_cols:], preferred_element_type=F32)

    r_gate = jax.nn.sigmoid(gates[:, :RG_WIDTH])
    i_gate = jax.nn.sigmoid(gates[:, RG_WIDTH:])
    neg_lam = -lam_ref[...]
    softplus = jnp.maximum(neg_lam, 0.0) + jnp.log1p(jnp.exp(-jnp.abs(neg_lam)))
    log_a = -RG_C * r_gate * softplus
    a = jnp.exp(log_a)
    u = jnp.sqrt(jnp.tanh(-log_a) * (1.0 + a * a)) * (i_gate * rg_in)

    s = 1
    while s < ts:
        u = a * _shift_rows(u, s, 0.0) + u
        a = a * _shift_rows(a, s, 1.0)
        s *= 2
    h = a * h_carry[...] + u
    h_carry[...] = h[ts - 1:ts, :]

    c = math.sqrt(2.0 / math.pi)
    gelu = 0.5 * rg_g * (1.0 + jnp.tanh(c * (rg_g + 0.044715 * (rg_g * rg_g * rg_g))))
    o_ref[:, 0:SC_WIDTH] = conv_out.astype(BF16)
    o_ref[:, SC_WIDTH:] = (h * gelu).astype(BF16)


def _in_proj_conv(x, g, w, sc_w, sc_b, rg_w, rg_b, w_gate, b_gate, lam, seq):
    t = x.shape[0]
    n = w.shape[1]
    tiles = t // TS
    assert n == ATTN_WIDTH + 2 * KV_DUP_WIDTH + REST_WIDTH
    proj = lambda width: pl.BlockSpec((TS, width), lambda i: (jnp.minimum(i, tiles - 1), 0))
    lagged = pl.BlockSpec((TS, SC_WIDTH + RG_WIDTH), lambda i: (jnp.maximum(i - 1, 0), 0))
    return pl.pallas_call(
        functools.partial(_in_proj_conv_kernel, tiles_per_seq=seq // TS),
        grid=(tiles + 1,),
        in_specs=[proj(D_MODEL), _const_spec((1, D_MODEL)), _const_spec((D_MODEL, n)),
                  _const_spec((SC_CONV, SC_WIDTH)), _const_spec((1, SC_WIDTH)),
                  _const_spec((RG_CONV, RG_WIDTH)), _const_spec((1, RG_WIDTH)),
                  _const_spec((RG_WIDTH, 2 * RG_WIDTH)), _const_spec((1, 2 * RG_WIDTH)),
                  _const_spec((1, RG_WIDTH))],
        out_specs=[proj(ATTN_WIDTH), proj(KV_DUP_WIDTH), proj(KV_DUP_WIDTH), lagged],
        out_shape=[jax.ShapeDtypeStruct((t, ATTN_WIDTH), BF16),
                   jax.ShapeDtypeStruct((t, KV_DUP_WIDTH), BF16),
                   jax.ShapeDtypeStruct((t, KV_DUP_WIDTH), BF16),
                   jax.ShapeDtypeStruct((t, SC_WIDTH + RG_WIDTH), BF16)],
        scratch_shapes=[pltpu.VMEM((TS, REST_WIDTH), F32),
                        pltpu.VMEM((TS + 2 * CARRY_ROWS, SC_WIDTH), F32),
                        pltpu.VMEM((TS + 2 * CARRY_ROWS, RG_WIDTH), F32),
                        pltpu.VMEM((1, RG_WIDTH), F32)],
        compiler_params=_params(("arbitrary",), 48),
        name="in_proj_conv_rglru",
    )(x, g, w, sc_w, sc_b, rg_w, rg_b, w_gate, b_gate, lam)


def _mem_kv_kernel(m_ref, g_ref, w_ref, k_ref, v_ref):
    h = _rms(m_ref[...], g_ref[...]).astype(BF16)
    p = jnp.dot(h, w_ref[...], preferred_element_type=F32)
    k_ref[...] = p[:, :XA_WIDTH].astype(BF16)
    v_ref[...] = p[:, XA_WIDTH:].astype(BF16)


def _mem_kv(mem, g, wkv):
    t = mem.shape[0]
    row = lambda width: pl.BlockSpec((MEM_LEN, width), lambda i: (i, 0))
    return pl.pallas_call(
        _mem_kv_kernel,
        grid=(t // MEM_LEN,),
        in_specs=[row(D_MODEL), _const_spec((1, D_MODEL)), _const_spec((D_MODEL, 2 * XA_WIDTH))],
        out_specs=[row(XA_WIDTH), row(XA_WIDTH)],
        out_shape=[jax.ShapeDtypeStruct((t, XA_WIDTH), BF16)] * 2,
        compiler_params=_params(("parallel",), 24),
        name="mem_kv",
    )(mem, g, wkv)


def _post_mixer_kernel(x_ref, a_ref, c_ref, wout_ref, g_ref, wq_ref, k_ref, v_ref, wo_ref, *rest):
    o_ref = rest[-1] if len(rest) == 1 else rest[4]
    halves = [slice(i * (TM // 2), (i + 1) * (TM // 2)) for i in range(2)]
    heads = [slice(hd * XA_HEAD_DIM, (hd + 1) * XA_HEAD_DIM) for hd in range(XA_HEADS)]
    nt = (((1,), (1,)), ((), ()))
    k = k_ref[...]
    v = v_ref[...]

    x1 = [x_ref[hs, :] + jnp.dot(jnp.concatenate([a_ref[hs, :], c_ref[hs, :]], axis=-1), wout_ref[...],
                                 preferred_element_type=F32) for hs in halves]
    q = [jnp.dot(_rms(xh, g_ref[...]).astype(BF16), wq_ref[...], preferred_element_type=F32).astype(BF16)
         for xh in x1]
    scores = [[lax.dot_general(qh[:, sl], k[:, sl], nt, preferred_element_type=F32) for sl in heads] for qh in q]
    x2 = []
    for xh, per_head in zip(x1, scores):
        probs, sums = [], []
        for s in per_head:
            s = s * (1.0 / math.sqrt(XA_HEAD_DIM))
            p = jnp.exp(s - jnp.max(s, axis=-1, keepdims=True))
            probs.append(p.astype(BF16))
            sums.append(jnp.sum(p, axis=-1, keepdims=True))
        att = jnp.concatenate([jnp.dot(p, v[:, sl], preferred_element_type=F32) / l
                               for p, sl, l in zip(probs, heads, sums)], axis=-1).astype(BF16)
        x2.append(xh + jnp.dot(att, wo_ref[...], preferred_element_type=F32))

    for hs, xh in zip(halves, x2):
        o_ref[hs, :] = xh
    if len(rest) > 1:
        ffn_g_ref, wr_ref, tri_ref, sel_ref, _, hf_ref, meta_ref, metat_ref, cnt_ref, base_ref, carry = rest
        h = _rms(jnp.concatenate(x2, axis=0), ffn_g_ref[...])
        _route(h, wr_ref, tri_ref, sel_ref, hf_ref, meta_ref, metat_ref, cnt_ref, base_ref, carry)


def _post_mixer(x, attn, cr, w_out, g, wq, k, v, wo, seq, route=None):
    t = x.shape[0]
    per_seq = seq // TM
    row = lambda width: pl.BlockSpec((TM, width), lambda i: (i, 0))
    mem_blk = pl.BlockSpec((MEM_LEN, XA_WIDTH), lambda i: (i // per_seq, 0))
    in_specs = [row(D_MODEL), row(ATTN_WIDTH), row(SC_WIDTH + RG_WIDTH), _const_spec((D_MODEL, D_MODEL)),
                _const_spec((1, D_MODEL)), _const_spec((D_MODEL, XA_WIDTH)), mem_blk, mem_blk,
                _const_spec((XA_WIDTH, D_MODEL))]
    out_specs = [row(D_MODEL)]
    out_shape = [jax.ShapeDtypeStruct((t, D_MODEL), F32)]
    scratch = []
    args = [x, attn, cr, w_out, g, wq, k, v, wo]
    if route is not None:
        assert TM % ROUTE_CHUNK == 0
        in_specs += [_const_spec((1, D_MODEL)), _const_spec((D_MODEL, LANES_V7X)),
                     _const_spec((ROUTE_CHUNK, ROUTE_CHUNK)), _const_spec((SUBLANES_V7X, LANES_V7X))]
        out_specs += [row(D_MODEL), row(LANES_V7X),
                      pl.BlockSpec((SUBLANES_V7X, TM), lambda i: (0, i)),
                      pl.BlockSpec((1, LANES_V7X), lambda i: (0, 0)),
                      pl.BlockSpec((SUBLANES_V7X, LANES_V7X), lambda i: (i, 0))]
        out_shape += [jax.ShapeDtypeStruct((t, D_MODEL), BF16),
                      jax.ShapeDtypeStruct((t, LANES_V7X), F32),
                      jax.ShapeDtypeStruct((SUBLANES_V7X, t), F32),
                      jax.ShapeDtypeStruct((1, LANES_V7X), F32),
                      jax.ShapeDtypeStruct((t // TM * SUBLANES_V7X, LANES_V7X), F32)]
        scratch = [pltpu.VMEM((1, LANES_V7X), F32)]
        args += list(route)
    return pl.pallas_call(
        _post_mixer_kernel,
        grid=(t // TM,),
        in_specs=in_specs,
        out_specs=out_specs,
        out_shape=out_shape,
        scratch_shapes=scratch,
        compiler_params=_params(("arbitrary",), 40),
        name="post_mixer_route" if route is not None else "post_mixer",
    )(*args)


def _ffn_kernel(x_ref, g_ref, wg_ref, wu_ref, wd_ref, *rest):
    n_cast = (len(rest) - 1) // 2
    o_ref = rest[n_cast]
    x = x_ref[...]
    h = _rms(x, g_ref[...]).astype(BF16)
    gate = jnp.dot(h, wg_ref[...], preferred_element_type=F32)
    up = jnp.dot(h, wu_ref[...], preferred_element_type=F32)
    act = (gate * jax.nn.sigmoid(gate) * up).astype(BF16)
    o_ref[...] = x + jnp.dot(act, wd_ref[...], preferred_element_type=F32)
    for src, dst in zip(rest[:n_cast], rest[n_cast + 1:]):
        dst[...] = src[...].astype(BF16)


def _dense_ffn(x, g, wg, wu, wd, cast_along=()):
    t = x.shape[0]
    steps = t // TM
    d_ff = wg.shape[1]
    row = pl.BlockSpec((TM, D_MODEL), lambda i: (i, 0))
    flat = [w.reshape(-1, w.shape[-1]) for w in cast_along]
    for w in flat:
        assert w.shape[0] % (steps * 2 * SUBLANES_V7X) == 0
    slabs = [pl.BlockSpec((w.shape[0] // steps, w.shape[1]), lambda i: (i, 0)) for w in flat]
    outs = pl.pallas_call(
        _ffn_kernel,
        grid=(steps,),
        in_specs=[row, _const_spec((1, D_MODEL)), _const_spec((D_MODEL, d_ff)), _const_spec((D_MODEL, d_ff)),
                  _const_spec((d_ff, D_MODEL))] + slabs,
        out_specs=[row] + slabs,
        out_shape=[jax.ShapeDtypeStruct((t, D_MODEL), F32)] + [jax.ShapeDtypeStruct(w.shape, BF16) for w in flat],
        compiler_params=_params(("parallel",), 60),
        name="dense_swiglu",
    )(x, g, wg, wu, wd, *flat)
    return outs[0], [o.reshape(w.shape) for o, w in zip(outs[1:], cast_along)]


META_E1, META_E2, META_R1, META_R2, META_W1, META_W2 = range(6)


def _to_token_tiles(ref, rows):
    m = rows.shape[0]
    for c in range(ROW_CHUNKS):
        ref[pl.ds(c, m, stride=ROW_CHUNKS), :] = rows[:, c * LANES_V7X:(c + 1) * LANES_V7X]


def _from_token_tiles(ref):
    m = ref.shape[0] // ROW_CHUNKS
    return jnp.concatenate([ref[pl.ds(c, m, stride=ROW_CHUNKS), :] for c in range(ROW_CHUNKS)], axis=-1)


def _route(h, wr_ref, tri_ref, sel_ref, hf_ref, meta_ref, metat_ref, cnt_ref, base_ref, carry):
    @pl.when(pl.program_id(0) == 0)
    def _():
        carry[...] = jnp.zeros_like(carry)

    base_ref[...] = jnp.broadcast_to(carry[...], base_ref.shape)
    chunks = [slice(c * ROUTE_CHUNK, (c + 1) * ROUTE_CHUNK) for c in range(h.shape[0] // ROUTE_CHUNK)]
    each = lambda fn, *lists: [fn(*vals) for vals in zip(*lists)]
    rowmax = lambda a: jnp.max(a, axis=-1, keepdims=True)
    rowsum = lambda a: jnp.sum(a, axis=-1, keepdims=True)
    lane = lax.broadcasted_iota(jnp.int32, (ROUTE_CHUNK, LANES_V7X), 1)
    first_hit = lambda lg, m: jnp.min(jnp.where(lg == m, lane, LANES_V7X), axis=-1, keepdims=True)

    hb = h.astype(BF16)
    hf_ref[...] = hb
    logits = [jnp.dot(hb[c, :], wr_ref[...], preferred_element_type=F32) for c in chunks]
    lg = each(lambda l: jnp.where(lane < N_EXPERTS, l, -jnp.inf), logits)
    m1 = each(rowmax, lg)
    e1 = each(first_hit, lg, m1)
    lg2 = each(lambda l, e: jnp.where(lane == e, -jnp.inf, l), lg, e1)
    m2 = each(rowmax, lg2)
    e2 = each(first_hit, lg2, m2)
    ex = each(lambda a, b: jnp.exp(b - a), m1, m2)
    w1 = each(lambda e: 1.0 / (1.0 + e), ex)
    w2 = each(lambda e: e / (1.0 + e), ex)
    hit1 = each(lambda e: lane == e, e1)
    hit2 = each(lambda e: lane == e, e2)
    onehot = each(lambda a, b: (a | b).astype(BF16), hit1, hit2)
    within = each(lambda o: jnp.dot(tri_ref[...], o, preferred_element_type=F32), onehot)
    totals = each(lambda o: jnp.sum(o.astype(F32), axis=0, keepdims=True), onehot)
    ahead, base = [], carry[...]
    for w, tot in zip(within, totals):
        ahead.append(w + base)
        base = base + tot
    carry[...] = base
    cnt_ref[...] = base
    r1 = each(lambda hit, a: rowsum(jnp.where(hit, a, 0.0)), hit1, ahead)
    r2 = each(lambda hit, a: rowsum(jnp.where(hit, a, 0.0)), hit2, ahead)

    def record(*vals):
        meta = jnp.zeros((ROUTE_CHUNK, LANES_V7X), F32)
        for col, val in zip((META_E1, META_E2, META_R1, META_R2, META_W1, META_W2), vals):
            meta = jnp.where(lane == col, val.astype(F32), meta)
        return meta

    meta = each(record, e1, e2, r1, r2, w1, w2)
    metat = each(lambda mt: lax.dot_general(sel_ref[...], mt, (((1,), (1,)), ((), ())), preferred_element_type=F32,
                                            precision=lax.Precision.HIGHEST), meta)
    for c, mt, mtt in zip(chunks, meta, metat):
        meta_ref[c, :] = mt
        metat_ref[:, c] = mtt


def _token_rows(ref, first_token, n_tokens):
    start = pl.multiple_of(first_token * ROW_CHUNKS, ROW_CHUNKS)
    return ref.at[pl.ds(start, n_tokens * ROW_CHUNKS)]


def _packed_rows(ref, first_token, n_tokens):
    start = pl.multiple_of(first_token * PACKED_ROWS, SUBLANES_V7X)
    return ref.at[pl.ds(start, n_tokens * PACKED_ROWS)]


def _dispatch_kernel(fill_ref, n_ref, off_ref, dst_ref, loc1_ref, loc2_ref, hf_ref, xs_ref, zeros, buf, sem_z, sems):
    i = pl.program_id(0)
    steps = pl.num_programs(0)
    slot_tokens = buf.shape[1] // PACKED_ROWS
    runs = N_EXPERTS + 1

    @pl.when(i == 0)
    def _():
        zeros[...] = jnp.zeros_like(zeros)

        def tile_fill(e):
            return pltpu.make_async_copy(zeros, _packed_rows(xs_ref, pl.multiple_of(fill_ref[e], TM_GROUP), TM_GROUP),
                                         sem_z)

        for e in range(fill_ref.shape[0]):
            @pl.when(fill_ref[e] >= 0)
            def _():
                tile_fill(e).start()
        for e in range(fill_ref.shape[0]):
            @pl.when(fill_ref[e] >= 0)
            def _():
                tile_fill(e).wait()

    k = lax.broadcasted_iota(jnp.int32, (slot_tokens, hf_ref.shape[0]), 0)
    onehot = ((k == loc1_ref[...]) | (k == loc2_ref[...])).astype(BF16)
    sorted_rows = jnp.dot(onehot, hf_ref[...], preferred_element_type=F32)
    bits = pltpu.bitcast(sorted_rows, jnp.uint32)
    half = D_MODEL // 2
    packed = (bits[:, :half] & jnp.uint32(0xFFFF0000)) | (bits[:, half:] >> 16)

    def run(slot):
        whole_slot = pltpu.make_async_copy(buf.at[slot], _packed_rows(xs_ref, 0, slot_tokens), sems.at[slot])

        @pl.when(i >= 2)
        def _():
            whole_slot.wait()

        for c in range(PACKED_ROWS):
            buf.at[slot][pl.ds(c, slot_tokens, stride=PACKED_ROWS), :] = packed[:, c * LANES_V7X:(c + 1) * LANES_V7X]
        for e in range(runs):
            n = n_ref[i * runs + e]
            off = off_ref[i * runs + e]
            dst = dst_ref[i * runs + e]
            bit = hf_ref.shape[0]
            while bit >= 2:
                head = n & ~(2 * bit - 1)

                @pl.when((n & bit) != 0)
                def _(bit=bit, head=head):
                    pltpu.make_async_copy(_packed_rows(buf.at[slot], off + head, bit),
                                          _packed_rows(xs_ref, dst + head, bit), sems.at[slot]).start()
                bit //= 2

        @pl.when(i == steps - 1)
        def _():
            whole_slot.wait()
            other = pltpu.make_async_copy(buf.at[1 - slot], _packed_rows(xs_ref, 0, slot_tokens), sems.at[1 - slot])

            @pl.when(steps >= 2)
            def _():
                other.wait()

    for slot in range(2):
        pl.when(i % 2 == slot)(functools.partial(run, slot))


def _dispatch(fill_start, n_tab, off_tab, dst_tab, loc1, loc2, hf, total_rows):
    t = hf.shape[0]
    loc = pl.BlockSpec((1, TM), lambda i, *_: (0, i))
    u32 = jnp.uint32
    return pl.pallas_call(
        _dispatch_kernel,
        grid_spec=pltpu.PrefetchScalarGridSpec(
            num_scalar_prefetch=4,
            grid=(t // TM,),
            in_specs=[loc, loc, pl.BlockSpec((TM, D_MODEL), lambda i, *_: (i, 0))],
            out_specs=pl.BlockSpec(memory_space=pl.ANY),
            scratch_shapes=[pltpu.VMEM((TM_GROUP * PACKED_ROWS, LANES_V7X), u32),
                            pltpu.VMEM((2, SORTED_TOKENS * PACKED_ROWS, LANES_V7X), u32),
                            pltpu.SemaphoreType.DMA(()), pltpu.SemaphoreType.DMA((2,))]),
        out_shape=jax.ShapeDtypeStruct((total_rows * PACKED_ROWS, LANES_V7X), u32),
        compiler_params=_params(("arbitrary",), 40),
        name="moe_dispatch",
    )(fill_start, n_tab, off_tab, dst_tab, loc1.reshape(1, t), loc2.reshape(1, t), hf)


def _grouped_kernel(te_ref, used_ref, x_ref, wg_ref, wu_ref, wd_ref, y_ref):
    i = pl.program_id(0)

    @pl.when(i < used_ref[0])
    def _():
        tm = x_ref.shape[0] // PACKED_ROWS
        words = jnp.concatenate([x_ref[pl.ds(c, tm, stride=PACKED_ROWS), :] for c in range(PACKED_ROWS)], axis=-1)
        high = pltpu.bitcast(words & jnp.uint32(0xFFFF0000), F32).astype(BF16)
        low = pltpu.bitcast(words << 16, F32).astype(BF16)
        h = jnp.concatenate([high, low], axis=-1)
        ff = wg_ref.shape[1] // FF_SPLIT
        cols = [slice(c * ff, (c + 1) * ff) for c in range(FF_SPLIT)]
        gates = [jnp.dot(h, wg_ref[:, c], preferred_element_type=F32) for c in cols]
        ups = [jnp.dot(h, wu_ref[:, c], preferred_element_type=F32) for c in cols]
        total = None
        for c, gate, up in zip(cols, gates, ups):
            act = (gate * jax.nn.sigmoid(gate) * up).astype(BF16)
            part = jnp.dot(act, wd_ref[c, :], preferred_element_type=F32)
            total = part if total is None else total + part
        _to_token_tiles(y_ref, total)

    @pl.when(i >= used_ref[0])
    def _():
        y_ref[...] = jnp.zeros_like(y_ref)


def _grouped_swiglu(tile_expert, n_used, xs, wg, wu, wd, n_rows):
    d_ff = wg.shape[2]
    tile = (TM_GROUP * ROW_CHUNKS, LANES_V7X)
    packed_tile = (TM_GROUP * PACKED_ROWS, LANES_V7X)
    src = lambda i, te, used: (jnp.maximum(jnp.minimum(i, used[0] - 1), 0), 0)
    expert = lambda shape: pl.BlockSpec((None,) + shape, lambda i, te, used: (te[i], 0, 0))
    return pl.pallas_call(
        _grouped_kernel,
        grid_spec=pltpu.PrefetchScalarGridSpec(
            num_scalar_prefetch=2,
            grid=(n_rows // TM_GROUP,),
            in_specs=[pl.BlockSpec(packed_tile, src), expert((D_MODEL, d_ff)), expert((D_MODEL, d_ff)),
                      expert((d_ff, D_MODEL))],
            out_specs=pl.BlockSpec(tile, lambda i, te, used: (i, 0))),
        out_shape=jax.ShapeDtypeStruct((n_rows * ROW_CHUNKS, LANES_V7X), F32),
        compiler_params=_params(("arbitrary",), 58),
        name="moe_grouped_swiglu",
    )(tile_expert, n_used, xs, wg, wu, wd)


def _combine_kernel(n_ref, off_ref, dst_ref, loc1_ref, loc2_ref, x_ref, meta_ref, g_ref, y_ref, o_ref,
                    slab, y1, y2, sems):
    i = pl.program_id(0)
    tm = x_ref.shape[0]

    def fetch(tile, slot):
        for e in range(N_EXPERTS):
            n = n_ref[tile * N_EXPERTS + e]
            off = off_ref[tile * N_EXPERTS + e]
            dst = dst_ref[tile * N_EXPERTS + e]
            bit = tm
            while bit >= 1:
                head = n & ~(2 * bit - 1)

                @pl.when((n & bit) != 0)
                def _(bit=bit, head=head):
                    pltpu.make_async_copy(_token_rows(y_ref, dst + head, bit),
                                          _token_rows(slab.at[slot], off + head, bit), sems.at[slot]).start()
                bit //= 2

    def finish(slot):
        pltpu.make_async_copy(_token_rows(y_ref, 0, 2 * tm), _token_rows(slab.at[slot], 0, 2 * tm),
                              sems.at[slot]).wait()

        def pick(r, carry):
            y1[pl.ds(pl.multiple_of(r * ROW_CHUNKS, ROW_CHUNKS), ROW_CHUNKS), :] = (
                _token_rows(slab.at[slot], loc1_ref[r], 1)[...])
            y2[pl.ds(pl.multiple_of(r * ROW_CHUNKS, ROW_CHUNKS), ROW_CHUNKS), :] = (
                _token_rows(slab.at[slot], loc2_ref[r], 1)[...])
            return carry

        lax.fori_loop(0, tm, pick, 0, unroll=ISSUE_UNROLL)
        w1 = meta_ref[:, META_W1:META_W1 + 1]
        w2 = meta_ref[:, META_W2:META_W2 + 1]
        out = x_ref[...] + (w1 * _from_token_tiles(y1) + w2 * _from_token_tiles(y2))
        o_ref[...] = _rms(out, g_ref[...])

    @pl.when(i == 0)
    def _():
        fetch(0, 0)

    for slot in range(2):
        @pl.when(i % 2 == slot)
        def _():
            @pl.when(i + 1 < pl.num_programs(0))
            def _():
                fetch(i + 1, 1 - slot)
            finish(slot)


def _combine(n_tab, off_tab, dst_tab, loc1, loc2, x, meta, g, y):
    t = x.shape[0]
    idx = pl.BlockSpec((TM,), lambda i, *_: (i,), memory_space=pltpu.SMEM)
    row = lambda width: pl.BlockSpec((TM, width), lambda i, *_: (i, 0))
    return pl.pallas_call(
        _combine_kernel,
        grid_spec=pltpu.PrefetchScalarGridSpec(
            num_scalar_prefetch=3,
            grid=(t // TM,),
            in_specs=[idx, idx, row(D_MODEL), row(LANES_V7X),
                      pl.BlockSpec((1, D_MODEL), lambda i, *_: (0, 0), pipeline_mode=pl.Buffered(1)),
                      pl.BlockSpec(memory_space=pl.ANY)],
            out_specs=row(D_MODEL),
            scratch_shapes=[pltpu.VMEM((2, SORTED_TOKENS * ROW_CHUNKS, LANES_V7X), F32),
                            pltpu.VMEM((TM * ROW_CHUNKS, LANES_V7X), F32),
                            pltpu.VMEM((TM * ROW_CHUNKS, LANES_V7X), F32),
                            pltpu.SemaphoreType.DMA((2,))]),
        out_shape=jax.ShapeDtypeStruct((t, D_MODEL), F32),
        compiler_params=_params(("arbitrary",), 40),
        name="moe_combine_norm",
    )(n_tab, off_tab, dst_tab, loc1, loc2, x, meta, g, y)


def _route_operands(ffn_g, router_w):
    wr_pad = jnp.zeros((D_MODEL, LANES_V7X), BF16).at[:, :N_EXPERTS].set(router_w.astype(BF16))
    tri = jnp.tril(jnp.ones((ROUTE_CHUNK, ROUTE_CHUNK), BF16), -1)
    sel = jnp.eye(SUBLANES_V7X, LANES_V7X, dtype=F32)
    return ffn_g, wr_pad, tri, sel


def _moe_layer(x, hf, meta, metat, counts, base, wg, wu, wd, final_g):
    t = x.shape[0]
    steps = t // TM
    n_rows = -(-(2 * t + N_EXPERTS * steps) // TM_GROUP) * TM_GROUP + N_EXPERTS * TM_GROUP

    cnt = counts[0, :N_EXPERTS].astype(jnp.int32)
    experts = jnp.arange(N_EXPERTS, dtype=jnp.int32)[:, None]
    e1, e2 = metat[META_E1].astype(jnp.int32), metat[META_E2].astype(jnp.int32)
    r1, r2 = metat[META_R1].astype(jnp.int32), metat[META_R2].astype(jnp.int32)
    pick = lambda e, table: jnp.sum(jnp.where(e[None, :] == experts, table, 0), axis=0)
    before = base.reshape(steps, SUBLANES_V7X, LANES_V7X)[:, 0, :N_EXPERTS].astype(jnp.int32)
    n_tab = jnp.concatenate([before[1:], cnt[None, :]], axis=0) - before
    n_even = n_tab + (n_tab & 1)
    padded = (jnp.sum(n_even, axis=0) + TM_GROUP - 1) // TM_GROUP * TM_GROUP
    ends = jnp.cumsum(padded)
    starts = ends - padded
    off_tab = jnp.cumsum(n_even, axis=1) - n_even
    dst_tab = starts[None, :] + jnp.cumsum(n_even, axis=0) - n_even
    shift = jnp.repeat((off_tab - before).T, TM, axis=1)
    loc1 = pick(e1, shift) + r1
    loc2 = pick(e2, shift) + r2
    sent = jnp.sum(n_even, axis=1, keepdims=True)
    scratch = n_rows + N_EXPERTS * jnp.arange(steps, dtype=jnp.int32)[:, None]
    send = [jnp.concatenate(pair, axis=1) for pair in ((n_even, SORTED_TOKENS - sent), (off_tab, sent),
                                                       (dst_tab, scratch))]
    n_used = (ends[-1] // TM_GROUP).astype(jnp.int32).reshape(1)
    tile_row = jnp.minimum(jnp.arange(n_rows // TM_GROUP, dtype=jnp.int32), n_used[0] - 1) * TM_GROUP
    tile_expert = jnp.sum(ends[None, :] <= tile_row[:, None], axis=1).astype(jnp.int32)
    tail_start = jnp.where(padded > 0, ends - TM_GROUP, -1)
    spare = ends[-1] + TM_GROUP * jnp.arange(2 * N_EXPERTS, dtype=jnp.int32)
    scratch_tiles = -(-N_EXPERTS * steps // TM_GROUP)
    scratch_fill = n_rows + TM_GROUP * jnp.arange(scratch_tiles, dtype=jnp.int32)
    fill_start = jnp.concatenate([tail_start, jnp.where(spare < n_rows, spare, -1), scratch_fill]).astype(jnp.int32)

    flat = lambda tables: [a.reshape(-1).astype(jnp.int32) for a in tables]
    xs = _dispatch(fill_start, *flat(send), loc1, loc2, hf, n_rows + scratch_tiles * TM_GROUP)
    y = _grouped_swiglu(tile_expert, n_used, xs, wg, wu, wd, n_rows)
    return _combine(*flat((n_tab, off_tab, dst_tab)), loc1, loc2, x, meta, final_g, y)


def _arrange_in_proj(w):
    q = (w[:, :ATTN_WIDTH] * (1.0 / math.sqrt(HEAD_DIM))).astype(BF16)
    w = w.astype(BF16)
    dup = lambda start: [w[:, start + h * HEAD_DIM:start + (h + 1) * HEAD_DIM]
                         for h in range(ATTN_KV_HEADS) for _ in range(2)]
    return jnp.concatenate([q] + dup(ATTN_WIDTH) + dup(ATTN_WIDTH + KV_WIDTH) + [w[:, ATTN_WIDTH + 2 * KV_WIDTH:]],
                           axis=1)


def _block_diag(w):
    heads, d, _ = w.shape
    eye = jnp.eye(heads, dtype=w.dtype)
    return jnp.einsum('hij,hg->higj', w, eye).reshape(heads * d, heads * d)


def kernel(x, mem, rel_bias, mix_norm, w_in, attn_sinks, sc_conv_w, sc_conv_b, rg_conv_w, rg_conv_b, rg_w_a,
           rg_b_a, rg_w_x, rg_b_x, rg_lambda, w_out, xa_norm, mem_norm, xa_wq, xa_wk, xa_wv, xa_wo, ffn_norm,
           dense_wg, dense_wu, dense_wd, moe_router, moe_wg, moe_wu, moe_wd, final_norm):
    batch, seq, _ = x.shape
    depth = w_in.shape[0]
    assert depth == 2 and seq % TS == 0 and seq % TM == 0
    xt = x.reshape(batch * seq, D_MODEL)
    memt = mem.reshape(batch * MEM_LEN, D_MODEL)
    bias_tbl = _attention_bias_tables(rel_bias)
    vec = lambda a: a.reshape(1, -1)

    for layer in range(depth):
        w_gate = jnp.concatenate([_block_diag(rg_w_a[layer]), _block_diag(rg_w_x[layer])], axis=1).astype(BF16)
        b_gate = jnp.concatenate([rg_b_a[layer], rg_b_x[layer]]).reshape(1, -1)
        q, k, v, cr = _in_proj_conv(xt, vec(mix_norm[layer]), _arrange_in_proj(w_in[layer]),
                                    sc_conv_w[layer], vec(sc_conv_b[layer]), rg_conv_w[layer],
                                    vec(rg_conv_b[layer]), w_gate, b_gate, vec(rg_lambda[layer]), seq)
        attn = _attention(q, k, v, attn_sinks[layer], bias_tbl, seq // BLOCK)
        wkv = jnp.concatenate([xa_wk[layer], xa_wv[layer]], axis=1).astype(BF16)
        mk, mv = _mem_kv(memt, vec(mem_norm[layer]), wkv)
        post = functools.partial(_post_mixer, xt, attn, cr, w_out[layer].astype(BF16), vec(xa_norm[layer]),
                                 xa_wq[layer].astype(BF16), mk, mv, xa_wo[layer].astype(BF16), seq)

        j = layer // 2
        if layer % 2 == 0:
            (xt,) = post()
            xt, expert_w = _dense_ffn(xt, vec(ffn_norm[layer]), dense_wg[j].astype(BF16), dense_wu[j].astype(BF16),
                                      dense_wd[j].astype(BF16), cast_along=(moe_wg[j], moe_wu[j], moe_wd[j]))
        else:
            xt, hf, meta, metat, counts, base = post(route=_route_operands(vec(ffn_norm[layer]), moe_router[j]))
            xt = _moe_layer(xt, hf, meta, metat, counts, base, *expert_w, vec(final_norm))
    return xt.reshape(batch, seq, D_MODEL)
```

```python
import functools
import math

import jax
import jax.numpy as jnp
import numpy as np
from jax import lax
from jax.experimental import pallas as pl
from jax.experimental.pallas import tpu as pltpu

F32 = jnp.float32
BF16 = jnp.bfloat16

D_MODEL = 1024
MEM_LEN = 256
HEAD_DIM = 64
ATTN_Q_HEADS = 8
ATTN_KV_HEADS = 2
ATTN_WIDTH = ATTN_Q_HEADS * HEAD_DIM
KV_WIDTH = ATTN_KV_HEADS * HEAD_DIM
KV_DUP_WIDTH = 2 * KV_WIDTH
BLOCK = 128
SC_WIDTH = 256
SC_CONV = 3
RG_WIDTH = 256
RG_CONV = 4
RG_C = 8.0
N_BUCKETS = 32
MAX_EXACT = N_BUCKETS // 2
MAX_DISTANCE = 128
XA_HEADS = 4
XA_HEAD_DIM = 128
XA_WIDTH = XA_HEADS * XA_HEAD_DIM
N_EXPERTS = 8
EPS = 1e-6
NEG_INF = -1e30
REST_WIDTH = 3 * SC_WIDTH + 2 * RG_WIDTH

LANES_V7X = 128
SUBLANES_V7X = 8
VMEM_BYTES_V7X = 64 * 1024 * 1024
ROW_CHUNKS = D_MODEL // LANES_V7X
assert ROW_CHUNKS == SUBLANES_V7X

TM = 512
ATTN_BLOCKS = 8
TS = 512
ROUTE_CHUNK = 256
TM_GROUP = 512
ISSUE_UNROLL = 8
PACKED_ROWS = ROW_CHUNKS // 2
SORTED_TOKENS = 2 * TM + N_EXPERTS
FF_SPLIT = 2
CARRY_ROWS = SUBLANES_V7X


def _mib(n):
    return int(n * 1024 * 1024)


def _params(semantics, vmem_mib):
    assert _mib(vmem_mib) < VMEM_BYTES_V7X
    return pltpu.CompilerParams(dimension_semantics=semantics, vmem_limit_bytes=_mib(vmem_mib))


def _rms(x, g):
    ms = jnp.mean(x * x, axis=-1, keepdims=True)
    return x * lax.rsqrt(ms + EPS) * g


def _const_spec(shape):
    nd = len(shape)
    return pl.BlockSpec(shape, lambda *_: (0,) * nd, pipeline_mode=pl.Buffered(1))


def _attn_kernel(sink_ref, q_ref, kp_ref, kc_ref, vp_ref, vc_ref, bias0_ref, bias_ref, o_ref):
    pairs_per_group = ATTN_Q_HEADS // ATTN_KV_HEADS // 2
    row = lax.broadcasted_iota(jnp.int32, (BLOCK, BLOCK), 0)
    col = lax.broadcasted_iota(jnp.int32, (BLOCK, BLOCK), 1)
    from_prev = col > row
    low_lanes = lax.broadcasted_iota(jnp.int32, (2 * BLOCK, 2 * HEAD_DIM), 1) < HEAD_DIM
    low_out = col < HEAD_DIM
    zero = jnp.zeros((), BF16)

    def block_diag(band):
        return jnp.concatenate([jnp.where(low_lanes, band, zero), jnp.where(low_lanes, zero, band)], axis=0)

    def scores(blk):
        rows = slice(blk * BLOCK, (blk + 1) * BLOCK)
        prev_rows = slice((blk - 1) * BLOCK, blk * BLOCK)
        out = []
        for g in range(ATTN_KV_HEADS):
            lanes = slice(g * 2 * HEAD_DIM, (g + 1) * 2 * HEAD_DIM)
            k_prev = kp_ref[:, lanes] if blk == 0 else kc_ref[prev_rows, lanes]
            v_prev = vp_ref[:, lanes] if blk == 0 else vc_ref[prev_rows, lanes]
            k_bd = block_diag(jnp.concatenate([k_prev, kc_ref[rows, lanes]], axis=0))
            v_bd = block_diag(jnp.concatenate([v_prev, vc_ref[rows, lanes]], axis=0))
            for pair in range(pairs_per_group):
                slab = g * pairs_per_group + pair
                q2 = q_ref[rows, slab * 2 * HEAD_DIM:(slab + 1) * 2 * HEAD_DIM]
                s = lax.dot_general(q2, k_bd, (((1,), (1,)), ((), ())), preferred_element_type=F32)
                out.append((slab, s, v_bd))
        return out

    def finish(blk, scored):
        rows = slice(blk * BLOCK, (blk + 1) * BLOCK)
        tbl_ref = bias0_ref if blk == 0 else bias_ref
        staged = []
        for slab, s, v_bd in scored:
            probs, denoms = [], []
            for side in range(2):
                h = 2 * slab + side
                sh = s[:, side * 2 * BLOCK:(side + 1) * 2 * BLOCK]
                logits = jnp.where(from_prev, sh[:, :BLOCK], sh[:, BLOCK:]) + tbl_ref[h]
                sink = sink_ref[h]
                m = jnp.maximum(jnp.max(logits, axis=-1, keepdims=True), sink)
                p = jnp.exp(logits - m)
                denoms.append(jnp.sum(p, axis=-1, keepdims=True) + jnp.exp(sink - m))
                probs += [jnp.where(from_prev, p, 0.0), jnp.where(from_prev, 0.0, p)]
            staged.append((slab, jnp.concatenate(probs, axis=-1).astype(BF16), v_bd, denoms))
        for slab, p_band, v_bd, denoms in staged:
            o = jnp.dot(p_band, v_bd, preferred_element_type=F32)
            o = o / jnp.where(low_out, denoms[0], denoms[1])
            o_ref[rows, slab * 2 * HEAD_DIM:(slab + 1) * 2 * HEAD_DIM] = o.astype(BF16)

    pending = scores(0)
    for blk in range(ATTN_BLOCKS):
        upcoming = scores(blk + 1) if blk + 1 < ATTN_BLOCKS else None
        finish(blk, pending)
        pending = upcoming


def _attention(q, k, v, sinks, bias_tbl, blocks_per_seq):
    t = q.shape[0]
    tile = ATTN_BLOCKS * BLOCK
    cur = lambda i: (i, 0)
    prev = lambda i: (jnp.maximum(i * ATTN_BLOCKS - 1, 0), 0)
    tbl = (None, ATTN_Q_HEADS, BLOCK, BLOCK)
    return pl.pallas_call(
        _attn_kernel,
        grid=(t // tile,),
        in_specs=[pl.BlockSpec(memory_space=pltpu.SMEM),
                  pl.BlockSpec((tile, ATTN_WIDTH), cur),
                  pl.BlockSpec((BLOCK, KV_DUP_WIDTH), prev),
                  pl.BlockSpec((tile, KV_DUP_WIDTH), cur),
                  pl.BlockSpec((BLOCK, KV_DUP_WIDTH), prev),
                  pl.BlockSpec((tile, KV_DUP_WIDTH), cur),
                  pl.BlockSpec(tbl, lambda i: (jnp.minimum((i * ATTN_BLOCKS) % blocks_per_seq, 1), 0, 0, 0)),
                  pl.BlockSpec(tbl, lambda i: (1, 0, 0, 0))],
        out_specs=pl.BlockSpec((tile, ATTN_WIDTH), cur),
        out_shape=jax.ShapeDtypeStruct((t, ATTN_WIDTH), BF16),
        compiler_params=_params(("parallel",), 24),
        name="swa_attention",
    )(sinks, q, k, k, v, v, bias_tbl, bias_tbl)


def _bias_table_kernel(rel_ref, bucket_ref, o_ref):
    for v in range(2):
        bucket = bucket_ref[v]
        hits = [bucket == b for b in range(N_BUCKETS)]
        for h in range(ATTN_Q_HEADS):
            tbl = jnp.full(bucket.shape, NEG_INF, F32)
            for b in range(N_BUCKETS):
                tbl = jnp.where(hits[b], rel_ref[b * ATTN_Q_HEADS + h], tbl)
            o_ref[v, h] = tbl


def _attention_bias_tables(rel_bias):
    q_idx = np.arange(BLOCK)[:, None]
    j_idx = np.arange(BLOCK)[None, :]
    from_prev = j_idx > q_idx
    n = np.where(from_prev, q_idx + BLOCK - j_idx, q_idx - j_idx)
    large = MAX_EXACT + (np.log(np.maximum(n, 1).astype(np.float32) / np.float32(MAX_EXACT))
                         / np.float32(math.log(MAX_DISTANCE / MAX_EXACT))
                         * np.float32(N_BUCKETS - MAX_EXACT)).astype(np.int32)
    bucket = np.where(n < MAX_EXACT, n, np.minimum(large, N_BUCKETS - 1))
    first = np.where(from_prev, -1, bucket)
    buckets = jnp.asarray(np.stack([first, bucket]).astype(np.int32))
    return pl.pallas_call(
        _bias_table_kernel,
        in_specs=[pl.BlockSpec(memory_space=pltpu.SMEM), pl.BlockSpec(memory_space=pltpu.VMEM)],
        out_specs=pl.BlockSpec(memory_space=pltpu.VMEM),
        out_shape=jax.ShapeDtypeStruct((2, ATTN_Q_HEADS, BLOCK, BLOCK), F32),
        name="t5_bias_table",
    )(rel_bias.astype(F32).reshape(-1), buckets)


def _shift_rows(x, s, fill):
    return jnp.concatenate([jnp.full((s, x.shape[1]), fill, x.dtype), x[:x.shape[0] - s]], axis=0)


def _in_proj_conv_kernel(x_ref, g_ref, w_ref, scw_ref, scb_ref, rgw_ref, rgb_ref, wgate_ref, bgate_ref, lam_ref,
                         q_ref, k_ref, v_ref, o_ref, r_ref, sc_ext, rg_ext, h_carry, *, tiles_per_seq):
    i = pl.program_id(0)
    ts = r_ref.shape[0]
    c0 = CARRY_ROWS

    @pl.when(i == 0)
    def _():
        r_ref[...] = jnp.zeros_like(r_ref)

    @pl.when((i == 0) | ((i + tiles_per_seq - 1) % tiles_per_seq == 0))
    def _():
        sc_ext[0:c0, :] = jnp.zeros((c0, SC_WIDTH), F32)
        rg_ext[0:c0, :] = jnp.zeros((c0, RG_WIDTH), F32)
        h_carry[...] = jnp.zeros_like(h_carry)

    attn_cols = ATTN_WIDTH + 2 * KV_DUP_WIDTH
    hx = _rms(x_ref[...], g_ref[...]).astype(BF16)
    p = jnp.dot(hx, w_ref[:, :attn_cols], preferred_element_type=F32)
    q_ref[...] = p[:, :ATTN_WIDTH].astype(BF16)
    k_ref[...] = p[:, ATTN_WIDTH:ATTN_WIDTH + KV_DUP_WIDTH].astype(BF16)
    v_ref[...] = p[:, ATTN_WIDTH + KV_DUP_WIDTH:].astype(BF16)

    sc_b = r_ref[:, 0:SC_WIDTH]
    sc_ext[c0:c0 + ts, :] = r_ref[:, SC_WIDTH:2 * SC_WIDTH] * r_ref[:, 2 * SC_WIDTH:3 * SC_WIDTH]
    rg_ext[c0:c0 + ts, :] = r_ref[:, 3 * SC_WIDTH:3 * SC_WIDTH + RG_WIDTH]
    rg_g = r_ref[:, 3 * SC_WIDTH + RG_WIDTH:]

    conv = scb_ref[...]
    for k in range(SC_CONV):
        off = c0 - (SC_CONV - 1) + k
        conv = conv + scw_ref[k:k + 1, :] * sc_ext[off:off + ts, :]
    conv_out = sc_b * conv

    rg_in = rgb_ref[...]
    for k in range(RG_CONV):
        off = c0 - (RG_CONV - 1) + k
        rg_in = rg_in + rgw_ref[k:k + 1, :] * rg_ext[off:off + ts, :]

    sc_ext[0:c0, :] = sc_ext[ts:ts + c0, :]
    rg_ext[0:c0, :] = rg_ext[ts:ts + c0, :]

    gates = jnp.dot(rg_in.astype(BF16), wgate_ref[...], preferred_element_type=F32) + bgate_ref[...]

    r_ref[...] = jnp.dot(hx, w_ref[:, attn_cols:], preferred_element_type=F32)

    r_gate = jax.nn.sigmoid(gates[:, :RG_WIDTH])
    i_gate = jax.nn.sigmoid(gates[:, RG_WIDTH:])
    neg_lam = -lam_ref[...]
    softplus = jnp.maximum(neg_lam, 0.0) + jnp.log1p(jnp.exp(-jnp.abs(neg_lam)))
    log_a = -RG_C * r_gate * softplus
    a = jnp.exp(log_a)
    u = jnp.sqrt(jnp.tanh(-log_a) * (1.0 + a * a)) * (i_gate * rg_in)

    s = 1
    while s < ts:
        u = a * _shift_rows(u, s, 0.0) + u
        a = a * _shift_rows(a, s, 1.0)
        s *= 2
    h = a * h_carry[...] + u
    h_carry[...] = h[ts - 1:ts, :]

    c = math.sqrt(2.0 / math.pi)
    gelu = 0.5 * rg_g * (1.0 + jnp.tanh(c * (rg_g + 0.044715 * (rg_g * rg_g * rg_g))))
    o_ref[:, 0:SC_WIDTH] = conv_out.astype(BF16)
    o_ref[:, SC_WIDTH:] = (h * gelu).astype(BF16)


def _in_proj_conv(x, g, w, sc_w, sc_b, rg_w, rg_b, w_gate, b_gate, lam, seq):
    t = x.shape[0]
    n = w.shape[1]
    tiles = t // TS
    assert n == ATTN_WIDTH + 2 * KV_DUP_WIDTH + REST_WIDTH
    proj = lambda width: pl.BlockSpec((TS, width), lambda i: (jnp.minimum(i, tiles - 1), 0))
    lagged = pl.BlockSpec((TS, SC_WIDTH + RG_WIDTH), lambda i: (jnp.maximum(i - 1, 0), 0))
    return pl.pallas_call(
        functools.partial(_in_proj_conv_kernel, tiles_per_seq=seq // TS),
        grid=(tiles + 1,),
        in_specs=[proj(D_MODEL), _const_spec((1, D_MODEL)), _const_spec((D_MODEL, n)),
                  _const_spec((SC_CONV, SC_WIDTH)), _const_spec((1, SC_WIDTH)),
                  _const_spec((RG_CONV, RG_WIDTH)), _const_spec((1, RG_WIDTH)),
                  _const_spec((RG_WIDTH, 2 * RG_WIDTH)), _const_spec((1, 2 * RG_WIDTH)),
                  _const_spec((1, RG_WIDTH))],
        out_specs=[proj(ATTN_WIDTH), proj(KV_DUP_WIDTH), proj(KV_DUP_WIDTH), lagged],
        out_shape=[jax.ShapeDtypeStruct((t, ATTN_WIDTH), BF16),
                   jax.ShapeDtypeStruct((t, KV_DUP_WIDTH), BF16),
                   jax.ShapeDtypeStruct((t, KV_DUP_WIDTH), BF16),
                   jax.ShapeDtypeStruct((t, SC_WIDTH + RG_WIDTH), BF16)],
        scratch_shapes=[pltpu.VMEM((TS, REST_WIDTH), F32),
                        pltpu.VMEM((TS + 2 * CARRY_ROWS, SC_WIDTH), F32),
                        pltpu.VMEM((TS + 2 * CARRY_ROWS, RG_WIDTH), F32),
                        pltpu.VMEM((1, RG_WIDTH), F32)],
        compiler_params=_params(("arbitrary",), 48),
        name="in_proj_conv_rglru",
    )(x, g, w, sc_w, sc_b, rg_w, rg_b, w_gate, b_gate, lam)


def _mem_kv_kernel(m_ref, g_ref, w_ref, k_ref, v_ref):
    h = _rms(m_ref[...], g_ref[...]).astype(BF16)
    p = jnp.dot(h, w_ref[...], preferred_element_type=F32)
    k_ref[...] = p[:, :XA_WIDTH].astype(BF16)
    v_ref[...] = p[:, XA_WIDTH:].astype(BF16)


def _mem_kv(mem, g, wkv):
    t = mem.shape[0]
    row = lambda width: pl.BlockSpec((MEM_LEN, width), lambda i: (i, 0))
    return pl.pallas_call(
        _mem_kv_kernel,
        grid=(t // MEM_LEN,),
        in_specs=[row(D_MODEL), _const_spec((1, D_MODEL)), _const_spec((D_MODEL, 2 * XA_WIDTH))],
        out_specs=[row(XA_WIDTH), row(XA_WIDTH)],
        out_shape=[jax.ShapeDtypeStruct((t, XA_WIDTH), BF16)] * 2,
        compiler_params=_params(("parallel",), 24),
        name="mem_kv",
    )(mem, g, wkv)


def _post_mixer_kernel(x_ref, a_ref, c_ref, wout_ref, g_ref, wq_ref, k_ref, v_ref, wo_ref, *rest):
    o_ref = rest[-1] if len(rest) == 1 else rest[4]
    halves = [slice(i * (TM // 2), (i + 1) * (TM // 2)) for i in range(2)]
    heads = [slice(hd * XA_HEAD_DIM, (hd + 1) * XA_HEAD_DIM) for hd in range(XA_HEADS)]
    nt = (((1,), (1,)), ((), ()))
    k = k_ref[...]
    v = v_ref[...]

    x1 = [x_ref[hs, :] + jnp.dot(jnp.concatenate([a_ref[hs, :], c_ref[hs, :]], axis=-1), wout_ref[...],
                                 preferred_element_type=F32) for hs in halves]
    q = [jnp.dot(_rms(xh, g_ref[...]).astype(BF16), wq_ref[...], preferred_element_type=F32).astype(BF16)
         for xh in x1]
    scores = [[lax.dot_general(qh[:, sl], k[:, sl], nt, preferred_element_type=F32) for sl in heads] for qh in q]
    x2 = []
    for xh, per_head in zip(x1, scores):
        probs, sums = [], []
        for s in per_head:
            s = s * (1.0 / math.sqrt(XA_HEAD_DIM))
            p = jnp.exp(s - jnp.max(s, axis=-1, keepdims=True))
            probs.append(p.astype(BF16))
            sums.append(jnp.sum(p, axis=-1, keepdims=True))
        att = jnp.concatenate([jnp.dot(p, v[:, sl], preferred_element_type=F32) / l
                               for p, sl, l in zip(probs, heads, sums)], axis=-1).astype(BF16)
        x2.append(xh + jnp.dot(att, wo_ref[...], preferred_element_type=F32))

    for hs, xh in zip(halves, x2):
        o_ref[hs, :] = xh
    if len(rest) > 1:
        ffn_g_ref, wr_ref, tri_ref, sel_ref, _, hf_ref, meta_ref, metat_ref, cnt_ref, base_ref, carry = rest
        h = _rms(jnp.concatenate(x2, axis=0), ffn_g_ref[...])
        _route(h, wr_ref, tri_ref, sel_ref, hf_ref, meta_ref, metat_ref, cnt_ref, base_ref, carry)


def _post_mixer(x, attn, cr, w_out, g, wq, k, v, wo, seq, route=None):
    t = x.shape[0]
    per_seq = seq // TM
    row = lambda width: pl.BlockSpec((TM, width), lambda i: (i, 0))
    mem_blk = pl.BlockSpec((MEM_LEN, XA_WIDTH), lambda i: (i // per_seq, 0))
    in_specs = [row(D_MODEL), row(ATTN_WIDTH), row(SC_WIDTH + RG_WIDTH), _const_spec((D_MODEL, D_MODEL)),
                _const_spec((1, D_MODEL)), _const_spec((D_MODEL, XA_WIDTH)), mem_blk, mem_blk,
                _const_spec((XA_WIDTH, D_MODEL))]
    out_specs = [row(D_MODEL)]
    out_shape = [jax.ShapeDtypeStruct((t, D_MODEL), F32)]
    scratch = []
    args = [x, attn, cr, w_out, g, wq, k, v, wo]
    if route is not None:
        assert TM % ROUTE_CHUNK == 0
        in_specs += [_const_spec((1, D_MODEL)), _const_spec((D_MODEL, LANES_V7X)),
                     _const_spec((ROUTE_CHUNK, ROUTE_CHUNK)), _const_spec((SUBLANES_V7X, LANES_V7X))]
        out_specs += [row(D_MODEL), row(LANES_V7X),
                      pl.BlockSpec((SUBLANES_V7X, TM), lambda i: (0, i)),
                      pl.BlockSpec((1, LANES_V7X), lambda i: (0, 0)),
                      pl.BlockSpec((SUBLANES_V7X, LANES_V7X), lambda i: (i, 0))]
        out_shape += [jax.ShapeDtypeStruct((t, D_MODEL), BF16),
                      jax.ShapeDtypeStruct((t, LANES_V7X), F32),
                      jax.ShapeDtypeStruct((SUBLANES_V7X, t), F32),
                      jax.ShapeDtypeStruct((1, LANES_V7X), F32),
                      jax.ShapeDtypeStruct((t // TM * SUBLANES_V7X, LANES_V7X), F32)]
        scratch = [pltpu.VMEM((1, LANES_V7X), F32)]
        args += list(route)
    return pl.pallas_call(
        _post_mixer_kernel,
        grid=(t // TM,),
        in_specs=in_specs,
        out_specs=out_specs,
        out_shape=out_shape,
        scratch_shapes=scratch,
        compiler_params=_params(("arbitrary",), 40),
        name="post_mixer_route" if route is not None else "post_mixer",
    )(*args)


def _ffn_kernel(x_ref, g_ref, wg_ref, wu_ref, wd_ref, *rest):
    n_cast = (len(rest) - 1) // 2
    o_ref = rest[n_cast]
    x = x_ref[...]
    h = _rms(x, g_ref[...]).astype(BF16)
    gate = jnp.dot(h, wg_ref[...], preferred_element_type=F32)
    up = jnp.dot(h, wu_ref[...], preferred_element_type=F32)
    act = (gate * jax.nn.sigmoid(gate) * up).astype(BF16)
    o_ref[...] = x + jnp.dot(act, wd_ref[...], preferred_element_type=F32)
    for src, dst in zip(rest[:n_cast], rest[n_cast + 1:]):
        dst[...] = src[...].astype(BF16)


def _dense_ffn(x, g, wg, wu, wd, cast_along=()):
    t = x.shape[0]
    steps = t // TM
    d_ff = wg.shape[1]
    row = pl.BlockSpec((TM, D_MODEL), lambda i: (i, 0))
    flat = [w.reshape(-1, w.shape[-1]) for w in cast_along]
    for w in flat:
        assert w.shape[0] % (steps * 2 * SUBLANES_V7X) == 0
    slabs = [pl.BlockSpec((w.shape[0] // steps, w.shape[1]), lambda i: (i, 0)) for w in flat]
    outs = pl.pallas_call(
        _ffn_kernel,
        grid=(steps,),
        in_specs=[row, _const_spec((1, D_MODEL)), _const_spec((D_MODEL, d_ff)), _const_spec((D_MODEL, d_ff)),
                  _const_spec((d_ff, D_MODEL))] + slabs,
        out_specs=[row] + slabs,
        out_shape=[jax.ShapeDtypeStruct((t, D_MODEL), F32)] + [jax.ShapeDtypeStruct(w.shape, BF16) for w in flat],
        compiler_params=_params(("parallel",), 60),
        name="dense_swiglu",
    )(x, g, wg, wu, wd, *flat)
    return outs[0], [o.reshape(w.shape) for o, w in zip(outs[1:], cast_along)]


META_E1, META_E2, META_R1, META_R2, META_W1, META_W2 = range(6)


def _to_token_tiles(ref, rows):
    m = rows.shape[0]
    for c in range(ROW_CHUNKS):
        ref[pl.ds(c, m, stride=ROW_CHUNKS), :] = rows[:, c * LANES_V7X:(c + 1) * LANES_V7X]


def _from_token_tiles(ref):
    m = ref.shape[0] // ROW_CHUNKS
    return jnp.concatenate([ref[pl.ds(c, m, stride=ROW_CHUNKS), :] for c in range(ROW_CHUNKS)], axis=-1)


def _route(h, wr_ref, tri_ref, sel_ref, hf_ref, meta_ref, metat_ref, cnt_ref, base_ref, carry):
    @pl.when(pl.program_id(0) == 0)
    def _():
        carry[...] = jnp.zeros_like(carry)

    base_ref[...] = jnp.broadcast_to(carry[...], base_ref.shape)
    chunks = [slice(c * ROUTE_CHUNK, (c + 1) * ROUTE_CHUNK) for c in range(h.shape[0] // ROUTE_CHUNK)]
    each = lambda fn, *lists: [fn(*vals) for vals in zip(*lists)]
    rowmax = lambda a: jnp.max(a, axis=-1, keepdims=True)
    rowsum = lambda a: jnp.sum(a, axis=-1, keepdims=True)
    lane = lax.broadcasted_iota(jnp.int32, (ROUTE_CHUNK, LANES_V7X), 1)
    first_hit = lambda lg, m: jnp.min(jnp.where(lg == m, lane, LANES_V7X), axis=-1, keepdims=True)

    hb = h.astype(BF16)
    hf_ref[...] = hb
    logits = [jnp.dot(hb[c, :], wr_ref[...], preferred_element_type=F32) for c in chunks]
    lg = each(lambda l: jnp.where(lane < N_EXPERTS, l, -jnp.inf), logits)
    m1 = each(rowmax, lg)
    e1 = each(first_hit, lg, m1)
    lg2 = each(lambda l, e: jnp.where(lane == e, -jnp.inf, l), lg, e1)
    m2 = each(rowmax, lg2)
    e2 = each(first_hit, lg2, m2)
    ex = each(lambda a, b: jnp.exp(b - a), m1, m2)
    w1 = each(lambda e: 1.0 / (1.0 + e), ex)
    w2 = each(lambda e: e / (1.0 + e), ex)
    hit1 = each(lambda e: lane == e, e1)
    hit2 = each(lambda e: lane == e, e2)
    onehot = each(lambda a, b: (a | b).astype(BF16), hit1, hit2)
    within = each(lambda o: jnp.dot(tri_ref[...], o, preferred_element_type=F32), onehot)
    totals = each(lambda o: jnp.sum(o.astype(F32), axis=0, keepdims=True), onehot)
    ahead, base = [], carry[...]
    for w, tot in zip(within, totals):
        ahead.append(w + base)
        base = base + tot
    carry[...] = base
    cnt_ref[...] = base
    r1 = each(lambda hit, a: rowsum(jnp.where(hit, a, 0.0)), hit1, ahead)
    r2 = each(lambda hit, a: rowsum(jnp.where(hit, a, 0.0)), hit2, ahead)

    def record(*vals):
        meta = jnp.zeros((ROUTE_CHUNK, LANES_V7X), F32)
        for col, val in zip((META_E1, META_E2, META_R1, META_R2, META_W1, META_W2), vals):
            meta = jnp.where(lane == col, val.astype(F32), meta)
        return meta

    meta = each(record, e1, e2, r1, r2, w1, w2)
    metat = each(lambda mt: lax.dot_general(sel_ref[...], mt, (((1,), (1,)), ((), ())), preferred_element_type=F32,
                                            precision=lax.Precision.HIGHEST), meta)
    for c, mt, mtt in zip(chunks, meta, metat):
        meta_ref[c, :] = mt
        metat_ref[:, c] = mtt


def _token_rows(ref, first_token, n_tokens):
    start = pl.multiple_of(first_token * ROW_CHUNKS, ROW_CHUNKS)
    return ref.at[pl.ds(start, n_tokens * ROW_CHUNKS)]


def _packed_rows(ref, first_token, n_tokens):
    start = pl.multiple_of(first_token * PACKED_ROWS, SUBLANES_V7X)
    return ref.at[pl.ds(start, n_tokens * PACKED_ROWS)]


def _dispatch_kernel(fill_ref, n_ref, off_ref, dst_ref, loc1_ref, loc2_ref, hf_ref, xs_ref, zeros, buf, sem_z, sems):
    i = pl.program_id(0)
    steps = pl.num_programs(0)
    slot_tokens = buf.shape[1] // PACKED_ROWS
    runs = N_EXPERTS + 1

    @pl.when(i == 0)
    def _():
        zeros[...] = jnp.zeros_like(zeros)

        def tile_fill(e):
            return pltpu.make_async_copy(zeros, _packed_rows(xs_ref, pl.multiple_of(fill_ref[e], TM_GROUP), TM_GROUP),
                                         sem_z)

        for e in range(fill_ref.shape[0]):
            @pl.when(fill_ref[e] >= 0)
            def _():
                tile_fill(e).start()
        for e in range(fill_ref.shape[0]):
            @pl.when(fill_ref[e] >= 0)
            def _():
                tile_fill(e).wait()

    k = lax.broadcasted_iota(jnp.int32, (slot_tokens, hf_ref.shape[0]), 0)
    onehot = ((k == loc1_ref[...]) | (k == loc2_ref[...])).astype(BF16)
    sorted_rows = jnp.dot(onehot, hf_ref[...], preferred_element_type=F32)
    bits = pltpu.bitcast(sorted_rows, jnp.uint32)
    half = D_MODEL // 2
    packed = (bits[:, :half] & jnp.uint32(0xFFFF0000)) | (bits[:, half:] >> 16)

    def run(slot):
        whole_slot = pltpu.make_async_copy(buf.at[slot], _packed_rows(xs_ref, 0, slot_tokens), sems.at[slot])

        @pl.when(i >= 2)
        def _():
            whole_slot.wait()

        for c in range(PACKED_ROWS):
            buf.at[slot][pl.ds(c, slot_tokens, stride=PACKED_ROWS), :] = packed[:, c * LANES_V7X:(c + 1) * LANES_V7X]
        for e in range(runs):
            n = n_ref[i * runs + e]
            off = off_ref[i * runs + e]
            dst = dst_ref[i * runs + e]
            bit = hf_ref.shape[0]
            while bit >= 2:
                head = n & ~(2 * bit - 1)

                @pl.when((n & bit) != 0)
                def _(bit=bit, head=head):
                    pltpu.make_async_copy(_packed_rows(buf.at[slot], off + head, bit),
                                          _packed_rows(xs_ref, dst + head, bit), sems.at[slot]).start()
                bit //= 2

        @pl.when(i == steps - 1)
        def _():
            whole_slot.wait()
            other = pltpu.make_async_copy(buf.at[1 - slot], _packed_rows(xs_ref, 0, slot_tokens), sems.at[1 - slot])

            @pl.when(steps >= 2)
            def _():
                other.wait()

    for slot in range(2):
        pl.when(i % 2 == slot)(functools.partial(run, slot))


def _dispatch(fill_start, n_tab, off_tab, dst_tab, loc1, loc2, hf, total_rows):
    t = hf.shape[0]
    loc = pl.BlockSpec((1, TM), lambda i, *_: (0, i))
    u32 = jnp.uint32
    return pl.pallas_call(
        _dispatch_kernel,
        grid_spec=pltpu.PrefetchScalarGridSpec(
            num_scalar_prefetch=4,
            grid=(t // TM,),
            in_specs=[loc, loc, pl.BlockSpec((TM, D_MODEL), lambda i, *_: (i, 0))],
            out_specs=pl.BlockSpec(memory_space=pl.ANY),
            scratch_shapes=[pltpu.VMEM((TM_GROUP * PACKED_ROWS, LANES_V7X), u32),
                            pltpu.VMEM((2, SORTED_TOKENS * PACKED_ROWS, LANES_V7X), u32),
                            pltpu.SemaphoreType.DMA(()), pltpu.SemaphoreType.DMA((2,))]),
        out_shape=jax.ShapeDtypeStruct((total_rows * PACKED_ROWS, LANES_V7X), u32),
        compiler_params=_params(("arbitrary",), 40),
        name="moe_dispatch",
    )(fill_start, n_tab, off_tab, dst_tab, loc1.reshape(1, t), loc2.reshape(1, t), hf)


def _grouped_kernel(te_ref, used_ref, x_ref, wg_ref, wu_ref, wd_ref, y_ref):
    i = pl.program_id(0)

    @pl.when(i < used_ref[0])
    def _():
        tm = x_ref.shape[0] // PACKED_ROWS
        words = jnp.concatenate([x_ref[pl.ds(c, tm, stride=PACKED_ROWS), :] for c in range(PACKED_ROWS)], axis=-1)
        high = pltpu.bitcast(words & jnp.uint32(0xFFFF0000), F32).astype(BF16)
        low = pltpu.bitcast(words << 16, F32).astype(BF16)
        h = jnp.concatenate([high, low], axis=-1)
        ff = wg_ref.shape[1] // FF_SPLIT
        cols = [slice(c * ff, (c + 1) * ff) for c in range(FF_SPLIT)]
        gates = [jnp.dot(h, wg_ref[:, c], preferred_element_type=F32) for c in cols]
        ups = [jnp.dot(h, wu_ref[:, c], preferred_element_type=F32) for c in cols]
        total = None
        for c, gate, up in zip(cols, gates, ups):
            act = (gate * jax.nn.sigmoid(gate) * up).astype(BF16)
            part = jnp.dot(act, wd_ref[c, :], preferred_element_type=F32)
            total = part if total is None else total + part
        _to_token_tiles(y_ref, total)

    @pl.when(i >= used_ref[0])
    def _():
        y_ref[...] = jnp.zeros_like(y_ref)


def _grouped_swiglu(tile_expert, n_used, xs, wg, wu, wd, n_rows):
    d_ff = wg.shape[2]
    tile = (TM_GROUP * ROW_CHUNKS, LANES_V7X)
    packed_tile = (TM_GROUP * PACKED_ROWS, LANES_V7X)
    src = lambda i, te, used: (jnp.maximum(jnp.minimum(i, used[0] - 1), 0), 0)
    expert = lambda shape: pl.BlockSpec((None,) + shape, lambda i, te, used: (te[i], 0, 0))
    return pl.pallas_call(
        _grouped_kernel,
        grid_spec=pltpu.PrefetchScalarGridSpec(
            num_scalar_prefetch=2,
            grid=(n_rows // TM_GROUP,),
            in_specs=[pl.BlockSpec(packed_tile, src), expert((D_MODEL, d_ff)), expert((D_MODEL, d_ff)),
                      expert((d_ff, D_MODEL))],
            out_specs=pl.BlockSpec(tile, lambda i, te, used: (i, 0))),
        out_shape=jax.ShapeDtypeStruct((n_rows * ROW_CHUNKS, LANES_V7X), F32),
        compiler_params=_params(("arbitrary",), 58),
        name="moe_grouped_swiglu",
    )(tile_expert, n_used, xs, wg, wu, wd)


def _combine_kernel(n_ref, off_ref, dst_ref, loc1_ref, loc2_ref, x_ref, meta_ref, g_ref, y_ref, o_ref,
                    slab, y1, y2, sems):
    i = pl.program_id(0)
    tm = x_ref.shape[0]

    def fetch(tile, slot):
        for e in range(N_EXPERTS):
            n = n_ref[tile * N_EXPERTS + e]
            off = off_ref[tile * N_EXPERTS + e]
            dst = dst_ref[tile * N_EXPERTS + e]
            bit = tm
            while bit >= 1:
                head = n & ~(2 * bit - 1)

                @pl.when((n & bit) != 0)
                def _(bit=bit, head=head):
                    pltpu.make_async_copy(_token_rows(y_ref, dst + head, bit),
                                          _token_rows(slab.at[slot], off + head, bit), sems.at[slot]).start()
                bit //= 2

    def finish(slot):
        pltpu.make_async_copy(_token_rows(y_ref, 0, 2 * tm), _token_rows(slab.at[slot], 0, 2 * tm),
                              sems.at[slot]).wait()

        def pick(r, carry):
            y1[pl.ds(pl.multiple_of(r * ROW_CHUNKS, ROW_CHUNKS), ROW_CHUNKS), :] = (
                _token_rows(slab.at[slot], loc1_ref[r], 1)[...])
            y2[pl.ds(pl.multiple_of(r * ROW_CHUNKS, ROW_CHUNKS), ROW_CHUNKS), :] = (
                _token_rows(slab.at[slot], loc2_ref[r], 1)[...])
            return carry

        lax.fori_loop(0, tm, pick, 0, unroll=ISSUE_UNROLL)
        w1 = meta_ref[:, META_W1:META_W1 + 1]
        w2 = meta_ref[:, META_W2:META_W2 + 1]
        out = x_ref[...] + (w1 * _from_token_tiles(y1) + w2 * _from_token_tiles(y2))
        o_ref[...] = _rms(out, g_ref[...])

    @pl.when(i == 0)
    def _():
        fetch(0, 0)

    for slot in range(2):
        @pl.when(i % 2 == slot)
        def _():
            @pl.when(i + 1 < pl.num_programs(0))
            def _():
                fetch(i + 1, 1 - slot)
            finish(slot)


def _combine(n_tab, off_tab, dst_tab, loc1, loc2, x, meta, g, y):
    t = x.shape[0]
    idx = pl.BlockSpec((TM,), lambda i, *_: (i,), memory_space=pltpu.SMEM)
    row = lambda width: pl.BlockSpec((TM, width), lambda i, *_: (i, 0))
    return pl.pallas_call(
        _combine_kernel,
        grid_spec=pltpu.PrefetchScalarGridSpec(
            num_scalar_prefetch=3,
            grid=(t // TM,),
            in_specs=[idx, idx, row(D_MODEL), row(LANES_V7X),
                      pl.BlockSpec((1, D_MODEL), lambda i, *_: (0, 0), pipeline_mode=pl.Buffered(1)),
                      pl.BlockSpec(memory_space=pl.ANY)],
            out_specs=row(D_MODEL),
            scratch_shapes=[pltpu.VMEM((2, SORTED_TOKENS * ROW_CHUNKS, LANES_V7X), F32),
                            pltpu.VMEM((TM * ROW_CHUNKS, LANES_V7X), F32),
                            pltpu.VMEM((TM * ROW_CHUNKS, LANES_V7X), F32),
                            pltpu.SemaphoreType.DMA((2,))]),
        out_shape=jax.ShapeDtypeStruct((t, D_MODEL), F32),
        compiler_params=_params(("arbitrary",), 40),
        name="moe_combine_norm",
    )(n_tab, off_tab, dst_tab, loc1, loc2, x, meta, g, y)


def _route_operands(ffn_g, router_w):
    wr_pad = jnp.zeros((D_MODEL, LANES_V7X), BF16).at[:, :N_EXPERTS].set(router_w.astype(BF16))
    tri = jnp.tril(jnp.ones((ROUTE_CHUNK, ROUTE_CHUNK), BF16), -1)
    sel = jnp.eye(SUBLANES_V7X, LANES_V7X, dtype=F32)
    return ffn_g, wr_pad, tri, sel


def _moe_layer(x, hf, meta, metat, counts, base, wg, wu, wd, final_g):
    t = x.shape[0]
    steps = t // TM
    n_rows = -(-(2 * t + N_EXPERTS * steps) // TM_GROUP) * TM_GROUP + N_EXPERTS * TM_GROUP

    cnt = counts[0, :N_EXPERTS].astype(jnp.int32)
    experts = jnp.arange(N_EXPERTS, dtype=jnp.int32)[:, None]
    e1, e2 = metat[META_E1].astype(jnp.int32), metat[META_E2].astype(jnp.int32)
    r1, r2 = metat[META_R1].astype(jnp.int32), metat[META_R2].astype(jnp.int32)
    pick = lambda e, table: jnp.sum(jnp.where(e[None, :] == experts, table, 0), axis=0)
    before = base.reshape(steps, SUBLANES_V7X, LANES_V7X)[:, 0, :N_EXPERTS].astype(jnp.int32)
    n_tab = jnp.concatenate([before[1:], cnt[None, :]], axis=0) - before
    n_even = n_tab + (n_tab & 1)
    padded = (jnp.sum(n_even, axis=0) + TM_GROUP - 1) // TM_GROUP * TM_GROUP
    ends = jnp.cumsum(padded)
    starts = ends - padded
    off_tab = jnp.cumsum(n_even, axis=1) - n_even
    dst_tab = starts[None, :] + jnp.cumsum(n_even, axis=0) - n_even
    shift = jnp.repeat((off_tab - before).T, TM, axis=1)
    loc1 = pick(e1, shift) + r1
    loc2 = pick(e2, shift) + r2
    sent = jnp.sum(n_even, axis=1, keepdims=True)
    scratch = n_rows + N_EXPERTS * jnp.arange(steps, dtype=jnp.int32)[:, None]
    send = [jnp.concatenate(pair, axis=1) for pair in ((n_even, SORTED_TOKENS - sent), (off_tab, sent),
                                                       (dst_tab, scratch))]
    n_used = (ends[-1] // TM_GROUP).astype(jnp.int32).reshape(1)
    tile_row = jnp.minimum(jnp.arange(n_rows // TM_GROUP, dtype=jnp.int32), n_used[0] - 1) * TM_GROUP
    tile_expert = jnp.sum(ends[None, :] <= tile_row[:, None], axis=1).astype(jnp.int32)
    tail_start = jnp.where(padded > 0, ends - TM_GROUP, -1)
    spare = ends[-1] + TM_GROUP * jnp.arange(2 * N_EXPERTS, dtype=jnp.int32)
    scratch_tiles = -(-N_EXPERTS * steps // TM_GROUP)
    scratch_fill = n_rows + TM_GROUP * jnp.arange(scratch_tiles, dtype=jnp.int32)
    fill_start = jnp.concatenate([tail_start, jnp.where(spare < n_rows, spare, -1), scratch_fill]).astype(jnp.int32)

    flat = lambda tables: [a.reshape(-1).astype(jnp.int32) for a in tables]
    xs = _dispatch(fill_start, *flat(send), loc1, loc2, hf, n_rows + scratch_tiles * TM_GROUP)
    y = _grouped_swiglu(tile_expert, n_used, xs, wg, wu, wd, n_rows)
    return _combine(*flat((n_tab, off_tab, dst_tab)), loc1, loc2, x, meta, final_g, y)


def _arrange_in_proj(w):
    q = (w[:, :ATTN_WIDTH] * (1.0 / math.sqrt(HEAD_DIM))).astype(BF16)
    w = w.astype(BF16)
    dup = lambda start: [w[:, start + h * HEAD_DIM:start + (h + 1) * HEAD_DIM]
                         for h in range(ATTN_KV_HEADS) for _ in range(2)]
    return jnp.concatenate([q] + dup(ATTN_WIDTH) + dup(ATTN_WIDTH + KV_WIDTH) + [w[:, ATTN_WIDTH + 2 * KV_WIDTH:]],
                           axis=1)


def _block_diag(w):
    heads, d, _ = w.shape
    eye = jnp.eye(heads, dtype=w.dtype)
    return jnp.einsum('hij,hg->higj', w, eye).reshape(heads * d, heads * d)


def kernel(x, mem, rel_bias, mix_norm, w_in, attn_sinks, sc_conv_w, sc_conv_b, rg_conv_w, rg_conv_b, rg_w_a,
           rg_b_a, rg_w_x, rg_b_x, rg_lambda, w_out, xa_norm, mem_norm, xa_wq, xa_wk, xa_wv, xa_wo, ffn_norm,
           dense_wg, dense_wu, dense_wd, moe_router, moe_wg, moe_wu, moe_wd, final_norm):
    batch, seq, _ = x.shape
    depth = w_in.shape[0]
    assert depth == 2 and seq % TS == 0 and seq % TM == 0
    xt = x.reshape(batch * seq, D_MODEL)
    memt = mem.reshape(batch * MEM_LEN, D_MODEL)
    bias_tbl = _attention_bias_tables(rel_bias)
    vec = lambda a: a.reshape(1, -1)

    for layer in range(depth):
        w_gate = jnp.concatenate([_block_diag(rg_w_a[layer]), _block_diag(rg_w_x[layer])], axis=1).astype(BF16)
        b_gate = jnp.concatenate([rg_b_a[layer], rg_b_x[layer]]).reshape(1, -1)
        q, k, v, cr = _in_proj_conv(xt, vec(mix_norm[layer]), _arrange_in_proj(w_in[layer]),
                                    sc_conv_w[layer], vec(sc_conv_b[layer]), rg_conv_w[layer],
                                    vec(rg_conv_b[layer]), w_gate, b_gate, vec(rg_lambda[layer]), seq)
        attn = _attention(q, k, v, attn_sinks[layer], bias_tbl, seq // BLOCK)
        wkv = jnp.concatenate([xa_wk[layer], xa_wv[layer]], axis=1).astype(BF16)
        mk, mv = _mem_kv(memt, vec(mem_norm[layer]), wkv)
        post = functools.partial(_post_mixer, xt, attn, cr, w_out[layer].astype(BF16), vec(xa_norm[layer]),
                                 xa_wq[layer].astype(BF16), mk, mv, xa_wo[layer].astype(BF16), seq)

        j = layer // 2
        if layer % 2 == 0:
            (xt,) = post()
            xt, expert_w = _dense_ffn(xt, vec(ffn_norm[layer]), dense_wg[j].astype(BF16), dense_wu[j].astype(BF16),
                                      dense_wd[j].astype(BF16), cast_along=(moe_wg[j], moe_wu[j], moe_wd[j]))
        else:
            xt, hf, meta, metat, counts, base = post(route=_route_operands(vec(ffn_norm[layer]), moe_router[j]))
            xt = _moe_layer(xt, hf, meta, metat, counts, base, *expert_w, vec(final_norm))
    return xt.reshape(batch, seq, D_MODEL)
```

```python
import functools
import math

import jax
import jax.numpy as jnp
import numpy as np
from jax import lax
from jax.experimental import pallas as pl
from jax.experimental.pallas import tpu as pltpu

F32 = jnp.float32
BF16 = jnp.bfloat16

D_MODEL = 1024
MEM_LEN = 256
HEAD_DIM = 64
ATTN_Q_HEADS = 8
ATTN_KV_HEADS = 2
ATTN_WIDTH = ATTN_Q_HEADS * HEAD_DIM
KV_WIDTH = ATTN_KV_HEADS * HEAD_DIM
KV_DUP_WIDTH = 2 * KV_WIDTH
BLOCK = 128
SC_WIDTH = 256
SC_CONV = 3
RG_WIDTH = 256
RG_CONV = 4
RG_C = 8.0
N_BUCKETS = 32
MAX_EXACT = N_BUCKETS // 2
MAX_DISTANCE = 128
XA_HEADS = 4
XA_HEAD_DIM = 128
XA_WIDTH = XA_HEADS * XA_HEAD_DIM
N_EXPERTS = 8
EPS = 1e-6
NEG_INF = -1e30
REST_WIDTH = 3 * SC_WIDTH + 2 * RG_WIDTH

LANES_V7X = 128
SUBLANES_V7X = 8
VMEM_BYTES_V7X = 64 * 1024 * 1024
ROW_CHUNKS = D_MODEL // LANES_V7X
assert ROW_CHUNKS == SUBLANES_V7X

TM = 512
ATTN_BLOCKS = 8
TS = 1024
ROUTE_CHUNK = 256
TM_GROUP = 512
ISSUE_UNROLL = 8
PACKED_ROWS = ROW_CHUNKS // 2
SORTED_TOKENS = 2 * TM + N_EXPERTS
FF_SPLIT = 2
CARRY_ROWS = SUBLANES_V7X


def _mib(n):
    return int(n * 1024 * 1024)


def _params(semantics, vmem_mib):
    assert _mib(vmem_mib) < VMEM_BYTES_V7X
    return pltpu.CompilerParams(dimension_semantics=semantics, vmem_limit_bytes=_mib(vmem_mib))


def _rms(x, g):
    ms = jnp.mean(x * x, axis=-1, keepdims=True)
    return x * lax.rsqrt(ms + EPS) * g


def _const_spec(shape):
    nd = len(shape)
    return pl.BlockSpec(shape, lambda *_: (0,) * nd, pipeline_mode=pl.Buffered(1))


def _attn_kernel(sink_ref, q_ref, kp_ref, kc_ref, vp_ref, vc_ref, bias0_ref, bias_ref, o_ref):
    pairs_per_group = ATTN_Q_HEADS // ATTN_KV_HEADS // 2
    row = lax.broadcasted_iota(jnp.int32, (BLOCK, BLOCK), 0)
    col = lax.broadcasted_iota(jnp.int32, (BLOCK, BLOCK), 1)
    from_prev = col > row
    low_lanes = lax.broadcasted_iota(jnp.int32, (2 * BLOCK, 2 * HEAD_DIM), 1) < HEAD_DIM
    low_out = col < HEAD_DIM
    zero = jnp.zeros((), BF16)

    def block_diag(band):
        return jnp.concatenate([jnp.where(low_lanes, band, zero), jnp.where(low_lanes, zero, band)], axis=0)

    def scores(blk):
        rows = slice(blk * BLOCK, (blk + 1) * BLOCK)
        prev_rows = slice((blk - 1) * BLOCK, blk * BLOCK)
        out = []
        for g in range(ATTN_KV_HEADS):
            lanes = slice(g * 2 * HEAD_DIM, (g + 1) * 2 * HEAD_DIM)
            k_prev = kp_ref[:, lanes] if blk == 0 else kc_ref[prev_rows, lanes]
            v_prev = vp_ref[:, lanes] if blk == 0 else vc_ref[prev_rows, lanes]
            k_bd = block_diag(jnp.concatenate([k_prev, kc_ref[rows, lanes]], axis=0))
            v_bd = block_diag(jnp.concatenate([v_prev, vc_ref[rows, lanes]], axis=0))
            for pair in range(pairs_per_group):
                slab = g * pairs_per_group + pair
                q2 = q_ref[rows, slab * 2 * HEAD_DIM:(slab + 1) * 2 * HEAD_DIM]
                s = lax.dot_general(q2, k_bd, (((1,), (1,)), ((), ())), preferred_element_type=F32)
                out.append((slab, s, v_bd))
        return out

    def finish(blk, scored):
        rows = slice(blk * BLOCK, (blk + 1) * BLOCK)
        tbl_ref = bias0_ref if blk == 0 else bias_ref
        staged = []
        for slab, s, v_bd in scored:
            probs, denoms = [], []
            for side in range(2):
                h = 2 * slab + side
                sh = s[:, side * 2 * BLOCK:(side + 1) * 2 * BLOCK]
                logits = jnp.where(from_prev, sh[:, :BLOCK], sh[:, BLOCK:]) + tbl_ref[h]
                sink = sink_ref[h]
                m = jnp.maximum(jnp.max(logits, axis=-1, keepdims=True), sink)
                p = jnp.exp(logits - m)
                denoms.append(jnp.sum(p, axis=-1, keepdims=True) + jnp.exp(sink - m))
                probs += [jnp.where(from_prev, p, 0.0), jnp.where(from_prev, 0.0, p)]
            staged.append((slab, jnp.concatenate(probs, axis=-1).astype(BF16), v_bd, denoms))
        for slab, p_band, v_bd, denoms in staged:
            o = jnp.dot(p_band, v_bd, preferred_element_type=F32)
            o = o / jnp.where(low_out, denoms[0], denoms[1])
            o_ref[rows, slab * 2 * HEAD_DIM:(slab + 1) * 2 * HEAD_DIM] = o.astype(BF16)

    pending = scores(0)
    for blk in range(ATTN_BLOCKS):
        upcoming = scores(blk + 1) if blk + 1 < ATTN_BLOCKS else None
        finish(blk, pending)
        pending = upcoming


def _attention(q, k, v, sinks, bias_tbl, blocks_per_seq):
    t = q.shape[0]
    tile = ATTN_BLOCKS * BLOCK
    cur = lambda i: (i, 0)
    prev = lambda i: (jnp.maximum(i * ATTN_BLOCKS - 1, 0), 0)
    tbl = (None, ATTN_Q_HEADS, BLOCK, BLOCK)
    return pl.pallas_call(
        _attn_kernel,
        grid=(t // tile,),
        in_specs=[pl.BlockSpec(memory_space=pltpu.SMEM),
                  pl.BlockSpec((tile, ATTN_WIDTH), cur),
                  pl.BlockSpec((BLOCK, KV_DUP_WIDTH), prev),
                  pl.BlockSpec((tile, KV_DUP_WIDTH), cur),
                  pl.BlockSpec((BLOCK, KV_DUP_WIDTH), prev),
                  pl.BlockSpec((tile, KV_DUP_WIDTH), cur),
                  pl.BlockSpec(tbl, lambda i: (jnp.minimum((i * ATTN_BLOCKS) % blocks_per_seq, 1), 0, 0, 0)),
                  pl.BlockSpec(tbl, lambda i: (1, 0, 0, 0))],
        out_specs=pl.BlockSpec((tile, ATTN_WIDTH), cur),
        out_shape=jax.ShapeDtypeStruct((t, ATTN_WIDTH), BF16),
        compiler_params=_params(("parallel",), 24),
        name="swa_attention",
    )(sinks, q, k, k, v, v, bias_tbl, bias_tbl)


def _bias_table_kernel(rel_ref, bucket_ref, o_ref):
    for v in range(2):
        bucket = bucket_ref[v]
        hits = [bucket == b for b in range(N_BUCKETS)]
        for h in range(ATTN_Q_HEADS):
            tbl = jnp.full(bucket.shape, NEG_INF, F32)
            for b in range(N_BUCKETS):
                tbl = jnp.where(hits[b], rel_ref[b * ATTN_Q_HEADS + h], tbl)
            o_ref[v, h] = tbl


def _attention_bias_tables(rel_bias):
    q_idx = np.arange(BLOCK)[:, None]
    j_idx = np.arange(BLOCK)[None, :]
    from_prev = j_idx > q_idx
    n = np.where(from_prev, q_idx + BLOCK - j_idx, q_idx - j_idx)
    large = MAX_EXACT + (np.log(np.maximum(n, 1).astype(np.float32) / np.float32(MAX_EXACT))
                         / np.float32(math.log(MAX_DISTANCE / MAX_EXACT))
                         * np.float32(N_BUCKETS - MAX_EXACT)).astype(np.int32)
    bucket = np.where(n < MAX_EXACT, n, np.minimum(large, N_BUCKETS - 1))
    first = np.where(from_prev, -1, bucket)
    buckets = jnp.asarray(np.stack([first, bucket]).astype(np.int32))
    return pl.pallas_call(
        _bias_table_kernel,
        in_specs=[pl.BlockSpec(memory_space=pltpu.SMEM), pl.BlockSpec(memory_space=pltpu.VMEM)],
        out_specs=pl.BlockSpec(memory_space=pltpu.VMEM),
        out_shape=jax.ShapeDtypeStruct((2, ATTN_Q_HEADS, BLOCK, BLOCK), F32),
        name="t5_bias_table",
    )(rel_bias.astype(F32).reshape(-1), buckets)


def _shift_rows(x, s, fill):
    return jnp.concatenate([jnp.full((s, x.shape[1]), fill, x.dtype), x[:x.shape[0] - s]], axis=0)


def _in_proj_conv_kernel(x_ref, g_ref, w_ref, scw_ref, scb_ref, rgw_ref, rgb_ref, wgate_ref, bgate_ref, lam_ref,
                         q_ref, k_ref, v_ref, o_ref, r_ref, sc_ext, rg_ext, h_carry, *, tiles_per_seq):
    i = pl.program_id(0)
    ts = r_ref.shape[0]
    c0 = CARRY_ROWS

    @pl.when(i == 0)
    def _():
        r_ref[...] = jnp.zeros_like(r_ref)

    @pl.when((i == 0) | ((i + tiles_per_seq - 1) % tiles_per_seq == 0))
    def _():
        sc_ext[0:c0, :] = jnp.zeros((c0, SC_WIDTH), F32)
        rg_ext[0:c0, :] = jnp.zeros((c0, RG_WIDTH), F32)
        h_carry[...] = jnp.zeros_like(h_carry)

    attn_cols = ATTN_WIDTH + 2 * KV_DUP_WIDTH
    hx = _rms(x_ref[...], g_ref[...]).astype(BF16)
    p = jnp.dot(hx, w_ref[:, :attn_cols], preferred_element_type=F32)
    q_ref[...] = p[:, :ATTN_WIDTH].astype(BF16)
    k_ref[...] = p[:, ATTN_WIDTH:ATTN_WIDTH + KV_DUP_WIDTH].astype(BF16)
    v_ref[...] = p[:, ATTN_WIDTH + KV_DUP_WIDTH:].astype(BF16)

    sc_b = r_ref[:, 0:SC_WIDTH]
    sc_ext[c0:c0 + ts, :] = r_ref[:, SC_WIDTH:2 * SC_WIDTH] * r_ref[:, 2 * SC_WIDTH:3 * SC_WIDTH]
    rg_ext[c0:c0 + ts, :] = r_ref[:, 3 * SC_WIDTH:3 * SC_WIDTH + RG_WIDTH]
    rg_g = r_ref[:, 3 * SC_WIDTH + RG_WIDTH:]

    conv = scb_ref[...]
    for k in range(SC_CONV):
        off = c0 - (SC_CONV - 1) + k
        conv = conv + scw_ref[k:k + 1, :] * sc_ext[off:off + ts, :]
    conv_out = sc_b * conv

    rg_in = rgb_ref[...]
    for k in range(RG_CONV):
        off = c0 - (RG_CONV - 1) + k
        rg_in = rg_in + rgw_ref[k:k + 1, :] * rg_ext[off:off + ts, :]

    sc_ext[0:c0, :] = sc_ext[ts:ts + c0, :]
    rg_ext[0:c0, :] = rg_ext[ts:ts + c0, :]

    gates = jnp.dot(rg_in.astype(BF16), wgate_ref[...], preferred_element_type=F32) + bgate_ref[...]

    r_ref[...] = jnp.dot(hx, w_ref[:, attn_cols:], preferred_element_type=F32)

    r_gate = jax.nn.sigmoid(gates[:, :RG_WIDTH])
    i_gate = jax.nn.sigmoid(gates[:, RG_WIDTH:])
    neg_lam = -lam_ref[...]
    softplus = jnp.maximum(neg_lam, 0.0) + jnp.log1p(jnp.exp(-jnp.abs(neg_lam)))
    log_a = -RG_C * r_gate * softplus
    a = jnp.exp(log_a)
    u = jnp.sqrt(jnp.tanh(-log_a) * (1.0 + a * a)) * (i_gate * rg_in)

    s = 1
    while s < ts:
        u = a * _shift_rows(u, s, 0.0) + u
        a = a * _shift_rows(a, s, 1.0)
        s *= 2
    h = a * h_carry[...] + u
    h_carry[...] = h[ts - 1:ts, :]

    c = math.sqrt(2.0 / math.pi)
    gelu = 0.5 * rg_g * (1.0 + jnp.tanh(c * (rg_g + 0.044715 * (rg_g * rg_g * rg_g))))
    o_ref[:, 0:SC_WIDTH] = conv_out.astype(BF16)
    o_ref[:, SC_WIDTH:] = (h * gelu).astype(BF16)


def _in_proj_conv(x, g, w, sc_w, sc_b, rg_w, rg_b, w_gate, b_gate, lam, seq):
    t = x.shape[0]
    n = w.shape[1]
    tiles = t // TS
    assert n == ATTN_WIDTH + 2 * KV_DUP_WIDTH + REST_WIDTH
    proj = lambda width: pl.BlockSpec((TS, width), lambda i: (jnp.minimum(i, tiles - 1), 0))
    lagged = pl.BlockSpec((TS, SC_WIDTH + RG_WIDTH), lambda i: (jnp.maximum(i - 1, 0), 0))
    return pl.pallas_call(
        functools.partial(_in_proj_conv_kernel, tiles_per_seq=seq // TS),
        grid=(tiles + 1,),
        in_specs=[proj(D_MODEL), _const_spec((1, D_MODEL)), _const_spec((D_MODEL, n)),
                  _const_spec((SC_CONV, SC_WIDTH)), _const_spec((1, SC_WIDTH)),
                  _const_spec((RG_CONV, RG_WIDTH)), _const_spec((1, RG_WIDTH)),
                  _const_spec((RG_WIDTH, 2 * RG_WIDTH)), _const_spec((1, 2 * RG_WIDTH)),
                  _const_spec((1, RG_WIDTH))],
        out_specs=[proj(ATTN_WIDTH), proj(KV_DUP_WIDTH), proj(KV_DUP_WIDTH), lagged],
        out_shape=[jax.ShapeDtypeStruct((t, ATTN_WIDTH), BF16),
                   jax.ShapeDtypeStruct((t, KV_DUP_WIDTH), BF16),
                   jax.ShapeDtypeStruct((t, KV_DUP_WIDTH), BF16),
                   jax.ShapeDtypeStruct((t, SC_WIDTH + RG_WIDTH), BF16)],
        scratch_shapes=[pltpu.VMEM((TS, REST_WIDTH), F32),
                        pltpu.VMEM((TS + 2 * CARRY_ROWS, SC_WIDTH), F32),
                        pltpu.VMEM((TS + 2 * CARRY_ROWS, RG_WIDTH), F32),
                        pltpu.VMEM((1, RG_WIDTH), F32)],
        compiler_params=_params(("arbitrary",), 48),
        name="in_proj_conv_rglru",
    )(x, g, w, sc_w, sc_b, rg_w, rg_b, w_gate, b_gate, lam)


def _mem_kv_kernel(m_ref, g_ref, w_ref, k_ref, v_ref):
    h = _rms(m_ref[...], g_ref[...]).astype(BF16)
    p = jnp.dot(h, w_ref[...], preferred_element_type=F32)
    k_ref[...] = p[:, :XA_WIDTH].astype(BF16)
    v_ref[...] = p[:, XA_WIDTH:].astype(BF16)


def _mem_kv(mem, g, wkv):
    t = mem.shape[0]
    row = lambda width: pl.BlockSpec((MEM_LEN, width), lambda i: (i, 0))
    return pl.pallas_call(
        _mem_kv_kernel,
        grid=(t // MEM_LEN,),
        in_specs=[row(D_MODEL), _const_spec((1, D_MODEL)), _const_spec((D_MODEL, 2 * XA_WIDTH))],
        out_specs=[row(XA_WIDTH), row(XA_WIDTH)],
        out_shape=[jax.ShapeDtypeStruct((t, XA_WIDTH), BF16)] * 2,
        compiler_params=_params(("parallel",), 24),
        name="mem_kv",
    )(mem, g, wkv)


def _post_mixer_kernel(x_ref, a_ref, c_ref, wout_ref, g_ref, wq_ref, k_ref, v_ref, wo_ref, *rest):
    o_ref = rest[-1] if len(rest) == 1 else rest[4]
    halves = [slice(i * (TM // 2), (i + 1) * (TM // 2)) for i in range(2)]
    heads = [slice(hd * XA_HEAD_DIM, (hd + 1) * XA_HEAD_DIM) for hd in range(XA_HEADS)]
    nt = (((1,), (1,)), ((), ()))
    k = k_ref[...]
    v = v_ref[...]

    x1 = [x_ref[hs, :] + jnp.dot(jnp.concatenate([a_ref[hs, :], c_ref[hs, :]], axis=-1), wout_ref[...],
                                 preferred_element_type=F32) for hs in halves]
    q = [jnp.dot(_rms(xh, g_ref[...]).astype(BF16), wq_ref[...], preferred_element_type=F32).astype(BF16)
         for xh in x1]
    scores = [[lax.dot_general(qh[:, sl], k[:, sl], nt, preferred_element_type=F32) for sl in heads] for qh in q]
    x2 = []
    for xh, per_head in zip(x1, scores):
        probs, sums = [], []
        for s in per_head:
            s = s * (1.0 / math.sqrt(XA_HEAD_DIM))
            p = jnp.exp(s - jnp.max(s, axis=-1, keepdims=True))
            probs.append(p.astype(BF16))
            sums.append(jnp.sum(p, axis=-1, keepdims=True))
        att = jnp.concatenate([jnp.dot(p, v[:, sl], preferred_element_type=F32) / l
                               for p, sl, l in zip(probs, heads, sums)], axis=-1).astype(BF16)
        x2.append(xh + jnp.dot(att, wo_ref[...], preferred_element_type=F32))

    for hs, xh in zip(halves, x2):
        o_ref[hs, :] = xh
    if len(rest) > 1:
        ffn_g_ref, wr_ref, tri_ref, sel_ref, _, hf_ref, meta_ref, metat_ref, cnt_ref, base_ref, carry = rest
        h = _rms(jnp.concatenate(x2, axis=0), ffn_g_ref[...])
        _route(h, wr_ref, tri_ref, sel_ref, hf_ref, meta_ref, metat_ref, cnt_ref, base_ref, carry)


def _post_mixer(x, attn, cr, w_out, g, wq, k, v, wo, seq, route=None):
    t = x.shape[0]
    per_seq = seq // TM
    row = lambda width: pl.BlockSpec((TM, width), lambda i: (i, 0))
    mem_blk = pl.BlockSpec((MEM_LEN, XA_WIDTH), lambda i: (i // per_seq, 0))
    in_specs = [row(D_MODEL), row(ATTN_WIDTH), row(SC_WIDTH + RG_WIDTH), _const_spec((D_MODEL, D_MODEL)),
                _const_spec((1, D_MODEL)), _const_spec((D_MODEL, XA_WIDTH)), mem_blk, mem_blk,
                _const_spec((XA_WIDTH, D_MODEL))]
    out_specs = [row(D_MODEL)]
    out_shape = [jax.ShapeDtypeStruct((t, D_MODEL), F32)]
    scratch = []
    args = [x, attn, cr, w_out, g, wq, k, v, wo]
    if route is not None:
        assert TM % ROUTE_CHUNK == 0
        in_specs += [_const_spec((1, D_MODEL)), _const_spec((D_MODEL, LANES_V7X)),
                     _const_spec((ROUTE_CHUNK, ROUTE_CHUNK)), _const_spec((SUBLANES_V7X, LANES_V7X))]
        out_specs += [row(D_MODEL), row(LANES_V7X),
                      pl.BlockSpec((SUBLANES_V7X, TM), lambda i: (0, i)),
                      pl.BlockSpec((1, LANES_V7X), lambda i: (0, 0)),
                      pl.BlockSpec((SUBLANES_V7X, LANES_V7X), lambda i: (i, 0))]
        out_shape += [jax.ShapeDtypeStruct((t, D_MODEL), BF16),
                      jax.ShapeDtypeStruct((t, LANES_V7X), F32),
                      jax.ShapeDtypeStruct((SUBLANES_V7X, t), F32),
                      jax.ShapeDtypeStruct((1, LANES_V7X), F32),
                      jax.ShapeDtypeStruct((t // TM * SUBLANES_V7X, LANES_V7X), F32)]
        scratch = [pltpu.VMEM((1, LANES_V7X), F32)]
        args += list(route)
    return pl.pallas_call(
        _post_mixer_kernel,
        grid=(t // TM,),
        in_specs=in_specs,
        out_specs=out_specs,
        out_shape=out_shape,
        scratch_shapes=scratch,
        compiler_params=_params(("arbitrary",), 40),
        name="post_mixer_route" if route is not None else "post_mixer",
    )(*args)


def _ffn_kernel(x_ref, g_ref, wg_ref, wu_ref, wd_ref, *rest):
    n_cast = (len(rest) - 1) // 2
    o_ref = rest[n_cast]
    x = x_ref[...]
    h = _rms(x, g_ref[...]).astype(BF16)
    gate = jnp.dot(h, wg_ref[...], preferred_element_type=F32)
    up = jnp.dot(h, wu_ref[...], preferred_element_type=F32)
    act = (gate * jax.nn.sigmoid(gate) * up).astype(BF16)
    o_ref[...] = x + jnp.dot(act, wd_ref[...], preferred_element_type=F32)
    for src, dst in zip(rest[:n_cast], rest[n_cast + 1:]):
        dst[...] = src[...].astype(BF16)


def _dense_ffn(x, g, wg, wu, wd, cast_along=()):
    t = x.shape[0]
    steps = t // TM
    d_ff = wg.shape[1]
    row = pl.BlockSpec((TM, D_MODEL), lambda i: (i, 0))
    flat = [w.reshape(-1, w.shape[-1]) for w in cast_along]
    for w in flat:
        assert w.shape[0] % (steps * 2 * SUBLANES_V7X) == 0
    slabs = [pl.BlockSpec((w.shape[0] // steps, w.shape[1]), lambda i: (i, 0)) for w in flat]
    outs = pl.pallas_call(
        _ffn_kernel,
        grid=(steps,),
        in_specs=[row, _const_spec((1, D_MODEL)), _const_spec((D_MODEL, d_ff)), _const_spec((D_MODEL, d_ff)),
                  _const_spec((d_ff, D_MODEL))] + slabs,
        out_specs=[row] + slabs,
        out_shape=[jax.ShapeDtypeStruct((t, D_MODEL), F32)] + [jax.ShapeDtypeStruct(w.shape, BF16) for w in flat],
        compiler_params=_params(("parallel",), 60),
        name="dense_swiglu",
    )(x, g, wg, wu, wd, *flat)
    return outs[0], [o.reshape(w.shape) for o, w in zip(outs[1:], cast_along)]


META_E1, META_E2, META_R1, META_R2, META_W1, META_W2 = range(6)


def _to_token_tiles(ref, rows):
    m = rows.shape[0]
    for c in range(ROW_CHUNKS):
        ref[pl.ds(c, m, stride=ROW_CHUNKS), :] = rows[:, c * LANES_V7X:(c + 1) * LANES_V7X]


def _from_token_tiles(ref):
    m = ref.shape[0] // ROW_CHUNKS
    return jnp.concatenate([ref[pl.ds(c, m, stride=ROW_CHUNKS), :] for c in range(ROW_CHUNKS)], axis=-1)


def _route(h, wr_ref, tri_ref, sel_ref, hf_ref, meta_ref, metat_ref, cnt_ref, base_ref, carry):
    @pl.when(pl.program_id(0) == 0)
    def _():
        carry[...] = jnp.zeros_like(carry)

    base_ref[...] = jnp.broadcast_to(carry[...], base_ref.shape)
    chunks = [slice(c * ROUTE_CHUNK, (c + 1) * ROUTE_CHUNK) for c in range(h.shape[0] // ROUTE_CHUNK)]
    each = lambda fn, *lists: [fn(*vals) for vals in zip(*lists)]
    rowmax = lambda a: jnp.max(a, axis=-1, keepdims=True)
    rowsum = lambda a: jnp.sum(a, axis=-1, keepdims=True)
    lane = lax.broadcasted_iota(jnp.int32, (ROUTE_CHUNK, LANES_V7X), 1)
    first_hit = lambda lg, m: jnp.min(jnp.where(lg == m, lane, LANES_V7X), axis=-1, keepdims=True)

    hb = h.astype(BF16)
    hf_ref[...] = hb
    logits = [jnp.dot(hb[c, :], wr_ref[...], preferred_element_type=F32) for c in chunks]
    lg = each(lambda l: jnp.where(lane < N_EXPERTS, l, -jnp.inf), logits)
    m1 = each(rowmax, lg)
    e1 = each(first_hit, lg, m1)
    lg2 = each(lambda l, e: jnp.where(lane == e, -jnp.inf, l), lg, e1)
    m2 = each(rowmax, lg2)
    e2 = each(first_hit, lg2, m2)
    ex = each(lambda a, b: jnp.exp(b - a), m1, m2)
    w1 = each(lambda e: 1.0 / (1.0 + e), ex)
    w2 = each(lambda e: e / (1.0 + e), ex)
    hit1 = each(lambda e: lane == e, e1)
    hit2 = each(lambda e: lane == e, e2)
    onehot = each(lambda a, b: (a | b).astype(BF16), hit1, hit2)
    within = each(lambda o: jnp.dot(tri_ref[...], o, preferred_element_type=F32), onehot)
    totals = each(lambda o: jnp.sum(o.astype(F32), axis=0, keepdims=True), onehot)
    ahead, base = [], carry[...]
    for w, tot in zip(within, totals):
        ahead.append(w + base)
        base = base + tot
    carry[...] = base
    cnt_ref[...] = base
    r1 = each(lambda hit, a: rowsum(jnp.where(hit, a, 0.0)), hit1, ahead)
    r2 = each(lambda hit, a: rowsum(jnp.where(hit, a, 0.0)), hit2, ahead)

    def record(*vals):
        meta = jnp.zeros((ROUTE_CHUNK, LANES_V7X), F32)
        for col, val in zip((META_E1, META_E2, META_R1, META_R2, META_W1, META_W2), vals):
            meta = jnp.where(lane == col, val.astype(F32), meta)
        return meta

    meta = each(record, e1, e2, r1, r2, w1, w2)
    metat = each(lambda mt: lax.dot_general(sel_ref[...], mt, (((1,), (1,)), ((), ())), preferred_element_type=F32,
                                            precision=lax.Precision.HIGHEST), meta)
    for c, mt, mtt in zip(chunks, meta, metat):
        meta_ref[c, :] = mt
        metat_ref[:, c] = mtt


def _token_rows(ref, first_token, n_tokens):
    start = pl.multiple_of(first_token * ROW_CHUNKS, ROW_CHUNKS)
    return ref.at[pl.ds(start, n_tokens * ROW_CHUNKS)]


def _packed_rows(ref, first_token, n_tokens):
    start = pl.multiple_of(first_token * PACKED_ROWS, SUBLANES_V7X)
    return ref.at[pl.ds(start, n_tokens * PACKED_ROWS)]


def _dispatch_kernel(fill_ref, n_ref, off_ref, dst_ref, loc1_ref, loc2_ref, hf_ref, xs_ref, zeros, buf, sem_z, sems):
    i = pl.program_id(0)
    steps = pl.num_programs(0)
    slot_tokens = buf.shape[1] // PACKED_ROWS
    runs = N_EXPERTS + 1

    @pl.when(i == 0)
    def _():
        zeros[...] = jnp.zeros_like(zeros)

        def tile_fill(e):
            return pltpu.make_async_copy(zeros, _packed_rows(xs_ref, pl.multiple_of(fill_ref[e], TM_GROUP), TM_GROUP),
                                         sem_z)

        for e in range(fill_ref.shape[0]):
            @pl.when(fill_ref[e] >= 0)
            def _():
                tile_fill(e).start()
        for e in range(fill_ref.shape[0]):
            @pl.when(fill_ref[e] >= 0)
            def _():
                tile_fill(e).wait()

    k = lax.broadcasted_iota(jnp.int32, (slot_tokens, hf_ref.shape[0]), 0)
    onehot = ((k == loc1_ref[...]) | (k == loc2_ref[...])).astype(BF16)
    sorted_rows = jnp.dot(onehot, hf_ref[...], preferred_element_type=F32)
    bits = pltpu.bitcast(sorted_rows, jnp.uint32)
    half = D_MODEL // 2
    packed = (bits[:, :half] & jnp.uint32(0xFFFF0000)) | (bits[:, half:] >> 16)

    def run(slot):
        whole_slot = pltpu.make_async_copy(buf.at[slot], _packed_rows(xs_ref, 0, slot_tokens), sems.at[slot])

        @pl.when(i >= 2)
        def _():
            whole_slot.wait()

        for c in range(PACKED_ROWS):
            buf.at[slot][pl.ds(c, slot_tokens, stride=PACKED_ROWS), :] = packed[:, c * LANES_V7X:(c + 1) * LANES_V7X]
        for e in range(runs):
            n = n_ref[i * runs + e]
            off = off_ref[i * runs + e]
            dst = dst_ref[i * runs + e]
            bit = hf_ref.shape[0]
            while bit >= 2:
                head = n & ~(2 * bit - 1)

                @pl.when((n & bit) != 0)
                def _(bit=bit, head=head):
                    pltpu.make_async_copy(_packed_rows(buf.at[slot], off + head, bit),
                                          _packed_rows(xs_ref, dst + head, bit), sems.at[slot]).start()
                bit //= 2

        @pl.when(i == steps - 1)
        def _():
            whole_slot.wait()
            other = pltpu.make_async_copy(buf.at[1 - slot], _packed_rows(xs_ref, 0, slot_tokens), sems.at[1 - slot])

            @pl.when(steps >= 2)
            def _():
                other.wait()

    for slot in range(2):
        pl.when(i % 2 == slot)(functools.partial(run, slot))


def _dispatch(fill_start, n_tab, off_tab, dst_tab, loc1, loc2, hf, total_rows):
    t = hf.shape[0]
    loc = pl.BlockSpec((1, TM), lambda i, *_: (0, i))
    u32 = jnp.uint32
    return pl.pallas_call(
        _dispatch_kernel,
        grid_spec=pltpu.PrefetchScalarGridSpec(
            num_scalar_prefetch=4,
            grid=(t // TM,),
            in_specs=[loc, loc, pl.BlockSpec((TM, D_MODEL), lambda i, *_: (i, 0))],
            out_specs=pl.BlockSpec(memory_space=pl.ANY),
            scratch_shapes=[pltpu.VMEM((TM_GROUP * PACKED_ROWS, LANES_V7X), u32),
                            pltpu.VMEM((2, SORTED_TOKENS * PACKED_ROWS, LANES_V7X), u32),
                            pltpu.SemaphoreType.DMA(()), pltpu.SemaphoreType.DMA((2,))]),
        out_shape=jax.ShapeDtypeStruct((total_rows * PACKED_ROWS, LANES_V7X), u32),
        compiler_params=_params(("arbitrary",), 40),
        name="moe_dispatch",
    )(fill_start, n_tab, off_tab, dst_tab, loc1.reshape(1, t), loc2.reshape(1, t), hf)


def _grouped_kernel(te_ref, used_ref, x_ref, wg_ref, wu_ref, wd_ref, y_ref):
    i = pl.program_id(0)

    @pl.when(i < used_ref[0])
    def _():
        tm = x_ref.shape[0] // PACKED_ROWS
        words = jnp.concatenate([x_ref[pl.ds(c, tm, stride=PACKED_ROWS), :] for c in range(PACKED_ROWS)], axis=-1)
        high = pltpu.bitcast(words & jnp.uint32(0xFFFF0000), F32).astype(BF16)
        low = pltpu.bitcast(words << 16, F32).astype(BF16)
        h = jnp.concatenate([high, low], axis=-1)
        ff = wg_ref.shape[1] // FF_SPLIT
        cols = [slice(c * ff, (c + 1) * ff) for c in range(FF_SPLIT)]
        gates = [jnp.dot(h, wg_ref[:, c], preferred_element_type=F32) for c in cols]
        ups = [jnp.dot(h, wu_ref[:, c], preferred_element_type=F32) for c in cols]
        total = None
        for c, gate, up in zip(cols, gates, ups):
            act = (gate * jax.nn.sigmoid(gate) * up).astype(BF16)
            part = jnp.dot(act, wd_ref[c, :], preferred_element_type=F32)
            total = part if total is None else total + part
        _to_token_tiles(y_ref, total)

    @pl.when(i >= used_ref[0])
    def _():
        y_ref[...] = jnp.zeros_like(y_ref)


def _grouped_swiglu(tile_expert, n_used, xs, wg, wu, wd, n_rows):
    d_ff = wg.shape[2]
    tile = (TM_GROUP * ROW_CHUNKS, LANES_V7X)
    packed_tile = (TM_GROUP * PACKED_ROWS, LANES_V7X)
    src = lambda i, te, used: (jnp.maximum(jnp.minimum(i, used[0] - 1), 0), 0)
    expert = lambda shape: pl.BlockSpec((None,) + shape, lambda i, te, used: (te[i], 0, 0))
    return pl.pallas_call(
        _grouped_kernel,
        grid_spec=pltpu.PrefetchScalarGridSpec(
            num_scalar_prefetch=2,
            grid=(n_rows // TM_GROUP,),
            in_specs=[pl.BlockSpec(packed_tile, src), expert((D_MODEL, d_ff)), expert((D_MODEL, d_ff)),
                      expert((d_ff, D_MODEL))],
            out_specs=pl.BlockSpec(tile, lambda i, te, used: (i, 0))),
        out_shape=jax.ShapeDtypeStruct((n_rows * ROW_CHUNKS, LANES_V7X), F32),
        compiler_params=_params(("arbitrary",), 58),
        name="moe_grouped_swiglu",
    )(tile_expert, n_used, xs, wg, wu, wd)


def _combine_kernel(n_ref, off_ref, dst_ref, loc1_ref, loc2_ref, x_ref, meta_ref, g_ref, y_ref, o_ref,
                    slab, y1, y2, sems):
    i = pl.program_id(0)
    tm = x_ref.shape[0]

    def fetch(tile, slot):
        for e in range(N_EXPERTS):
            n = n_ref[tile * N_EXPERTS + e]
            off = off_ref[tile * N_EXPERTS + e]
            dst = dst_ref[tile * N_EXPERTS + e]
            bit = tm
            while bit >= 1:
                head = n & ~(2 * bit - 1)

                @pl.when((n & bit) != 0)
                def _(bit=bit, head=head):
                    pltpu.make_async_copy(_token_rows(y_ref, dst + head, bit),
                                          _token_rows(slab.at[slot], off + head, bit), sems.at[slot]).start()
                bit //= 2

    def finish(slot):
        pltpu.make_async_copy(_token_rows(y_ref, 0, 2 * tm), _token_rows(slab.at[slot], 0, 2 * tm),
                              sems.at[slot]).wait()

        def pick(r, carry):
            y1[pl.ds(pl.multiple_of(r * ROW_CHUNKS, ROW_CHUNKS), ROW_CHUNKS), :] = (
                _token_rows(slab.at[slot], loc1_ref[r], 1)[...])
            y2[pl.ds(pl.multiple_of(r * ROW_CHUNKS, ROW_CHUNKS), ROW_CHUNKS), :] = (
                _token_rows(slab.at[slot], loc2_ref[r], 1)[...])
            return carry

        lax.fori_loop(0, tm, pick, 0, unroll=ISSUE_UNROLL)
        w1 = meta_ref[:, META_W1:META_W1 + 1]
        w2 = meta_ref[:, META_W2:META_W2 + 1]
        out = x_ref[...] + (w1 * _from_token_tiles(y1) + w2 * _from_token_tiles(y2))
        o_ref[...] = _rms(out, g_ref[...])

    @pl.when(i == 0)
    def _():
        fetch(0, 0)

    for slot in range(2):
        @pl.when(i % 2 == slot)
        def _():
            @pl.when(i + 1 < pl.num_programs(0))
            def _():
                fetch(i + 1, 1 - slot)
            finish(slot)


def _combine(n_tab, off_tab, dst_tab, loc1, loc2, x, meta, g, y):
    t = x.shape[0]
    idx = pl.BlockSpec((TM,), lambda i, *_: (i,), memory_space=pltpu.SMEM)
    row = lambda width: pl.BlockSpec((TM, width), lambda i, *_: (i, 0))
    return pl.pallas_call(
        _combine_kernel,
        grid_spec=pltpu.PrefetchScalarGridSpec(
            num_scalar_prefetch=3,
            grid=(t // TM,),
            in_specs=[idx, idx, row(D_MODEL), row(LANES_V7X),
                      pl.BlockSpec((1, D_MODEL), lambda i, *_: (0, 0), pipeline_mode=pl.Buffered(1)),
                      pl.BlockSpec(memory_space=pl.ANY)],
            out_specs=row(D_MODEL),
            scratch_shapes=[pltpu.VMEM((2, SORTED_TOKENS * ROW_CHUNKS, LANES_V7X), F32),
                            pltpu.VMEM((TM * ROW_CHUNKS, LANES_V7X), F32),
                            pltpu.VMEM((TM * ROW_CHUNKS, LANES_V7X), F32),
                            pltpu.SemaphoreType.DMA((2,))]),
        out_shape=jax.ShapeDtypeStruct((t, D_MODEL), F32),
        compiler_params=_params(("arbitrary",), 40),
        name="moe_combine_norm",
    )(n_tab, off_tab, dst_tab, loc1, loc2, x, meta, g, y)


def _route_operands(ffn_g, router_w):
    wr_pad = jnp.zeros((D_MODEL, LANES_V7X), BF16).at[:, :N_EXPERTS].set(router_w.astype(BF16))
    tri = jnp.tril(jnp.ones((ROUTE_CHUNK, ROUTE_CHUNK), BF16), -1)
    sel = jnp.eye(SUBLANES_V7X, LANES_V7X, dtype=F32)
    return ffn_g, wr_pad, tri, sel


def _moe_layer(x, hf, meta, metat, counts, base, wg, wu, wd, final_g):
    t = x.shape[0]
    steps = t // TM
    n_rows = -(-(2 * t + N_EXPERTS * steps) // TM_GROUP) * TM_GROUP + N_EXPERTS * TM_GROUP

    cnt = counts[0, :N_EXPERTS].astype(jnp.int32)
    experts = jnp.arange(N_EXPERTS, dtype=jnp.int32)[:, None]
    e1, e2 = metat[META_E1].astype(jnp.int32), metat[META_E2].astype(jnp.int32)
    r1, r2 = metat[META_R1].astype(jnp.int32), metat[META_R2].astype(jnp.int32)
    pick = lambda e, table: jnp.sum(jnp.where(e[None, :] == experts, table, 0), axis=0)
    before = base.reshape(steps, SUBLANES_V7X, LANES_V7X)[:, 0, :N_EXPERTS].astype(jnp.int32)
    n_tab = jnp.concatenate([before[1:], cnt[None, :]], axis=0) - before
    n_even = n_tab + (n_tab & 1)
    padded = (jnp.sum(n_even, axis=0) + TM_GROUP - 1) // TM_GROUP * TM_GROUP
    ends = jnp.cumsum(padded)
    starts = ends - padded
    off_tab = jnp.cumsum(n_even, axis=1) - n_even
    dst_tab = starts[None, :] + jnp.cumsum(n_even, axis=0) - n_even
    shift = jnp.repeat((off_tab - before).T, TM, axis=1)
    loc1 = pick(e1, shift) + r1
    loc2 = pick(e2, shift) + r2
    sent = jnp.sum(n_even, axis=1, keepdims=True)
    scratch = n_rows + N_EXPERTS * jnp.arange(steps, dtype=jnp.int32)[:, None]
    send = [jnp.concatenate(pair, axis=1) for pair in ((n_even, SORTED_TOKENS - sent), (off_tab, sent),
                                                       (dst_tab, scratch))]
    n_used = (ends[-1] // TM_GROUP).astype(jnp.int32).reshape(1)
    tile_row = jnp.minimum(jnp.arange(n_rows // TM_GROUP, dtype=jnp.int32), n_used[0] - 1) * TM_GROUP
    tile_expert = jnp.sum(ends[None, :] <= tile_row[:, None], axis=1).astype(jnp.int32)
    tail_start = jnp.where(padded > 0, ends - TM_GROUP, -1)
    spare = ends[-1] + TM_GROUP * jnp.arange(2 * N_EXPERTS, dtype=jnp.int32)
    scratch_tiles = -(-N_EXPERTS * steps // TM_GROUP)
    scratch_fill = n_rows + TM_GROUP * jnp.arange(scratch_tiles, dtype=jnp.int32)
    fill_start = jnp.concatenate([tail_start, jnp.where(spare < n_rows, spare, -1), scratch_fill]).astype(jnp.int32)

    flat = lambda tables: [a.reshape(-1).astype(jnp.int32) for a in tables]
    xs = _dispatch(fill_start, *flat(send), loc1, loc2, hf, n_rows + scratch_tiles * TM_GROUP)
    y = _grouped_swiglu(tile_expert, n_used, xs, wg, wu, wd, n_rows)
    return _combine(*flat((n_tab, off_tab, dst_tab)), loc1, loc2, x, meta, final_g, y)


def _arrange_in_proj(w):
    q = (w[:, :ATTN_WIDTH] * (1.0 / math.sqrt(HEAD_DIM))).astype(BF16)
    w = w.astype(BF16)
    dup = lambda start: [w[:, start + h * HEAD_DIM:start + (h + 1) * HEAD_DIM]
                         for h in range(ATTN_KV_HEADS) for _ in range(2)]
    return jnp.concatenate([q] + dup(ATTN_WIDTH) + dup(ATTN_WIDTH + KV_WIDTH) + [w[:, ATTN_WIDTH + 2 * KV_WIDTH:]],
                           axis=1)


def _block_diag(w):
    heads, d, _ = w.shape
    eye = jnp.eye(heads, dtype=w.dtype)
    return jnp.einsum('hij,hg->higj', w, eye).reshape(heads * d, heads * d)


def kernel(x, mem, rel_bias, mix_norm, w_in, attn_sinks, sc_conv_w, sc_conv_b, rg_conv_w, rg_conv_b, rg_w_a,
           rg_b_a, rg_w_x, rg_b_x, rg_lambda, w_out, xa_norm, mem_norm, xa_wq, xa_wk, xa_wv, xa_wo, ffn_norm,
           dense_wg, dense_wu, dense_wd, moe_router, moe_wg, moe_wu, moe_wd, final_norm):
    batch, seq, _ = x.shape
    depth = w_in.shape[0]
    assert depth == 2 and seq % TS == 0 and seq % TM == 0
    xt = x.reshape(batch * seq, D_MODEL)
    memt = mem.reshape(batch * MEM_LEN, D_MODEL)
    bias_tbl = _attention_bias_tables(rel_bias)
    vec = lambda a: a.reshape(1, -1)

    for layer in range(depth):
        w_gate = jnp.concatenate([_block_diag(rg_w_a[layer]), _block_diag(rg_w_x[layer])], axis=1).astype(BF16)
        b_gate = jnp.concatenate([rg_b_a[layer], rg_b_x[layer]]).reshape(1, -1)
        q, k, v, cr = _in_proj_conv(xt, vec(mix_norm[layer]), _arrange_in_proj(w_in[layer]),
                                    sc_conv_w[layer], vec(sc_conv_b[layer]), rg_conv_w[layer],
                                    vec(rg_conv_b[layer]), w_gate, b_gate, vec(rg_lambda[layer]), seq)
        attn = _attention(q, k, v, attn_sinks[layer], bias_tbl, seq // BLOCK)
        wkv = jnp.concatenate([xa_wk[layer], xa_wv[layer]], axis=1).astype(BF16)
        mk, mv = _mem_kv(memt, vec(mem_norm[layer]), wkv)
        post = functools.partial(_post_mixer, xt, attn, cr, w_out[layer].astype(BF16), vec(xa_norm[layer]),
                                 xa_wq[layer].astype(BF16), mk, mv, xa_wo[layer].astype(BF16), seq)

        j = layer // 2
        if layer % 2 == 0:
            (xt,) = post()
            xt, expert_w = _dense_ffn(xt, vec(ffn_norm[layer]), dense_wg[j].astype(BF16), dense_wu[j].astype(BF16),
                                      dense_wd[j].astype(BF16), cast_along=(moe_wg[j], moe_wu[j], moe_wd[j]))
        else:
            xt, hf, meta, metat, counts, base = post(route=_route_operands(vec(ffn_norm[layer]), moe_router[j]))
            xt = _moe_layer(xt, hf, meta, metat, counts, base, *expert_w, vec(final_norm))
    return xt.reshape(batch, seq, D_MODEL)
```
